```python
import math
import jax, jax.numpy as jnp
from jax import lax
import numpy as np

D_MODEL = 1024
BATCH = 8
SEQ = 8192
DEPTH = 2

D_MIX = D_MODEL
HEAD_DIM = 64
D_SC = D_MIX // 4
D_ATT = D_MIX // 2
D_CC = D_MIX // 4
N_Q_HEADS = D_ATT // HEAD_DIM
N_KV_HEADS = 2
GQA_GROUP = N_Q_HEADS // N_KV_HEADS
SC_WIDTH = 3
CC_WIDTH = 31
WINDOW = 128
BLOCK = 128
ROPE_THETA = 10000.0
D_FF = 2816
LN_EPS = 1e-5
ALPHA = (2.0 * DEPTH) ** 0.25
BETA = (8.0 * DEPTH) ** -0.25
IN_SIZES = (D_SC, D_SC, D_SC, N_Q_HEADS * HEAD_DIM, N_KV_HEADS * HEAD_DIM, N_KV_HEADS * HEAD_DIM, D_CC, D_CC)
D_IN = sum(IN_SIZES)
IN_OFFSETS = tuple(int(v) for v in np.cumsum(IN_SIZES)[:-1])

kernel_name = "hybrid_parallel_conv_swa_conformer_encoder"


def layer_norm(x, g, b):
    xf = x.astype(jnp.float32)
    mu = jnp.mean(xf, axis=-1, keepdims=True)
    var = jnp.mean(jnp.square(xf - mu), axis=-1, keepdims=True)
    y = (xf - mu) * lax.rsqrt(var + LN_EPS)
    return (y * g.astype(jnp.float32) + b.astype(jnp.float32)).astype(x.dtype)


def swiglu(x, w_gu, w_down):
    g, u = jnp.split(x @ w_gu, 2, axis=-1)
    return (jax.nn.silu(g) * u) @ w_down


def depthwise_conv(x, w):
    k = w.shape[0]
    pad = (k - 1) // 2
    return lax.conv_general_dilated(
        x, w[:, None, :].astype(x.dtype), window_strides=(1,), padding=[(pad, pad)],
        dimension_numbers=("NWC", "WIO", "NWC"), feature_group_count=x.shape[-1])


def rope(x, positions):
    half = HEAD_DIM // 2
    inv_freq = ROPE_THETA ** (-jnp.arange(half, dtype=jnp.float32) / half)
    ang = positions.astype(jnp.float32)[:, None] * inv_freq[None, :]
    cos = jnp.cos(ang)[None, :, None, :]
    sin = jnp.sin(ang)[None, :, None, :]
    xf = x.astype(jnp.float32)
    x1, x2 = xf[..., :half], xf[..., half:]
    return jnp.concatenate([x1 * cos - x2 * sin, x2 * cos + x1 * sin], axis=-1).astype(x.dtype)


def banded_window_attention(q, k, v, sink):
    b, s = q.shape[0], q.shape[1]
    nb = s // BLOCK
    qb = q.reshape(b, nb, BLOCK, N_KV_HEADS, GQA_GROUP, HEAD_DIM)

    def band(t):
        tp = jnp.pad(t, ((0, 0), (BLOCK, BLOCK), (0, 0), (0, 0)))
        parts = [tp[:, o * BLOCK:o * BLOCK + s].reshape(b, nb, BLOCK, N_KV_HEADS, HEAD_DIM) for o in range(3)]
        return jnp.concatenate(parts, axis=2)

    kb, vb = band(k), band(v)
    scores = jnp.einsum("bnqkgd,bnskd->bnkgqs", qb, kb).astype(jnp.float32) * (HEAD_DIM ** -0.5)
    qpos = jnp.arange(s).reshape(nb, BLOCK)
    kpos = jnp.arange(nb)[:, None] * BLOCK - BLOCK + jnp.arange(3 * BLOCK)[None, :]
    valid = (jnp.abs(qpos[:, :, None] - kpos[:, None, :]) <= WINDOW) \
        & (kpos >= 0)[:, None, :] & (kpos < s)[:, None, :]
    scores = jnp.where(valid[None, :, None, None], scores, -1e30)
    sink_f = sink.astype(jnp.float32).reshape(N_KV_HEADS, GQA_GROUP)[None, None, :, :, None]
    m = jnp.maximum(jnp.max(scores, axis=-1), sink_f)
    p = jnp.exp(scores - m[..., None])
    denom = jnp.sum(p, axis=-1) + jnp.exp(sink_f - m)
    o = jnp.einsum("bnkgqs,bnskd->bnqkgd", p.astype(v.dtype), vb).astype(jnp.float32)
    o = o / jnp.transpose(denom, (0, 1, 4, 2, 3))[..., None]
    return o.reshape(b, s, D_ATT).astype(q.dtype)


def hybrid_mixer(x, w_in, sc_conv_w, attn_sink, cc_conv_w, cc_conv_b, cc_ln_g, cc_ln_b, w_out):
    b, s, _ = x.shape
    positions = jnp.arange(s)
    z = x @ w_in
    sc_b, sc_c, sc_h, q, k, v, cc_a, cc_gate = jnp.split(z, IN_OFFSETS, axis=-1)
    y_sc = sc_b * depthwise_conv(sc_c * sc_h, sc_conv_w)
    q = rope(q.reshape(b, s, N_Q_HEADS, HEAD_DIM), positions)
    k = rope(k.reshape(b, s, N_KV_HEADS, HEAD_DIM), positions)
    v = v.reshape(b, s, N_KV_HEADS, HEAD_DIM)
    y_att = banded_window_attention(q, k, v, attn_sink)
    u = cc_a * jax.nn.sigmoid(cc_gate)
    u = depthwise_conv(u, cc_conv_w) + cc_conv_b
    y_cc = jax.nn.silu(layer_norm(u, cc_ln_g, cc_ln_b))
    return jnp.concatenate([y_sc, y_att, y_cc], axis=-1) @ w_out


def _fwd_setup_inputs(seed: int = 0) -> dict:
    key = jax.random.key(seed)
    ks = jax.random.split(key, 24)
    f32 = jnp.float32

    def nrm(k, shape, scale):
        return jax.random.normal(k, shape, f32) * scale

    def gain(k, n):
        return 1.0 + 0.02 * jax.random.normal(k, (DEPTH, n), f32)

    return {
        "x": jax.random.normal(ks[0], (BATCH, SEQ, D_MODEL), f32),
        "ffn1_w_gu": nrm(ks[1], (DEPTH, D_MODEL, 2 * D_FF), D_MODEL ** -0.5),
        "ffn1_w_down": nrm(ks[2], (DEPTH, D_FF, D_MODEL), BETA * D_FF ** -0.5),
        "ln1_g": gain(ks[3], D_MODEL),
        "ln1_b": nrm(ks[4], (DEPTH, D_MODEL), 0.02),
        "w_in": nrm(ks[5], (DEPTH, D_MODEL, D_IN), D_MODEL ** -0.5),
        "sc_conv_w": nrm(ks[6], (DEPTH, SC_WIDTH, D_SC), SC_WIDTH ** -0.5),
        "attn_sink": nrm(ks[7], (DEPTH, N_Q_HEADS), 0.5),
        "cc_conv_w": nrm(ks[8], (DEPTH, CC_WIDTH, D_CC), CC_WIDTH ** -0.5),
        "cc_conv_b": nrm(ks[9], (DEPTH, D_CC), 0.02),
        "cc_ln_g": gain(ks[10], D_CC),
        "cc_ln_b": nrm(ks[11], (DEPTH, D_CC), 0.02),
        "w_out": nrm(ks[12], (DEPTH, D_MIX, D_MODEL), BETA * D_MIX ** -0.5),
        "ln2_g": gain(ks[13], D_MODEL),
        "ln2_b": nrm(ks[14], (DEPTH, D_MODEL), 0.02),
        "ffn2_w_gu": nrm(ks[15], (DEPTH, D_MODEL, 2 * D_FF), D_MODEL ** -0.5),
        "ffn2_w_down": nrm(ks[16], (DEPTH, D_FF, D_MODEL), BETA * D_FF ** -0.5),
        "ln3_g": gain(ks[17], D_MODEL),
        "ln3_b": nrm(ks[18], (DEPTH, D_MODEL), 0.02),
    }


def _fwd_reference(x, ffn1_w_gu, ffn1_w_down, ln1_g, ln1_b, w_in, sc_conv_w, attn_sink, cc_conv_w,
              cc_conv_b, cc_ln_g, cc_ln_b, w_out, ln2_g, ln2_b, ffn2_w_gu, ffn2_w_down, ln3_g, ln3_b):
    for l in range(DEPTH):
        x = layer_norm(ALPHA * x + 0.5 * swiglu(x, ffn1_w_gu[l], ffn1_w_down[l]), ln1_g[l], ln1_b[l])
        x = layer_norm(ALPHA * x + hybrid_mixer(x, w_in[l], sc_conv_w[l], attn_sink[l], cc_conv_w[l],
                                                 cc_conv_b[l], cc_ln_g[l], cc_ln_b[l], w_out[l]),
                       ln2_g[l], ln2_b[l])
        x = layer_norm(ALPHA * x + 0.5 * swiglu(x, ffn2_w_gu[l], ffn2_w_down[l]), ln3_g[l], ln3_b[l])
    return x


import jax as _jax
import jax.numpy as _jnp

TWIN_FORMAT = 'train_step'
FWD_PARAMS = ['x', 'ffn1_w_gu', 'ffn1_w_down', 'ln1_g', 'ln1_b', 'w_in', 'sc_conv_w', 'attn_sink', 'cc_conv_w', 'cc_conv_b', 'cc_ln_g', 'cc_ln_b', 'w_out', 'ln2_g', 'ln2_b', 'ffn2_w_gu', 'ffn2_w_down', 'ln3_g', 'ln3_b']
TWIN_WEIGHTS = ['ffn1_w_gu', 'ffn1_w_down', 'ln1_g', 'ln1_b', 'w_in', 'sc_conv_w', 'attn_sink', 'cc_conv_w', 'cc_conv_b', 'cc_ln_g', 'cc_ln_b', 'w_out', 'ln2_g', 'ln2_b', 'ffn2_w_gu', 'ffn2_w_down', 'ln3_g', 'ln3_b']
TWIN_DIFF_INPUT = 'x'
TWIN_INPUTS = ['x', 'ffn1_w_gu', 'ffn1_w_down', 'ln1_g', 'ln1_b', 'w_in', 'sc_conv_w', 'attn_sink', 'cc_conv_w', 'cc_conv_b', 'cc_ln_g', 'cc_ln_b', 'w_out', 'ln2_g', 'ln2_b', 'ffn2_w_gu', 'ffn2_w_down', 'ln3_g', 'ln3_b', 'loss_target', 'm_ffn1_w_gu', 'm_ffn1_w_down', 'm_ln1_g', 'm_ln1_b', 'm_w_in', 'm_sc_conv_w', 'm_attn_sink', 'm_cc_conv_w', 'm_cc_conv_b', 'm_cc_ln_g', 'm_cc_ln_b', 'm_w_out', 'm_ln2_g', 'm_ln2_b', 'm_ffn2_w_gu', 'm_ffn2_w_down', 'm_ln3_g', 'm_ln3_b', 'v_ffn1_w_gu', 'v_ffn1_w_down', 'v_ln1_g', 'v_ln1_b', 'v_w_in', 'v_sc_conv_w', 'v_attn_sink', 'v_cc_conv_w', 'v_cc_conv_b', 'v_cc_ln_g', 'v_cc_ln_b', 'v_w_out', 'v_ln2_g', 'v_ln2_b', 'v_ffn2_w_gu', 'v_ffn2_w_down', 'v_ln3_g', 'v_ln3_b']
TWIN_OUTPUTS = ['loss', 'grad_x', 'grad_ffn1_w_gu', 'grad_ffn1_w_down', 'grad_ln1_g', 'grad_ln1_b', 'grad_w_in', 'grad_sc_conv_w', 'grad_attn_sink', 'grad_cc_conv_w', 'grad_cc_conv_b', 'grad_cc_ln_g', 'grad_cc_ln_b', 'grad_w_out', 'grad_ln2_g', 'grad_ln2_b', 'grad_ffn2_w_gu', 'grad_ffn2_w_down', 'grad_ln3_g', 'grad_ln3_b', 'delta_ffn1_w_gu', 'delta_ffn1_w_down', 'delta_ln1_g', 'delta_ln1_b', 'delta_w_in', 'delta_sc_conv_w', 'delta_attn_sink', 'delta_cc_conv_w', 'delta_cc_conv_b', 'delta_cc_ln_g', 'delta_cc_ln_b', 'delta_w_out', 'delta_ln2_g', 'delta_ln2_b', 'delta_ffn2_w_gu', 'delta_ffn2_w_down', 'delta_ln3_g', 'delta_ln3_b', 'new_m_ffn1_w_gu', 'new_m_ffn1_w_down', 'new_m_ln1_g', 'new_m_ln1_b', 'new_m_w_in', 'new_m_sc_conv_w', 'new_m_attn_sink', 'new_m_cc_conv_w', 'new_m_cc_conv_b', 'new_m_cc_ln_g', 'new_m_cc_ln_b', 'new_m_w_out', 'new_m_ln2_g', 'new_m_ln2_b', 'new_m_ffn2_w_gu', 'new_m_ffn2_w_down', 'new_m_ln3_g', 'new_m_ln3_b', 'new_v_ffn1_w_gu', 'new_v_ffn1_w_down', 'new_v_ln1_g', 'new_v_ln1_b', 'new_v_w_in', 'new_v_sc_conv_w', 'new_v_attn_sink', 'new_v_cc_conv_w', 'new_v_cc_conv_b', 'new_v_cc_ln_g', 'new_v_cc_ln_b', 'new_v_w_out', 'new_v_ln2_g', 'new_v_ln2_b', 'new_v_ffn2_w_gu', 'new_v_ffn2_w_down', 'new_v_ln3_g', 'new_v_ln3_b']
TWIN_LEAF_KINDS = {'loss': 'loss', 'grad_x': 'grad_x', 'grad_ffn1_w_gu': 'grad_w', 'grad_ffn1_w_down': 'grad_w', 'grad_ln1_g': 'grad_w', 'grad_ln1_b': 'grad_w', 'grad_w_in': 'grad_w', 'grad_sc_conv_w': 'grad_w', 'grad_attn_sink': 'grad_w', 'grad_cc_conv_w': 'grad_w', 'grad_cc_conv_b': 'grad_w', 'grad_cc_ln_g': 'grad_w', 'grad_cc_ln_b': 'grad_w', 'grad_w_out': 'grad_w', 'grad_ln2_g': 'grad_w', 'grad_ln2_b': 'grad_w', 'grad_ffn2_w_gu': 'grad_w', 'grad_ffn2_w_down': 'grad_w', 'grad_ln3_g': 'grad_w', 'grad_ln3_b': 'grad_w', 'delta_ffn1_w_gu': 'delta_w', 'delta_ffn1_w_down': 'delta_w', 'delta_ln1_g': 'delta_w', 'delta_ln1_b': 'delta_w', 'delta_w_in': 'delta_w', 'delta_sc_conv_w': 'delta_w', 'delta_attn_sink': 'delta_w', 'delta_cc_conv_w': 'delta_w', 'delta_cc_conv_b': 'delta_w', 'delta_cc_ln_g': 'delta_w', 'delta_cc_ln_b': 'delta_w', 'delta_w_out': 'delta_w', 'delta_ln2_g': 'delta_w', 'delta_ln2_b': 'delta_w', 'delta_ffn2_w_gu': 'delta_w', 'delta_ffn2_w_down': 'delta_w', 'delta_ln3_g': 'delta_w', 'delta_ln3_b': 'delta_w', 'new_m_ffn1_w_gu': 'new_m', 'new_m_ffn1_w_down': 'new_m', 'new_m_ln1_g': 'new_m', 'new_m_ln1_b': 'new_m', 'new_m_w_in': 'new_m', 'new_m_sc_conv_w': 'new_m', 'new_m_attn_sink': 'new_m', 'new_m_cc_conv_w': 'new_m', 'new_m_cc_conv_b': 'new_m', 'new_m_cc_ln_g': 'new_m', 'new_m_cc_ln_b': 'new_m', 'new_m_w_out': 'new_m', 'new_m_ln2_g': 'new_m', 'new_m_ln2_b': 'new_m', 'new_m_ffn2_w_gu': 'new_m', 'new_m_ffn2_w_down': 'new_m', 'new_m_ln3_g': 'new_m', 'new_m_ln3_b': 'new_m', 'new_v_ffn1_w_gu': 'new_v', 'new_v_ffn1_w_down': 'new_v', 'new_v_ln1_g': 'new_v', 'new_v_ln1_b': 'new_v', 'new_v_w_in': 'new_v', 'new_v_sc_conv_w': 'new_v', 'new_v_attn_sink': 'new_v', 'new_v_cc_conv_w': 'new_v', 'new_v_cc_conv_b': 'new_v', 'new_v_cc_ln_g': 'new_v', 'new_v_cc_ln_b': 'new_v', 'new_v_w_out': 'new_v', 'new_v_ln2_g': 'new_v', 'new_v_ln2_b': 'new_v', 'new_v_ffn2_w_gu': 'new_v', 'new_v_ffn2_w_down': 'new_v', 'new_v_ln3_g': 'new_v', 'new_v_ln3_b': 'new_v'}


def _forward(args):
    return _fwd_reference(*[args[k] for k in FWD_PARAMS])


def _output_shape():
    def fwd():
        inp = _fwd_setup_inputs(0)
        return _fwd_reference(*[inp[k] for k in FWD_PARAMS])
    out = _jax.eval_shape(fwd)
    return out.shape, out.dtype

N_MICROBATCH = 1
ADAM_LR = 0.001
ADAM_B1 = 0.9
ADAM_B2 = 0.999
ADAM_EPS = 1e-08
ADAM_WD = 0.01
ADAM_STEP = 10
PER_EXAMPLE_BATCH_AXIS = {'x': 0, 'loss_target': 0}
SHARED_INPUTS = []
_WEIGHT_DTYPES = {'ffn1_w_gu': _jnp.float32, 'ffn1_w_down': _jnp.float32, 'ln1_g': _jnp.float32, 'ln1_b': _jnp.float32, 'w_in': _jnp.float32, 'sc_conv_w': _jnp.float32, 'attn_sink': _jnp.float32, 'cc_conv_w': _jnp.float32, 'cc_conv_b': _jnp.float32, 'cc_ln_g': _jnp.float32, 'cc_ln_b': _jnp.float32, 'w_out': _jnp.float32, 'ln2_g': _jnp.float32, 'ln2_b': _jnp.float32, 'ffn2_w_gu': _jnp.float32, 'ffn2_w_down': _jnp.float32, 'ln3_g': _jnp.float32, 'ln3_b': _jnp.float32}
MOMENT_SCALE = {'ffn1_w_gu': 1.708237e-02, 'ffn1_w_down': 5.569689e-02, 'ln1_g': 2.083665e+00, 'ln1_b': 1.048765e+00, 'w_in': 5.987131e-02, 'sc_conv_w': 8.860983e-02, 'attn_sink': 2.800108e-04, 'cc_conv_w': 5.717667e-02, 'cc_conv_b': 3.259683e-01, 'cc_ln_g': 1.194727e-01, 'cc_ln_b': 1.768099e-01, 'w_out': 1.181644e-01, 'ln2_g': 2.196229e+00, 'ln2_b': 1.037936e+00, 'ffn2_w_gu': 1.651519e-02, 'ffn2_w_down': 5.404392e-02, 'ln3_g': 4.539912e+01, 'ln3_b': 3.186146e+00}


def _to_microbatches(a, axis):
    t = _jnp.moveaxis(a, axis, 0)
    t = t.reshape((N_MICROBATCH, t.shape[0] // N_MICROBATCH) + t.shape[1:])
    return _jnp.moveaxis(t, 1, axis + 1)


def setup_inputs(seed: int = 0) -> dict:
    inp = _fwd_setup_inputs(seed)
    key = _jax.random.fold_in(_jax.random.key(seed), 7919)
    shape, _ = _output_shape()
    out = dict(inp)
    out["loss_target"] = _jax.random.normal(_jax.random.fold_in(key, 0), shape, _jnp.float32)
    for i, name in enumerate(TWIN_WEIGHTS):
        w = inp[name].astype(_jnp.float32)
        if MOMENT_SCALE is None:
            s = _jnp.sqrt(_jnp.mean(_jnp.square(w)) + 1e-30)
        else:
            s = MOMENT_SCALE[name]
        km, kv = _jax.random.split(_jax.random.fold_in(key, i + 1))
        out[name] = w
        out["m_" + name] = s * _jax.random.normal(km, w.shape, _jnp.float32)
        out["v_" + name] = (s * s) * _jax.random.uniform(kv, w.shape, _jnp.float32, 0.5, 1.5)
    if N_MICROBATCH > 1:
        for name, axis in PER_EXAMPLE_BATCH_AXIS.items():
            out[name] = _to_microbatches(out[name], axis)
    return {'x': out['x'], 'ffn1_w_gu': out['ffn1_w_gu'], 'ffn1_w_down': out['ffn1_w_down'], 'ln1_g': out['ln1_g'], 'ln1_b': out['ln1_b'], 'w_in': out['w_in'], 'sc_conv_w': out['sc_conv_w'], 'attn_sink': out['attn_sink'], 'cc_conv_w': out['cc_conv_w'], 'cc_conv_b': out['cc_conv_b'], 'cc_ln_g': out['cc_ln_g'], 'cc_ln_b': out['cc_ln_b'], 'w_out': out['w_out'], 'ln2_g': out['ln2_g'], 'ln2_b': out['ln2_b'], 'ffn2_w_gu': out['ffn2_w_gu'], 'ffn2_w_down': out['ffn2_w_down'], 'ln3_g': out['ln3_g'], 'ln3_b': out['ln3_b'], 'loss_target': out['loss_target'], 'm_ffn1_w_gu': out['m_ffn1_w_gu'], 'm_ffn1_w_down': out['m_ffn1_w_down'], 'm_ln1_g': out['m_ln1_g'], 'm_ln1_b': out['m_ln1_b'], 'm_w_in': out['m_w_in'], 'm_sc_conv_w': out['m_sc_conv_w'], 'm_attn_sink': out['m_attn_sink'], 'm_cc_conv_w': out['m_cc_conv_w'], 'm_cc_conv_b': out['m_cc_conv_b'], 'm_cc_ln_g': out['m_cc_ln_g'], 'm_cc_ln_b': out['m_cc_ln_b'], 'm_w_out': out['m_w_out'], 'm_ln2_g': out['m_ln2_g'], 'm_ln2_b': out['m_ln2_b'], 'm_ffn2_w_gu': out['m_ffn2_w_gu'], 'm_ffn2_w_down': out['m_ffn2_w_down'], 'm_ln3_g': out['m_ln3_g'], 'm_ln3_b': out['m_ln3_b'], 'v_ffn1_w_gu': out['v_ffn1_w_gu'], 'v_ffn1_w_down': out['v_ffn1_w_down'], 'v_ln1_g': out['v_ln1_g'], 'v_ln1_b': out['v_ln1_b'], 'v_w_in': out['v_w_in'], 'v_sc_conv_w': out['v_sc_conv_w'], 'v_attn_sink': out['v_attn_sink'], 'v_cc_conv_w': out['v_cc_conv_w'], 'v_cc_conv_b': out['v_cc_conv_b'], 'v_cc_ln_g': out['v_cc_ln_g'], 'v_cc_ln_b': out['v_cc_ln_b'], 'v_w_out': out['v_w_out'], 'v_ln2_g': out['v_ln2_g'], 'v_ln2_b': out['v_ln2_b'], 'v_ffn2_w_gu': out['v_ffn2_w_gu'], 'v_ffn2_w_down': out['v_ffn2_w_down'], 'v_ln3_g': out['v_ln3_g'], 'v_ln3_b': out['v_ln3_b']}


def _loss(weights, diff, rest, loss_target):
    with _jax.named_scope("forward"):
        args = {**rest, TWIN_DIFF_INPUT: diff, **{k: w.astype(_WEIGHT_DTYPES[k]) for k, w in weights.items()}}
        y = _forward(args)
    with _jax.named_scope("loss_head"):
        err = _jnp.square(y.astype(_jnp.float32) - loss_target)
        return 0.5 * _jnp.sum(_jnp.mean(err, axis=-1)) if err.ndim else 0.5 * err


def _adamw(w, g, m, v):
    m = ADAM_B1 * m + (1.0 - ADAM_B1) * g
    v = ADAM_B2 * v + (1.0 - ADAM_B2) * _jnp.square(g)
    m_hat = m / (1.0 - ADAM_B1 ** ADAM_STEP)
    v_hat = v / (1.0 - ADAM_B2 ** ADAM_STEP)
    delta = -ADAM_LR * (m_hat / (_jnp.sqrt(v_hat) + ADAM_EPS) + ADAM_WD * w)
    return delta, m, v


def reference(x, ffn1_w_gu, ffn1_w_down, ln1_g, ln1_b, w_in, sc_conv_w, attn_sink, cc_conv_w, cc_conv_b, cc_ln_g, cc_ln_b, w_out, ln2_g, ln2_b, ffn2_w_gu, ffn2_w_down, ln3_g, ln3_b, loss_target, m_ffn1_w_gu, m_ffn1_w_down, m_ln1_g, m_ln1_b, m_w_in, m_sc_conv_w, m_attn_sink, m_cc_conv_w, m_cc_conv_b, m_cc_ln_g, m_cc_ln_b, m_w_out, m_ln2_g, m_ln2_b, m_ffn2_w_gu, m_ffn2_w_down, m_ln3_g, m_ln3_b, v_ffn1_w_gu, v_ffn1_w_down, v_ln1_g, v_ln1_b, v_w_in, v_sc_conv_w, v_attn_sink, v_cc_conv_w, v_cc_conv_b, v_cc_ln_g, v_cc_ln_b, v_w_out, v_ln2_g, v_ln2_b, v_ffn2_w_gu, v_ffn2_w_down, v_ln3_g, v_ln3_b):
    given = dict(x=x, ffn1_w_gu=ffn1_w_gu, ffn1_w_down=ffn1_w_down, ln1_g=ln1_g, ln1_b=ln1_b, w_in=w_in, sc_conv_w=sc_conv_w, attn_sink=attn_sink, cc_conv_w=cc_conv_w, cc_conv_b=cc_conv_b, cc_ln_g=cc_ln_g, cc_ln_b=cc_ln_b, w_out=w_out, ln2_g=ln2_g, ln2_b=ln2_b, ffn2_w_gu=ffn2_w_gu, ffn2_w_down=ffn2_w_down, ln3_g=ln3_g, ln3_b=ln3_b, loss_target=loss_target, m_ffn1_w_gu=m_ffn1_w_gu, m_ffn1_w_down=m_ffn1_w_down, m_ln1_g=m_ln1_g, m_ln1_b=m_ln1_b, m_w_in=m_w_in, m_sc_conv_w=m_sc_conv_w, m_attn_sink=m_attn_sink, m_cc_conv_w=m_cc_conv_w, m_cc_conv_b=m_cc_conv_b, m_cc_ln_g=m_cc_ln_g, m_cc_ln_b=m_cc_ln_b, m_w_out=m_w_out, m_ln2_g=m_ln2_g, m_ln2_b=m_ln2_b, m_ffn2_w_gu=m_ffn2_w_gu, m_ffn2_w_down=m_ffn2_w_down, m_ln3_g=m_ln3_g, m_ln3_b=m_ln3_b, v_ffn1_w_gu=v_ffn1_w_gu, v_ffn1_w_down=v_ffn1_w_down, v_ln1_g=v_ln1_g, v_ln1_b=v_ln1_b, v_w_in=v_w_in, v_sc_conv_w=v_sc_conv_w, v_attn_sink=v_attn_sink, v_cc_conv_w=v_cc_conv_w, v_cc_conv_b=v_cc_conv_b, v_cc_ln_g=v_cc_ln_g, v_cc_ln_b=v_cc_ln_b, v_w_out=v_w_out, v_ln2_g=v_ln2_g, v_ln2_b=v_ln2_b, v_ffn2_w_gu=v_ffn2_w_gu, v_ffn2_w_down=v_ffn2_w_down, v_ln3_g=v_ln3_g, v_ln3_b=v_ln3_b)
    weights = {n: given[n] for n in TWIN_WEIGHTS}
    shared = {n: given[n] for n in SHARED_INPUTS}
    per_example = {n: given[n] for n in ['x']}
    grad_fn = _jax.value_and_grad(_loss, argnums=(0, 1))

    def one_microbatch(ex, loss_target):
        ex = dict(ex)
        diff = ex.pop(TWIN_DIFF_INPUT)
        return grad_fn(weights, diff, {**shared, **ex}, loss_target)

    if N_MICROBATCH == 1:
        loss, (grad_w, grad_x) = one_microbatch(per_example, given["loss_target"])
    else:
        def body(carry, xs):
            loss_sum, grad_sum = carry
            l_k, (gw_k, gx_k) = one_microbatch(xs[0], xs[1])
            with _jax.named_scope("update"):
                return (loss_sum + l_k, _jax.tree.map(_jnp.add, grad_sum, gw_k)), gx_k

        init = (_jnp.zeros((), _jnp.float32), _jax.tree.map(_jnp.zeros_like, weights))
        (loss, grad_w), grad_x = _jax.lax.scan(body, init, (per_example, given["loss_target"]))
    with _jax.named_scope("update"):
        delta_w, new_m, new_v = {}, {}, {}
        for n in TWIN_WEIGHTS:
            delta_w[n], new_m[n], new_v[n] = _adamw(weights[n], grad_w[n], given["m_" + n], given["v_" + n])
    return (loss, grad_x, *[grad_w[n] for n in TWIN_WEIGHTS], *[delta_w[n] for n in TWIN_WEIGHTS],
            *[new_m[n] for n in TWIN_WEIGHTS], *[new_v[n] for n in TWIN_WEIGHTS])
```

```python
import functools

import jax
import jax.numpy as jnp
import numpy as np
from jax import lax
from jax.experimental import pallas as pl
from jax.experimental.pallas import tpu as pltpu

F32 = jnp.float32
BF16 = jnp.bfloat16
S = jax.ShapeDtypeStruct

N_DEV = 8
DEPTH = 2
D = 1024
F = 2816
D_IN = 2048
HEAD_DIM = 64
N_Q_HEADS = 8
N_KV_HEADS = 2
GROUP = 4
D_SC = 256
D_ATT = 512
D_CC = 256
CC_W = 31
SC_W = 3
BLOCK = 128
ROPE_THETA = 10000.0
LN_EPS = 1e-5
ALPHA = (2.0 * DEPTH) ** 0.25
ADAM_LR = 0.001
ADAM_B1 = 0.9
ADAM_B2 = 0.999
ADAM_EPS = 1e-08
ADAM_WD = 0.01
ADAM_STEP = 10

O_SCB, O_SCC, O_SCH, O_Q, O_K, O_V, O_CCA, O_CCG = 0, 256, 512, 768, 1280, 1408, 1536, 1792

V7X_VMEM_BYTES = 64 * 1024 * 1024
VMEM_LIMIT = V7X_VMEM_BYTES - 8 * 1024 * 1024
TOKEN_TILE = 256
MIX_BWD_TILE = 128
HALO_FWD = 16
HALO_BWD = 32

NT = (((1,), (1,)), ((), ()))
TN = (((0,), (0,)), ((), ()))
MESH = pl.DeviceIdType.MESH


def _params(sem=None):
    return pltpu.CompilerParams(dimension_semantics=sem, vmem_limit_bytes=VMEM_LIMIT)


def _sigmoid(v):
    return 1.0 / (1.0 + jnp.exp(-v))


def _ln_stats(r):
    mu = jnp.mean(r, axis=-1, keepdims=True)
    d = r - mu
    var = jnp.mean(d * d, axis=-1, keepdims=True)
    rstd = lax.rsqrt(var + LN_EPS)
    return d * rstd, rstd


def _ln_bwd(dn, xhat, rstd, gam):
    dxh = dn * gam
    return rstd * (dxh - jnp.mean(dxh, axis=-1, keepdims=True) - xhat * jnp.mean(dxh * xhat, axis=-1, keepdims=True))


def _swap_halves(v):
    n = v.shape[-1]
    lane = lax.broadcasted_iota(jnp.int32, v.shape, v.ndim - 1) % HEAD_DIM
    return jnp.where(lane < HEAD_DIM // 2, pltpu.roll(v, n - HEAD_DIM // 2, v.ndim - 1), pltpu.roll(v, HEAD_DIM // 2, v.ndim - 1))


def _wide(tab, n):
    return tab if n == 128 else jnp.concatenate([tab] * (n // 128), axis=1)


def _me():
    x, y, c = lax.axis_index("x"), lax.axis_index("y"), lax.axis_index("c")
    return x, y, c


def _peer(rel):
    x, y, c = _me()
    px = 1 - x if rel & 4 else x
    py = 1 - y if rel & 2 else y
    pc = 1 - c if rel & 1 else c
    return (px, py, pc), 4 * px + 2 * py + pc


def _exchange(srcs, dsts_shape, dst_index, src_of, dst_of, name):
    n = len(srcs)

    def body(*refs):
        ins = refs[:n]
        outs = [refs[n + dst_index[k]] for k in range(n)]
        send, recv, lsem = refs[n + len(dsts_shape):]
        x, y, c = _me()
        me = 4 * x + 2 * y + c
        local = [pltpu.make_async_copy(src_of(ins[k], k, me), dst_of(outs[k], k, me), lsem.at[k]) for k in range(n)]
        for cp in local:
            cp.start()
        sends, recvs = [], []
        for k in range(n):
            for rel in range(1, N_DEV):
                peer, pidx = _peer(rel)
                sends.append(pltpu.make_async_remote_copy(
                    src_ref=src_of(ins[k], k, pidx), dst_ref=dst_of(outs[k], k, me),
                    send_sem=send.at[k, rel - 1], recv_sem=recv.at[k, rel - 1], device_id=peer, device_id_type=MESH))
                recvs.append(pltpu.make_async_remote_copy(
                    src_ref=src_of(ins[k], k, pidx), dst_ref=dst_of(outs[k], k, pidx),
                    send_sem=send.at[k, rel - 1], recv_sem=recv.at[k, rel - 1], device_id=peer, device_id_type=MESH))
        for cp in sends:
            cp.start()
        for cp in recvs:
            cp.wait_recv()
        for cp in sends:
            cp.wait_send()
        for cp in local:
            cp.wait()

    hbm = pl.BlockSpec(memory_space=pltpu.HBM)
    return pl.pallas_call(
        body, name=name, in_specs=[hbm] * n, out_specs=[hbm] * len(dsts_shape), out_shape=dsts_shape,
        scratch_shapes=[pltpu.SemaphoreType.DMA((n, N_DEV - 1)), pltpu.SemaphoreType.DMA((n, N_DEV - 1)),
                        pltpu.SemaphoreType.DMA((n,))],
    )(*srcs)


def _all_gather(blocks, name):
    shapes = [S((N_DEV,) + b.shape, b.dtype) for b in blocks]
    return _exchange(blocks, shapes, list(range(len(blocks))), lambda ref, k, idx: ref, lambda ref, k, idx: ref.at[idx], name)


def _exchange_grads(big, small_pack):
    srcs, shapes, dst_index, layer = [], [], [], []
    for per_layer in big:
        rows = per_layer[0].shape[0] // N_DEV
        for l, g in enumerate(per_layer):
            srcs.append(g.reshape(N_DEV, rows, g.shape[1]))
            dst_index.append(len(shapes))
            layer.append(l)
        shapes.append(S((N_DEV, len(per_layer), rows, per_layer[0].shape[1]), per_layer[0].dtype))
    n_big = len(srcs)
    srcs.append(small_pack)
    dst_index.append(len(shapes))
    shapes.append(S((N_DEV,) + small_pack.shape, small_pack.dtype))
    return _exchange(
        srcs, shapes, dst_index,
        lambda ref, k, idx: ref.at[idx] if k < n_big else ref,
        lambda ref, k, idx: ref.at[idx, layer[k]] if k < n_big else ref.at[idx], "exchange_grads")


def _ffn_up(xb, wgut):
    t = xb.shape[0]
    tm = min(TOKEN_TILE, t)
    half = F // 2

    def body(x_ref, w_ref, gu_ref, a_ref):
        x = x_ref[...]
        for ch in range(2):
            lo = ch * half
            g = lax.dot_general(x, w_ref[lo:lo + half, :], NT, preferred_element_type=F32)
            u = lax.dot_general(x, w_ref[F + lo:F + lo + half, :], NT, preferred_element_type=F32)
            gu_ref[:, lo:lo + half] = g.astype(BF16)
            gu_ref[:, F + lo:F + lo + half] = u.astype(BF16)
            a_ref[:, lo:lo + half] = (g * _sigmoid(g) * u).astype(BF16)

    return pl.pallas_call(
        body, name="ffn_up", grid=(t // tm,),
        in_specs=[pl.BlockSpec((tm, D), lambda i: (i, 0)), pl.BlockSpec((2 * F, D), lambda i: (0, 0))],
        out_specs=[pl.BlockSpec((tm, 2 * F), lambda i: (i, 0)), pl.BlockSpec((tm, F), lambda i: (i, 0))],
        out_shape=[S((t, 2 * F), BF16), S((t, F), BF16)], compiler_params=_params(("parallel",)),
    )(xb, wgut)


def _residual_ln_out(x_ref, f, scale, g_ref, b_ref, r_ref, y_ref, yb_ref):
    r = ALPHA * x_ref[...] + scale * f
    xhat, _ = _ln_stats(r)
    y = xhat * g_ref[...] + b_ref[...]
    r_ref[...] = r
    y_ref[...] = y
    yb_ref[...] = y.astype(BF16)


def _ffn_down_ln(a, wd, x, gam, bet):
    t = x.shape[0]
    tm = min(TOKEN_TILE, t)

    def body(a_ref, w_ref, x_ref, g_ref, b_ref, r_ref, y_ref, yb_ref):
        f = jnp.dot(a_ref[...], w_ref[...], preferred_element_type=F32)
        _residual_ln_out(x_ref, f, 0.5, g_ref, b_ref, r_ref, y_ref, yb_ref)

    row = pl.BlockSpec((tm, D), lambda i: (i, 0))
    vec = pl.BlockSpec((1, D), lambda i: (0, 0))
    return pl.pallas_call(
        body, name="ffn_down_ln", grid=(t // tm,),
        in_specs=[pl.BlockSpec((tm, F), lambda i: (i, 0)), pl.BlockSpec((F, D), lambda i: (0, 0)), row, vec, vec],
        out_specs=[row, row, row], out_shape=[S((t, D), F32), S((t, D), F32), S((t, D), BF16)],
        compiler_params=_params(("parallel",)),
    )(a, wd, x, gam, bet)


def _proj_in(xb, wint):
    t = xb.shape[0]
    tm = min(TOKEN_TILE, t)

    def body(x_ref, w_ref, z_ref):
        z_ref[...] = lax.dot_general(x_ref[...], w_ref[...], NT, preferred_element_type=F32)

    return pl.pallas_call(
        body, name="proj_in", grid=(t // tm,),
        in_specs=[pl.BlockSpec((tm, D), lambda i: (i, 0)), pl.BlockSpec((D_IN, D), lambda i: (0, 0))],
        out_specs=pl.BlockSpec((tm, D_IN), lambda i: (i, 0)), out_shape=S((t, D_IN), F32),
        compiler_params=_params(("parallel",)),
    )(xb, wint)


def _halo_specs(t, tm, halo, width):
    per = tm // halo
    last = t // halo - 1
    return [pl.BlockSpec((tm, width), lambda i: (i, 0)),
            pl.BlockSpec((halo, width), lambda i: (jnp.maximum(i * per - 1, 0), 0)),
            pl.BlockSpec((halo, width), lambda i: (jnp.minimum((i + 1) * per, last), 0))]


def _mix_fwd(z, scw, ccw, ccb, ccg, ccbb, cos, sin):
    t = z.shape[0]
    tm = min(TOKEN_TILE, t)
    nt = t // tm
    h = HALO_FWD
    rc = 64

    def body(z_ref, zp_ref, zn_ref, scw_ref, ccw_ref, ccb_ref, ccg_ref, ccbb_ref, cos_ref, sin_ref,
             ysc_ref, ycc_ref, q_ref, k_ref, v_ref, u_s, ch_s):
        i = pl.program_id(0)
        pz = jnp.where(i == 0, 0.0, zp_ref[...])
        nz = jnp.where(i == nt - 1, 0.0, zn_ref[...])

        def u_of(zz):
            return zz[:, O_CCA:O_CCA + D_CC] * _sigmoid(zz[:, O_CCG:O_CCG + D_CC])

        def ch_of(zz):
            return zz[:, O_SCC:O_SCC + D_SC] * zz[:, O_SCH:O_SCH + D_SC]

        u_s[0:h, :] = u_of(pz)
        u_s[h:h + tm, :] = z_ref[:, O_CCA:O_CCA + D_CC] * _sigmoid(z_ref[:, O_CCG:O_CCG + D_CC])
        u_s[h + tm:2 * h + tm, :] = u_of(nz)
        ch_s[0:h, :] = ch_of(pz)
        ch_s[h:h + tm, :] = z_ref[:, O_SCC:O_SCC + D_SC] * z_ref[:, O_SCH:O_SCH + D_SC]
        ch_s[h + tm:2 * h + tm, :] = ch_of(nz)
        for r0 in range(0, tm, rc):
            acc = jnp.zeros((rc, D_CC), F32)
            for j in range(CC_W):
                acc = acc + ccw_ref[j:j + 1, :] * u_s[pl.ds(r0 + h + j - CC_W // 2, rc), :]
            xhat, _ = _ln_stats(acc + ccb_ref[...])
            n = xhat * ccg_ref[...] + ccbb_ref[...]
            ycc_ref[r0:r0 + rc, :] = (n * _sigmoid(n)).astype(BF16)
            acc = jnp.zeros((rc, D_SC), F32)
            for j in range(SC_W):
                acc = acc + scw_ref[j:j + 1, :] * ch_s[pl.ds(r0 + h + j - SC_W // 2, rc), :]
            ysc_ref[r0:r0 + rc, :] = (z_ref[r0:r0 + rc, O_SCB:O_SCB + D_SC] * acc).astype(BF16)
        q = z_ref[:, O_Q:O_Q + D_ATT]
        q_ref[...] = (q * _wide(cos_ref[...], D_ATT) + _swap_halves(q) * _wide(sin_ref[...], D_ATT)).astype(BF16)
        k = z_ref[:, O_K:O_K + 128]
        k_ref[...] = (k * cos_ref[...] + _swap_halves(k) * sin_ref[...]).astype(BF16)
        v_ref[...] = z_ref[:, O_V:O_V + 128].astype(BF16)

    def full(a):
        return pl.BlockSpec(a.shape, lambda i: (0, 0))

    def rows(w):
        return pl.BlockSpec((tm, w), lambda i: (i, 0))

    return pl.pallas_call(
        body, name="mix_fwd", grid=(nt,),
        in_specs=_halo_specs(t, tm, h, D_IN) + [full(scw), full(ccw), full(ccb), full(ccg), full(ccbb), rows(128), rows(128)],
        out_specs=[rows(D_SC), rows(D_CC), rows(D_ATT), rows(128), rows(128)],
        out_shape=[S((t, D_SC), BF16), S((t, D_CC), BF16), S((t, D_ATT), BF16), S((t, 128), BF16), S((t, 128), BF16)],
        scratch_shapes=[pltpu.VMEM((tm + 2 * h, D_CC), F32), pltpu.VMEM((tm + 2 * h, D_SC), F32)],
        compiler_params=_params(("parallel",)),
    )(z, z, z, scw, ccw, ccb, ccg, ccbb, cos, sin)


def _band_specs(nb, width):
    return [pl.BlockSpec((BLOCK, width), lambda n: (jnp.maximum(n - 1, 0), 0)),
            pl.BlockSpec((BLOCK, width), lambda n: (n, 0)),
            pl.BlockSpec((BLOCK, width), lambda n: (jnp.minimum(n + 1, nb - 1), 0))]


def _band_valid(n, nb):
    rows = GROUP * BLOCK
    qpos = lax.broadcasted_iota(jnp.int32, (rows, 3 * BLOCK), 0) % BLOCK
    col = lax.broadcasted_iota(jnp.int32, (rows, 3 * BLOCK), 1)
    ok = jnp.abs(qpos - (col - BLOCK)) <= BLOCK
    ok = jnp.logical_and(ok, jnp.logical_or(col >= BLOCK, n > 0))
    return jnp.logical_and(ok, jnp.logical_or(col < 2 * BLOCK, n < nb - 1))


def _stack_heads(ref, kvh):
    return jnp.concatenate([ref[:, (GROUP * kvh + g) * HEAD_DIM:(GROUP * kvh + g + 1) * HEAD_DIM] for g in range(GROUP)], axis=0)


def _band_cat(refs, kvh):
    return jnp.concatenate([r[:, kvh * HEAD_DIM:(kvh + 1) * HEAD_DIM] for r in refs], axis=0)


def _sink_col(sink_ref, kvh):
    return jnp.concatenate([jnp.full((BLOCK, 1), sink_ref[GROUP * kvh + g], F32) for g in range(GROUP)], axis=0)


def _attn_probs(q_ref, k_refs, sink_ref, kvh, valid):
    qs = _stack_heads(q_ref, kvh)
    kc = _band_cat(k_refs, kvh)
    s = lax.dot_general(qs, kc, NT, preferred_element_type=F32) * (HEAD_DIM ** -0.5)
    s = jnp.where(valid, s, -1e30)
    sk = _sink_col(sink_ref, kvh)
    m = jnp.maximum(jnp.max(s, axis=-1, keepdims=True), sk)
    p = jnp.exp(s - m)
    ps = jnp.exp(sk - m)
    denom = jnp.sum(p, axis=-1, keepdims=True) + ps
    return qs, kc, p, ps, denom


def _attn_fwd(qr, kr, vv, sink):
    t = qr.shape[0]
    nb = t // BLOCK

    def body(q_ref, kp_ref, ko_ref, kn_ref, vp_ref, vo_ref, vn_ref, sink_ref, o_ref):
        n = pl.program_id(0)
        valid = _band_valid(n, nb)
        for kvh in range(N_KV_HEADS):
            _, _, p, _, denom = _attn_probs(q_ref, (kp_ref, ko_ref, kn_ref), sink_ref, kvh, valid)
            vc = _band_cat((vp_ref, vo_ref, vn_ref), kvh)
            o = jnp.dot(p.astype(BF16), vc, preferred_element_type=F32) / denom
            for g in range(GROUP):
                hh = GROUP * kvh + g
                o_ref[:, hh * HEAD_DIM:(hh + 1) * HEAD_DIM] = o[g * BLOCK:(g + 1) * BLOCK, :].astype(BF16)

    qspec = pl.BlockSpec((BLOCK, D_ATT), lambda n: (n, 0))
    return pl.pallas_call(
        body, name="attn_fwd", grid=(nb,),
        in_specs=[qspec] + _band_specs(nb, 128) + _band_specs(nb, 128) + [pl.BlockSpec(memory_space=pltpu.SMEM)],
        out_specs=qspec, out_shape=S((t, D_ATT), BF16), compiler_params=_params(("parallel",)),
    )(qr, kr, kr, kr, vv, vv, vv, sink)


def _out_ln(ysc, yatt, ycc, wout, x, gam, bet):
    t = x.shape[0]
    tm = min(TOKEN_TILE, t)

    def body(sc_ref, at_ref, cc_ref, w_ref, x_ref, g_ref, b_ref, cat_ref, r_ref, y_ref, yb_ref):
        cat = jnp.concatenate([sc_ref[...], at_ref[...], cc_ref[...]], axis=1)
        cat_ref[...] = cat
        f = jnp.dot(cat, w_ref[...], preferred_element_type=F32)
        _residual_ln_out(x_ref, f, 1.0, g_ref, b_ref, r_ref, y_ref, yb_ref)

    def rows(w):
        return pl.BlockSpec((tm, w), lambda i: (i, 0))

    vec = pl.BlockSpec((1, D), lambda i: (0, 0))
    return pl.pallas_call(
        body, name="out_ln", grid=(t // tm,),
        in_specs=[rows(D_SC), rows(D_ATT), rows(D_CC), pl.BlockSpec((D, D), lambda i: (0, 0)), rows(D), vec, vec],
        out_specs=[rows(D), rows(D), rows(D), rows(D)],
        out_shape=[S((t, D), BF16), S((t, D), F32), S((t, D), F32), S((t, D), BF16)],
        compiler_params=_params(("parallel",)),
    )(ysc, yatt, ycc, wout, x, gam, bet)


def _loss_head(y, target):
    t = y.shape[0]
    tm = min(TOKEN_TILE, t)

    def body(y_ref, t_ref, dy_ref, part_ref):
        e = y_ref[...] - t_ref[...]
        dy_ref[...] = e / D

        @pl.when(pl.program_id(0) == 0)
        def _():
            part_ref[...] = jnp.zeros_like(part_ref)

        part_ref[...] += jnp.sum(e * e, axis=0, keepdims=True)

    row = pl.BlockSpec((tm, D), lambda i: (i, 0))
    return pl.pallas_call(
        body, name="loss_head", grid=(t // tm,), in_specs=[row, row],
        out_specs=[row, pl.BlockSpec((1, D), lambda i: (0, 0))], out_shape=[S((t, D), F32), S((1, D), F32)],
        compiler_params=_params(("arbitrary",)),
    )(y, target)


def _ln_bwd_block(dy_ref, r_ref, g_ref, dgam_ref, dbet_ref):
    xhat, rstd = _ln_stats(r_ref[...])
    dy = dy_ref[...]

    @pl.when(pl.program_id(0) == 0)
    def _():
        dgam_ref[...] = jnp.zeros_like(dgam_ref)
        dbet_ref[...] = jnp.zeros_like(dbet_ref)

    dgam_ref[...] += jnp.sum(dy * xhat, axis=0, keepdims=True)
    dbet_ref[...] += jnp.sum(dy, axis=0, keepdims=True)
    return _ln_bwd(dy, xhat, rstd, g_ref[...])


def _ffn_bwd(dy, r, gam, wd, gu):
    t = dy.shape[0]
    tm = min(TOKEN_TILE, t)
    half = F // 2

    def body(dy_ref, r_ref, g_ref, w_ref, gu_ref, dr_ref, df_ref, dh_ref, dgam_ref, dbet_ref):
        dr = _ln_bwd_block(dy_ref, r_ref, g_ref, dgam_ref, dbet_ref)
        dr_ref[...] = dr
        dfb = (0.5 * dr).astype(BF16)
        df_ref[...] = dfb
        for ch in range(2):
            lo = ch * half
            da = lax.dot_general(dfb, w_ref[lo:lo + half, :], NT, preferred_element_type=F32)
            g = gu_ref[:, lo:lo + half].astype(F32)
            u = gu_ref[:, F + lo:F + lo + half].astype(F32)
            sg = _sigmoid(g)
            dh_ref[:, lo:lo + half] = (da * u * (sg * (1.0 + g * (1.0 - sg)))).astype(BF16)
            dh_ref[:, F + lo:F + lo + half] = (da * (g * sg)).astype(BF16)

    row = pl.BlockSpec((tm, D), lambda i: (i, 0))
    vec = pl.BlockSpec((1, D), lambda i: (0, 0))
    wide = pl.BlockSpec((tm, 2 * F), lambda i: (i, 0))
    return pl.pallas_call(
        body, name="ffn_bwd", grid=(t // tm,),
        in_specs=[row, row, vec, pl.BlockSpec((F, D), lambda i: (0, 0)), wide],
        out_specs=[row, row, wide, vec, vec],
        out_shape=[S((t, D), F32), S((t, D), BF16), S((t, 2 * F), BF16), S((1, D), F32), S((1, D), F32)],
        compiler_params=_params(("arbitrary",)),
    )(dy, r, gam, wd, gu)


def _dx(dr, dh, w):
    t = dr.shape[0]
    tm = min(TOKEN_TILE, t)
    kk = dh.shape[1]

    def body(dr_ref, dh_ref, w_ref, o_ref):
        o_ref[...] = ALPHA * dr_ref[...] + jnp.dot(dh_ref[...], w_ref[...], preferred_element_type=F32)

    row = pl.BlockSpec((tm, D), lambda i: (i, 0))
    return pl.pallas_call(
        body, name="dx", grid=(t // tm,),
        in_specs=[row, pl.BlockSpec((tm, kk), lambda i: (i, 0)), pl.BlockSpec((kk, D), lambda i: (0, 0))],
        out_specs=row, out_shape=S((t, D), F32), compiler_params=_params(("parallel",)),
    )(dr, dh, w)


def _wgrad(a, b, ta):
    t, ka = a.shape
    tk = min(1024, t)
    nk = t // tk

    def body(a_ref, b_ref, o_ref, acc):
        k = pl.program_id(1)

        @pl.when(k == 0)
        def _():
            acc[...] = jnp.zeros_like(acc)

        acc[...] += lax.dot_general(a_ref[...], b_ref[...], TN, preferred_element_type=F32)

        @pl.when(k == nk - 1)
        def _():
            o_ref[...] = acc[...].astype(BF16)

    return pl.pallas_call(
        body, name="wgrad", grid=(ka // ta, nk),
        in_specs=[pl.BlockSpec((tk, ta), lambda i, k: (k, i)), pl.BlockSpec((tk, D), lambda i, k: (k, 0))],
        out_specs=pl.BlockSpec((ta, D), lambda i, k: (i, 0)), out_shape=S((ka, D), BF16),
        scratch_shapes=[pltpu.VMEM((ta, D), F32)], compiler_params=_params(("parallel", "arbitrary")),
    )(a, b)


def _out_bwd(dy, r, gam, wout):
    t = dy.shape[0]
    tm = min(TOKEN_TILE, t)

    def body(dy_ref, r_ref, g_ref, w_ref, dr_ref, dm_ref, dsc_ref, dat_ref, dcc_ref, dgam_ref, dbet_ref):
        dr = _ln_bwd_block(dy_ref, r_ref, g_ref, dgam_ref, dbet_ref)
        dr_ref[...] = dr
        dmb = dr.astype(BF16)
        dm_ref[...] = dmb
        dcat = lax.dot_general(dmb, w_ref[...], NT, preferred_element_type=F32)
        dsc_ref[...] = dcat[:, 0:D_SC]
        dat_ref[...] = dcat[:, D_SC:D_SC + D_ATT]
        dcc_ref[...] = dcat[:, D_SC + D_ATT:D]

    def rows(w):
        return pl.BlockSpec((tm, w), lambda i: (i, 0))

    vec = pl.BlockSpec((1, D), lambda i: (0, 0))
    return pl.pallas_call(
        body, name="out_bwd", grid=(t // tm,),
        in_specs=[rows(D), rows(D), vec, pl.BlockSpec((D, D), lambda i: (0, 0))],
        out_specs=[rows(D), rows(D), rows(D_SC), rows(D_ATT), rows(D_CC), vec, vec],
        out_shape=[S((t, D), F32), S((t, D), BF16), S((t, D_SC), F32), S((t, D_ATT), F32), S((t, D_CC), F32),
                   S((1, D), F32), S((1, D), F32)],
        compiler_params=_params(("arbitrary",)),
    )(dy, r, gam, wout)


def _attn_bwd(qr, kr, vv, sink, do, yatt):
    t = qr.shape[0]
    nb = t // BLOCK

    def body(q_ref, kp_ref, ko_ref, kn_ref, vp_ref, vo_ref, vn_ref, sink_ref, do_ref, o_ref,
             dq_ref, dk_ref, dv_ref, dsink_ref):
        n = pl.program_id(0)
        valid = _band_valid(n, nb)

        @pl.when(n == 0)
        def _():
            dsink_ref[...] = jnp.zeros_like(dsink_ref)

        for kvh in range(N_KV_HEADS):
            qs, kc, p, ps, denom = _attn_probs(q_ref, (kp_ref, ko_ref, kn_ref), sink_ref, kvh, valid)
            vc = _band_cat((vp_ref, vo_ref, vn_ref), kvh)
            dos = _stack_heads(do_ref, kvh)
            os_ = _stack_heads(o_ref, kvh).astype(F32)
            dd = jnp.sum(dos * os_, axis=-1, keepdims=True) / denom
            dou = (dos / denom).astype(BF16)
            dp = lax.dot_general(dou, vc, NT, preferred_element_type=F32)
            ds = (p * (dp - dd) * (HEAD_DIM ** -0.5)).astype(BF16)
            dq = jnp.dot(ds, kc, preferred_element_type=F32)
            dk = lax.dot_general(ds, qs, TN, preferred_element_type=F32)
            dv = lax.dot_general(p.astype(BF16), dou, TN, preferred_element_type=F32)
            dsk = ps * dd
            for g in range(GROUP):
                hh = GROUP * kvh + g
                dq_ref[:, hh * HEAD_DIM:(hh + 1) * HEAD_DIM] = dq[g * BLOCK:(g + 1) * BLOCK, :]
                dsink_ref[hh:hh + 1, :] += jnp.zeros((1, 128), F32) - jnp.sum(dsk[g * BLOCK:(g + 1) * BLOCK, :])
            for j in range(3):
                dk_ref[j, :, kvh * HEAD_DIM:(kvh + 1) * HEAD_DIM] = dk[j * BLOCK:(j + 1) * BLOCK, :]
                dv_ref[j, :, kvh * HEAD_DIM:(kvh + 1) * HEAD_DIM] = dv[j * BLOCK:(j + 1) * BLOCK, :]

    qspec = pl.BlockSpec((BLOCK, D_ATT), lambda n: (n, 0))
    part = pl.BlockSpec((3, BLOCK, 128), lambda n: (0, n, 0))
    return pl.pallas_call(
        body, name="attn_bwd", grid=(nb,),
        in_specs=[qspec] + _band_specs(nb, 128) + _band_specs(nb, 128) + [pl.BlockSpec(memory_space=pltpu.SMEM), qspec, qspec],
        out_specs=[qspec, part, part, pl.BlockSpec((N_Q_HEADS, 128), lambda n: (0, 0))],
        out_shape=[S((t, D_ATT), F32), S((3, t, 128), F32), S((3, t, 128), F32), S((N_Q_HEADS, 128), F32)],
        compiler_params=_params(("arbitrary",)),
    )(qr, kr, kr, kr, vv, vv, vv, sink, do, yatt)


def _mix_bwd(z, dysc, dycc, dqr, dkp, dvp, scw, ccw, ccb, ccg, ccbb, cos, sin):
    t = z.shape[0]
    tm = min(MIX_BWD_TILE, t)
    nt = t // tm
    h = HALO_BWD
    hh = h // 2
    half = CC_W // 2
    ext = tm + 2 * h
    mid = tm + 2 * hh

    def body(z_ref, zp_ref, zn_ref, dsc_ref, dscp_ref, dscn_ref, dcc_ref, dccp_ref, dccn_ref, dq_ref,
             dk0_ref, dk1_ref, dk2_ref, dv0_ref, dv1_ref, dv2_ref,
             scw_ref, ccw_ref, ccb_ref, ccg_ref, ccbb_ref, cos_ref, sin_ref,
             dz_ref, dscw_ref, dccw_ref, dvec_ref, u_s, dc_s, ch_s, g_s):
        i = pl.program_id(0)
        first, last = i == 0, i == nt - 1

        @pl.when(first)
        def _():
            dscw_ref[...] = jnp.zeros_like(dscw_ref)
            dccw_ref[...] = jnp.zeros_like(dccw_ref)
            dvec_ref[...] = jnp.zeros_like(dvec_ref)

        pz = jnp.where(first, 0.0, zp_ref[...])
        nz = jnp.where(last, 0.0, zn_ref[...])
        zo = z_ref[...]

        def u_of(zz):
            return zz[:, O_CCA:O_CCA + D_CC] * _sigmoid(zz[:, O_CCG:O_CCG + D_CC])

        u_s[0:h, :] = u_of(pz)
        u_s[h:h + tm, :] = u_of(zo)
        u_s[h + tm:ext, :] = u_of(nz)
        acc = jnp.zeros((mid, D_CC), F32)
        for j in range(CC_W):
            acc = acc + ccw_ref[j:j + 1, :] * u_s[pl.ds(hh + j - half, mid), :]
        xhat, rstd = _ln_stats(acc + ccb_ref[...])
        nn = xhat * ccg_ref[...] + ccbb_ref[...]
        sg = _sigmoid(nn)
        dycc_mid = jnp.concatenate([jnp.where(first, 0.0, dccp_ref[hh:h, :]), dcc_ref[...],
                                    jnp.where(last, 0.0, dccn_ref[0:hh, :])], axis=0)
        dn = dycc_mid * (sg * (1.0 + nn * (1.0 - sg)))
        dc = _ln_bwd(dn, xhat, rstd, ccg_ref[...])
        dc_s[...] = dc
        dn_own = dn[hh:hh + tm, :]
        dvec_ref[0:1, :] += jnp.sum(dc[hh:hh + tm, :], axis=0, keepdims=True)
        dvec_ref[1:2, :] += jnp.sum(dn_own * xhat[hh:hh + tm, :], axis=0, keepdims=True)
        dvec_ref[2:3, :] += jnp.sum(dn_own, axis=0, keepdims=True)
        du = jnp.zeros((tm, D_CC), F32)
        dc_own = dc[hh:hh + tm, :]
        for j in range(CC_W):
            du = du + ccw_ref[j:j + 1, :] * dc_s[pl.ds(hh + half - j, tm), :]
            dccw_ref[j:j + 1, :] += jnp.sum(dc_own * u_s[pl.ds(h + j - half, tm), :], axis=0, keepdims=True)
        gate = _sigmoid(zo[:, O_CCG:O_CCG + D_CC])
        a_own = zo[:, O_CCA:O_CCA + D_CC]
        dz_ref[:, O_CCA:O_CCA + D_CC] = (du * gate).astype(BF16)
        dz_ref[:, O_CCG:O_CCG + D_CC] = (du * a_own * gate * (1.0 - gate)).astype(BF16)

        def ch_of(zz):
            return zz[:, O_SCC:O_SCC + D_SC] * zz[:, O_SCH:O_SCH + D_SC]

        ch_s[0:h, :] = ch_of(pz)
        ch_s[h:h + tm, :] = ch_of(zo)
        ch_s[h + tm:ext, :] = ch_of(nz)
        g_s[0:h, :] = jnp.where(first, 0.0, dscp_ref[...]) * pz[:, O_SCB:O_SCB + D_SC]
        g_s[h:h + tm, :] = dsc_ref[...] * zo[:, O_SCB:O_SCB + D_SC]
        g_s[h + tm:ext, :] = jnp.where(last, 0.0, dscn_ref[...]) * nz[:, O_SCB:O_SCB + D_SC]
        conv = jnp.zeros((tm, D_SC), F32)
        dch = jnp.zeros((tm, D_SC), F32)
        g_own = g_s[h:h + tm, :]
        for j in range(SC_W):
            chj = ch_s[pl.ds(h + j - SC_W // 2, tm), :]
            conv = conv + scw_ref[j:j + 1, :] * chj
            dch = dch + scw_ref[j:j + 1, :] * g_s[pl.ds(h + SC_W // 2 - j, tm), :]
            dscw_ref[j:j + 1, :] += jnp.sum(g_own * chj, axis=0, keepdims=True)
        dz_ref[:, O_SCB:O_SCB + D_SC] = (dsc_ref[...] * conv).astype(BF16)
        dz_ref[:, O_SCC:O_SCC + D_SC] = (dch * zo[:, O_SCH:O_SCH + D_SC]).astype(BF16)
        dz_ref[:, O_SCH:O_SCH + D_SC] = (dch * zo[:, O_SCC:O_SCC + D_SC]).astype(BF16)

        dq = dq_ref[...]
        dz_ref[:, O_Q:O_Q + D_ATT] = (dq * _wide(cos_ref[...], D_ATT) + _swap_halves(dq * _wide(sin_ref[...], D_ATT))).astype(BF16)
        dk = dk1_ref[0] + jnp.where(last, 0.0, dk0_ref[0]) + jnp.where(first, 0.0, dk2_ref[0])
        dz_ref[:, O_K:O_K + 128] = (dk * cos_ref[...] + _swap_halves(dk * sin_ref[...])).astype(BF16)
        dv = dv1_ref[0] + jnp.where(last, 0.0, dv0_ref[0]) + jnp.where(first, 0.0, dv2_ref[0])
        dz_ref[:, O_V:O_V + 128] = dv.astype(BF16)

    def full(a):
        return pl.BlockSpec(a.shape, lambda i: (0, 0))

    def rows(w):
        return pl.BlockSpec((tm, w), lambda i: (i, 0))

    parts = [pl.BlockSpec((1, tm, 128), lambda i: (0, jnp.minimum(i + 1, nt - 1), 0)),
             pl.BlockSpec((1, tm, 128), lambda i: (1, i, 0)),
             pl.BlockSpec((1, tm, 128), lambda i: (2, jnp.maximum(i - 1, 0), 0))]
    acc_spec = lambda r: pl.BlockSpec((r, D_CC), lambda i: (0, 0))
    return pl.pallas_call(
        body, name="mix_bwd", grid=(nt,),
        in_specs=(_halo_specs(t, tm, h, D_IN) + _halo_specs(t, tm, h, D_SC) + _halo_specs(t, tm, h, D_CC)
                  + [rows(D_ATT)] + parts + parts
                  + [full(scw), full(ccw), full(ccb), full(ccg), full(ccbb), rows(128), rows(128)]),
        out_specs=[rows(D_IN), acc_spec(SC_W), acc_spec(CC_W), acc_spec(3)],
        out_shape=[S((t, D_IN), BF16), S((SC_W, D_SC), F32), S((CC_W, D_CC), F32), S((3, D_CC), F32)],
        scratch_shapes=[pltpu.VMEM((ext, D_CC), F32), pltpu.VMEM((mid, D_CC), F32),
                        pltpu.VMEM((ext, D_SC), F32), pltpu.VMEM((ext, D_SC), F32)],
        compiler_params=_params(("arbitrary",)),
    )(z, z, z, dysc, dysc, dysc, dycc, dycc, dycc, dqr, dkp, dkp, dkp, dvp, dvp, dvp,
      scw, ccw, ccb, ccg, ccbb, cos, sin)


def _adamw(w, g, m, v):
    m = ADAM_B1 * m + (1.0 - ADAM_B1) * g
    v = ADAM_B2 * v + (1.0 - ADAM_B2) * (g * g)
    m_hat = m / (1.0 - ADAM_B1 ** ADAM_STEP)
    v_hat = v / (1.0 - ADAM_B2 ** ADAM_STEP)
    delta = -ADAM_LR * (m_hat / (jnp.sqrt(v_hat) + ADAM_EPS) + ADAM_WD * w)
    return delta, m, v


def _row_tile(rows):
    for cand in (256, 176, 128):
        if rows % cand == 0:
            return cand
    return rows


def _sum_adam(recv, w, m, v, transposed):
    nl, rows = recv.shape[1], recv.shape[2]

    def body(r_ref, w_ref, m_ref, v_ref, g_ref, d_ref, mo_ref, vo_ref):
        g = r_ref[0, 0].astype(F32)
        for s in range(1, N_DEV):
            g = g + r_ref[s, 0].astype(F32)
        if transposed:
            g = g.T
        g_ref[0] = g
        d_ref[0], mo_ref[0], vo_ref[0] = _adamw(w_ref[0], g, m_ref[0], v_ref[0])

    if transposed:
        tc = 256
        blk = pl.BlockSpec((1, tc, rows), lambda l, c: (l, c, 0))
        rspec = pl.BlockSpec((N_DEV, 1, rows, tc), lambda l, c: (0, l, 0, c))
        grid = (nl, D // tc)
    else:
        tr = _row_tile(rows)
        blk = pl.BlockSpec((1, tr, D), lambda l, c: (l, c, 0))
        rspec = pl.BlockSpec((N_DEV, 1, tr, D), lambda l, c: (0, l, c, 0))
        grid = (nl, rows // tr)
    out = S(w.shape, F32)
    return pl.pallas_call(
        body, name="sum_adam_t" if transposed else "sum_adam", grid=grid,
        in_specs=[rspec, blk, blk, blk], out_specs=[blk] * 4, out_shape=[out] * 4,
        compiler_params=_params(("parallel", "parallel")),
    )(recv, w, m, v)


def _small_sum(gathered):
    rows = gathered.shape[1]

    def body(g_ref, o_ref):
        acc = g_ref[0]
        for s in range(1, N_DEV):
            acc = acc + g_ref[s]
        o_ref[...] = acc

    return pl.pallas_call(
        body, name="small_sum", in_specs=[pl.BlockSpec(gathered.shape, lambda: (0, 0, 0))],
        out_specs=pl.BlockSpec((rows, 128), lambda: (0, 0)), out_shape=S((rows, 128), F32),
    )(gathered)


def _small_adam(w, g, m, v):
    def body(w_ref, g_ref, m_ref, v_ref, d_ref, mo_ref, vo_ref):
        d_ref[...], mo_ref[...], vo_ref[...] = _adamw(w_ref[...], g_ref[...], m_ref[...], v_ref[...])

    spec = pl.BlockSpec(w.shape, lambda: (0, 0))
    return pl.pallas_call(
        body, name="small_adam", in_specs=[spec] * 4, out_specs=[spec] * 3, out_shape=[S(w.shape, F32)] * 3,
    )(w, g, m, v)


def _pack(pieces):
    flat = jnp.concatenate([p.reshape(-1).astype(F32) for p in pieces])
    n = flat.shape[0]
    rows = -(-n // 1024) * 8
    return jnp.pad(flat, (0, rows * 128 - n)).reshape(rows, 128)


def _unpack(packed, shapes):
    flat = packed.reshape(-1)
    out, o = [], 0
    for shp in shapes:
        n = int(np.prod(shp))
        out.append(flat[o:o + n].reshape(shp))
        o += n
    return out


def _rope_tables(t):
    half = HEAD_DIM // 2
    inv_freq = ROPE_THETA ** (-jnp.arange(half, dtype=F32) / half)
    ang = jnp.arange(t).astype(F32)[:, None] * inv_freq[None, :]
    cos, sin = jnp.cos(ang), jnp.sin(ang)
    cos128 = jnp.concatenate([cos, cos, cos, cos], axis=1)
    sin128 = jnp.concatenate([-sin, sin, -sin, sin], axis=1)
    return cos128, sin128


BIG = ("ffn1_w_gu", "ffn1_w_down", "w_in", "w_out", "ffn2_w_gu", "ffn2_w_down")
BIG_T = {"ffn1_w_gu": True, "ffn1_w_down": False, "w_in": True, "w_out": False, "ffn2_w_gu": True, "ffn2_w_down": False}
REPLICATED = ("ln1_g", "ln1_b", "attn_sink", "cc_conv_b", "cc_ln_g", "cc_ln_b", "ln2_g", "ln2_b", "ln3_g", "ln3_b")
CONVS = ("sc_conv_w", "cc_conv_w")
WEIGHTS = ("ffn1_w_gu", "ffn1_w_down", "ln1_g", "ln1_b", "w_in", "sc_conv_w", "attn_sink", "cc_conv_w", "cc_conv_b",
           "cc_ln_g", "cc_ln_b", "w_out", "ln2_g", "ln2_b", "ffn2_w_gu", "ffn2_w_down", "ln3_g", "ln3_b")


def kernel(x, ffn1_w_gu, ffn1_w_down, ln1_g, ln1_b, w_in, sc_conv_w, attn_sink, cc_conv_w, cc_conv_b, cc_ln_g, cc_ln_b, w_out, ln2_g, ln2_b, ffn2_w_gu, ffn2_w_down, ln3_g, ln3_b, loss_target, m_ffn1_w_gu, m_ffn1_w_down, m_ln1_g, m_ln1_b, m_w_in, m_sc_conv_w, m_attn_sink, m_cc_conv_w, m_cc_conv_b, m_cc_ln_g, m_cc_ln_b, m_w_out, m_ln2_g, m_ln2_b, m_ffn2_w_gu, m_ffn2_w_down, m_ln3_g, m_ln3_b, v_ffn1_w_gu, v_ffn1_w_down, v_ln1_g, v_ln1_b, v_w_in, v_sc_conv_w, v_attn_sink, v_cc_conv_w, v_cc_conv_b, v_cc_ln_g, v_cc_ln_b, v_w_out, v_ln2_g, v_ln2_b, v_ffn2_w_gu, v_ffn2_w_down, v_ln3_g, v_ln3_b):
    args = dict(locals())
    w = {n: args[n] for n in WEIGHTS}
    mom = {n: args["m_" + n] for n in WEIGHTS}
    var = {n: args["v_" + n] for n in WEIGHTS}
    x0 = x[0]
    target = loss_target[0]
    t = x0.shape[0]
    idx = 4 * lax.axis_index("x") + 2 * lax.axis_index("y") + lax.axis_index("c")

    blocks = []
    for l in range(DEPTH):
        for n in BIG:
            blk = w[n][l]
            blocks.append((blk.T if BIG_T[n] else blk).astype(BF16))
    conv_pack = _pack([w["sc_conv_w"], w["cc_conv_w"]])
    gathered = _all_gather(blocks + [conv_pack], "gather_weights")
    full = {}
    for l in range(DEPTH):
        for j, n in enumerate(BIG):
            g = gathered[l * len(BIG) + j]
            full[n, l] = g.reshape(N_DEV * g.shape[1], g.shape[2])
    conv_all = gathered[-1].reshape(N_DEV, -1)
    n_sc = DEPTH * SC_W * 32
    scw_full = conv_all[:, :n_sc].reshape(N_DEV, DEPTH, SC_W, 32).transpose(1, 2, 0, 3).reshape(DEPTH, SC_W, D_SC)
    ccw_full = conv_all[:, n_sc:n_sc + DEPTH * CC_W * 32].reshape(N_DEV, DEPTH, CC_W, 32).transpose(1, 2, 0, 3).reshape(DEPTH, CC_W, D_CC)

    cos, sin = _rope_tables(t)
    row = lambda a, l: a[l].reshape(1, -1)

    saved = []
    xf, xb = x0, x0.astype(BF16)
    for l in range(DEPTH):
        sv = {"x0b": xb}
        gu1, a1 = _ffn_up(xb, full["ffn1_w_gu", l])
        r1, x1, x1b = _ffn_down_ln(a1, full["ffn1_w_down", l], xf, row(ln1_g, l), row(ln1_b, l))
        z = _proj_in(x1b, full["w_in", l])
        ysc, ycc, qr, kr, vv = _mix_fwd(z, scw_full[l], ccw_full[l], row(cc_conv_b, l), row(cc_ln_g, l), row(cc_ln_b, l), cos, sin)
        yatt = _attn_fwd(qr, kr, vv, attn_sink[l])
        ycat, r2, x2, x2b = _out_ln(ysc, yatt, ycc, full["w_out", l], x1, row(ln2_g, l), row(ln2_b, l))
        gu2, a2 = _ffn_up(x2b, full["ffn2_w_gu", l])
        r3, x3, x3b = _ffn_down_ln(a2, full["ffn2_w_down", l], x2, row(ln3_g, l), row(ln3_b, l))
        sv.update(gu1=gu1, a1=a1, r1=r1, x1b=x1b, z=z, qr=qr, kr=kr, vv=vv, yatt=yatt, ycat=ycat, r2=r2, x2b=x2b,
                  gu2=gu2, a2=a2, r3=r3)
        saved.append(sv)
        xf, xb = x3, x3b

    dy, sq = _loss_head(xf, target)
    loss = lax.psum(0.5 * jnp.sum(sq) / D, ("x", "y", "c"))

    big_grads = {n: [None] * DEPTH for n in BIG}
    small = {n: [None] * DEPTH for n in REPLICATED + CONVS}
    for l in reversed(range(DEPTH)):
        sv = saved[l]
        dr, dfb, dh, dg, db = _ffn_bwd(dy, sv["r3"], row(ln3_g, l), full["ffn2_w_down", l], sv["gu2"])
        small["ln3_g"][l], small["ln3_b"][l] = dg, db
        big_grads["ffn2_w_down"][l] = _wgrad(sv["a2"], dfb, F // 2)
        big_grads["ffn2_w_gu"][l] = _wgrad(dh, sv["x2b"], F // 2)
        dy = _dx(dr, dh, full["ffn2_w_gu", l])

        dr, dmb, dysc, dyatt, dycc, dg, db = _out_bwd(dy, sv["r2"], row(ln2_g, l), full["w_out", l])
        small["ln2_g"][l], small["ln2_b"][l] = dg, db
        big_grads["w_out"][l] = _wgrad(sv["ycat"], dmb, D)
        dqr, dkp, dvp, dsink = _attn_bwd(sv["qr"], sv["kr"], sv["vv"], attn_sink[l], dyatt, sv["yatt"])
        small["attn_sink"][l] = dsink[:, 0]
        dz, dscw, dccw, dvec = _mix_bwd(sv["z"], dysc, dycc, dqr, dkp, dvp, scw_full[l], ccw_full[l],
                                        row(cc_conv_b, l), row(cc_ln_g, l), row(cc_ln_b, l), cos, sin)
        small["sc_conv_w"][l], small["cc_conv_w"][l] = dscw, dccw
        small["cc_conv_b"][l], small["cc_ln_g"][l], small["cc_ln_b"][l] = dvec[0], dvec[1], dvec[2]
        big_grads["w_in"][l] = _wgrad(dz, sv["x1b"], D)
        dy = _dx(dr, dz, full["w_in", l])

        dr, dfb, dh, dg, db = _ffn_bwd(dy, sv["r1"], row(ln1_g, l), full["ffn1_w_down", l], sv["gu1"])
        small["ln1_g"][l], small["ln1_b"][l] = dg, db
        big_grads["ffn1_w_down"][l] = _wgrad(sv["a1"], dfb, F // 2)
        big_grads["ffn1_w_gu"][l] = _wgrad(dh, sv["x0b"], F // 2)
        dy = _dx(dr, dh, full["ffn1_w_gu", l])
    grad_x = dy[None]

    small_names = REPLICATED + CONVS
    small_shapes = [(DEPTH,) + tuple(np.shape(small[n][0].reshape(-1))) for n in small_names]
    small_pack = _pack([jnp.stack([small[n][l].reshape(-1) for l in range(DEPTH)]) for n in small_names])
    recv = _exchange_grads([big_grads[n] for n in BIG], small_pack)

    grads, deltas, new_m, new_v = {}, {}, {}, {}
    for j, n in enumerate(BIG):
        grads[n], deltas[n], new_m[n], new_v[n] = _sum_adam(recv[j], w[n], mom[n], var[n], BIG_T[n])
    small_total = _unpack(_small_sum(recv[-1]), small_shapes)
    for n, g in zip(small_names, small_total):
        if n in CONVS:
            taps = SC_W if n == "sc_conv_w" else CC_W
            g = lax.dynamic_slice_in_dim(g.reshape(DEPTH, taps, D_SC), idx * 32, 32, axis=2)
        grads[n] = g.reshape(w[n].shape)
    wp = _pack([w[n] for n in small_names])
    gp = _pack([grads[n] for n in small_names])
    mp = _pack([mom[n] for n in small_names])
    vp = _pack([var[n] for n in small_names])
    shapes = [w[n].shape for n in small_names]
    for dst, packed in zip((deltas, new_m, new_v), _small_adam(wp, gp, mp, vp)):
        for n, a in zip(small_names, _unpack(packed, shapes)):
            dst[n] = a

    return (loss, grad_x, *[grads[n] for n in WEIGHTS], *[deltas[n] for n in WEIGHTS],
            *[new_m[n] for n in WEIGHTS], *[new_v[n] for n in WEIGHTS])
```

```python
import functools

import jax
import jax.numpy as jnp
import numpy as np
from jax import lax
from jax.experimental import pallas as pl
from jax.experimental.pallas import tpu as pltpu

F32 = jnp.float32
BF16 = jnp.bfloat16
S = jax.ShapeDtypeStruct

N_DEV = 8
DEPTH = 2
D = 1024
F = 2816
D_IN = 2048
HEAD_DIM = 64
N_Q_HEADS = 8
N_KV_HEADS = 2
GROUP = 4
D_SC = 256
D_ATT = 512
D_CC = 256
CC_W = 31
SC_W = 3
BLOCK = 128
ROPE_THETA = 10000.0
LN_EPS = 1e-5
ALPHA = (2.0 * DEPTH) ** 0.25
ADAM_LR = 0.001
ADAM_B1 = 0.9
ADAM_B2 = 0.999
ADAM_EPS = 1e-08
ADAM_WD = 0.01
ADAM_STEP = 10

O_SCB, O_SCC, O_SCH, O_Q, O_K, O_V, O_CCA, O_CCG = 0, 256, 512, 768, 1280, 1408, 1536, 1792

V7X_VMEM_BYTES = 64 * 1024 * 1024
VMEM_LIMIT = V7X_VMEM_BYTES - 8 * 1024 * 1024
TOKEN_TILE = 256
MIX_BWD_TILE = 128
HALO_FWD = 16
HALO_BWD = 32

NT = (((1,), (1,)), ((), ()))
TN = (((0,), (0,)), ((), ()))
MESH = pl.DeviceIdType.MESH


def _params(sem=None):
    return pltpu.CompilerParams(dimension_semantics=sem, vmem_limit_bytes=VMEM_LIMIT)


def _sigmoid(v):
    return 1.0 / (1.0 + jnp.exp(-v))


def _ln_stats(r):
    mu = jnp.mean(r, axis=-1, keepdims=True)
    d = r - mu
    var = jnp.mean(d * d, axis=-1, keepdims=True)
    rstd = lax.rsqrt(var + LN_EPS)
    return d * rstd, rstd


def _ln_bwd(dn, xhat, rstd, gam):
    dxh = dn * gam
    return rstd * (dxh - jnp.mean(dxh, axis=-1, keepdims=True) - xhat * jnp.mean(dxh * xhat, axis=-1, keepdims=True))


def _swap_halves(v):
    n = v.shape[-1]
    lane = lax.broadcasted_iota(jnp.int32, v.shape, v.ndim - 1) % HEAD_DIM
    return jnp.where(lane < HEAD_DIM // 2, pltpu.roll(v, n - HEAD_DIM // 2, v.ndim - 1), pltpu.roll(v, HEAD_DIM // 2, v.ndim - 1))


def _wide(tab, n):
    return tab if n == 128 else jnp.concatenate([tab] * (n // 128), axis=1)


def _me():
    x, y, c = lax.axis_index("x"), lax.axis_index("y"), lax.axis_index("c")
    return x, y, c


def _peer(rel):
    x, y, c = _me()
    px = 1 - x if rel & 4 else x
    py = 1 - y if rel & 2 else y
    pc = 1 - c if rel & 1 else c
    return (px, py, pc), 4 * px + 2 * py + pc


def _exchange(srcs, dsts_shape, dst_index, src_of, dst_of, name):
    n = len(srcs)

    def body(*refs):
        ins = refs[:n]
        outs = [refs[n + dst_index[k]] for k in range(n)]
        send, recv, lsem = refs[n + len(dsts_shape):]
        x, y, c = _me()
        me = 4 * x + 2 * y + c
        local = [pltpu.make_async_copy(src_of(ins[k], k, me), dst_of(outs[k], k, me), lsem.at[k]) for k in range(n)]
        for cp in local:
            cp.start()
        sends, recvs = [], []
        for k in range(n):
            for rel in range(1, N_DEV):
                peer, pidx = _peer(rel)
                sends.append(pltpu.make_async_remote_copy(
                    src_ref=src_of(ins[k], k, pidx), dst_ref=dst_of(outs[k], k, me),
                    send_sem=send.at[k, rel - 1], recv_sem=recv.at[k, rel - 1], device_id=peer, device_id_type=MESH))
                recvs.append(pltpu.make_async_remote_copy(
                    src_ref=src_of(ins[k], k, pidx), dst_ref=dst_of(outs[k], k, pidx),
                    send_sem=send.at[k, rel - 1], recv_sem=recv.at[k, rel - 1], device_id=peer, device_id_type=MESH))
        for cp in sends:
            cp.start()
        for cp in recvs:
            cp.wait_recv()
        for cp in sends:
            cp.wait_send()
        for cp in local:
            cp.wait()

    hbm = pl.BlockSpec(memory_space=pltpu.HBM)
    return pl.pallas_call(
        body, name=name, in_specs=[hbm] * n, out_specs=[hbm] * len(dsts_shape), out_shape=dsts_shape,
        scratch_shapes=[pltpu.SemaphoreType.DMA((n, N_DEV - 1)), pltpu.SemaphoreType.DMA((n, N_DEV - 1)),
                        pltpu.SemaphoreType.DMA((n,))],
    )(*srcs)


def _all_gather(blocks, name):
    shapes = [S((N_DEV,) + b.shape, b.dtype) for b in blocks]
    return _exchange(blocks, shapes, list(range(len(blocks))), lambda ref, k, idx: ref, lambda ref, k, idx: ref.at[idx], name)


HBM_SPEC = pl.BlockSpec(memory_space=pltpu.HBM)
SEM_SPEC = pl.BlockSpec(memory_space=pltpu.SEMAPHORE)
ANY_SPEC = pl.BlockSpec(memory_space=pl.ANY)
EFFECT = pltpu.SideEffectType.DATAFLOW_SIDE_EFFECTING
N_PEERS = N_DEV - 1


def _own_slot(block):
    x, y, c = _me()
    return lax.dynamic_update_index_in_dim(lax.empty((N_DEV,) + block.shape, block.dtype), block, 4 * x + 2 * y + c, 0)


def _send_start(srcs, lands, src_of, name):
    n = len(srcs)

    def body(*refs):
        ins, zones = refs[:n], refs[n:2 * n]
        send, recv = refs[2 * n], refs[2 * n + 1]
        token = refs[-1]
        x, y, c = _me()
        me = 4 * x + 2 * y + c
        for k in range(n):
            for rel in range(1, N_DEV):
                peer, pidx = _peer(rel)
                pltpu.make_async_remote_copy(
                    src_ref=src_of(ins[k], pidx), dst_ref=zones[k].at[me],
                    send_sem=send.at[k * N_PEERS + rel - 1], recv_sem=recv.at[k * N_PEERS + rel - 1],
                    device_id=peer, device_id_type=MESH).start()
        token[...] = jnp.zeros_like(token)

    outs = pl.pallas_call(
        body, name=name,
        out_shape=(pltpu.SemaphoreType.DMA((n * N_PEERS,)), pltpu.SemaphoreType.DMA((n * N_PEERS,)),
                   *[pltpu.HBM(a.shape, a.dtype) for a in srcs], *[pltpu.HBM(a.shape, a.dtype) for a in lands],
                   S((8, 128), F32)),
        in_specs=[HBM_SPEC] * (2 * n),
        out_specs=(SEM_SPEC, SEM_SPEC, *[HBM_SPEC] * (2 * n), pl.BlockSpec(memory_space=pltpu.VMEM)),
        input_output_aliases={i: 2 + i for i in range(2 * n)},
        compiler_params=pltpu.CompilerParams(has_side_effects=EFFECT),
    )(*[pltpu.with_memory_space_constraint(a, pltpu.HBM) for a in list(srcs) + list(lands)])
    return outs[0], outs[1], list(outs[2:2 + n]), list(outs[2 + n:2 + 2 * n]), outs[-1]


def _recv_wait(send, recv, ks, srcs, lands, src_of, after, name):
    n = len(ks)

    def body(*refs):
        ins, zones = refs[:n], refs[n:2 * n]
        send_sems, recv_sems = refs[2 * n], refs[2 * n + 1]
        for j, k in enumerate(ks):
            for rel in range(1, N_DEV):
                peer, pidx = _peer(rel)
                cp = pltpu.make_async_remote_copy(
                    src_ref=src_of(ins[j], pidx), dst_ref=zones[j].at[pidx],
                    send_sem=send_sems.at[k * N_PEERS + rel - 1], recv_sem=recv_sems.at[k * N_PEERS + rel - 1],
                    device_id=peer, device_id_type=MESH)
                cp.wait_send()
                cp.wait_recv()

    outs = pl.pallas_call(
        body, name=name,
        out_shape=(*[pltpu.HBM(a.shape, a.dtype) for a in srcs], *[pltpu.HBM(a.shape, a.dtype) for a in lands]),
        in_specs=[HBM_SPEC] * (2 * n) + [SEM_SPEC, SEM_SPEC, ANY_SPEC], out_specs=[HBM_SPEC] * (2 * n),
        input_output_aliases={i: i for i in range(2 * n)},
        compiler_params=pltpu.CompilerParams(has_side_effects=EFFECT),
    )(*srcs, *lands, send, recv, after)
    return list(outs[n:])


def _whole(ref, idx):
    return ref


def _block_of(ref, idx):
    return ref.at[idx]


def _ffn_up(xb, wgut):
    t = xb.shape[0]
    tm = min(TOKEN_TILE, t)
    half = F // 2

    def body(x_ref, w_ref, gu_ref, a_ref):
        x = x_ref[...]
        for ch in range(2):
            lo = ch * half
            g = lax.dot_general(x, w_ref[lo:lo + half, :], NT, preferred_element_type=F32)
            u = lax.dot_general(x, w_ref[F + lo:F + lo + half, :], NT, preferred_element_type=F32)
            gu_ref[:, lo:lo + half] = g.astype(BF16)
            gu_ref[:, F + lo:F + lo + half] = u.astype(BF16)
            a_ref[:, lo:lo + half] = (g * _sigmoid(g) * u).astype(BF16)

    return pl.pallas_call(
        body, name="ffn_up", grid=(t // tm,),
        in_specs=[pl.BlockSpec((tm, D), lambda i: (i, 0)), pl.BlockSpec((2 * F, D), lambda i: (0, 0))],
        out_specs=[pl.BlockSpec((tm, 2 * F), lambda i: (i, 0)), pl.BlockSpec((tm, F), lambda i: (i, 0))],
        out_shape=[S((t, 2 * F), BF16), S((t, F), BF16)], compiler_params=_params(("parallel",)),
    )(xb, wgut)


def _residual_ln_out(x_ref, f, scale, g_ref, b_ref, r_ref, y_ref, yb_ref):
    r = ALPHA * x_ref[...] + scale * f
    xhat, _ = _ln_stats(r)
    y = xhat * g_ref[...] + b_ref[...]
    r_ref[...] = r
    y_ref[...] = y
    yb_ref[...] = y.astype(BF16)


def _ffn_down_ln(a, wd, x, gam, bet):
    t = x.shape[0]
    tm = min(TOKEN_TILE, t)

    def body(a_ref, w_ref, x_ref, g_ref, b_ref, r_ref, y_ref, yb_ref):
        f = jnp.dot(a_ref[...], w_ref[...], preferred_element_type=F32)
        _residual_ln_out(x_ref, f, 0.5, g_ref, b_ref, r_ref, y_ref, yb_ref)

    row = pl.BlockSpec((tm, D), lambda i: (i, 0))
    vec = pl.BlockSpec((1, D), lambda i: (0, 0))
    return pl.pallas_call(
        body, name="ffn_down_ln", grid=(t // tm,),
        in_specs=[pl.BlockSpec((tm, F), lambda i: (i, 0)), pl.BlockSpec((F, D), lambda i: (0, 0)), row, vec, vec],
        out_specs=[row, row, row], out_shape=[S((t, D), F32), S((t, D), F32), S((t, D), BF16)],
        compiler_params=_params(("parallel",)),
    )(a, wd, x, gam, bet)


def _proj_in(xb, wint):
    t = xb.shape[0]
    tm = min(TOKEN_TILE, t)

    def body(x_ref, w_ref, z_ref):
        z_ref[...] = lax.dot_general(x_ref[...], w_ref[...], NT, preferred_element_type=F32)

    return pl.pallas_call(
        body, name="proj_in", grid=(t // tm,),
        in_specs=[pl.BlockSpec((tm, D), lambda i: (i, 0)), pl.BlockSpec((D_IN, D), lambda i: (0, 0))],
        out_specs=pl.BlockSpec((tm, D_IN), lambda i: (i, 0)), out_shape=S((t, D_IN), F32),
        compiler_params=_params(("parallel",)),
    )(xb, wint)


def _halo_specs(t, tm, halo, width):
    per = tm // halo
    last = t // halo - 1
    return [pl.BlockSpec((tm, width), lambda i: (i, 0)),
            pl.BlockSpec((halo, width), lambda i: (jnp.maximum(i * per - 1, 0), 0)),
            pl.BlockSpec((halo, width), lambda i: (jnp.minimum((i + 1) * per, last), 0))]


def _mix_fwd(z, scw, ccw, ccb, ccg, ccbb, cos, sin):
    t = z.shape[0]
    tm = min(TOKEN_TILE, t)
    nt = t // tm
    h = HALO_FWD
    rc = 64

    def body(z_ref, zp_ref, zn_ref, scw_ref, ccw_ref, ccb_ref, ccg_ref, ccbb_ref, cos_ref, sin_ref,
             ysc_ref, ycc_ref, q_ref, k_ref, v_ref, u_s, ch_s):
        i = pl.program_id(0)
        pz = jnp.where(i == 0, 0.0, zp_ref[...])
        nz = jnp.where(i == nt - 1, 0.0, zn_ref[...])

        def u_of(zz):
            return zz[:, O_CCA:O_CCA + D_CC] * _sigmoid(zz[:, O_CCG:O_CCG + D_CC])

        def ch_of(zz):
            return zz[:, O_SCC:O_SCC + D_SC] * zz[:, O_SCH:O_SCH + D_SC]

        u_s[0:h, :] = u_of(pz)
        u_s[h:h + tm, :] = z_ref[:, O_CCA:O_CCA + D_CC] * _sigmoid(z_ref[:, O_CCG:O_CCG + D_CC])
        u_s[h + tm:2 * h + tm, :] = u_of(nz)
        ch_s[0:h, :] = ch_of(pz)
        ch_s[h:h + tm, :] = z_ref[:, O_SCC:O_SCC + D_SC] * z_ref[:, O_SCH:O_SCH + D_SC]
        ch_s[h + tm:2 * h + tm, :] = ch_of(nz)
        for r0 in range(0, tm, rc):
            acc = jnp.zeros((rc, D_CC), F32)
            for j in range(CC_W):
                acc = acc + ccw_ref[j:j + 1, :] * u_s[pl.ds(r0 + h + j - CC_W // 2, rc), :]
            xhat, _ = _ln_stats(acc + ccb_ref[...])
            n = xhat * ccg_ref[...] + ccbb_ref[...]
            ycc_ref[r0:r0 + rc, :] = (n * _sigmoid(n)).astype(BF16)
            acc = jnp.zeros((rc, D_SC), F32)
            for j in range(SC_W):
                acc = acc + scw_ref[j:j + 1, :] * ch_s[pl.ds(r0 + h + j - SC_W // 2, rc), :]
            ysc_ref[r0:r0 + rc, :] = (z_ref[r0:r0 + rc, O_SCB:O_SCB + D_SC] * acc).astype(BF16)
        q = z_ref[:, O_Q:O_Q + D_ATT]
        q_ref[...] = (q * _wide(cos_ref[...], D_ATT) + _swap_halves(q) * _wide(sin_ref[...], D_ATT)).astype(BF16)
        k = z_ref[:, O_K:O_K + 128]
        k_ref[...] = (k * cos_ref[...] + _swap_halves(k) * sin_ref[...]).astype(BF16)
        v_ref[...] = z_ref[:, O_V:O_V + 128].astype(BF16)

    def full(a):
        return pl.BlockSpec(a.shape, lambda i: (0, 0))

    def rows(w):
        return pl.BlockSpec((tm, w), lambda i: (i, 0))

    return pl.pallas_call(
        body, name="mix_fwd", grid=(nt,),
        in_specs=_halo_specs(t, tm, h, D_IN) + [full(scw), full(ccw), full(ccb), full(ccg), full(ccbb), rows(128), rows(128)],
        out_specs=[rows(D_SC), rows(D_CC), rows(D_ATT), rows(128), rows(128)],
        out_shape=[S((t, D_SC), BF16), S((t, D_CC), BF16), S((t, D_ATT), BF16), S((t, 128), BF16), S((t, 128), BF16)],
        scratch_shapes=[pltpu.VMEM((tm + 2 * h, D_CC), F32), pltpu.VMEM((tm + 2 * h, D_SC), F32)],
        compiler_params=_params(("parallel",)),
    )(z, z, z, scw, ccw, ccb, ccg, ccbb, cos, sin)


def _band_specs(nb, width):
    return [pl.BlockSpec((BLOCK, width), lambda n: (jnp.maximum(n - 1, 0), 0)),
            pl.BlockSpec((BLOCK, width), lambda n: (n, 0)),
            pl.BlockSpec((BLOCK, width), lambda n: (jnp.minimum(n + 1, nb - 1), 0))]


def _band_valid(n, nb):
    rows = GROUP * BLOCK
    qpos = lax.broadcasted_iota(jnp.int32, (rows, 3 * BLOCK), 0) % BLOCK
    col = lax.broadcasted_iota(jnp.int32, (rows, 3 * BLOCK), 1)
    ok = jnp.abs(qpos - (col - BLOCK)) <= BLOCK
    ok = jnp.logical_and(ok, jnp.logical_or(col >= BLOCK, n > 0))
    return jnp.logical_and(ok, jnp.logical_or(col < 2 * BLOCK, n < nb - 1))


def _stack_heads(ref, kvh):
    return jnp.concatenate([ref[:, (GROUP * kvh + g) * HEAD_DIM:(GROUP * kvh + g + 1) * HEAD_DIM] for g in range(GROUP)], axis=0)


def _band_cat(refs, kvh):
    return jnp.concatenate([r[:, kvh * HEAD_DIM:(kvh + 1) * HEAD_DIM] for r in refs], axis=0)


def _sink_col(sink_ref, kvh):
    return jnp.concatenate([jnp.full((BLOCK, 1), sink_ref[GROUP * kvh + g], F32) for g in range(GROUP)], axis=0)


def _attn_probs(q_ref, k_refs, sink_ref, kvh, valid):
    qs = _stack_heads(q_ref, kvh)
    kc = _band_cat(k_refs, kvh)
    s = lax.dot_general(qs, kc, NT, preferred_element_type=F32) * (HEAD_DIM ** -0.5)
    s = jnp.where(valid, s, -1e30)
    sk = _sink_col(sink_ref, kvh)
    m = jnp.maximum(jnp.max(s, axis=-1, keepdims=True), sk)
    p = jnp.exp(s - m)
    ps = jnp.exp(sk - m)
    denom = jnp.sum(p, axis=-1, keepdims=True) + ps
    return qs, kc, p, ps, denom


def _attn_fwd(qr, kr, vv, sink):
    t = qr.shape[0]
    nb = t // BLOCK

    def body(q_ref, kp_ref, ko_ref, kn_ref, vp_ref, vo_ref, vn_ref, sink_ref, o_ref):
        n = pl.program_id(0)
        valid = _band_valid(n, nb)
        for kvh in range(N_KV_HEADS):
            _, _, p, _, denom = _attn_probs(q_ref, (kp_ref, ko_ref, kn_ref), sink_ref, kvh, valid)
            vc = _band_cat((vp_ref, vo_ref, vn_ref), kvh)
            o = jnp.dot(p.astype(BF16), vc, preferred_element_type=F32) / denom
            for g in range(GROUP):
                hh = GROUP * kvh + g
                o_ref[:, hh * HEAD_DIM:(hh + 1) * HEAD_DIM] = o[g * BLOCK:(g + 1) * BLOCK, :].astype(BF16)

    qspec = pl.BlockSpec((BLOCK, D_ATT), lambda n: (n, 0))
    return pl.pallas_call(
        body, name="attn_fwd", grid=(nb,),
        in_specs=[qspec] + _band_specs(nb, 128) + _band_specs(nb, 128) + [pl.BlockSpec(memory_space=pltpu.SMEM)],
        out_specs=qspec, out_shape=S((t, D_ATT), BF16), compiler_params=_params(("parallel",)),
    )(qr, kr, kr, kr, vv, vv, vv, sink)


def _out_ln(ysc, yatt, ycc, wout, x, gam, bet):
    t = x.shape[0]
    tm = min(TOKEN_TILE, t)

    def body(sc_ref, at_ref, cc_ref, w_ref, x_ref, g_ref, b_ref, cat_ref, r_ref, y_ref, yb_ref):
        cat = jnp.concatenate([sc_ref[...], at_ref[...], cc_ref[...]], axis=1)
        cat_ref[...] = cat
        f = jnp.dot(cat, w_ref[...], preferred_element_type=F32)
        _residual_ln_out(x_ref, f, 1.0, g_ref, b_ref, r_ref, y_ref, yb_ref)

    def rows(w):
        return pl.BlockSpec((tm, w), lambda i: (i, 0))

    vec = pl.BlockSpec((1, D), lambda i: (0, 0))
    return pl.pallas_call(
        body, name="out_ln", grid=(t // tm,),
        in_specs=[rows(D_SC), rows(D_ATT), rows(D_CC), pl.BlockSpec((D, D), lambda i: (0, 0)), rows(D), vec, vec],
        out_specs=[rows(D), rows(D), rows(D), rows(D)],
        out_shape=[S((t, D), BF16), S((t, D), F32), S((t, D), F32), S((t, D), BF16)],
        compiler_params=_params(("parallel",)),
    )(ysc, yatt, ycc, wout, x, gam, bet)


def _loss_head(y, target):
    t = y.shape[0]
    tm = min(TOKEN_TILE, t)

    def body(y_ref, t_ref, dy_ref, part_ref):
        e = y_ref[...] - t_ref[...]
        dy_ref[...] = e / D

        @pl.when(pl.program_id(0) == 0)
        def _():
            part_ref[...] = jnp.zeros_like(part_ref)

        part_ref[...] += jnp.sum(e * e, axis=0, keepdims=True)

    row = pl.BlockSpec((tm, D), lambda i: (i, 0))
    return pl.pallas_call(
        body, name="loss_head", grid=(t // tm,), in_specs=[row, row],
        out_specs=[row, pl.BlockSpec((1, D), lambda i: (0, 0))], out_shape=[S((t, D), F32), S((1, D), F32)],
        compiler_params=_params(("arbitrary",)),
    )(y, target)


def _ln_bwd_block(dy_ref, r_ref, g_ref, dgam_ref, dbet_ref):
    xhat, rstd = _ln_stats(r_ref[...])
    dy = dy_ref[...]

    @pl.when(pl.program_id(0) == 0)
    def _():
        dgam_ref[...] = jnp.zeros_like(dgam_ref)
        dbet_ref[...] = jnp.zeros_like(dbet_ref)

    dgam_ref[...] += jnp.sum(dy * xhat, axis=0, keepdims=True)
    dbet_ref[...] += jnp.sum(dy, axis=0, keepdims=True)
    return _ln_bwd(dy, xhat, rstd, g_ref[...])


def _ffn_bwd(dy, r, gam, wd, gu):
    t = dy.shape[0]
    tm = min(TOKEN_TILE, t)
    half = F // 2

    def body(dy_ref, r_ref, g_ref, w_ref, gu_ref, dr_ref, df_ref, dh_ref, dgam_ref, dbet_ref):
        dr = _ln_bwd_block(dy_ref, r_ref, g_ref, dgam_ref, dbet_ref)
        dr_ref[...] = dr
        dfb = (0.5 * dr).astype(BF16)
        df_ref[...] = dfb
        for ch in range(2):
            lo = ch * half
            da = lax.dot_general(dfb, w_ref[lo:lo + half, :], NT, preferred_element_type=F32)
            g = gu_ref[:, lo:lo + half].astype(F32)
            u = gu_ref[:, F + lo:F + lo + half].astype(F32)
            sg = _sigmoid(g)
            dh_ref[:, lo:lo + half] = (da * u * (sg * (1.0 + g * (1.0 - sg)))).astype(BF16)
            dh_ref[:, F + lo:F + lo + half] = (da * (g * sg)).astype(BF16)

    row = pl.BlockSpec((tm, D), lambda i: (i, 0))
    vec = pl.BlockSpec((1, D), lambda i: (0, 0))
    wide = pl.BlockSpec((tm, 2 * F), lambda i: (i, 0))
    return pl.pallas_call(
        body, name="ffn_bwd", grid=(t // tm,),
        in_specs=[row, row, vec, pl.BlockSpec((F, D), lambda i: (0, 0)), wide],
        out_specs=[row, row, wide, vec, vec],
        out_shape=[S((t, D), F32), S((t, D), BF16), S((t, 2 * F), BF16), S((1, D), F32), S((1, D), F32)],
        compiler_params=_params(("arbitrary",)),
    )(dy, r, gam, wd, gu)


def _dx(dr, dh, w, after):
    t = dr.shape[0]
    tm = min(TOKEN_TILE, t)
    kk = dh.shape[1]

    def body(dr_ref, dh_ref, w_ref, after_ref, o_ref):
        o_ref[...] = ALPHA * dr_ref[...] + jnp.dot(dh_ref[...], w_ref[...], preferred_element_type=F32)

    row = pl.BlockSpec((tm, D), lambda i: (i, 0))
    return pl.pallas_call(
        body, name="dx", grid=(t // tm,),
        in_specs=[row, pl.BlockSpec((tm, kk), lambda i: (i, 0)), pl.BlockSpec((kk, D), lambda i: (0, 0)), ANY_SPEC],
        out_specs=row, out_shape=S((t, D), F32), compiler_params=_params(("parallel",)),
    )(dr, dh, w, after)


def _wgrad(a, b, ta):
    t, ka = a.shape
    tk = min(1024, t)
    nk = t // tk

    def body(a_ref, b_ref, o_ref, acc):
        k = pl.program_id(1)

        @pl.when(k == 0)
        def _():
            acc[...] = jnp.zeros_like(acc)

        acc[...] += lax.dot_general(a_ref[...], b_ref[...], TN, preferred_element_type=F32)

        @pl.when(k == nk - 1)
        def _():
            o_ref[...] = acc[...].astype(BF16)

    return pl.pallas_call(
        body, name="wgrad", grid=(ka // ta, nk),
        in_specs=[pl.BlockSpec((tk, ta), lambda i, k: (k, i)), pl.BlockSpec((tk, D), lambda i, k: (k, 0))],
        out_specs=pl.BlockSpec((ta, D), lambda i, k: (i, 0)), out_shape=S((ka, D), BF16),
        scratch_shapes=[pltpu.VMEM((ta, D), F32)], compiler_params=_params(("parallel", "arbitrary")),
    )(a, b)


def _out_bwd(dy, r, gam, wout):
    t = dy.shape[0]
    tm = min(TOKEN_TILE, t)

    def body(dy_ref, r_ref, g_ref, w_ref, dr_ref, dm_ref, dsc_ref, dat_ref, dcc_ref, dgam_ref, dbet_ref):
        dr = _ln_bwd_block(dy_ref, r_ref, g_ref, dgam_ref, dbet_ref)
        dr_ref[...] = dr
        dmb = dr.astype(BF16)
        dm_ref[...] = dmb
        dcat = lax.dot_general(dmb, w_ref[...], NT, preferred_element_type=F32)
        dsc_ref[...] = dcat[:, 0:D_SC]
        dat_ref[...] = dcat[:, D_SC:D_SC + D_ATT]
        dcc_ref[...] = dcat[:, D_SC + D_ATT:D]

    def rows(w):
        return pl.BlockSpec((tm, w), lambda i: (i, 0))

    vec = pl.BlockSpec((1, D), lambda i: (0, 0))
    return pl.pallas_call(
        body, name="out_bwd", grid=(t // tm,),
        in_specs=[rows(D), rows(D), vec, pl.BlockSpec((D, D), lambda i: (0, 0))],
        out_specs=[rows(D), rows(D), rows(D_SC), rows(D_ATT), rows(D_CC), vec, vec],
        out_shape=[S((t, D), F32), S((t, D), BF16), S((t, D_SC), F32), S((t, D_ATT), F32), S((t, D_CC), F32),
                   S((1, D), F32), S((1, D), F32)],
        compiler_params=_params(("arbitrary",)),
    )(dy, r, gam, wout)


def _attn_bwd(qr, kr, vv, sink, do, yatt):
    t = qr.shape[0]
    nb = t // BLOCK

    def body(q_ref, kp_ref, ko_ref, kn_ref, vp_ref, vo_ref, vn_ref, sink_ref, do_ref, o_ref,
             dq_ref, dk_ref, dv_ref, dsink_ref):
        n = pl.program_id(0)
        valid = _band_valid(n, nb)

        @pl.when(n == 0)
        def _():
            dsink_ref[...] = jnp.zeros_like(dsink_ref)

        for kvh in range(N_KV_HEADS):
            qs, kc, p, ps, denom = _attn_probs(q_ref, (kp_ref, ko_ref, kn_ref), sink_ref, kvh, valid)
            vc = _band_cat((vp_ref, vo_ref, vn_ref), kvh)
            dos = _stack_heads(do_ref, kvh)
            os_ = _stack_heads(o_ref, kvh).astype(F32)
            dd = jnp.sum(dos * os_, axis=-1, keepdims=True) / denom
            dou = (dos / denom).astype(BF16)
            dp = lax.dot_general(dou, vc, NT, preferred_element_type=F32)
            ds = (p * (dp - dd) * (HEAD_DIM ** -0.5)).astype(BF16)
            dq = jnp.dot(ds, kc, preferred_element_type=F32)
            dk = lax.dot_general(ds, qs, TN, preferred_element_type=F32)
            dv = lax.dot_general(p.astype(BF16), dou, TN, preferred_element_type=F32)
            dsk = ps * dd
            for g in range(GROUP):
                hh = GROUP * kvh + g
                dq_ref[:, hh * HEAD_DIM:(hh + 1) * HEAD_DIM] = dq[g * BLOCK:(g + 1) * BLOCK, :]
                dsink_ref[hh:hh + 1, :] += jnp.zeros((1, 128), F32) - jnp.sum(dsk[g * BLOCK:(g + 1) * BLOCK, :])
            for j in range(3):
                dk_ref[j, :, kvh * HEAD_DIM:(kvh + 1) * HEAD_DIM] = dk[j * BLOCK:(j + 1) * BLOCK, :]
                dv_ref[j, :, kvh * HEAD_DIM:(kvh + 1) * HEAD_DIM] = dv[j * BLOCK:(j + 1) * BLOCK, :]

    qspec = pl.BlockSpec((BLOCK, D_ATT), lambda n: (n, 0))
    part = pl.BlockSpec((3, BLOCK, 128), lambda n: (0, n, 0))
    return pl.pallas_call(
        body, name="attn_bwd", grid=(nb,),
        in_specs=[qspec] + _band_specs(nb, 128) + _band_specs(nb, 128) + [pl.BlockSpec(memory_space=pltpu.SMEM), qspec, qspec],
        out_specs=[qspec, part, part, pl.BlockSpec((N_Q_HEADS, 128), lambda n: (0, 0))],
        out_shape=[S((t, D_ATT), F32), S((3, t, 128), F32), S((3, t, 128), F32), S((N_Q_HEADS, 128), F32)],
        compiler_params=_params(("arbitrary",)),
    )(qr, kr, kr, kr, vv, vv, vv, sink, do, yatt)


def _mix_bwd(z, dysc, dycc, dqr, dkp, dvp, scw, ccw, ccb, ccg, ccbb, cos, sin):
    t = z.shape[0]
    tm = min(MIX_BWD_TILE, t)
    nt = t // tm
    h = HALO_BWD
    hh = h // 2
    half = CC_W // 2
    ext = tm + 2 * h
    mid = tm + 2 * hh

    def body(z_ref, zp_ref, zn_ref, dsc_ref, dscp_ref, dscn_ref, dcc_ref, dccp_ref, dccn_ref, dq_ref,
             dk0_ref, dk1_ref, dk2_ref, dv0_ref, dv1_ref, dv2_ref,
             scw_ref, ccw_ref, ccb_ref, ccg_ref, ccbb_ref, cos_ref, sin_ref,
             dz_ref, dscw_ref, dccw_ref, dvec_ref, u_s, dc_s, ch_s, g_s):
        i = pl.program_id(0)
        first, last = i == 0, i == nt - 1

        @pl.when(first)
        def _():
            dscw_ref[...] = jnp.zeros_like(dscw_ref)
            dccw_ref[...] = jnp.zeros_like(dccw_ref)
            dvec_ref[...] = jnp.zeros_like(dvec_ref)

        pz = jnp.where(first, 0.0, zp_ref[...])
        nz = jnp.where(last, 0.0, zn_ref[...])
        zo = z_ref[...]

        def u_of(zz):
            return zz[:, O_CCA:O_CCA + D_CC] * _sigmoid(zz[:, O_CCG:O_CCG + D_CC])

        u_s[0:h, :] = u_of(pz)
        u_s[h:h + tm, :] = u_of(zo)
        u_s[h + tm:ext, :] = u_of(nz)
        acc = jnp.zeros((mid, D_CC), F32)
        for j in range(CC_W):
            acc = acc + ccw_ref[j:j + 1, :] * u_s[pl.ds(hh + j - half, mid), :]
        xhat, rstd = _ln_stats(acc + ccb_ref[...])
        nn = xhat * ccg_ref[...] + ccbb_ref[...]
        sg = _sigmoid(nn)
        dycc_mid = jnp.concatenate([jnp.where(first, 0.0, dccp_ref[hh:h, :]), dcc_ref[...],
                                    jnp.where(last, 0.0, dccn_ref[0:hh, :])], axis=0)
        dn = dycc_mid * (sg * (1.0 + nn * (1.0 - sg)))
        dc = _ln_bwd(dn, xhat, rstd, ccg_ref[...])
        dc_s[...] = dc
        dn_own = dn[hh:hh + tm, :]
        dvec_ref[0:1, :] += jnp.sum(dc[hh:hh + tm, :], axis=0, keepdims=True)
        dvec_ref[1:2, :] += jnp.sum(dn_own * xhat[hh:hh + tm, :], axis=0, keepdims=True)
        dvec_ref[2:3, :] += jnp.sum(dn_own, axis=0, keepdims=True)
        du = jnp.zeros((tm, D_CC), F32)
        dc_own = dc[hh:hh + tm, :]
        for j in range(CC_W):
            du = du + ccw_ref[j:j + 1, :] * dc_s[pl.ds(hh + half - j, tm), :]
            dccw_ref[j:j + 1, :] += jnp.sum(dc_own * u_s[pl.ds(h + j - half, tm), :], axis=0, keepdims=True)
        gate = _sigmoid(zo[:, O_CCG:O_CCG + D_CC])
        a_own = zo[:, O_CCA:O_CCA + D_CC]
        dz_ref[:, O_CCA:O_CCA + D_CC] = (du * gate).astype(BF16)
        dz_ref[:, O_CCG:O_CCG + D_CC] = (du * a_own * gate * (1.0 - gate)).astype(BF16)

        def ch_of(zz):
            return zz[:, O_SCC:O_SCC + D_SC] * zz[:, O_SCH:O_SCH + D_SC]

        ch_s[0:h, :] = ch_of(pz)
        ch_s[h:h + tm, :] = ch_of(zo)
        ch_s[h + tm:ext, :] = ch_of(nz)
        g_s[0:h, :] = jnp.where(first, 0.0, dscp_ref[...]) * pz[:, O_SCB:O_SCB + D_SC]
        g_s[h:h + tm, :] = dsc_ref[...] * zo[:, O_SCB:O_SCB + D_SC]
        g_s[h + tm:ext, :] = jnp.where(last, 0.0, dscn_ref[...]) * nz[:, O_SCB:O_SCB + D_SC]
        conv = jnp.zeros((tm, D_SC), F32)
        dch = jnp.zeros((tm, D_SC), F32)
        g_own = g_s[h:h + tm, :]
        for j in range(SC_W):
            chj = ch_s[pl.ds(h + j - SC_W // 2, tm), :]
            conv = conv + scw_ref[j:j + 1, :] * chj
            dch = dch + scw_ref[j:j + 1, :] * g_s[pl.ds(h + SC_W // 2 - j, tm), :]
            dscw_ref[j:j + 1, :] += jnp.sum(g_own * chj, axis=0, keepdims=True)
        dz_ref[:, O_SCB:O_SCB + D_SC] = (dsc_ref[...] * conv).astype(BF16)
        dz_ref[:, O_SCC:O_SCC + D_SC] = (dch * zo[:, O_SCH:O_SCH + D_SC]).astype(BF16)
        dz_ref[:, O_SCH:O_SCH + D_SC] = (dch * zo[:, O_SCC:O_SCC + D_SC]).astype(BF16)

        dq = dq_ref[...]
        dz_ref[:, O_Q:O_Q + D_ATT] = (dq * _wide(cos_ref[...], D_ATT) + _swap_halves(dq * _wide(sin_ref[...], D_ATT))).astype(BF16)
        dk = dk1_ref[0] + jnp.where(last, 0.0, dk0_ref[0]) + jnp.where(first, 0.0, dk2_ref[0])
        dz_ref[:, O_K:O_K + 128] = (dk * cos_ref[...] + _swap_halves(dk * sin_ref[...])).astype(BF16)
        dv = dv1_ref[0] + jnp.where(last, 0.0, dv0_ref[0]) + jnp.where(first, 0.0, dv2_ref[0])
        dz_ref[:, O_V:O_V + 128] = dv.astype(BF16)

    def full(a):
        return pl.BlockSpec(a.shape, lambda i: (0, 0))

    def rows(w):
        return pl.BlockSpec((tm, w), lambda i: (i, 0))

    parts = [pl.BlockSpec((1, tm, 128), lambda i: (0, jnp.minimum(i + 1, nt - 1), 0)),
             pl.BlockSpec((1, tm, 128), lambda i: (1, i, 0)),
             pl.BlockSpec((1, tm, 128), lambda i: (2, jnp.maximum(i - 1, 0), 0))]
    acc_spec = lambda r: pl.BlockSpec((r, D_CC), lambda i: (0, 0))
    return pl.pallas_call(
        body, name="mix_bwd", grid=(nt,),
        in_specs=(_halo_specs(t, tm, h, D_IN) + _halo_specs(t, tm, h, D_SC) + _halo_specs(t, tm, h, D_CC)
                  + [rows(D_ATT)] + parts + parts
                  + [full(scw), full(ccw), full(ccb), full(ccg), full(ccbb), rows(128), rows(128)]),
        out_specs=[rows(D_IN), acc_spec(SC_W), acc_spec(CC_W), acc_spec(3)],
        out_shape=[S((t, D_IN), BF16), S((SC_W, D_SC), F32), S((CC_W, D_CC), F32), S((3, D_CC), F32)],
        scratch_shapes=[pltpu.VMEM((ext, D_CC), F32), pltpu.VMEM((mid, D_CC), F32),
                        pltpu.VMEM((ext, D_SC), F32), pltpu.VMEM((ext, D_SC), F32)],
        compiler_params=_params(("arbitrary",)),
    )(z, z, z, dysc, dysc, dysc, dycc, dycc, dycc, dqr, dkp, dkp, dkp, dvp, dvp, dvp,
      scw, ccw, ccb, ccg, ccbb, cos, sin)


def _adamw(w, g, m, v):
    m = ADAM_B1 * m + (1.0 - ADAM_B1) * g
    v = ADAM_B2 * v + (1.0 - ADAM_B2) * (g * g)
    m_hat = m / (1.0 - ADAM_B1 ** ADAM_STEP)
    v_hat = v / (1.0 - ADAM_B2 ** ADAM_STEP)
    delta = -ADAM_LR * (m_hat / (jnp.sqrt(v_hat) + ADAM_EPS) + ADAM_WD * w)
    return delta, m, v


def _row_tile(rows):
    for cand in (256, 176, 128):
        if rows % cand == 0:
            return cand
    return rows


def _sum_adam(recv, w, m, v, transposed):
    nl, rows = len(recv), recv[0].shape[1]
    tile = 256 if transposed else _row_tile(rows)
    nc = (D if transposed else rows) // tile

    def body(*refs):
        w_ref, m_ref, v_ref, g_ref, d_ref, mo_ref, vo_ref = refs[nl:]
        for layer in range(nl):
            @pl.when(pl.program_id(0) == layer)
            def _(r_ref=refs[layer]):
                g = r_ref[0].astype(F32)
                for s in range(1, N_DEV):
                    g = g + r_ref[s].astype(F32)
                if transposed:
                    g = g.T
                g_ref[0] = g
                d_ref[0], mo_ref[0], vo_ref[0] = _adamw(w_ref[0], g, m_ref[0], v_ref[0])

    def held(layer):
        def at(l, c):
            return jnp.where(l == layer, c, jnp.where(l < layer, 0, nc - 1))
        if transposed:
            return pl.BlockSpec((N_DEV, rows, tile), lambda l, c: (0, 0, at(l, c)))
        return pl.BlockSpec((N_DEV, tile, D), lambda l, c: (0, at(l, c), 0))

    if transposed:
        blk = pl.BlockSpec((1, tile, rows), lambda l, c: (l, c, 0))
    else:
        blk = pl.BlockSpec((1, tile, D), lambda l, c: (l, c, 0))
    out = S(w.shape, F32)
    return pl.pallas_call(
        body, name="sum_adam_t" if transposed else "sum_adam", grid=(nl, nc),
        in_specs=[held(layer) for layer in range(nl)] + [blk, blk, blk], out_specs=[blk] * 4, out_shape=[out] * 4,
        compiler_params=_params(("arbitrary", "arbitrary")),
    )(*recv, w, m, v)


def _small_sum(gathered):
    rows = gathered.shape[1]

    def body(g_ref, o_ref):
        acc = g_ref[0]
        for s in range(1, N_DEV):
            acc = acc + g_ref[s]
        o_ref[...] = acc

    return pl.pallas_call(
        body, name="small_sum", in_specs=[pl.BlockSpec(gathered.shape, lambda: (0, 0, 0))],
        out_specs=pl.BlockSpec((rows, 128), lambda: (0, 0)), out_shape=S((rows, 128), F32),
    )(gathered)


def _small_adam(w, g, m, v):
    def body(w_ref, g_ref, m_ref, v_ref, d_ref, mo_ref, vo_ref):
        d_ref[...], mo_ref[...], vo_ref[...] = _adamw(w_ref[...], g_ref[...], m_ref[...], v_ref[...])

    spec = pl.BlockSpec(w.shape, lambda: (0, 0))
    return pl.pallas_call(
        body, name="small_adam", in_specs=[spec] * 4, out_specs=[spec] * 3, out_shape=[S(w.shape, F32)] * 3,
    )(w, g, m, v)


def _pack(pieces):
    flat = jnp.concatenate([p.reshape(-1).astype(F32) for p in pieces])
    n = flat.shape[0]
    rows = -(-n // 1024) * 8
    return jnp.pad(flat, (0, rows * 128 - n)).reshape(rows, 128)


def _unpack(packed, shapes):
    flat = packed.reshape(-1)
    out, o = [], 0
    for shp in shapes:
        n = int(np.prod(shp))
        out.append(flat[o:o + n].reshape(shp))
        o += n
    return out


def _rope_tables(t):
    half = HEAD_DIM // 2
    inv_freq = ROPE_THETA ** (-jnp.arange(half, dtype=F32) / half)
    ang = jnp.arange(t).astype(F32)[:, None] * inv_freq[None, :]
    cos, sin = jnp.cos(ang), jnp.sin(ang)
    cos128 = jnp.concatenate([cos, cos, cos, cos], axis=1)
    sin128 = jnp.concatenate([-sin, sin, -sin, sin], axis=1)
    return cos128, sin128


BIG = ("ffn1_w_gu", "ffn1_w_down", "w_in", "w_out", "ffn2_w_gu", "ffn2_w_down")
BIG_T = {"ffn1_w_gu": True, "ffn1_w_down": False, "w_in": True, "w_out": False, "ffn2_w_gu": True, "ffn2_w_down": False}
REPLICATED = ("ln1_g", "ln1_b", "attn_sink", "cc_conv_b", "cc_ln_g", "cc_ln_b", "ln2_g", "ln2_b", "ln3_g", "ln3_b")
CONVS = ("sc_conv_w", "cc_conv_w")
WEIGHTS = ("ffn1_w_gu", "ffn1_w_down", "ln1_g", "ln1_b", "w_in", "sc_conv_w", "attn_sink", "cc_conv_w", "cc_conv_b",
           "cc_ln_g", "cc_ln_b", "w_out", "ln2_g", "ln2_b", "ffn2_w_gu", "ffn2_w_down", "ln3_g", "ln3_b")


def kernel(x, ffn1_w_gu, ffn1_w_down, ln1_g, ln1_b, w_in, sc_conv_w, attn_sink, cc_conv_w, cc_conv_b, cc_ln_g, cc_ln_b, w_out, ln2_g, ln2_b, ffn2_w_gu, ffn2_w_down, ln3_g, ln3_b, loss_target, m_ffn1_w_gu, m_ffn1_w_down, m_ln1_g, m_ln1_b, m_w_in, m_sc_conv_w, m_attn_sink, m_cc_conv_w, m_cc_conv_b, m_cc_ln_g, m_cc_ln_b, m_w_out, m_ln2_g, m_ln2_b, m_ffn2_w_gu, m_ffn2_w_down, m_ln3_g, m_ln3_b, v_ffn1_w_gu, v_ffn1_w_down, v_ln1_g, v_ln1_b, v_w_in, v_sc_conv_w, v_attn_sink, v_cc_conv_w, v_cc_conv_b, v_cc_ln_g, v_cc_ln_b, v_w_out, v_ln2_g, v_ln2_b, v_ffn2_w_gu, v_ffn2_w_down, v_ln3_g, v_ln3_b):
    args = dict(locals())
    w = {n: args[n] for n in WEIGHTS}
    mom = {n: args["m_" + n] for n in WEIGHTS}
    var = {n: args["v_" + n] for n in WEIGHTS}
    x0 = x[0]
    target = loss_target[0]
    t = x0.shape[0]
    idx = 4 * lax.axis_index("x") + 2 * lax.axis_index("y") + lax.axis_index("c")

    conv_pack = _pack([w["sc_conv_w"], w["cc_conv_w"]])
    conv_all = _all_gather([conv_pack], "gather_convs")[0].reshape(N_DEV, -1)
    n_sc = DEPTH * SC_W * 32
    scw_full = conv_all[:, :n_sc].reshape(N_DEV, DEPTH, SC_W, 32).transpose(1, 2, 0, 3).reshape(DEPTH, SC_W, D_SC)
    ccw_full = conv_all[:, n_sc:n_sc + DEPTH * CC_W * 32].reshape(N_DEV, DEPTH, CC_W, 32).transpose(1, 2, 0, 3).reshape(DEPTH, CC_W, D_CC)
    blocks = []
    for l in range(DEPTH):
        for n in BIG:
            blk = w[n][l]
            blocks.append((blk.T if BIG_T[n] else blk).astype(BF16))
    g_send, g_recv, g_srcs, g_lands, _ = _send_start(blocks, [_own_slot(b) for b in blocks], _whole, "gather_start")

    def weight(n, l, after):
        k = l * len(BIG) + BIG.index(n)
        g = _recv_wait(g_send, g_recv, [k], [g_srcs[k]], [g_lands[k]], _whole, after, f"gather_wait_{n}_{l}")[0]
        return g.reshape(N_DEV * g.shape[1], g.shape[2])

    cos, sin = _rope_tables(t)
    row = lambda a, l: a[l].reshape(1, -1)

    saved, full = [], {}
    xf, xb = x0, x0.astype(BF16)
    for l in range(DEPTH):
        sv = {"x0b": xb}
        full["ffn1_w_gu", l] = weight("ffn1_w_gu", l, xb)
        gu1, a1 = _ffn_up(xb, full["ffn1_w_gu", l])
        full["ffn1_w_down", l] = weight("ffn1_w_down", l, a1)
        r1, x1, x1b = _ffn_down_ln(a1, full["ffn1_w_down", l], xf, row(ln1_g, l), row(ln1_b, l))
        full["w_in", l] = weight("w_in", l, x1b)
        z = _proj_in(x1b, full["w_in", l])
        ysc, ycc, qr, kr, vv = _mix_fwd(z, scw_full[l], ccw_full[l], row(cc_conv_b, l), row(cc_ln_g, l), row(cc_ln_b, l), cos, sin)
        yatt = _attn_fwd(qr, kr, vv, attn_sink[l])
        full["w_out", l] = weight("w_out", l, yatt)
        ycat, r2, x2, x2b = _out_ln(ysc, yatt, ycc, full["w_out", l], x1, row(ln2_g, l), row(ln2_b, l))
        full["ffn2_w_gu", l] = weight("ffn2_w_gu", l, x2b)
        gu2, a2 = _ffn_up(x2b, full["ffn2_w_gu", l])
        full["ffn2_w_down", l] = weight("ffn2_w_down", l, a2)
        r3, x3, x3b = _ffn_down_ln(a2, full["ffn2_w_down", l], x2, row(ln3_g, l), row(ln3_b, l))
        sv.update(gu1=gu1, a1=a1, r1=r1, x1b=x1b, z=z, qr=qr, kr=kr, vv=vv, yatt=yatt, ycat=ycat, r2=r2, x2b=x2b,
                  gu2=gu2, a2=a2, r3=r3)
        saved.append(sv)
        xf, xb = x3, x3b

    dy, sq = _loss_head(xf, target)
    loss = lax.psum(0.5 * jnp.sum(sq) / D, ("x", "y", "c"))

    sent = []
    small = {n: [None] * DEPTH for n in REPLICATED + CONVS}

    def send_grads(names, l, gs):
        srcs = [g.reshape(N_DEV, g.shape[0] // N_DEV, g.shape[1]) for g in gs]
        lands = [_own_slot(lax.dynamic_index_in_dim(s3, idx, 0, keepdims=False)) for s3 in srcs]
        started = _send_start(srcs, lands, _block_of, f"grads_start_{names[0]}_{l}")
        sent.append((names, l, started))
        return started[-1]

    for l in reversed(range(DEPTH)):
        sv = saved[l]
        dr, dfb, dh, dg, db = _ffn_bwd(dy, sv["r3"], row(ln3_g, l), full["ffn2_w_down", l], sv["gu2"])
        small["ln3_g"][l], small["ln3_b"][l] = dg, db
        token = send_grads(("ffn2_w_down", "ffn2_w_gu"), l,
                           [_wgrad(sv["a2"], dfb, F // 2), _wgrad(dh, sv["x2b"], F // 2)])
        dy = _dx(dr, dh, full["ffn2_w_gu", l], token)

        dr, dmb, dysc, dyatt, dycc, dg, db = _out_bwd(dy, sv["r2"], row(ln2_g, l), full["w_out", l])
        small["ln2_g"][l], small["ln2_b"][l] = dg, db
        g_out = _wgrad(sv["ycat"], dmb, D)
        dqr, dkp, dvp, dsink = _attn_bwd(sv["qr"], sv["kr"], sv["vv"], attn_sink[l], dyatt, sv["yatt"])
        small["attn_sink"][l] = dsink[:, 0]
        dz, dscw, dccw, dvec = _mix_bwd(sv["z"], dysc, dycc, dqr, dkp, dvp, scw_full[l], ccw_full[l],
                                        row(cc_conv_b, l), row(cc_ln_g, l), row(cc_ln_b, l), cos, sin)
        small["sc_conv_w"][l], small["cc_conv_w"][l] = dscw, dccw
        small["cc_conv_b"][l], small["cc_ln_g"][l], small["cc_ln_b"][l] = dvec[0], dvec[1], dvec[2]
        token = send_grads(("w_out", "w_in"), l, [g_out, _wgrad(dz, sv["x1b"], D)])
        dy = _dx(dr, dz, full["w_in", l], token)

        dr, dfb, dh, dg, db = _ffn_bwd(dy, sv["r1"], row(ln1_g, l), full["ffn1_w_down", l], sv["gu1"])
        small["ln1_g"][l], small["ln1_b"][l] = dg, db
        token = send_grads(("ffn1_w_down", "ffn1_w_gu"), l,
                           [_wgrad(sv["a1"], dfb, F // 2), _wgrad(dh, sv["x0b"], F // 2)])
        dy = _dx(dr, dh, full["ffn1_w_gu", l], token)
    grad_x = dy[None]

    small_names = REPLICATED + CONVS
    small_shapes = [(DEPTH,) + tuple(np.shape(small[n][0].reshape(-1))) for n in small_names]
    small_pack = _pack([jnp.stack([small[n][l].reshape(-1) for l in range(DEPTH)]) for n in small_names])
    small_all = _all_gather([small_pack], "gather_small_grads")[0]

    recv = {n: [None] * DEPTH for n in BIG}
    for names, l, (send, rcv, srcs, lands, _) in sent:
        got = _recv_wait(send, rcv, list(range(len(names))), srcs, lands, _block_of, dy, f"grads_wait_{names[0]}_{l}")
        for n, g in zip(names, got):
            recv[n][l] = g
    grads, deltas, new_m, new_v = {}, {}, {}, {}
    for n in BIG:
        grads[n], deltas[n], new_m[n], new_v[n] = _sum_adam(recv[n], w[n], mom[n], var[n], BIG_T[n])
    small_total = _unpack(_small_sum(small_all), small_shapes)
    for n, g in zip(small_names, small_total):
        if n in CONVS:
            taps = SC_W if n == "sc_conv_w" else CC_W
            g = lax.dynamic_slice_in_dim(g.reshape(DEPTH, taps, D_SC), idx * 32, 32, axis=2)
        grads[n] = g.reshape(w[n].shape)
    wp = _pack([w[n] for n in small_names])
    gp = _pack([grads[n] for n in small_names])
    mp = _pack([mom[n] for n in small_names])
    vp = _pack([var[n] for n in small_names])
    shapes = [w[n].shape for n in small_names]
    for dst, packed in zip((deltas, new_m, new_v), _small_adam(wp, gp, mp, vp)):
        for n, a in zip(small_names, _unpack(packed, shapes)):
            dst[n] = a

    return (loss, grad_x, *[grads[n] for n in WEIGHTS], *[deltas[n] for n in WEIGHTS],
            *[new_m[n] for n in WEIGHTS], *[new_v[n] for n in WEIGHTS])
```

```python
import functools

import jax
import jax.numpy as jnp
import numpy as np
from jax import lax
from jax.experimental import pallas as pl
from jax.experimental.pallas import tpu as pltpu

F32 = jnp.float32
BF16 = jnp.bfloat16
S = jax.ShapeDtypeStruct

N_DEV = 8
DEPTH = 2
D = 1024
F = 2816
D_IN = 2048
HEAD_DIM = 64
N_Q_HEADS = 8
N_KV_HEADS = 2
GROUP = 4
D_SC = 256
D_ATT = 512
D_CC = 256
CC_W = 31
SC_W = 3
BLOCK = 128
ROPE_THETA = 10000.0
LN_EPS = 1e-5
ALPHA = (2.0 * DEPTH) ** 0.25
ADAM_LR = 0.001
ADAM_B1 = 0.9
ADAM_B2 = 0.999
ADAM_EPS = 1e-08
ADAM_WD = 0.01
ADAM_STEP = 10

O_SCB, O_SCC, O_SCH, O_Q, O_K, O_V, O_CCA, O_CCG = 0, 256, 512, 768, 1280, 1408, 1536, 1792

V7X_VMEM_BYTES = 64 * 1024 * 1024
VMEM_LIMIT = V7X_VMEM_BYTES - 8 * 1024 * 1024
TOKEN_TILE = 256
MIX_BWD_TILE = 128
HALO_FWD = 16
HALO_BWD = 32

NT = (((1,), (1,)), ((), ()))
TN = (((0,), (0,)), ((), ()))
MESH = pl.DeviceIdType.MESH


def _params(sem=None):
    return pltpu.CompilerParams(dimension_semantics=sem, vmem_limit_bytes=VMEM_LIMIT)


def _sigmoid(v):
    return 1.0 / (1.0 + jnp.exp(-v))


def _ln_stats(r):
    mu = jnp.mean(r, axis=-1, keepdims=True)
    d = r - mu
    var = jnp.mean(d * d, axis=-1, keepdims=True)
    rstd = lax.rsqrt(var + LN_EPS)
    return d * rstd, rstd


def _ln_bwd(dn, xhat, rstd, gam):
    dxh = dn * gam
    return rstd * (dxh - jnp.mean(dxh, axis=-1, keepdims=True) - xhat * jnp.mean(dxh * xhat, axis=-1, keepdims=True))


def _swap_halves(v):
    n = v.shape[-1]
    lane = lax.broadcasted_iota(jnp.int32, v.shape, v.ndim - 1) % HEAD_DIM
    return jnp.where(lane < HEAD_DIM // 2, pltpu.roll(v, n - HEAD_DIM // 2, v.ndim - 1), pltpu.roll(v, HEAD_DIM // 2, v.ndim - 1))


def _wide(tab, n):
    return tab if n == 128 else jnp.concatenate([tab] * (n // 128), axis=1)


def _me():
    x, y, c = lax.axis_index("x"), lax.axis_index("y"), lax.axis_index("c")
    return x, y, c


def _peer(rel):
    x, y, c = _me()
    px = 1 - x if rel & 4 else x
    py = 1 - y if rel & 2 else y
    pc = 1 - c if rel & 1 else c
    return (px, py, pc), 4 * px + 2 * py + pc


def _exchange(srcs, dsts_shape, dst_index, src_of, dst_of, name):
    n = len(srcs)

    def body(*refs):
        ins = refs[:n]
        outs = [refs[n + dst_index[k]] for k in range(n)]
        send, recv, lsem = refs[n + len(dsts_shape):]
        x, y, c = _me()
        me = 4 * x + 2 * y + c
        local = [pltpu.make_async_copy(src_of(ins[k], k, me), dst_of(outs[k], k, me), lsem.at[k]) for k in range(n)]
        for cp in local:
            cp.start()
        sends, recvs = [], []
        for k in range(n):
            for rel in range(1, N_DEV):
                peer, pidx = _peer(rel)
                sends.append(pltpu.make_async_remote_copy(
                    src_ref=src_of(ins[k], k, pidx), dst_ref=dst_of(outs[k], k, me),
                    send_sem=send.at[k, rel - 1], recv_sem=recv.at[k, rel - 1], device_id=peer, device_id_type=MESH))
                recvs.append(pltpu.make_async_remote_copy(
                    src_ref=src_of(ins[k], k, pidx), dst_ref=dst_of(outs[k], k, pidx),
                    send_sem=send.at[k, rel - 1], recv_sem=recv.at[k, rel - 1], device_id=peer, device_id_type=MESH))
        for cp in sends:
            cp.start()
        for cp in recvs:
            cp.wait_recv()
        for cp in sends:
            cp.wait_send()
        for cp in local:
            cp.wait()

    hbm = pl.BlockSpec(memory_space=pltpu.HBM)
    return pl.pallas_call(
        body, name=name, in_specs=[hbm] * n, out_specs=[hbm] * len(dsts_shape), out_shape=dsts_shape,
        scratch_shapes=[pltpu.SemaphoreType.DMA((n, N_DEV - 1)), pltpu.SemaphoreType.DMA((n, N_DEV - 1)),
                        pltpu.SemaphoreType.DMA((n,))],
    )(*srcs)


def _all_gather(blocks, name):
    shapes = [S((N_DEV,) + b.shape, b.dtype) for b in blocks]
    return _exchange(blocks, shapes, list(range(len(blocks))), lambda ref, k, idx: ref, lambda ref, k, idx: ref.at[idx], name)


HBM_SPEC = pl.BlockSpec(memory_space=pltpu.HBM)
SEM_SPEC = pl.BlockSpec(memory_space=pltpu.SEMAPHORE)
ANY_SPEC = pl.BlockSpec(memory_space=pl.ANY)
EFFECT = pltpu.SideEffectType.DATAFLOW_SIDE_EFFECTING
N_PEERS = N_DEV - 1


def _own_slot(block):
    x, y, c = _me()
    return lax.dynamic_update_index_in_dim(lax.empty((N_DEV,) + block.shape, block.dtype), block, 4 * x + 2 * y + c, 0)


def _follow(body, n_in, in_specs, operands, after):
    if after is None:
        return body, list(in_specs), list(operands)

    def tail(*refs):
        return body(*refs[:n_in], *refs[n_in + 1:])

    return tail, list(in_specs) + [ANY_SPEC], list(operands) + [after]


def _send_start(srcs, lands, src_of, name, after):
    n = len(srcs)

    def body(*refs):
        ins, zones = refs[:n], refs[n:2 * n]
        send, recv = refs[2 * n + 1], refs[2 * n + 2]
        token = refs[-1]
        x, y, c = _me()
        me = 4 * x + 2 * y + c
        for k in range(n):
            for rel in range(1, N_DEV):
                peer, pidx = _peer(rel)
                pltpu.make_async_remote_copy(
                    src_ref=src_of(ins[k], pidx), dst_ref=zones[k].at[me],
                    send_sem=send.at[k * N_PEERS + rel - 1], recv_sem=recv.at[k * N_PEERS + rel - 1],
                    device_id=peer, device_id_type=MESH).start()
        token[...] = jnp.zeros_like(token)

    outs = pl.pallas_call(
        body, name=name,
        out_shape=(pltpu.SemaphoreType.DMA((n * N_PEERS,)), pltpu.SemaphoreType.DMA((n * N_PEERS,)),
                   *[pltpu.HBM(a.shape, a.dtype) for a in srcs], *[pltpu.HBM(a.shape, a.dtype) for a in lands],
                   S((8, 128), F32)),
        in_specs=[HBM_SPEC] * (2 * n) + [ANY_SPEC],
        out_specs=(SEM_SPEC, SEM_SPEC, *[HBM_SPEC] * (2 * n), pl.BlockSpec(memory_space=pltpu.VMEM)),
        input_output_aliases={i: 2 + i for i in range(2 * n)},
        compiler_params=pltpu.CompilerParams(has_side_effects=EFFECT),
    )(*[pltpu.with_memory_space_constraint(a, pltpu.HBM) for a in list(srcs) + list(lands)], after)
    return outs[0], outs[1], list(outs[2:2 + n]), list(outs[2 + n:2 + 2 * n]), outs[-1]


def _recv_wait(send, recv, ks, srcs, lands, src_of, after, name):
    n = len(ks)

    def body(*refs):
        ins, zones = refs[:n], refs[n:2 * n]
        send_sems, recv_sems = refs[2 * n], refs[2 * n + 1]
        for j, k in enumerate(ks):
            for rel in range(1, N_DEV):
                peer, pidx = _peer(rel)
                cp = pltpu.make_async_remote_copy(
                    src_ref=src_of(ins[j], pidx), dst_ref=zones[j].at[pidx],
                    send_sem=send_sems.at[k * N_PEERS + rel - 1], recv_sem=recv_sems.at[k * N_PEERS + rel - 1],
                    device_id=peer, device_id_type=MESH)
                cp.wait_send()
                cp.wait_recv()

    outs = pl.pallas_call(
        body, name=name,
        out_shape=(*[pltpu.HBM(a.shape, a.dtype) for a in srcs], *[pltpu.HBM(a.shape, a.dtype) for a in lands]),
        in_specs=[HBM_SPEC] * (2 * n) + [SEM_SPEC, SEM_SPEC, ANY_SPEC], out_specs=[HBM_SPEC] * (2 * n),
        input_output_aliases={i: i for i in range(2 * n)},
        compiler_params=pltpu.CompilerParams(has_side_effects=EFFECT),
    )(*srcs, *lands, send, recv, after)
    return list(outs[n:])


def _whole(ref, idx):
    return ref


def _block_of(ref, idx):
    return ref.at[idx]


def _ffn_up(xb, wgut, after=None):
    t = xb.shape[0]
    tm = min(TOKEN_TILE, t)
    half = F // 2

    def body(x_ref, w_ref, gu_ref, a_ref):
        x = x_ref[...]
        for ch in range(2):
            lo = ch * half
            g = lax.dot_general(x, w_ref[lo:lo + half, :], NT, preferred_element_type=F32)
            u = lax.dot_general(x, w_ref[F + lo:F + lo + half, :], NT, preferred_element_type=F32)
            gu_ref[:, lo:lo + half] = g.astype(BF16)
            gu_ref[:, F + lo:F + lo + half] = u.astype(BF16)
            a_ref[:, lo:lo + half] = (g * _sigmoid(g) * u).astype(BF16)

    body, in_specs, operands = _follow(
        body, 2, [pl.BlockSpec((tm, D), lambda i: (i, 0)), pl.BlockSpec((2 * F, D), lambda i: (0, 0))], [xb, wgut], after)
    return pl.pallas_call(
        body, name="ffn_up", grid=(t // tm,), in_specs=in_specs,
        out_specs=[pl.BlockSpec((tm, 2 * F), lambda i: (i, 0)), pl.BlockSpec((tm, F), lambda i: (i, 0))],
        out_shape=[S((t, 2 * F), BF16), S((t, F), BF16)], compiler_params=_params(("parallel",)),
    )(*operands)


def _residual_ln_out(x_ref, f, scale, g_ref, b_ref, r_ref, y_ref, yb_ref):
    r = ALPHA * x_ref[...] + scale * f
    xhat, _ = _ln_stats(r)
    y = xhat * g_ref[...] + b_ref[...]
    r_ref[...] = r
    y_ref[...] = y
    yb_ref[...] = y.astype(BF16)


def _ffn_down_ln(a, wd, x, gam, bet, after=None):
    t = x.shape[0]
    tm = min(TOKEN_TILE, t)

    def body(a_ref, w_ref, x_ref, g_ref, b_ref, r_ref, y_ref, yb_ref):
        f = jnp.dot(a_ref[...], w_ref[...], preferred_element_type=F32)
        _residual_ln_out(x_ref, f, 0.5, g_ref, b_ref, r_ref, y_ref, yb_ref)

    row = pl.BlockSpec((tm, D), lambda i: (i, 0))
    vec = pl.BlockSpec((1, D), lambda i: (0, 0))
    body, in_specs, operands = _follow(
        body, 5, [pl.BlockSpec((tm, F), lambda i: (i, 0)), pl.BlockSpec((F, D), lambda i: (0, 0)), row, vec, vec],
        [a, wd, x, gam, bet], after)
    return pl.pallas_call(
        body, name="ffn_down_ln", grid=(t // tm,), in_specs=in_specs,
        out_specs=[row, row, row], out_shape=[S((t, D), F32), S((t, D), F32), S((t, D), BF16)],
        compiler_params=_params(("parallel",)),
    )(*operands)


def _proj_in(xb, wint):
    t = xb.shape[0]
    tm = min(TOKEN_TILE, t)

    def body(x_ref, w_ref, z_ref):
        z_ref[...] = lax.dot_general(x_ref[...], w_ref[...], NT, preferred_element_type=F32)

    return pl.pallas_call(
        body, name="proj_in", grid=(t // tm,),
        in_specs=[pl.BlockSpec((tm, D), lambda i: (i, 0)), pl.BlockSpec((D_IN, D), lambda i: (0, 0))],
        out_specs=pl.BlockSpec((tm, D_IN), lambda i: (i, 0)), out_shape=S((t, D_IN), F32),
        compiler_params=_params(("parallel",)),
    )(xb, wint)


def _halo_specs(t, tm, halo, width):
    per = tm // halo
    last = t // halo - 1
    return [pl.BlockSpec((tm, width), lambda i: (i, 0)),
            pl.BlockSpec((halo, width), lambda i: (jnp.maximum(i * per - 1, 0), 0)),
            pl.BlockSpec((halo, width), lambda i: (jnp.minimum((i + 1) * per, last), 0))]


def _mix_fwd(z, scw, ccw, ccb, ccg, ccbb, cos, sin):
    t = z.shape[0]
    tm = min(TOKEN_TILE, t)
    nt = t // tm
    h = HALO_FWD
    rc = 64

    def body(z_ref, zp_ref, zn_ref, scw_ref, ccw_ref, ccb_ref, ccg_ref, ccbb_ref, cos_ref, sin_ref,
             ysc_ref, ycc_ref, q_ref, k_ref, v_ref, u_s, ch_s):
        i = pl.program_id(0)
        pz = jnp.where(i == 0, 0.0, zp_ref[...])
        nz = jnp.where(i == nt - 1, 0.0, zn_ref[...])

        def u_of(zz):
            return zz[:, O_CCA:O_CCA + D_CC] * _sigmoid(zz[:, O_CCG:O_CCG + D_CC])

        def ch_of(zz):
            return zz[:, O_SCC:O_SCC + D_SC] * zz[:, O_SCH:O_SCH + D_SC]

        u_s[0:h, :] = u_of(pz)
        u_s[h:h + tm, :] = z_ref[:, O_CCA:O_CCA + D_CC] * _sigmoid(z_ref[:, O_CCG:O_CCG + D_CC])
        u_s[h + tm:2 * h + tm, :] = u_of(nz)
        ch_s[0:h, :] = ch_of(pz)
        ch_s[h:h + tm, :] = z_ref[:, O_SCC:O_SCC + D_SC] * z_ref[:, O_SCH:O_SCH + D_SC]
        ch_s[h + tm:2 * h + tm, :] = ch_of(nz)
        for r0 in range(0, tm, rc):
            acc = jnp.zeros((rc, D_CC), F32)
            for j in range(CC_W):
                acc = acc + ccw_ref[j:j + 1, :] * u_s[pl.ds(r0 + h + j - CC_W // 2, rc), :]
            xhat, _ = _ln_stats(acc + ccb_ref[...])
            n = xhat * ccg_ref[...] + ccbb_ref[...]
            ycc_ref[r0:r0 + rc, :] = (n * _sigmoid(n)).astype(BF16)
            acc = jnp.zeros((rc, D_SC), F32)
            for j in range(SC_W):
                acc = acc + scw_ref[j:j + 1, :] * ch_s[pl.ds(r0 + h + j - SC_W // 2, rc), :]
            ysc_ref[r0:r0 + rc, :] = (z_ref[r0:r0 + rc, O_SCB:O_SCB + D_SC] * acc).astype(BF16)
        q = z_ref[:, O_Q:O_Q + D_ATT]
        q_ref[...] = (q * _wide(cos_ref[...], D_ATT) + _swap_halves(q) * _wide(sin_ref[...], D_ATT)).astype(BF16)
        k = z_ref[:, O_K:O_K + 128]
        k_ref[...] = (k * cos_ref[...] + _swap_halves(k) * sin_ref[...]).astype(BF16)
        v_ref[...] = z_ref[:, O_V:O_V + 128].astype(BF16)

    def full(a):
        return pl.BlockSpec(a.shape, lambda i: (0, 0))

    def rows(w):
        return pl.BlockSpec((tm, w), lambda i: (i, 0))

    return pl.pallas_call(
        body, name="mix_fwd", grid=(nt,),
        in_specs=_halo_specs(t, tm, h, D_IN) + [full(scw), full(ccw), full(ccb), full(ccg), full(ccbb), rows(128), rows(128)],
        out_specs=[rows(D_SC), rows(D_CC), rows(D_ATT), rows(128), rows(128)],
        out_shape=[S((t, D_SC), BF16), S((t, D_CC), BF16), S((t, D_ATT), BF16), S((t, 128), BF16), S((t, 128), BF16)],
        scratch_shapes=[pltpu.VMEM((tm + 2 * h, D_CC), F32), pltpu.VMEM((tm + 2 * h, D_SC), F32)],
        compiler_params=_params(("parallel",)),
    )(z, z, z, scw, ccw, ccb, ccg, ccbb, cos, sin)


def _band_specs(nb, width):
    return [pl.BlockSpec((BLOCK, width), lambda n: (jnp.maximum(n - 1, 0), 0)),
            pl.BlockSpec((BLOCK, width), lambda n: (n, 0)),
            pl.BlockSpec((BLOCK, width), lambda n: (jnp.minimum(n + 1, nb - 1), 0))]


def _band_valid(n, nb):
    rows = GROUP * BLOCK
    qpos = lax.broadcasted_iota(jnp.int32, (rows, 3 * BLOCK), 0) % BLOCK
    col = lax.broadcasted_iota(jnp.int32, (rows, 3 * BLOCK), 1)
    ok = jnp.abs(qpos - (col - BLOCK)) <= BLOCK
    ok = jnp.logical_and(ok, jnp.logical_or(col >= BLOCK, n > 0))
    return jnp.logical_and(ok, jnp.logical_or(col < 2 * BLOCK, n < nb - 1))


def _stack_heads(ref, kvh):
    return jnp.concatenate([ref[:, (GROUP * kvh + g) * HEAD_DIM:(GROUP * kvh + g + 1) * HEAD_DIM] for g in range(GROUP)], axis=0)


def _band_cat(refs, kvh):
    return jnp.concatenate([r[:, kvh * HEAD_DIM:(kvh + 1) * HEAD_DIM] for r in refs], axis=0)


def _sink_col(sink_ref, kvh):
    return jnp.concatenate([jnp.full((BLOCK, 1), sink_ref[GROUP * kvh + g], F32) for g in range(GROUP)], axis=0)


def _attn_probs(q_ref, k_refs, sink_ref, kvh, valid):
    qs = _stack_heads(q_ref, kvh)
    kc = _band_cat(k_refs, kvh)
    s = lax.dot_general(qs, kc, NT, preferred_element_type=F32) * (HEAD_DIM ** -0.5)
    s = jnp.where(valid, s, -1e30)
    sk = _sink_col(sink_ref, kvh)
    m = jnp.maximum(jnp.max(s, axis=-1, keepdims=True), sk)
    p = jnp.exp(s - m)
    ps = jnp.exp(sk - m)
    denom = jnp.sum(p, axis=-1, keepdims=True) + ps
    return qs, kc, p, ps, denom


def _attn_fwd(qr, kr, vv, sink, after=None):
    t = qr.shape[0]
    nb = t // BLOCK

    def body(q_ref, kp_ref, ko_ref, kn_ref, vp_ref, vo_ref, vn_ref, sink_ref, o_ref):
        n = pl.program_id(0)
        valid = _band_valid(n, nb)
        for kvh in range(N_KV_HEADS):
            _, _, p, _, denom = _attn_probs(q_ref, (kp_ref, ko_ref, kn_ref), sink_ref, kvh, valid)
            vc = _band_cat((vp_ref, vo_ref, vn_ref), kvh)
            o = jnp.dot(p.astype(BF16), vc, preferred_element_type=F32) / denom
            for g in range(GROUP):
                hh = GROUP * kvh + g
                o_ref[:, hh * HEAD_DIM:(hh + 1) * HEAD_DIM] = o[g * BLOCK:(g + 1) * BLOCK, :].astype(BF16)

    qspec = pl.BlockSpec((BLOCK, D_ATT), lambda n: (n, 0))
    body, in_specs, operands = _follow(
        body, 8, [qspec] + _band_specs(nb, 128) + _band_specs(nb, 128) + [pl.BlockSpec(memory_space=pltpu.SMEM)],
        [qr, kr, kr, kr, vv, vv, vv, sink], after)
    return pl.pallas_call(
        body, name="attn_fwd", grid=(nb,), in_specs=in_specs,
        out_specs=qspec, out_shape=S((t, D_ATT), BF16), compiler_params=_params(("parallel",)),
    )(*operands)


def _out_ln(ysc, yatt, ycc, wout, x, gam, bet):
    t = x.shape[0]
    tm = min(TOKEN_TILE, t)

    def body(sc_ref, at_ref, cc_ref, w_ref, x_ref, g_ref, b_ref, cat_ref, r_ref, y_ref, yb_ref):
        cat = jnp.concatenate([sc_ref[...], at_ref[...], cc_ref[...]], axis=1)
        cat_ref[...] = cat
        f = jnp.dot(cat, w_ref[...], preferred_element_type=F32)
        _residual_ln_out(x_ref, f, 1.0, g_ref, b_ref, r_ref, y_ref, yb_ref)

    def rows(w):
        return pl.BlockSpec((tm, w), lambda i: (i, 0))

    vec = pl.BlockSpec((1, D), lambda i: (0, 0))
    return pl.pallas_call(
        body, name="out_ln", grid=(t // tm,),
        in_specs=[rows(D_SC), rows(D_ATT), rows(D_CC), pl.BlockSpec((D, D), lambda i: (0, 0)), rows(D), vec, vec],
        out_specs=[rows(D), rows(D), rows(D), rows(D)],
        out_shape=[S((t, D), BF16), S((t, D), F32), S((t, D), F32), S((t, D), BF16)],
        compiler_params=_params(("parallel",)),
    )(ysc, yatt, ycc, wout, x, gam, bet)


def _loss_head(y, target):
    t = y.shape[0]
    tm = min(TOKEN_TILE, t)

    def body(y_ref, t_ref, dy_ref, part_ref):
        e = y_ref[...] - t_ref[...]
        dy_ref[...] = e / D

        @pl.when(pl.program_id(0) == 0)
        def _():
            part_ref[...] = jnp.zeros_like(part_ref)

        part_ref[...] += jnp.sum(e * e, axis=0, keepdims=True)

    row = pl.BlockSpec((tm, D), lambda i: (i, 0))
    return pl.pallas_call(
        body, name="loss_head", grid=(t // tm,), in_specs=[row, row],
        out_specs=[row, pl.BlockSpec((1, D), lambda i: (0, 0))], out_shape=[S((t, D), F32), S((1, D), F32)],
        compiler_params=_params(("arbitrary",)),
    )(y, target)


def _ln_bwd_block(dy_ref, r_ref, g_ref, dgam_ref, dbet_ref):
    xhat, rstd = _ln_stats(r_ref[...])
    dy = dy_ref[...]

    @pl.when(pl.program_id(0) == 0)
    def _():
        dgam_ref[...] = jnp.zeros_like(dgam_ref)
        dbet_ref[...] = jnp.zeros_like(dbet_ref)

    dgam_ref[...] += jnp.sum(dy * xhat, axis=0, keepdims=True)
    dbet_ref[...] += jnp.sum(dy, axis=0, keepdims=True)
    return _ln_bwd(dy, xhat, rstd, g_ref[...])


def _ffn_bwd(dy, r, gam, wd, gu):
    t = dy.shape[0]
    tm = min(TOKEN_TILE, t)
    half = F // 2

    def body(dy_ref, r_ref, g_ref, w_ref, gu_ref, dr_ref, df_ref, dh_ref, dgam_ref, dbet_ref):
        dr = _ln_bwd_block(dy_ref, r_ref, g_ref, dgam_ref, dbet_ref)
        dr_ref[...] = dr
        dfb = (0.5 * dr).astype(BF16)
        df_ref[...] = dfb
        for ch in range(2):
            lo = ch * half
            da = lax.dot_general(dfb, w_ref[lo:lo + half, :], NT, preferred_element_type=F32)
            g = gu_ref[:, lo:lo + half].astype(F32)
            u = gu_ref[:, F + lo:F + lo + half].astype(F32)
            sg = _sigmoid(g)
            dh_ref[:, lo:lo + half] = (da * u * (sg * (1.0 + g * (1.0 - sg)))).astype(BF16)
            dh_ref[:, F + lo:F + lo + half] = (da * (g * sg)).astype(BF16)

    row = pl.BlockSpec((tm, D), lambda i: (i, 0))
    vec = pl.BlockSpec((1, D), lambda i: (0, 0))
    wide = pl.BlockSpec((tm, 2 * F), lambda i: (i, 0))
    return pl.pallas_call(
        body, name="ffn_bwd", grid=(t // tm,),
        in_specs=[row, row, vec, pl.BlockSpec((F, D), lambda i: (0, 0)), wide],
        out_specs=[row, row, wide, vec, vec],
        out_shape=[S((t, D), F32), S((t, D), BF16), S((t, 2 * F), BF16), S((1, D), F32), S((1, D), F32)],
        compiler_params=_params(("arbitrary",)),
    )(dy, r, gam, wd, gu)


def _dx(dr, dh, w, after=None):
    t = dr.shape[0]
    tm = min(TOKEN_TILE, t)
    kk = dh.shape[1]

    def body(dr_ref, dh_ref, w_ref, o_ref):
        o_ref[...] = ALPHA * dr_ref[...] + jnp.dot(dh_ref[...], w_ref[...], preferred_element_type=F32)

    row = pl.BlockSpec((tm, D), lambda i: (i, 0))
    body, in_specs, operands = _follow(
        body, 3, [row, pl.BlockSpec((tm, kk), lambda i: (i, 0)), pl.BlockSpec((kk, D), lambda i: (0, 0))], [dr, dh, w], after)
    return pl.pallas_call(
        body, name="dx", grid=(t // tm,), in_specs=in_specs,
        out_specs=row, out_shape=S((t, D), F32), compiler_params=_params(("parallel",)),
    )(*operands)


def _wgrad(a, b, ta, after=None):
    t, ka = a.shape
    tk = min(1024, t)
    nk = t // tk

    def body(a_ref, b_ref, o_ref, acc):
        k = pl.program_id(1)

        @pl.when(k == 0)
        def _():
            acc[...] = jnp.zeros_like(acc)

        acc[...] += lax.dot_general(a_ref[...], b_ref[...], TN, preferred_element_type=F32)

        @pl.when(k == nk - 1)
        def _():
            o_ref[...] = acc[...].astype(BF16)

    body, in_specs, operands = _follow(
        body, 2, [pl.BlockSpec((tk, ta), lambda i, k: (k, i)), pl.BlockSpec((tk, D), lambda i, k: (k, 0))], [a, b], after)
    return pl.pallas_call(
        body, name="wgrad", grid=(ka // ta, nk), in_specs=in_specs,
        out_specs=pl.BlockSpec((ta, D), lambda i, k: (i, 0)), out_shape=S((ka, D), BF16),
        scratch_shapes=[pltpu.VMEM((ta, D), F32)], compiler_params=_params(("parallel", "arbitrary")),
    )(*operands)


def _out_bwd(dy, r, gam, wout):
    t = dy.shape[0]
    tm = min(TOKEN_TILE, t)

    def body(dy_ref, r_ref, g_ref, w_ref, dr_ref, dm_ref, dsc_ref, dat_ref, dcc_ref, dgam_ref, dbet_ref):
        dr = _ln_bwd_block(dy_ref, r_ref, g_ref, dgam_ref, dbet_ref)
        dr_ref[...] = dr
        dmb = dr.astype(BF16)
        dm_ref[...] = dmb
        dcat = lax.dot_general(dmb, w_ref[...], NT, preferred_element_type=F32)
        dsc_ref[...] = dcat[:, 0:D_SC]
        dat_ref[...] = dcat[:, D_SC:D_SC + D_ATT]
        dcc_ref[...] = dcat[:, D_SC + D_ATT:D]

    def rows(w):
        return pl.BlockSpec((tm, w), lambda i: (i, 0))

    vec = pl.BlockSpec((1, D), lambda i: (0, 0))
    return pl.pallas_call(
        body, name="out_bwd", grid=(t // tm,),
        in_specs=[rows(D), rows(D), vec, pl.BlockSpec((D, D), lambda i: (0, 0))],
        out_specs=[rows(D), rows(D), rows(D_SC), rows(D_ATT), rows(D_CC), vec, vec],
        out_shape=[S((t, D), F32), S((t, D), BF16), S((t, D_SC), F32), S((t, D_ATT), F32), S((t, D_CC), F32),
                   S((1, D), F32), S((1, D), F32)],
        compiler_params=_params(("arbitrary",)),
    )(dy, r, gam, wout)


def _attn_bwd(qr, kr, vv, sink, do, yatt):
    t = qr.shape[0]
    nb = t // BLOCK

    def body(q_ref, kp_ref, ko_ref, kn_ref, vp_ref, vo_ref, vn_ref, sink_ref, do_ref, o_ref,
             dq_ref, dk_ref, dv_ref, dsink_ref):
        n = pl.program_id(0)
        valid = _band_valid(n, nb)

        @pl.when(n == 0)
        def _():
            dsink_ref[...] = jnp.zeros_like(dsink_ref)

        for kvh in range(N_KV_HEADS):
            qs, kc, p, ps, denom = _attn_probs(q_ref, (kp_ref, ko_ref, kn_ref), sink_ref, kvh, valid)
            vc = _band_cat((vp_ref, vo_ref, vn_ref), kvh)
            dos = _stack_heads(do_ref, kvh)
            os_ = _stack_heads(o_ref, kvh).astype(F32)
            dd = jnp.sum(dos * os_, axis=-1, keepdims=True) / denom
            dou = (dos / denom).astype(BF16)
            dp = lax.dot_general(dou, vc, NT, preferred_element_type=F32)
            ds = (p * (dp - dd) * (HEAD_DIM ** -0.5)).astype(BF16)
            dq = jnp.dot(ds, kc, preferred_element_type=F32)
            dk = lax.dot_general(ds, qs, TN, preferred_element_type=F32)
            dv = lax.dot_general(p.astype(BF16), dou, TN, preferred_element_type=F32)
            dsk = ps * dd
            for g in range(GROUP):
                hh = GROUP * kvh + g
                dq_ref[:, hh * HEAD_DIM:(hh + 1) * HEAD_DIM] = dq[g * BLOCK:(g + 1) * BLOCK, :]
                dsink_ref[hh:hh + 1, :] += jnp.zeros((1, 128), F32) - jnp.sum(dsk[g * BLOCK:(g + 1) * BLOCK, :])
            for j in range(3):
                dk_ref[j, :, kvh * HEAD_DIM:(kvh + 1) * HEAD_DIM] = dk[j * BLOCK:(j + 1) * BLOCK, :]
                dv_ref[j, :, kvh * HEAD_DIM:(kvh + 1) * HEAD_DIM] = dv[j * BLOCK:(j + 1) * BLOCK, :]

    qspec = pl.BlockSpec((BLOCK, D_ATT), lambda n: (n, 0))
    part = pl.BlockSpec((3, BLOCK, 128), lambda n: (0, n, 0))
    return pl.pallas_call(
        body, name="attn_bwd", grid=(nb,),
        in_specs=[qspec] + _band_specs(nb, 128) + _band_specs(nb, 128) + [pl.BlockSpec(memory_space=pltpu.SMEM), qspec, qspec],
        out_specs=[qspec, part, part, pl.BlockSpec((N_Q_HEADS, 128), lambda n: (0, 0))],
        out_shape=[S((t, D_ATT), F32), S((3, t, 128), F32), S((3, t, 128), F32), S((N_Q_HEADS, 128), F32)],
        compiler_params=_params(("arbitrary",)),
    )(qr, kr, kr, kr, vv, vv, vv, sink, do, yatt)


def _mix_bwd(z, dysc, dycc, dqr, dkp, dvp, scw, ccw, ccb, ccg, ccbb, cos, sin):
    t = z.shape[0]
    tm = min(MIX_BWD_TILE, t)
    nt = t // tm
    h = HALO_BWD
    hh = h // 2
    half = CC_W // 2
    ext = tm + 2 * h
    mid = tm + 2 * hh

    def body(z_ref, zp_ref, zn_ref, dsc_ref, dscp_ref, dscn_ref, dcc_ref, dccp_ref, dccn_ref, dq_ref,
             dk0_ref, dk1_ref, dk2_ref, dv0_ref, dv1_ref, dv2_ref,
             scw_ref, ccw_ref, ccb_ref, ccg_ref, ccbb_ref, cos_ref, sin_ref,
             dz_ref, dscw_ref, dccw_ref, dvec_ref, u_s, dc_s, ch_s, g_s):
        i = pl.program_id(0)
        first, last = i == 0, i == nt - 1

        @pl.when(first)
        def _():
            dscw_ref[...] = jnp.zeros_like(dscw_ref)
            dccw_ref[...] = jnp.zeros_like(dccw_ref)
            dvec_ref[...] = jnp.zeros_like(dvec_ref)

        pz = jnp.where(first, 0.0, zp_ref[...])
        nz = jnp.where(last, 0.0, zn_ref[...])
        zo = z_ref[...]

        def u_of(zz):
            return zz[:, O_CCA:O_CCA + D_CC] * _sigmoid(zz[:, O_CCG:O_CCG + D_CC])

        u_s[0:h, :] = u_of(pz)
        u_s[h:h + tm, :] = u_of(zo)
        u_s[h + tm:ext, :] = u_of(nz)
        acc = jnp.zeros((mid, D_CC), F32)
        for j in range(CC_W):
            acc = acc + ccw_ref[j:j + 1, :] * u_s[pl.ds(hh + j - half, mid), :]
        xhat, rstd = _ln_stats(acc + ccb_ref[...])
        nn = xhat * ccg_ref[...] + ccbb_ref[...]
        sg = _sigmoid(nn)
        dycc_mid = jnp.concatenate([jnp.where(first, 0.0, dccp_ref[hh:h, :]), dcc_ref[...],
                                    jnp.where(last, 0.0, dccn_ref[0:hh, :])], axis=0)
        dn = dycc_mid * (sg * (1.0 + nn * (1.0 - sg)))
        dc = _ln_bwd(dn, xhat, rstd, ccg_ref[...])
        dc_s[...] = dc
        dn_own = dn[hh:hh + tm, :]
        dvec_ref[0:1, :] += jnp.sum(dc[hh:hh + tm, :], axis=0, keepdims=True)
        dvec_ref[1:2, :] += jnp.sum(dn_own * xhat[hh:hh + tm, :], axis=0, keepdims=True)
        dvec_ref[2:3, :] += jnp.sum(dn_own, axis=0, keepdims=True)
        du = jnp.zeros((tm, D_CC), F32)
        dc_own = dc[hh:hh + tm, :]
        for j in range(CC_W):
            du = du + ccw_ref[j:j + 1, :] * dc_s[pl.ds(hh + half - j, tm), :]
            dccw_ref[j:j + 1, :] += jnp.sum(dc_own * u_s[pl.ds(h + j - half, tm), :], axis=0, keepdims=True)
        gate = _sigmoid(zo[:, O_CCG:O_CCG + D_CC])
        a_own = zo[:, O_CCA:O_CCA + D_CC]
        dz_ref[:, O_CCA:O_CCA + D_CC] = (du * gate).astype(BF16)
        dz_ref[:, O_CCG:O_CCG + D_CC] = (du * a_own * gate * (1.0 - gate)).astype(BF16)

        def ch_of(zz):
            return zz[:, O_SCC:O_SCC + D_SC] * zz[:, O_SCH:O_SCH + D_SC]

        ch_s[0:h, :] = ch_of(pz)
        ch_s[h:h + tm, :] = ch_of(zo)
        ch_s[h + tm:ext, :] = ch_of(nz)
        g_s[0:h, :] = jnp.where(first, 0.0, dscp_ref[...]) * pz[:, O_SCB:O_SCB + D_SC]
        g_s[h:h + tm, :] = dsc_ref[...] * zo[:, O_SCB:O_SCB + D_SC]
        g_s[h + tm:ext, :] = jnp.where(last, 0.0, dscn_ref[...]) * nz[:, O_SCB:O_SCB + D_SC]
        conv = jnp.zeros((tm, D_SC), F32)
        dch = jnp.zeros((tm, D_SC), F32)
        g_own = g_s[h:h + tm, :]
        for j in range(SC_W):
            chj = ch_s[pl.ds(h + j - SC_W // 2, tm), :]
            conv = conv + scw_ref[j:j + 1, :] * chj
            dch = dch + scw_ref[j:j + 1, :] * g_s[pl.ds(h + SC_W // 2 - j, tm), :]
            dscw_ref[j:j + 1, :] += jnp.sum(g_own * chj, axis=0, keepdims=True)
        dz_ref[:, O_SCB:O_SCB + D_SC] = (dsc_ref[...] * conv).astype(BF16)
        dz_ref[:, O_SCC:O_SCC + D_SC] = (dch * zo[:, O_SCH:O_SCH + D_SC]).astype(BF16)
        dz_ref[:, O_SCH:O_SCH + D_SC] = (dch * zo[:, O_SCC:O_SCC + D_SC]).astype(BF16)

        dq = dq_ref[...]
        dz_ref[:, O_Q:O_Q + D_ATT] = (dq * _wide(cos_ref[...], D_ATT) + _swap_halves(dq * _wide(sin_ref[...], D_ATT))).astype(BF16)
        dk = dk1_ref[0] + jnp.where(last, 0.0, dk0_ref[0]) + jnp.where(first, 0.0, dk2_ref[0])
        dz_ref[:, O_K:O_K + 128] = (dk * cos_ref[...] + _swap_halves(dk * sin_ref[...])).astype(BF16)
        dv = dv1_ref[0] + jnp.where(last, 0.0, dv0_ref[0]) + jnp.where(first, 0.0, dv2_ref[0])
        dz_ref[:, O_V:O_V + 128] = dv.astype(BF16)

    def full(a):
        return pl.BlockSpec(a.shape, lambda i: (0, 0))

    def rows(w):
        return pl.BlockSpec((tm, w), lambda i: (i, 0))

    parts = [pl.BlockSpec((1, tm, 128), lambda i: (0, jnp.minimum(i + 1, nt - 1), 0)),
             pl.BlockSpec((1, tm, 128), lambda i: (1, i, 0)),
             pl.BlockSpec((1, tm, 128), lambda i: (2, jnp.maximum(i - 1, 0), 0))]
    acc_spec = lambda r: pl.BlockSpec((r, D_CC), lambda i: (0, 0))
    return pl.pallas_call(
        body, name="mix_bwd", grid=(nt,),
        in_specs=(_halo_specs(t, tm, h, D_IN) + _halo_specs(t, tm, h, D_SC) + _halo_specs(t, tm, h, D_CC)
                  + [rows(D_ATT)] + parts + parts
                  + [full(scw), full(ccw), full(ccb), full(ccg), full(ccbb), rows(128), rows(128)]),
        out_specs=[rows(D_IN), acc_spec(SC_W), acc_spec(CC_W), acc_spec(3)],
        out_shape=[S((t, D_IN), BF16), S((SC_W, D_SC), F32), S((CC_W, D_CC), F32), S((3, D_CC), F32)],
        scratch_shapes=[pltpu.VMEM((ext, D_CC), F32), pltpu.VMEM((mid, D_CC), F32),
                        pltpu.VMEM((ext, D_SC), F32), pltpu.VMEM((ext, D_SC), F32)],
        compiler_params=_params(("arbitrary",)),
    )(z, z, z, dysc, dysc, dysc, dycc, dycc, dycc, dqr, dkp, dkp, dkp, dvp, dvp, dvp,
      scw, ccw, ccb, ccg, ccbb, cos, sin)


def _adamw(w, g, m, v):
    m = ADAM_B1 * m + (1.0 - ADAM_B1) * g
    v = ADAM_B2 * v + (1.0 - ADAM_B2) * (g * g)
    m_hat = m / (1.0 - ADAM_B1 ** ADAM_STEP)
    v_hat = v / (1.0 - ADAM_B2 ** ADAM_STEP)
    delta = -ADAM_LR * (m_hat / (jnp.sqrt(v_hat) + ADAM_EPS) + ADAM_WD * w)
    return delta, m, v


def _row_tile(rows):
    for cand in (256, 176, 128):
        if rows % cand == 0:
            return cand
    return rows


def _sum_adam(recv, w, m, v, transposed):
    nl, rows = len(recv), recv[0].shape[1]
    tile = 256 if transposed else _row_tile(rows)
    nc = (D if transposed else rows) // tile

    def body(*refs):
        w_ref, m_ref, v_ref, g_ref, d_ref, mo_ref, vo_ref = refs[nl:]
        for layer in range(nl):
            @pl.when(pl.program_id(0) == layer)
            def _(r_ref=refs[layer]):
                g = r_ref[0].astype(F32)
                for s in range(1, N_DEV):
                    g = g + r_ref[s].astype(F32)
                if transposed:
                    g = g.T
                g_ref[0] = g
                d_ref[0], mo_ref[0], vo_ref[0] = _adamw(w_ref[0], g, m_ref[0], v_ref[0])

    def held(layer):
        def at(l, c):
            return jnp.where(l == layer, c, jnp.where(l < layer, 0, nc - 1))
        if transposed:
            return pl.BlockSpec((N_DEV, rows, tile), lambda l, c: (0, 0, at(l, c)))
        return pl.BlockSpec((N_DEV, tile, D), lambda l, c: (0, at(l, c), 0))

    if transposed:
        blk = pl.BlockSpec((1, tile, rows), lambda l, c: (l, c, 0))
    else:
        blk = pl.BlockSpec((1, tile, D), lambda l, c: (l, c, 0))
    out = S(w.shape, F32)
    return pl.pallas_call(
        body, name="sum_adam_t" if transposed else "sum_adam", grid=(nl, nc),
        in_specs=[held(layer) for layer in range(nl)] + [blk, blk, blk], out_specs=[blk] * 4, out_shape=[out] * 4,
        compiler_params=_params(("arbitrary", "arbitrary")),
    )(*recv, w, m, v)


def _small_sum(gathered):
    rows = gathered.shape[1]

    def body(g_ref, o_ref):
        acc = g_ref[0]
        for s in range(1, N_DEV):
            acc = acc + g_ref[s]
        o_ref[...] = acc

    return pl.pallas_call(
        body, name="small_sum", in_specs=[pl.BlockSpec(gathered.shape, lambda: (0, 0, 0))],
        out_specs=pl.BlockSpec((rows, 128), lambda: (0, 0)), out_shape=S((rows, 128), F32),
    )(gathered)


def _small_adam(w, g, m, v):
    def body(w_ref, g_ref, m_ref, v_ref, d_ref, mo_ref, vo_ref):
        d_ref[...], mo_ref[...], vo_ref[...] = _adamw(w_ref[...], g_ref[...], m_ref[...], v_ref[...])

    spec = pl.BlockSpec(w.shape, lambda: (0, 0))
    return pl.pallas_call(
        body, name="small_adam", in_specs=[spec] * 4, out_specs=[spec] * 3, out_shape=[S(w.shape, F32)] * 3,
    )(w, g, m, v)


def _pack(pieces):
    flat = jnp.concatenate([p.reshape(-1).astype(F32) for p in pieces])
    n = flat.shape[0]
    rows = -(-n // 1024) * 8
    return jnp.pad(flat, (0, rows * 128 - n)).reshape(rows, 128)


def _unpack(packed, shapes):
    flat = packed.reshape(-1)
    out, o = [], 0
    for shp in shapes:
        n = int(np.prod(shp))
        out.append(flat[o:o + n].reshape(shp))
        o += n
    return out


def _rope_tables(t):
    half = HEAD_DIM // 2
    inv_freq = ROPE_THETA ** (-jnp.arange(half, dtype=F32) / half)
    ang = jnp.arange(t).astype(F32)[:, None] * inv_freq[None, :]
    cos, sin = jnp.cos(ang), jnp.sin(ang)
    cos128 = jnp.concatenate([cos, cos, cos, cos], axis=1)
    sin128 = jnp.concatenate([-sin, sin, -sin, sin], axis=1)
    return cos128, sin128


BIG = ("ffn1_w_gu", "ffn1_w_down", "w_in", "w_out", "ffn2_w_gu", "ffn2_w_down")
BIG_T = {"ffn1_w_gu": True, "ffn1_w_down": False, "w_in": True, "w_out": False, "ffn2_w_gu": True, "ffn2_w_down": False}
SWAPPED = ("ffn1_w_gu", "ffn2_w_gu")
REPLICATED = ("ln1_g", "ln1_b", "attn_sink", "cc_conv_b", "cc_ln_g", "cc_ln_b", "ln2_g", "ln2_b", "ln3_g", "ln3_b")
CONVS = ("sc_conv_w", "cc_conv_w")
WEIGHTS = ("ffn1_w_gu", "ffn1_w_down", "ln1_g", "ln1_b", "w_in", "sc_conv_w", "attn_sink", "cc_conv_w", "cc_conv_b",
           "cc_ln_g", "cc_ln_b", "w_out", "ln2_g", "ln2_b", "ffn2_w_gu", "ffn2_w_down", "ln3_g", "ln3_b")


def kernel(x, ffn1_w_gu, ffn1_w_down, ln1_g, ln1_b, w_in, sc_conv_w, attn_sink, cc_conv_w, cc_conv_b, cc_ln_g, cc_ln_b, w_out, ln2_g, ln2_b, ffn2_w_gu, ffn2_w_down, ln3_g, ln3_b, loss_target, m_ffn1_w_gu, m_ffn1_w_down, m_ln1_g, m_ln1_b, m_w_in, m_sc_conv_w, m_attn_sink, m_cc_conv_w, m_cc_conv_b, m_cc_ln_g, m_cc_ln_b, m_w_out, m_ln2_g, m_ln2_b, m_ffn2_w_gu, m_ffn2_w_down, m_ln3_g, m_ln3_b, v_ffn1_w_gu, v_ffn1_w_down, v_ln1_g, v_ln1_b, v_w_in, v_sc_conv_w, v_attn_sink, v_cc_conv_w, v_cc_conv_b, v_cc_ln_g, v_cc_ln_b, v_w_out, v_ln2_g, v_ln2_b, v_ffn2_w_gu, v_ffn2_w_down, v_ln3_g, v_ln3_b):
    args = dict(locals())
    w = {n: args[n] for n in WEIGHTS}
    mom = {n: args["m_" + n] for n in WEIGHTS}
    var = {n: args["v_" + n] for n in WEIGHTS}
    x0 = x[0]
    target = loss_target[0]
    t = x0.shape[0]
    idx = 4 * lax.axis_index("x") + 2 * lax.axis_index("y") + lax.axis_index("c")

    blocks = {(n, l): (w[n][l].T if BIG_T[n] else w[n][l]).astype(BF16) for l in range(DEPTH) for n in BIG}
    where = {}

    def start_stage(tag, members, after, extra=()):
        srcs = list(extra) + [blocks[m] for m in members]
        started = _send_start(srcs, [_own_slot(s) for s in srcs], _whole, f"gather_start_{tag}", after)
        for j, m in enumerate(members):
            where[m] = (started, len(extra) + j)
        return started

    def wait_stage(started, k, after, name):
        send, rcv, srcs, lands, _ = started
        return _recv_wait(send, rcv, [k], [srcs[k]], [lands[k]], _whole, after, name)[0]

    def weight(n, l, after):
        g = wait_stage(*where[n, l], after, f"gather_wait_{n}_{l}")
        return g.reshape(N_DEV * g.shape[1], g.shape[2])

    xf, xb = x0, x0.astype(BF16)
    first = start_stage("a", [("ffn1_w_gu", 0)], xb, extra=[_pack([w["sc_conv_w"], w["cc_conv_w"]])])
    conv_all = wait_stage(first, 0, xb, "gather_wait_convs").reshape(N_DEV, -1)
    n_sc = DEPTH * SC_W * 32
    scw_full = conv_all[:, :n_sc].reshape(N_DEV, DEPTH, SC_W, 32).transpose(1, 2, 0, 3).reshape(DEPTH, SC_W, D_SC)
    ccw_full = conv_all[:, n_sc:n_sc + DEPTH * CC_W * 32].reshape(N_DEV, DEPTH, CC_W, 32).transpose(1, 2, 0, 3).reshape(DEPTH, CC_W, D_CC)

    cos, sin = _rope_tables(t)
    row = lambda a, l: a[l].reshape(1, -1)

    saved, full = [], {}
    for l in range(DEPTH):
        sv = {"x0b": xb}
        token = None
        full["ffn1_w_gu", l] = weight("ffn1_w_gu", l, xb)
        if l == 0:
            token = start_stage("b", [("ffn1_w_down", 0), ("w_in", 0), ("w_out", 0)], full["ffn1_w_gu", l])[-1]
        gu1, a1 = _ffn_up(xb, full["ffn1_w_gu", l], token)
        full["ffn1_w_down", l] = weight("ffn1_w_down", l, a1)
        if l == 0:
            token = start_stage("c", [("ffn2_w_gu", 0), ("ffn2_w_down", 0)], full["ffn1_w_down", l])[-1]
        r1, x1, x1b = _ffn_down_ln(a1, full["ffn1_w_down", l], xf, row(ln1_g, l), row(ln1_b, l), token)
        full["w_in", l] = weight("w_in", l, x1b)
        z = _proj_in(x1b, full["w_in", l])
        ysc, ycc, qr, kr, vv = _mix_fwd(z, scw_full[l], ccw_full[l], row(cc_conv_b, l), row(cc_ln_g, l), row(cc_ln_b, l), cos, sin)
        if l == 0:
            token = start_stage("d", [("ffn1_w_gu", 1), ("ffn1_w_down", 1)], ysc)[-1]
        yatt = _attn_fwd(qr, kr, vv, attn_sink[l], token)
        full["w_out", l] = weight("w_out", l, yatt)
        ycat, r2, x2, x2b = _out_ln(ysc, yatt, ycc, full["w_out", l], x1, row(ln2_g, l), row(ln2_b, l))
        full["ffn2_w_gu", l] = weight("ffn2_w_gu", l, x2b)
        if l == 0:
            token = start_stage("e", [("w_in", 1), ("w_out", 1), ("ffn2_w_gu", 1), ("ffn2_w_down", 1)], full["ffn2_w_gu", l])[-1]
        gu2, a2 = _ffn_up(x2b, full["ffn2_w_gu", l], token)
        full["ffn2_w_down", l] = weight("ffn2_w_down", l, a2)
        r3, x3, x3b = _ffn_down_ln(a2, full["ffn2_w_down", l], x2, row(ln3_g, l), row(ln3_b, l))
        sv.update(gu1=gu1, a1=a1, r1=r1, x1b=x1b, z=z, qr=qr, kr=kr, vv=vv, yatt=yatt, ycat=ycat, r2=r2, x2b=x2b,
                  gu2=gu2, a2=a2, r3=r3)
        saved.append(sv)
        xf, xb = x3, x3b

    dy, sq = _loss_head(xf, target)
    loss = lax.psum(0.5 * jnp.sum(sq) / D, ("x", "y", "c"))

    sent = []
    small = {n: [None] * DEPTH for n in REPLICATED + CONVS}

    def send_grads(names, l, gs):
        srcs = [g.reshape(N_DEV, g.shape[0] // N_DEV, g.shape[1]) for g in gs]
        lands = [_own_slot(lax.dynamic_index_in_dim(s3, idx, 0, keepdims=False)) for s3 in srcs]
        started = _send_start(srcs, lands, _block_of, f"grads_start_{names[0]}_{l}", gs[-1])
        sent.append((names, l, started))
        return started[-1]

    for l in reversed(range(DEPTH)):
        sv = saved[l]
        dr, dfb, dh, dg, db = _ffn_bwd(dy, sv["r3"], row(ln3_g, l), full["ffn2_w_down", l], sv["gu2"])
        small["ln3_g"][l], small["ln3_b"][l] = dg, db
        token = send_grads(("ffn2_w_down", "ffn2_w_gu"), l,
                           [_wgrad(sv["a2"], dfb, F // 2), _wgrad(dh, sv["x2b"], F // 2)])
        dy = _dx(dr, dh, full["ffn2_w_gu", l], token)

        dr, dmb, dysc, dyatt, dycc, dg, db = _out_bwd(dy, sv["r2"], row(ln2_g, l), full["w_out", l])
        small["ln2_g"][l], small["ln2_b"][l] = dg, db
        g_out = _wgrad(sv["ycat"], dmb, D)
        dqr, dkp, dvp, dsink = _attn_bwd(sv["qr"], sv["kr"], sv["vv"], attn_sink[l], dyatt, sv["yatt"])
        small["attn_sink"][l] = dsink[:, 0]
        dz, dscw, dccw, dvec = _mix_bwd(sv["z"], dysc, dycc, dqr, dkp, dvp, scw_full[l], ccw_full[l],
                                        row(cc_conv_b, l), row(cc_ln_g, l), row(cc_ln_b, l), cos, sin)
        small["sc_conv_w"][l], small["cc_conv_w"][l] = dscw, dccw
        small["cc_conv_b"][l], small["cc_ln_g"][l], small["cc_ln_b"][l] = dvec[0], dvec[1], dvec[2]
        token = send_grads(("w_out", "w_in"), l, [g_out, _wgrad(dz, sv["x1b"], D)])
        dy = _dx(dr, dz, full["w_in", l], token)

        dr, dfb, dh, dg, db = _ffn_bwd(dy, sv["r1"], row(ln1_g, l), full["ffn1_w_down", l], sv["gu1"])
        small["ln1_g"][l], small["ln1_b"][l] = dg, db
        if l > 0:
            token = send_grads(("ffn1_w_down", "ffn1_w_gu"), l,
                               [_wgrad(sv["a1"], dfb, F // 2), _wgrad(dh, sv["x0b"], F // 2)])
        else:
            token = send_grads(("ffn1_w_gu",), l, [_wgrad(dh, sv["x0b"], F // 2)])
            token = send_grads(("ffn1_w_down",), l, [_wgrad(sv["a1"], dfb, F // 2, token)])
        dy = _dx(dr, dh, full["ffn1_w_gu", l], token)
    grad_x = dy[None]

    small_names = REPLICATED + CONVS
    small_shapes = [(DEPTH,) + tuple(np.shape(small[n][0].reshape(-1))) for n in small_names]
    small_pack = _pack([jnp.stack([small[n][l].reshape(-1) for l in range(DEPTH)]) for n in small_names])
    small_all = _all_gather([small_pack], "gather_small_grads")[0]

    recv = {n: [None] * DEPTH for n in BIG}
    grads, deltas, new_m, new_v = {}, {}, {}, {}

    def receive(upto, after):
        while len(sent) > upto:
            names, l, (send, rcv, srcs, lands, _) = sent.pop(0)
            got = _recv_wait(send, rcv, list(range(len(names))), srcs, lands, _block_of, after, f"grads_wait_{names[0]}_{l}")
            for n, g in zip(names, got):
                recv[n][l] = g

    def update(n):
        if n in SWAPPED:
            outs = _sum_adam(recv[n], *[jnp.swapaxes(a, 1, 2) for a in (w[n], mom[n], var[n])], False)
            grads[n], deltas[n], new_m[n], new_v[n] = [jnp.swapaxes(a, 1, 2) for a in outs]
        else:
            grads[n], deltas[n], new_m[n], new_v[n] = _sum_adam(recv[n], w[n], mom[n], var[n], BIG_T[n])

    receive(2, dy)
    for n in ("ffn2_w_down", "ffn2_w_gu", "w_out", "w_in"):
        update(n)
    receive(0, new_v["w_in"])
    update("ffn1_w_gu")
    update("ffn1_w_down")
    small_total = _unpack(_small_sum(small_all), small_shapes)
    for n, g in zip(small_names, small_total):
        if n in CONVS:
            taps = SC_W if n == "sc_conv_w" else CC_W
            g = lax.dynamic_slice_in_dim(g.reshape(DEPTH, taps, D_SC), idx * 32, 32, axis=2)
        grads[n] = g.reshape(w[n].shape)
    wp = _pack([w[n] for n in small_names])
    gp = _pack([grads[n] for n in small_names])
    mp = _pack([mom[n] for n in small_names])
    vp = _pack([var[n] for n in small_names])
    shapes = [w[n].shape for n in small_names]
    for dst, packed in zip((deltas, new_m, new_v), _small_adam(wp, gp, mp, vp)):
        for n, a in zip(small_names, _unpack(packed, shapes)):
            dst[n] = a

    return (loss, grad_x, *[grads[n] for n in WEIGHTS], *[deltas[n] for n in WEIGHTS],
            *[new_m[n] for n in WEIGHTS], *[new_v[n] for n in WEIGHTS])
```

```python
import functools

import jax
import jax.numpy as jnp
import numpy as np
from jax import lax
from jax.experimental import pallas as pl
from jax.experimental.pallas import tpu as pltpu

F32 = jnp.float32
BF16 = jnp.bfloat16
S = jax.ShapeDtypeStruct

N_DEV = 8
DEPTH = 2
D = 1024
F = 2816
D_IN = 2048
HEAD_DIM = 64
N_Q_HEADS = 8
N_KV_HEADS = 2
GROUP = 4
D_SC = 256
D_ATT = 512
D_CC = 256
CC_W = 31
SC_W = 3
BLOCK = 128
ROPE_THETA = 10000.0
LN_EPS = 1e-5
ALPHA = (2.0 * DEPTH) ** 0.25
ADAM_LR = 0.001
ADAM_B1 = 0.9
ADAM_B2 = 0.999
ADAM_EPS = 1e-08
ADAM_WD = 0.01
ADAM_STEP = 10

O_SCB, O_SCC, O_SCH, O_Q, O_K, O_V, O_CCA, O_CCG = 0, 256, 512, 768, 1280, 1408, 1536, 1792

V7X_VMEM_BYTES = 64 * 1024 * 1024
VMEM_LIMIT = V7X_VMEM_BYTES - 8 * 1024 * 1024
TOKEN_TILE = 256
WIDE_TILE = 512
MIX_BWD_TILE = 128
HALO_FWD = 16
HALO_BWD = 32

NT = (((1,), (1,)), ((), ()))
TN = (((0,), (0,)), ((), ()))
MESH = pl.DeviceIdType.MESH


def _params(sem=None):
    return pltpu.CompilerParams(dimension_semantics=sem, vmem_limit_bytes=VMEM_LIMIT)


def _sigmoid(v):
    return 1.0 / (1.0 + jnp.exp(-v))


def _ln_stats(r):
    mu = jnp.mean(r, axis=-1, keepdims=True)
    d = r - mu
    var = jnp.mean(d * d, axis=-1, keepdims=True)
    rstd = lax.rsqrt(var + LN_EPS)
    return d * rstd, rstd


def _ln_bwd(dn, xhat, rstd, gam):
    dxh = dn * gam
    return rstd * (dxh - jnp.mean(dxh, axis=-1, keepdims=True) - xhat * jnp.mean(dxh * xhat, axis=-1, keepdims=True))


def _swap_halves(v):
    n = v.shape[-1]
    lane = lax.broadcasted_iota(jnp.int32, v.shape, v.ndim - 1) % HEAD_DIM
    return jnp.where(lane < HEAD_DIM // 2, pltpu.roll(v, n - HEAD_DIM // 2, v.ndim - 1), pltpu.roll(v, HEAD_DIM // 2, v.ndim - 1))


def _wide(tab, n):
    return tab if n == 128 else jnp.concatenate([tab] * (n // 128), axis=1)


def _me():
    x, y, c = lax.axis_index("x"), lax.axis_index("y"), lax.axis_index("c")
    return x, y, c


def _peer(rel):
    x, y, c = _me()
    px = 1 - x if rel & 4 else x
    py = 1 - y if rel & 2 else y
    pc = 1 - c if rel & 1 else c
    return (px, py, pc), 4 * px + 2 * py + pc


def _exchange(srcs, dsts_shape, dst_index, src_of, dst_of, name):
    n = len(srcs)

    def body(*refs):
        ins = refs[:n]
        outs = [refs[n + dst_index[k]] for k in range(n)]
        send, recv, lsem = refs[n + len(dsts_shape):]
        x, y, c = _me()
        me = 4 * x + 2 * y + c
        local = [pltpu.make_async_copy(src_of(ins[k], k, me), dst_of(outs[k], k, me), lsem.at[k]) for k in range(n)]
        for cp in local:
            cp.start()
        sends, recvs = [], []
        for k in range(n):
            for rel in range(1, N_DEV):
                peer, pidx = _peer(rel)
                sends.append(pltpu.make_async_remote_copy(
                    src_ref=src_of(ins[k], k, pidx), dst_ref=dst_of(outs[k], k, me),
                    send_sem=send.at[k, rel - 1], recv_sem=recv.at[k, rel - 1], device_id=peer, device_id_type=MESH))
                recvs.append(pltpu.make_async_remote_copy(
                    src_ref=src_of(ins[k], k, pidx), dst_ref=dst_of(outs[k], k, pidx),
                    send_sem=send.at[k, rel - 1], recv_sem=recv.at[k, rel - 1], device_id=peer, device_id_type=MESH))
        for cp in sends:
            cp.start()
        for cp in recvs:
            cp.wait_recv()
        for cp in sends:
            cp.wait_send()
        for cp in local:
            cp.wait()

    hbm = pl.BlockSpec(memory_space=pltpu.HBM)
    return pl.pallas_call(
        body, name=name, in_specs=[hbm] * n, out_specs=[hbm] * len(dsts_shape), out_shape=dsts_shape,
        scratch_shapes=[pltpu.SemaphoreType.DMA((n, N_DEV - 1)), pltpu.SemaphoreType.DMA((n, N_DEV - 1)),
                        pltpu.SemaphoreType.DMA((n,))],
    )(*srcs)


def _all_gather(blocks, name):
    shapes = [S((N_DEV,) + b.shape, b.dtype) for b in blocks]
    return _exchange(blocks, shapes, list(range(len(blocks))), lambda ref, k, idx: ref, lambda ref, k, idx: ref.at[idx], name)


HBM_SPEC = pl.BlockSpec(memory_space=pltpu.HBM)
SEM_SPEC = pl.BlockSpec(memory_space=pltpu.SEMAPHORE)
ANY_SPEC = pl.BlockSpec(memory_space=pl.ANY)
EFFECT = pltpu.SideEffectType.DATAFLOW_SIDE_EFFECTING
N_PEERS = N_DEV - 1


def _own_slot(block):
    x, y, c = _me()
    return lax.dynamic_update_index_in_dim(lax.empty((N_DEV,) + block.shape, block.dtype), block, 4 * x + 2 * y + c, 0)


def _follow(body, n_in, in_specs, operands, after):
    if after is None:
        return body, list(in_specs), list(operands)

    def tail(*refs):
        return body(*refs[:n_in], *refs[n_in + 1:])

    return tail, list(in_specs) + [ANY_SPEC], list(operands) + [after]


def _send_start(srcs, lands, src_of, name, after):
    n = len(srcs)

    def body(*refs):
        ins, zones = refs[:n], refs[n:2 * n]
        send, recv = refs[2 * n + 1], refs[2 * n + 2]
        token = refs[-1]
        x, y, c = _me()
        me = 4 * x + 2 * y + c
        for k in range(n):
            for rel in range(1, N_DEV):
                peer, pidx = _peer(rel)
                pltpu.make_async_remote_copy(
                    src_ref=src_of(ins[k], pidx), dst_ref=zones[k].at[me],
                    send_sem=send.at[k * N_PEERS + rel - 1], recv_sem=recv.at[k * N_PEERS + rel - 1],
                    device_id=peer, device_id_type=MESH).start()
        token[...] = jnp.zeros_like(token)

    outs = pl.pallas_call(
        body, name=name,
        out_shape=(pltpu.SemaphoreType.DMA((n * N_PEERS,)), pltpu.SemaphoreType.DMA((n * N_PEERS,)),
                   *[pltpu.HBM(a.shape, a.dtype) for a in srcs], *[pltpu.HBM(a.shape, a.dtype) for a in lands],
                   S((8, 128), F32)),
        in_specs=[HBM_SPEC] * (2 * n) + [ANY_SPEC],
        out_specs=(SEM_SPEC, SEM_SPEC, *[HBM_SPEC] * (2 * n), pl.BlockSpec(memory_space=pltpu.VMEM)),
        input_output_aliases={i: 2 + i for i in range(2 * n)},
        compiler_params=pltpu.CompilerParams(has_side_effects=EFFECT),
    )(*[pltpu.with_memory_space_constraint(a, pltpu.HBM) for a in list(srcs) + list(lands)], after)
    return outs[0], outs[1], list(outs[2:2 + n]), list(outs[2 + n:2 + 2 * n]), outs[-1]


def _recv_wait(send, recv, ks, srcs, lands, src_of, after, name):
    n = len(ks)

    def body(*refs):
        ins, zones = refs[:n], refs[n:2 * n]
        send_sems, recv_sems = refs[2 * n], refs[2 * n + 1]
        for j, k in enumerate(ks):
            for rel in range(1, N_DEV):
                peer, pidx = _peer(rel)
                cp = pltpu.make_async_remote_copy(
                    src_ref=src_of(ins[j], pidx), dst_ref=zones[j].at[pidx],
                    send_sem=send_sems.at[k * N_PEERS + rel - 1], recv_sem=recv_sems.at[k * N_PEERS + rel - 1],
                    device_id=peer, device_id_type=MESH)
                cp.wait_send()
                cp.wait_recv()

    outs = pl.pallas_call(
        body, name=name,
        out_shape=(*[pltpu.HBM(a.shape, a.dtype) for a in srcs], *[pltpu.HBM(a.shape, a.dtype) for a in lands]),
        in_specs=[HBM_SPEC] * (2 * n) + [SEM_SPEC, SEM_SPEC, ANY_SPEC], out_specs=[HBM_SPEC] * (2 * n),
        input_output_aliases={i: i for i in range(2 * n)},
        compiler_params=pltpu.CompilerParams(has_side_effects=EFFECT),
    )(*srcs, *lands, send, recv, after)
    return list(outs[n:])


def _whole(ref, idx):
    return ref


def _block_of(ref, idx):
    return ref.at[idx]


def _resident(shape):
    return pl.BlockSpec(shape, lambda i: (0,) * len(shape), pipeline_mode=pl.Buffered(1))


def _ffn_up(xb, wgut, after=None):
    t = xb.shape[0]
    tm = min(WIDE_TILE, t)
    half = F // 2

    def body(x_ref, w_ref, gu_ref, a_ref):
        x = x_ref[...]
        for ch in range(2):
            lo = ch * half
            g = lax.dot_general(x, w_ref[lo:lo + half, :], NT, preferred_element_type=F32)
            u = lax.dot_general(x, w_ref[F + lo:F + lo + half, :], NT, preferred_element_type=F32)
            gu_ref[:, lo:lo + half] = g.astype(BF16)
            gu_ref[:, F + lo:F + lo + half] = u.astype(BF16)
            a_ref[:, lo:lo + half] = (g * _sigmoid(g) * u).astype(BF16)

    body, in_specs, operands = _follow(
        body, 2, [pl.BlockSpec((tm, D), lambda i: (i, 0)), _resident((2 * F, D))], [xb, wgut], after)
    return pl.pallas_call(
        body, name="ffn_up", grid=(t // tm,), in_specs=in_specs,
        out_specs=[pl.BlockSpec((tm, 2 * F), lambda i: (i, 0)), pl.BlockSpec((tm, F), lambda i: (i, 0))],
        out_shape=[S((t, 2 * F), BF16), S((t, F), BF16)], compiler_params=_params(("parallel",)),
    )(*operands)


def _residual_ln_out(x_ref, f, scale, g_ref, b_ref, r_ref, y_ref, yb_ref):
    r = ALPHA * x_ref[...] + scale * f
    xhat, _ = _ln_stats(r)
    y = xhat * g_ref[...] + b_ref[...]
    r_ref[...] = r
    y_ref[...] = y
    yb_ref[...] = y.astype(BF16)


def _ffn_down_ln(a, wd, x, gam, bet, after=None):
    t = x.shape[0]
    tm = min(WIDE_TILE, t)

    def body(a_ref, w_ref, x_ref, g_ref, b_ref, r_ref, y_ref, yb_ref):
        f = jnp.dot(a_ref[...], w_ref[...], preferred_element_type=F32)
        _residual_ln_out(x_ref, f, 0.5, g_ref, b_ref, r_ref, y_ref, yb_ref)

    row = pl.BlockSpec((tm, D), lambda i: (i, 0))
    vec = pl.BlockSpec((1, D), lambda i: (0, 0))
    body, in_specs, operands = _follow(
        body, 5, [pl.BlockSpec((tm, F), lambda i: (i, 0)), _resident((F, D)), row, vec, vec],
        [a, wd, x, gam, bet], after)
    return pl.pallas_call(
        body, name="ffn_down_ln", grid=(t // tm,), in_specs=in_specs,
        out_specs=[row, row, row], out_shape=[S((t, D), F32), S((t, D), F32), S((t, D), BF16)],
        compiler_params=_params(("parallel",)),
    )(*operands)


def _proj_in(xb, wint):
    t = xb.shape[0]
    tm = min(WIDE_TILE, t)

    def body(x_ref, w_ref, z_ref):
        z_ref[...] = lax.dot_general(x_ref[...], w_ref[...], NT, preferred_element_type=F32)

    return pl.pallas_call(
        body, name="proj_in", grid=(t // tm,),
        in_specs=[pl.BlockSpec((tm, D), lambda i: (i, 0)), _resident((D_IN, D))],
        out_specs=pl.BlockSpec((tm, D_IN), lambda i: (i, 0)), out_shape=S((t, D_IN), F32),
        compiler_params=_params(("parallel",)),
    )(xb, wint)


def _halo_specs(t, tm, halo, width):
    per = tm // halo
    last = t // halo - 1
    return [pl.BlockSpec((tm, width), lambda i: (i, 0)),
            pl.BlockSpec((halo, width), lambda i: (jnp.maximum(i * per - 1, 0), 0)),
            pl.BlockSpec((halo, width), lambda i: (jnp.minimum((i + 1) * per, last), 0))]


def _mix_fwd(z, scw, ccw, ccb, ccg, ccbb, cos, sin):
    t = z.shape[0]
    tm = min(TOKEN_TILE, t)
    nt = t // tm
    h = HALO_FWD
    rc = 64

    def body(z_ref, zp_ref, zn_ref, scw_ref, ccw_ref, ccb_ref, ccg_ref, ccbb_ref, cos_ref, sin_ref,
             ysc_ref, ycc_ref, q_ref, k_ref, v_ref, u_s, ch_s):
        i = pl.program_id(0)
        pz = jnp.where(i == 0, 0.0, zp_ref[...])
        nz = jnp.where(i == nt - 1, 0.0, zn_ref[...])

        def u_of(zz):
            return zz[:, O_CCA:O_CCA + D_CC] * _sigmoid(zz[:, O_CCG:O_CCG + D_CC])

        def ch_of(zz):
            return zz[:, O_SCC:O_SCC + D_SC] * zz[:, O_SCH:O_SCH + D_SC]

        u_s[0:h, :] = u_of(pz)
        u_s[h:h + tm, :] = z_ref[:, O_CCA:O_CCA + D_CC] * _sigmoid(z_ref[:, O_CCG:O_CCG + D_CC])
        u_s[h + tm:2 * h + tm, :] = u_of(nz)
        ch_s[0:h, :] = ch_of(pz)
        ch_s[h:h + tm, :] = z_ref[:, O_SCC:O_SCC + D_SC] * z_ref[:, O_SCH:O_SCH + D_SC]
        ch_s[h + tm:2 * h + tm, :] = ch_of(nz)
        for r0 in range(0, tm, rc):
            acc = jnp.zeros((rc, D_CC), F32)
            for j in range(CC_W):
                acc = acc + ccw_ref[j:j + 1, :] * u_s[pl.ds(r0 + h + j - CC_W // 2, rc), :]
            xhat, _ = _ln_stats(acc + ccb_ref[...])
            n = xhat * ccg_ref[...] + ccbb_ref[...]
            ycc_ref[r0:r0 + rc, :] = (n * _sigmoid(n)).astype(BF16)
            acc = jnp.zeros((rc, D_SC), F32)
            for j in range(SC_W):
                acc = acc + scw_ref[j:j + 1, :] * ch_s[pl.ds(r0 + h + j - SC_W // 2, rc), :]
            ysc_ref[r0:r0 + rc, :] = (z_ref[r0:r0 + rc, O_SCB:O_SCB + D_SC] * acc).astype(BF16)
        q = z_ref[:, O_Q:O_Q + D_ATT]
        q_ref[...] = ((q * _wide(cos_ref[...], D_ATT) + _swap_halves(q) * _wide(sin_ref[...], D_ATT)) * (HEAD_DIM ** -0.5)).astype(BF16)
        k = z_ref[:, O_K:O_K + 128]
        k_ref[...] = (k * cos_ref[...] + _swap_halves(k) * sin_ref[...]).astype(BF16)
        v_ref[...] = z_ref[:, O_V:O_V + 128].astype(BF16)

    def full(a):
        return pl.BlockSpec(a.shape, lambda i: (0, 0))

    def rows(w):
        return pl.BlockSpec((tm, w), lambda i: (i, 0))

    return pl.pallas_call(
        body, name="mix_fwd", grid=(nt,),
        in_specs=_halo_specs(t, tm, h, D_IN) + [full(scw), full(ccw), full(ccb), full(ccg), full(ccbb), rows(128), rows(128)],
        out_specs=[rows(D_SC), rows(D_CC), rows(D_ATT), rows(128), rows(128)],
        out_shape=[S((t, D_SC), BF16), S((t, D_CC), BF16), S((t, D_ATT), BF16), S((t, 128), BF16), S((t, 128), BF16)],
        scratch_shapes=[pltpu.VMEM((tm + 2 * h, D_CC), F32), pltpu.VMEM((tm + 2 * h, D_SC), F32)],
        compiler_params=_params(("parallel",)),
    )(z, z, z, scw, ccw, ccb, ccg, ccbb, cos, sin)


def _band_specs(nb, width):
    return [pl.BlockSpec((BLOCK, width), lambda n: (jnp.maximum(n - 1, 0), 0)),
            pl.BlockSpec((BLOCK, width), lambda n: (n, 0)),
            pl.BlockSpec((BLOCK, width), lambda n: (jnp.minimum(n + 1, nb - 1), 0))]


def _band_bias(n, nb, bias_s):
    qpos = lax.broadcasted_iota(jnp.int32, (BLOCK, 3 * BLOCK), 0)
    col = lax.broadcasted_iota(jnp.int32, (BLOCK, 3 * BLOCK), 1)
    ok = jnp.abs(qpos - (col - BLOCK)) <= BLOCK
    ok = jnp.logical_and(ok, jnp.logical_or(col >= BLOCK, n > 0))
    ok = jnp.logical_and(ok, jnp.logical_or(col < 2 * BLOCK, n < nb - 1))
    bias_s[...] = jnp.where(ok, 0.0, -1e30)


def _band_cat(refs, kvh):
    return jnp.concatenate([r[:, kvh * HEAD_DIM:(kvh + 1) * HEAD_DIM] for r in refs], axis=0)


def _head_scores(q_ref, kc, h, bias_s):
    qh = q_ref[:, h * HEAD_DIM:(h + 1) * HEAD_DIM]
    return qh, lax.dot_general(qh, kc, NT, preferred_element_type=F32) + bias_s[...]


def _softmax_parts(s, sk):
    m = jnp.maximum(jnp.max(s, axis=-1, keepdims=True), sk)
    p = jnp.exp(s - m)
    ps = jnp.exp(sk - m)
    return p, ps, jnp.sum(p, axis=-1, keepdims=True) + ps


def _head_probs(q_ref, kc, sink_ref, h, bias_s):
    qh, s = _head_scores(q_ref, kc, h, bias_s)
    return (qh,) + _softmax_parts(s, sink_ref[h])


def _attn_fwd(qr, kr, vv, sink, after=None):
    t = qr.shape[0]
    nb = t // BLOCK

    def body(q_ref, kp_ref, ko_ref, kn_ref, vp_ref, vo_ref, vn_ref, sink_ref, o_ref, bias_s):
        _band_bias(pl.program_id(0), nb, bias_s)
        kcs = [_band_cat((kp_ref, ko_ref, kn_ref), kvh) for kvh in range(N_KV_HEADS)]
        vcs = [_band_cat((vp_ref, vo_ref, vn_ref), kvh) for kvh in range(N_KV_HEADS)]
        s_next = _head_scores(q_ref, kcs[0], 0, bias_s)[1]
        for h in range(N_Q_HEADS):
            s = s_next
            if h + 1 < N_Q_HEADS:
                s_next = _head_scores(q_ref, kcs[(h + 1) // GROUP], h + 1, bias_s)[1]
            p, _, denom = _softmax_parts(s, sink_ref[h])
            o = jnp.dot(p.astype(BF16), vcs[h // GROUP], preferred_element_type=F32) / denom
            o_ref[:, h * HEAD_DIM:(h + 1) * HEAD_DIM] = o.astype(BF16)

    qspec = pl.BlockSpec((BLOCK, D_ATT), lambda n: (n, 0))
    body, in_specs, operands = _follow(
        body, 8, [qspec] + _band_specs(nb, 128) + _band_specs(nb, 128) + [pl.BlockSpec(memory_space=pltpu.SMEM)],
        [qr, kr, kr, kr, vv, vv, vv, sink], after)
    return pl.pallas_call(
        body, name="attn_fwd", grid=(nb,), in_specs=in_specs,
        out_specs=qspec, out_shape=S((t, D_ATT), BF16), scratch_shapes=[pltpu.VMEM((BLOCK, 3 * BLOCK), F32)],
        compiler_params=_params(("parallel",)),
    )(*operands)


def _out_ln(ysc, yatt, ycc, wout, x, gam, bet):
    t = x.shape[0]
    tm = min(WIDE_TILE, t)

    def body(sc_ref, at_ref, cc_ref, w_ref, x_ref, g_ref, b_ref, cat_ref, r_ref, y_ref, yb_ref):
        cat = jnp.concatenate([sc_ref[...], at_ref[...], cc_ref[...]], axis=1)
        cat_ref[...] = cat
        f = jnp.dot(cat, w_ref[...], preferred_element_type=F32)
        _residual_ln_out(x_ref, f, 1.0, g_ref, b_ref, r_ref, y_ref, yb_ref)

    def rows(w):
        return pl.BlockSpec((tm, w), lambda i: (i, 0))

    vec = pl.BlockSpec((1, D), lambda i: (0, 0))
    return pl.pallas_call(
        body, name="out_ln", grid=(t // tm,),
        in_specs=[rows(D_SC), rows(D_ATT), rows(D_CC), _resident((D, D)), rows(D), vec, vec],
        out_specs=[rows(D), rows(D), rows(D), rows(D)],
        out_shape=[S((t, D), BF16), S((t, D), F32), S((t, D), F32), S((t, D), BF16)],
        compiler_params=_params(("parallel",)),
    )(ysc, yatt, ycc, wout, x, gam, bet)


def _loss_head(y, target):
    t = y.shape[0]
    tm = min(TOKEN_TILE, t)

    def body(y_ref, t_ref, dy_ref, part_ref):
        e = y_ref[...] - t_ref[...]
        dy_ref[...] = e / D

        @pl.when(pl.program_id(0) == 0)
        def _():
            part_ref[...] = jnp.zeros_like(part_ref)

        part_ref[...] += jnp.sum(e * e, axis=0, keepdims=True)

    row = pl.BlockSpec((tm, D), lambda i: (i, 0))
    return pl.pallas_call(
        body, name="loss_head", grid=(t // tm,), in_specs=[row, row],
        out_specs=[row, pl.BlockSpec((1, D), lambda i: (0, 0))], out_shape=[S((t, D), F32), S((1, D), F32)],
        compiler_params=_params(("arbitrary",)),
    )(y, target)


def _ln_bwd_block(dy_ref, r_ref, g_ref, dgam_ref, dbet_ref):
    xhat, rstd = _ln_stats(r_ref[...])
    dy = dy_ref[...]

    @pl.when(pl.program_id(0) == 0)
    def _():
        dgam_ref[...] = jnp.zeros_like(dgam_ref)
        dbet_ref[...] = jnp.zeros_like(dbet_ref)

    dgam_ref[...] += jnp.sum(dy * xhat, axis=0, keepdims=True)
    dbet_ref[...] += jnp.sum(dy, axis=0, keepdims=True)
    return _ln_bwd(dy, xhat, rstd, g_ref[...])


def _ffn_bwd(dy, r, gam, wd, gu):
    t = dy.shape[0]
    tm = min(TOKEN_TILE, t)
    half = F // 2

    def body(dy_ref, r_ref, g_ref, w_ref, gu_ref, dr_ref, df_ref, dh_ref, dgam_ref, dbet_ref):
        dr = _ln_bwd_block(dy_ref, r_ref, g_ref, dgam_ref, dbet_ref)
        dr_ref[...] = dr
        dfb = (0.5 * dr).astype(BF16)
        df_ref[...] = dfb
        for ch in range(2):
            lo = ch * half
            da = lax.dot_general(dfb, w_ref[lo:lo + half, :], NT, preferred_element_type=F32)
            g = gu_ref[:, lo:lo + half].astype(F32)
            u = gu_ref[:, F + lo:F + lo + half].astype(F32)
            sg = _sigmoid(g)
            dh_ref[:, lo:lo + half] = (da * u * (sg * (1.0 + g * (1.0 - sg)))).astype(BF16)
            dh_ref[:, F + lo:F + lo + half] = (da * (g * sg)).astype(BF16)

    row = pl.BlockSpec((tm, D), lambda i: (i, 0))
    vec = pl.BlockSpec((1, D), lambda i: (0, 0))
    wide = pl.BlockSpec((tm, 2 * F), lambda i: (i, 0))
    return pl.pallas_call(
        body, name="ffn_bwd", grid=(t // tm,),
        in_specs=[row, row, vec, _resident((F, D)), wide],
        out_specs=[row, row, wide, vec, vec],
        out_shape=[S((t, D), F32), S((t, D), BF16), S((t, 2 * F), BF16), S((1, D), F32), S((1, D), F32)],
        compiler_params=_params(("arbitrary",)),
    )(dy, r, gam, wd, gu)


def _dx(dr, dh, w, after=None):
    t = dr.shape[0]
    tm = min(WIDE_TILE, t)
    kk = dh.shape[1]

    def body(dr_ref, dh_ref, w_ref, o_ref):
        o_ref[...] = ALPHA * dr_ref[...] + jnp.dot(dh_ref[...], w_ref[...], preferred_element_type=F32)

    row = pl.BlockSpec((tm, D), lambda i: (i, 0))
    body, in_specs, operands = _follow(
        body, 3, [row, pl.BlockSpec((tm, kk), lambda i: (i, 0)), _resident((kk, D))], [dr, dh, w], after)
    return pl.pallas_call(
        body, name="dx", grid=(t // tm,), in_specs=in_specs,
        out_specs=row, out_shape=S((t, D), F32), compiler_params=_params(("parallel",)),
    )(*operands)


def _wgrad(a, b, ta, after=None):
    t, ka = a.shape
    tk = min(1024, t)
    nk = t // tk

    def body(a_ref, b_ref, o_ref, acc):
        k = pl.program_id(1)

        @pl.when(k == 0)
        def _():
            acc[...] = jnp.zeros_like(acc)

        acc[...] += lax.dot_general(a_ref[...], b_ref[...], TN, preferred_element_type=F32)

        @pl.when(k == nk - 1)
        def _():
            o_ref[...] = acc[...].astype(BF16)

    body, in_specs, operands = _follow(
        body, 2, [pl.BlockSpec((tk, ta), lambda i, k: (k, i)), pl.BlockSpec((tk, D), lambda i, k: (k, 0))], [a, b], after)
    return pl.pallas_call(
        body, name="wgrad", grid=(ka // ta, nk), in_specs=in_specs,
        out_specs=pl.BlockSpec((ta, D), lambda i, k: (i, 0)), out_shape=S((ka, D), BF16),
        scratch_shapes=[pltpu.VMEM((ta, D), F32)], compiler_params=_params(("parallel", "arbitrary")),
    )(*operands)


def _out_bwd(dy, r, gam, wout):
    t = dy.shape[0]
    tm = min(TOKEN_TILE, t)

    def body(dy_ref, r_ref, g_ref, w_ref, dr_ref, dm_ref, dsc_ref, dat_ref, dcc_ref, dgam_ref, dbet_ref):
        dr = _ln_bwd_block(dy_ref, r_ref, g_ref, dgam_ref, dbet_ref)
        dr_ref[...] = dr
        dmb = dr.astype(BF16)
        dm_ref[...] = dmb
        dcat = lax.dot_general(dmb, w_ref[...], NT, preferred_element_type=F32)
        dsc_ref[...] = dcat[:, 0:D_SC]
        dat_ref[...] = dcat[:, D_SC:D_SC + D_ATT]
        dcc_ref[...] = dcat[:, D_SC + D_ATT:D]

    def rows(w):
        return pl.BlockSpec((tm, w), lambda i: (i, 0))

    vec = pl.BlockSpec((1, D), lambda i: (0, 0))
    return pl.pallas_call(
        body, name="out_bwd", grid=(t // tm,),
        in_specs=[rows(D), rows(D), vec, _resident((D, D))],
        out_specs=[rows(D), rows(D), rows(D_SC), rows(D_ATT), rows(D_CC), vec, vec],
        out_shape=[S((t, D), F32), S((t, D), BF16), S((t, D_SC), F32), S((t, D_ATT), F32), S((t, D_CC), F32),
                   S((1, D), F32), S((1, D), F32)],
        compiler_params=_params(("arbitrary",)),
    )(dy, r, gam, wout)


def _attn_bwd(qr, kr, vv, sink, do, yatt):
    t = qr.shape[0]
    nb = t // BLOCK
    scale = HEAD_DIM ** -0.5

    def body(q_ref, kp_ref, ko_ref, kn_ref, vp_ref, vo_ref, vn_ref, sink_ref, do_ref, o_ref,
             dq_ref, dk_ref, dv_ref, dsink_ref, bias_s, ds_s, p_s, q_s, dou_s):
        n = pl.program_id(0)
        _band_bias(n, nb, bias_s)

        @pl.when(n == 0)
        def _():
            dsink_ref[...] = jnp.zeros_like(dsink_ref)

        kcs = [_band_cat((kp_ref, ko_ref, kn_ref), kvh) for kvh in range(N_KV_HEADS)]
        vcs = [_band_cat((vp_ref, vo_ref, vn_ref), kvh) for kvh in range(N_KV_HEADS)]

        def scores(h):
            return _head_scores(q_ref, kcs[h // GROUP], h, bias_s)

        def probs(h, qh, s):
            cols = slice(h * HEAD_DIM, (h + 1) * HEAD_DIM)
            p, ps, denom = _softmax_parts(s, sink_ref[h])
            doh = do_ref[:, cols]
            dd = jnp.sum(doh * o_ref[:, cols].astype(F32), axis=-1, keepdims=True) / denom
            dou = (doh / denom).astype(BF16)
            dp = lax.dot_general(dou, vcs[h // GROUP], NT, preferred_element_type=F32)
            dsink_ref[h:h + 1, :] += jnp.zeros((1, 128), F32) - jnp.sum(ps * dd)
            return qh, p, dd, dou, dp

        def grads(h, qh, p, dd, dou, dp):
            kvh, g = divmod(h, GROUP)
            cols = slice(h * HEAD_DIM, (h + 1) * HEAD_DIM)
            rows = slice(g * BLOCK, (g + 1) * BLOCK)
            ds = (p * (dp - dd)).astype(BF16)
            dq_ref[:, cols] = jnp.dot(ds, kcs[kvh], preferred_element_type=F32) * scale
            ds_s[rows, :] = ds
            p_s[rows, :] = p.astype(BF16)
            q_s[rows, :] = qh
            dou_s[rows, :] = dou
            if g == GROUP - 1:
                dk = lax.dot_general(ds_s[...], q_s[...], TN, preferred_element_type=F32)
                dv = lax.dot_general(p_s[...], dou_s[...], TN, preferred_element_type=F32)
                for j in range(3):
                    dk_ref[j, :, kvh * HEAD_DIM:(kvh + 1) * HEAD_DIM] = dk[j * BLOCK:(j + 1) * BLOCK, :]
                    dv_ref[j, :, kvh * HEAD_DIM:(kvh + 1) * HEAD_DIM] = dv[j * BLOCK:(j + 1) * BLOCK, :]

        sc = {0: scores(0), 1: scores(1)}
        pr = {0: probs(0, *sc.pop(0))}
        for h in range(N_Q_HEADS):
            if h + 2 < N_Q_HEADS:
                sc[h + 2] = scores(h + 2)
            if h + 1 < N_Q_HEADS:
                pr[h + 1] = probs(h + 1, *sc.pop(h + 1))
            grads(h, *pr.pop(h))

    qspec = pl.BlockSpec((BLOCK, D_ATT), lambda n: (n, 0))
    part = pl.BlockSpec((3, BLOCK, 128), lambda n: (0, n, 0))
    stacked = GROUP * BLOCK
    return pl.pallas_call(
        body, name="attn_bwd", grid=(nb,),
        in_specs=[qspec] + _band_specs(nb, 128) + _band_specs(nb, 128) + [pl.BlockSpec(memory_space=pltpu.SMEM), qspec, qspec],
        out_specs=[qspec, part, part, pl.BlockSpec((N_Q_HEADS, 128), lambda n: (0, 0))],
        out_shape=[S((t, D_ATT), F32), S((3, t, 128), F32), S((3, t, 128), F32), S((N_Q_HEADS, 128), F32)],
        scratch_shapes=[pltpu.VMEM((BLOCK, 3 * BLOCK), F32), pltpu.VMEM((stacked, 3 * BLOCK), BF16),
                        pltpu.VMEM((stacked, 3 * BLOCK), BF16), pltpu.VMEM((stacked, HEAD_DIM), BF16),
                        pltpu.VMEM((stacked, HEAD_DIM), BF16)],
        compiler_params=_params(("arbitrary",)),
    )(qr, kr, kr, kr, vv, vv, vv, sink, do, yatt)


def _mix_bwd(z, dysc, dycc, dqr, dkp, dvp, scw, ccw, ccb, ccg, ccbb, cos, sin):
    t = z.shape[0]
    tm = min(MIX_BWD_TILE, t)
    nt = t // tm
    h = HALO_BWD
    hh = h // 2
    half = CC_W // 2
    ext = tm + 2 * h
    mid = tm + 2 * hh

    def body(z_ref, zp_ref, zn_ref, dsc_ref, dscp_ref, dscn_ref, dcc_ref, dccp_ref, dccn_ref, dq_ref,
             dk0_ref, dk1_ref, dk2_ref, dv0_ref, dv1_ref, dv2_ref,
             scw_ref, ccw_ref, ccb_ref, ccg_ref, ccbb_ref, cos_ref, sin_ref,
             dz_ref, dscw_ref, dccw_ref, dvec_ref, u_s, dc_s, ch_s, g_s):
        i = pl.program_id(0)
        first, last = i == 0, i == nt - 1

        @pl.when(first)
        def _():
            dscw_ref[...] = jnp.zeros_like(dscw_ref)
            dccw_ref[...] = jnp.zeros_like(dccw_ref)
            dvec_ref[...] = jnp.zeros_like(dvec_ref)

        pz = jnp.where(first, 0.0, zp_ref[...])
        nz = jnp.where(last, 0.0, zn_ref[...])
        zo = z_ref[...]

        def u_of(zz):
            return zz[:, O_CCA:O_CCA + D_CC] * _sigmoid(zz[:, O_CCG:O_CCG + D_CC])

        u_s[0:h, :] = u_of(pz)
        u_s[h:h + tm, :] = u_of(zo)
        u_s[h + tm:ext, :] = u_of(nz)
        acc = jnp.zeros((mid, D_CC), F32)
        for j in range(CC_W):
            acc = acc + ccw_ref[j:j + 1, :] * u_s[pl.ds(hh + j - half, mid), :]
        xhat, rstd = _ln_stats(acc + ccb_ref[...])
        nn = xhat * ccg_ref[...] + ccbb_ref[...]
        sg = _sigmoid(nn)
        dycc_mid = jnp.concatenate([jnp.where(first, 0.0, dccp_ref[hh:h, :]), dcc_ref[...],
                                    jnp.where(last, 0.0, dccn_ref[0:hh, :])], axis=0)
        dn = dycc_mid * (sg * (1.0 + nn * (1.0 - sg)))
        dc = _ln_bwd(dn, xhat, rstd, ccg_ref[...])
        dc_s[...] = dc
        dn_own = dn[hh:hh + tm, :]
        dvec_ref[0:1, :] += jnp.sum(dc[hh:hh + tm, :], axis=0, keepdims=True)
        dvec_ref[1:2, :] += jnp.sum(dn_own * xhat[hh:hh + tm, :], axis=0, keepdims=True)
        dvec_ref[2:3, :] += jnp.sum(dn_own, axis=0, keepdims=True)
        du = jnp.zeros((tm, D_CC), F32)
        dc_own = dc[hh:hh + tm, :]
        for j in range(CC_W):
            du = du + ccw_ref[j:j + 1, :] * dc_s[pl.ds(hh + half - j, tm), :]
            dccw_ref[j:j + 1, :] += jnp.sum(dc_own * u_s[pl.ds(h + j - half, tm), :], axis=0, keepdims=True)
        gate = _sigmoid(zo[:, O_CCG:O_CCG + D_CC])
        a_own = zo[:, O_CCA:O_CCA + D_CC]
        dz_ref[:, O_CCA:O_CCA + D_CC] = (du * gate).astype(BF16)
        dz_ref[:, O_CCG:O_CCG + D_CC] = (du * a_own * gate * (1.0 - gate)).astype(BF16)

        def ch_of(zz):
            return zz[:, O_SCC:O_SCC + D_SC] * zz[:, O_SCH:O_SCH + D_SC]

        ch_s[0:h, :] = ch_of(pz)
        ch_s[h:h + tm, :] = ch_of(zo)
        ch_s[h + tm:ext, :] = ch_of(nz)
        g_s[0:h, :] = jnp.where(first, 0.0, dscp_ref[...]) * pz[:, O_SCB:O_SCB + D_SC]
        g_s[h:h + tm, :] = dsc_ref[...] * zo[:, O_SCB:O_SCB + D_SC]
        g_s[h + tm:ext, :] = jnp.where(last, 0.0, dscn_ref[...]) * nz[:, O_SCB:O_SCB + D_SC]
        conv = jnp.zeros((tm, D_SC), F32)
        dch = jnp.zeros((tm, D_SC), F32)
        g_own = g_s[h:h + tm, :]
        for j in range(SC_W):
            chj = ch_s[pl.ds(h + j - SC_W // 2, tm), :]
            conv = conv + scw_ref[j:j + 1, :] * chj
            dch = dch + scw_ref[j:j + 1, :] * g_s[pl.ds(h + SC_W // 2 - j, tm), :]
            dscw_ref[j:j + 1, :] += jnp.sum(g_own * chj, axis=0, keepdims=True)
        dz_ref[:, O_SCB:O_SCB + D_SC] = (dsc_ref[...] * conv).astype(BF16)
        dz_ref[:, O_SCC:O_SCC + D_SC] = (dch * zo[:, O_SCH:O_SCH + D_SC]).astype(BF16)
        dz_ref[:, O_SCH:O_SCH + D_SC] = (dch * zo[:, O_SCC:O_SCC + D_SC]).astype(BF16)

        dq = dq_ref[...]
        dz_ref[:, O_Q:O_Q + D_ATT] = (dq * _wide(cos_ref[...], D_ATT) + _swap_halves(dq * _wide(sin_ref[...], D_ATT))).astype(BF16)
        dk = dk1_ref[0] + jnp.where(last, 0.0, dk0_ref[0]) + jnp.where(first, 0.0, dk2_ref[0])
        dz_ref[:, O_K:O_K + 128] = (dk * cos_ref[...] + _swap_halves(dk * sin_ref[...])).astype(BF16)
        dv = dv1_ref[0] + jnp.where(last, 0.0, dv0_ref[0]) + jnp.where(first, 0.0, dv2_ref[0])
        dz_ref[:, O_V:O_V + 128] = dv.astype(BF16)

    def full(a):
        return pl.BlockSpec(a.shape, lambda i: (0, 0))

    def rows(w):
        return pl.BlockSpec((tm, w), lambda i: (i, 0))

    parts = [pl.BlockSpec((1, tm, 128), lambda i: (0, jnp.minimum(i + 1, nt - 1), 0)),
             pl.BlockSpec((1, tm, 128), lambda i: (1, i, 0)),
             pl.BlockSpec((1, tm, 128), lambda i: (2, jnp.maximum(i - 1, 0), 0))]
    acc_spec = lambda r: pl.BlockSpec((r, D_CC), lambda i: (0, 0))
    return pl.pallas_call(
        body, name="mix_bwd", grid=(nt,),
        in_specs=(_halo_specs(t, tm, h, D_IN) + _halo_specs(t, tm, h, D_SC) + _halo_specs(t, tm, h, D_CC)
                  + [rows(D_ATT)] + parts + parts
                  + [full(scw), full(ccw), full(ccb), full(ccg), full(ccbb), rows(128), rows(128)]),
        out_specs=[rows(D_IN), acc_spec(SC_W), acc_spec(CC_W), acc_spec(3)],
        out_shape=[S((t, D_IN), BF16), S((SC_W, D_SC), F32), S((CC_W, D_CC), F32), S((3, D_CC), F32)],
        scratch_shapes=[pltpu.VMEM((ext, D_CC), F32), pltpu.VMEM((mid, D_CC), F32),
                        pltpu.VMEM((ext, D_SC), F32), pltpu.VMEM((ext, D_SC), F32)],
        compiler_params=_params(("arbitrary",)),
    )(z, z, z, dysc, dysc, dysc, dycc, dycc, dycc, dqr, dkp, dkp, dkp, dvp, dvp, dvp,
      scw, ccw, ccb, ccg, ccbb, cos, sin)


def _adamw(w, g, m, v):
    m = ADAM_B1 * m + (1.0 - ADAM_B1) * g
    v = ADAM_B2 * v + (1.0 - ADAM_B2) * (g * g)
    m_hat = m / (1.0 - ADAM_B1 ** ADAM_STEP)
    v_hat = v / (1.0 - ADAM_B2 ** ADAM_STEP)
    delta = -ADAM_LR * (m_hat / (jnp.sqrt(v_hat) + ADAM_EPS) + ADAM_WD * w)
    return delta, m, v


def _row_tile(rows):
    for cand in (256, 176, 128):
        if rows % cand == 0:
            return cand
    return rows


def _sum_adam(recv, w, m, v, transposed):
    nl, rows = len(recv), recv[0].shape[1]
    tile = 256 if transposed else _row_tile(rows)
    nc = (D if transposed else rows) // tile

    def body(*refs):
        w_ref, m_ref, v_ref, g_ref, d_ref, mo_ref, vo_ref = refs[nl:]
        for layer in range(nl):
            @pl.when(pl.program_id(0) == layer)
            def _(r_ref=refs[layer]):
                g = r_ref[0].astype(F32)
                for s in range(1, N_DEV):
                    g = g + r_ref[s].astype(F32)
                if transposed:
                    g = g.T
                g_ref[0] = g
                d_ref[0], mo_ref[0], vo_ref[0] = _adamw(w_ref[0], g, m_ref[0], v_ref[0])

    def held(layer):
        def at(l, c):
            return jnp.where(l == layer, c, jnp.where(l < layer, 0, nc - 1))
        if transposed:
            return pl.BlockSpec((N_DEV, rows, tile), lambda l, c: (0, 0, at(l, c)))
        return pl.BlockSpec((N_DEV, tile, D), lambda l, c: (0, at(l, c), 0))

    if transposed:
        blk = pl.BlockSpec((1, tile, rows), lambda l, c: (l, c, 0))
    else:
        blk = pl.BlockSpec((1, tile, D), lambda l, c: (l, c, 0))
    out = S(w.shape, F32)
    return pl.pallas_call(
        body, name="sum_adam_t" if transposed else "sum_adam", grid=(nl, nc),
        in_specs=[held(layer) for layer in range(nl)] + [blk, blk, blk], out_specs=[blk] * 4, out_shape=[out] * 4,
        compiler_params=_params(("arbitrary", "arbitrary")),
    )(*recv, w, m, v)


def _small_sum(gathered):
    rows = gathered.shape[1]

    def body(g_ref, o_ref):
        acc = g_ref[0]
        for s in range(1, N_DEV):
            acc = acc + g_ref[s]
        o_ref[...] = acc

    return pl.pallas_call(
        body, name="small_sum", in_specs=[pl.BlockSpec(gathered.shape, lambda: (0, 0, 0))],
        out_specs=pl.BlockSpec((rows, 128), lambda: (0, 0)), out_shape=S((rows, 128), F32),
    )(gathered)


def _small_adam(w, g, m, v):
    def body(w_ref, g_ref, m_ref, v_ref, d_ref, mo_ref, vo_ref):
        d_ref[...], mo_ref[...], vo_ref[...] = _adamw(w_ref[...], g_ref[...], m_ref[...], v_ref[...])

    spec = pl.BlockSpec(w.shape, lambda: (0, 0))
    return pl.pallas_call(
        body, name="small_adam", in_specs=[spec] * 4, out_specs=[spec] * 3, out_shape=[S(w.shape, F32)] * 3,
    )(w, g, m, v)


def _pack(pieces):
    flat = jnp.concatenate([p.reshape(-1).astype(F32) for p in pieces])
    n = flat.shape[0]
    rows = -(-n // 1024) * 8
    return jnp.pad(flat, (0, rows * 128 - n)).reshape(rows, 128)


def _unpack(packed, shapes):
    flat = packed.reshape(-1)
    out, o = [], 0
    for shp in shapes:
        n = int(np.prod(shp))
        out.append(flat[o:o + n].reshape(shp))
        o += n
    return out


def _rope_tables(t):
    half = HEAD_DIM // 2
    inv_freq = ROPE_THETA ** (-jnp.arange(half, dtype=F32) / half)
    ang = jnp.arange(t).astype(F32)[:, None] * inv_freq[None, :]
    cos, sin = jnp.cos(ang), jnp.sin(ang)
    cos128 = jnp.concatenate([cos, cos, cos, cos], axis=1)
    sin128 = jnp.concatenate([-sin, sin, -sin, sin], axis=1)
    return cos128, sin128


BIG = ("ffn1_w_gu", "ffn1_w_down", "w_in", "w_out", "ffn2_w_gu", "ffn2_w_down")
BIG_T = {"ffn1_w_gu": True, "ffn1_w_down": False, "w_in": True, "w_out": False, "ffn2_w_gu": True, "ffn2_w_down": False}
SWAPPED = ("ffn1_w_gu", "ffn2_w_gu")
REPLICATED = ("ln1_g", "ln1_b", "attn_sink", "cc_conv_b", "cc_ln_g", "cc_ln_b", "ln2_g", "ln2_b", "ln3_g", "ln3_b")
CONVS = ("sc_conv_w", "cc_conv_w")
WEIGHTS = ("ffn1_w_gu", "ffn1_w_down", "ln1_g", "ln1_b", "w_in", "sc_conv_w", "attn_sink", "cc_conv_w", "cc_conv_b",
           "cc_ln_g", "cc_ln_b", "w_out", "ln2_g", "ln2_b", "ffn2_w_gu", "ffn2_w_down", "ln3_g", "ln3_b")


def kernel(x, ffn1_w_gu, ffn1_w_down, ln1_g, ln1_b, w_in, sc_conv_w, attn_sink, cc_conv_w, cc_conv_b, cc_ln_g, cc_ln_b, w_out, ln2_g, ln2_b, ffn2_w_gu, ffn2_w_down, ln3_g, ln3_b, loss_target, m_ffn1_w_gu, m_ffn1_w_down, m_ln1_g, m_ln1_b, m_w_in, m_sc_conv_w, m_attn_sink, m_cc_conv_w, m_cc_conv_b, m_cc_ln_g, m_cc_ln_b, m_w_out, m_ln2_g, m_ln2_b, m_ffn2_w_gu, m_ffn2_w_down, m_ln3_g, m_ln3_b, v_ffn1_w_gu, v_ffn1_w_down, v_ln1_g, v_ln1_b, v_w_in, v_sc_conv_w, v_attn_sink, v_cc_conv_w, v_cc_conv_b, v_cc_ln_g, v_cc_ln_b, v_w_out, v_ln2_g, v_ln2_b, v_ffn2_w_gu, v_ffn2_w_down, v_ln3_g, v_ln3_b):
    args = dict(locals())
    w = {n: args[n] for n in WEIGHTS}
    mom = {n: args["m_" + n] for n in WEIGHTS}
    var = {n: args["v_" + n] for n in WEIGHTS}
    x0 = x[0]
    target = loss_target[0]
    t = x0.shape[0]
    idx = 4 * lax.axis_index("x") + 2 * lax.axis_index("y") + lax.axis_index("c")

    blocks = {(n, l): (w[n][l].T if BIG_T[n] else w[n][l]).astype(BF16) for l in range(DEPTH) for n in BIG}
    where = {}

    def start_stage(tag, members, after, extra=()):
        srcs = list(extra) + [blocks[m] for m in members]
        started = _send_start(srcs, [_own_slot(s) for s in srcs], _whole, f"gather_start_{tag}", after)
        for j, m in enumerate(members):
            where[m] = (started, len(extra) + j)
        return started

    def wait_stage(started, k, after, name):
        send, rcv, srcs, lands, _ = started
        return _recv_wait(send, rcv, [k], [srcs[k]], [lands[k]], _whole, after, name)[0]

    def weight(n, l, after):
        g = wait_stage(*where[n, l], after, f"gather_wait_{n}_{l}")
        return g.reshape(N_DEV * g.shape[1], g.shape[2])

    xf, xb = x0, x0.astype(BF16)
    first = start_stage("a", [("ffn1_w_gu", 0)], xb, extra=[_pack([w["sc_conv_w"], w["cc_conv_w"]])])
    conv_all = wait_stage(first, 0, xb, "gather_wait_convs").reshape(N_DEV, -1)
    n_sc = DEPTH * SC_W * 32
    scw_full = conv_all[:, :n_sc].reshape(N_DEV, DEPTH, SC_W, 32).transpose(1, 2, 0, 3).reshape(DEPTH, SC_W, D_SC)
    ccw_full = conv_all[:, n_sc:n_sc + DEPTH * CC_W * 32].reshape(N_DEV, DEPTH, CC_W, 32).transpose(1, 2, 0, 3).reshape(DEPTH, CC_W, D_CC)

    cos, sin = _rope_tables(t)
    row = lambda a, l: a[l].reshape(1, -1)

    saved, full = [], {}
    for l in range(DEPTH):
        sv = {"x0b": xb}
        token = None
        full["ffn1_w_gu", l] = weight("ffn1_w_gu", l, xb)
        if l == 0:
            token = start_stage("b", [("ffn1_w_down", 0), ("w_in", 0), ("w_out", 0)], full["ffn1_w_gu", l])[-1]
        gu1, a1 = _ffn_up(xb, full["ffn1_w_gu", l], token)
        full["ffn1_w_down", l] = weight("ffn1_w_down", l, a1)
        if l == 0:
            token = start_stage("c", [("ffn2_w_gu", 0), ("ffn2_w_down", 0)], full["ffn1_w_down", l])[-1]
        r1, x1, x1b = _ffn_down_ln(a1, full["ffn1_w_down", l], xf, row(ln1_g, l), row(ln1_b, l), token)
        full["w_in", l] = weight("w_in", l, x1b)
        z = _proj_in(x1b, full["w_in", l])
        ysc, ycc, qr, kr, vv = _mix_fwd(z, scw_full[l], ccw_full[l], row(cc_conv_b, l), row(cc_ln_g, l), row(cc_ln_b, l), cos, sin)
        if l == 0:
            token = start_stage("d", [("ffn1_w_gu", 1), ("ffn1_w_down", 1)], ysc)[-1]
        yatt = _attn_fwd(qr, kr, vv, attn_sink[l], token)
        full["w_out", l] = weight("w_out", l, yatt)
        ycat, r2, x2, x2b = _out_ln(ysc, yatt, ycc, full["w_out", l], x1, row(ln2_g, l), row(ln2_b, l))
        full["ffn2_w_gu", l] = weight("ffn2_w_gu", l, x2b)
        if l == 0:
            token = start_stage("e", [("w_in", 1), ("w_out", 1), ("ffn2_w_gu", 1), ("ffn2_w_down", 1)], full["ffn2_w_gu", l])[-1]
        gu2, a2 = _ffn_up(x2b, full["ffn2_w_gu", l], token)
        full["ffn2_w_down", l] = weight("ffn2_w_down", l, a2)
        r3, x3, x3b = _ffn_down_ln(a2, full["ffn2_w_down", l], x2, row(ln3_g, l), row(ln3_b, l))
        sv.update(gu1=gu1, a1=a1, r1=r1, x1b=x1b, z=z, qr=qr, kr=kr, vv=vv, yatt=yatt, ycat=ycat, r2=r2, x2b=x2b,
                  gu2=gu2, a2=a2, r3=r3)
        saved.append(sv)
        xf, xb = x3, x3b

    dy, sq = _loss_head(xf, target)
    loss = lax.psum(0.5 * jnp.sum(sq) / D, ("x", "y", "c"))

    sent = []
    small = {n: [None] * DEPTH for n in REPLICATED + CONVS}

    def send_grads(names, l, gs):
        srcs = [g.reshape(N_DEV, g.shape[0] // N_DEV, g.shape[1]) for g in gs]
        lands = [_own_slot(lax.dynamic_index_in_dim(s3, idx, 0, keepdims=False)) for s3 in srcs]
        started = _send_start(srcs, lands, _block_of, f"grads_start_{names[0]}_{l}", gs[-1])
        sent.append((names, l, started))
        return started[-1]

    for l in reversed(range(DEPTH)):
        sv = saved[l]
        dr, dfb, dh, dg, db = _ffn_bwd(dy, sv["r3"], row(ln3_g, l), full["ffn2_w_down", l], sv["gu2"])
        small["ln3_g"][l], small["ln3_b"][l] = dg, db
        token = send_grads(("ffn2_w_down", "ffn2_w_gu"), l,
                           [_wgrad(sv["a2"], dfb, F // 2), _wgrad(dh, sv["x2b"], F // 2)])
        dy = _dx(dr, dh, full["ffn2_w_gu", l], token)

        dr, dmb, dysc, dyatt, dycc, dg, db = _out_bwd(dy, sv["r2"], row(ln2_g, l), full["w_out", l])
        small["ln2_g"][l], small["ln2_b"][l] = dg, db
        g_out = _wgrad(sv["ycat"], dmb, D)
        dqr, dkp, dvp, dsink = _attn_bwd(sv["qr"], sv["kr"], sv["vv"], attn_sink[l], dyatt, sv["yatt"])
        small["attn_sink"][l] = dsink[:, 0]
        dz, dscw, dccw, dvec = _mix_bwd(sv["z"], dysc, dycc, dqr, dkp, dvp, scw_full[l], ccw_full[l],
                                        row(cc_conv_b, l), row(cc_ln_g, l), row(cc_ln_b, l), cos, sin)
        small["sc_conv_w"][l], small["cc_conv_w"][l] = dscw, dccw
        small["cc_conv_b"][l], small["cc_ln_g"][l], small["cc_ln_b"][l] = dvec[0], dvec[1], dvec[2]
        token = send_grads(("w_out", "w_in"), l, [g_out, _wgrad(dz, sv["x1b"], D)])
        dy = _dx(dr, dz, full["w_in", l], token)

        dr, dfb, dh, dg, db = _ffn_bwd(dy, sv["r1"], row(ln1_g, l), full["ffn1_w_down", l], sv["gu1"])
        small["ln1_g"][l], small["ln1_b"][l] = dg, db
        if l > 0:
            token = send_grads(("ffn1_w_down", "ffn1_w_gu"), l,
                               [_wgrad(sv["a1"], dfb, F // 2), _wgrad(dh, sv["x0b"], F // 2)])
        else:
            token = send_grads(("ffn1_w_gu",), l, [_wgrad(dh, sv["x0b"], F // 2)])
            token = send_grads(("ffn1_w_down",), l, [_wgrad(sv["a1"], dfb, F // 2, token)])
        dy = _dx(dr, dh, full["ffn1_w_gu", l], token)
    grad_x = dy[None]

    small_names = REPLICATED + CONVS
    small_shapes = [(DEPTH,) + tuple(np.shape(small[n][0].reshape(-1))) for n in small_names]
    small_pack = _pack([jnp.stack([small[n][l].reshape(-1) for l in range(DEPTH)]) for n in small_names])
    small_all = _all_gather([small_pack], "gather_small_grads")[0]

    recv = {n: [None] * DEPTH for n in BIG}
    grads, deltas, new_m, new_v = {}, {}, {}, {}

    def receive(upto, after):
        while len(sent) > upto:
            names, l, (send, rcv, srcs, lands, _) = sent.pop(0)
            got = _recv_wait(send, rcv, list(range(len(names))), srcs, lands, _block_of, after, f"grads_wait_{names[0]}_{l}")
            for n, g in zip(names, got):
                recv[n][l] = g

    def update(n):
        if n in SWAPPED:
            outs = _sum_adam(recv[n], *[jnp.swapaxes(a, 1, 2) for a in (w[n], mom[n], var[n])], False)
            grads[n], deltas[n], new_m[n], new_v[n] = [jnp.swapaxes(a, 1, 2) for a in outs]
        else:
            grads[n], deltas[n], new_m[n], new_v[n] = _sum_adam(recv[n], w[n], mom[n], var[n], BIG_T[n])

    receive(2, dy)
    for n in ("ffn2_w_down", "ffn2_w_gu", "w_out", "w_in"):
        update(n)
    receive(0, new_v["w_in"])
    update("ffn1_w_gu")
    update("ffn1_w_down")
    small_total = _unpack(_small_sum(small_all), small_shapes)
    for n, g in zip(small_names, small_total):
        if n in CONVS:
            taps = SC_W if n == "sc_conv_w" else CC_W
            g = lax.dynamic_slice_in_dim(g.reshape(DEPTH, taps, D_SC), idx * 32, 32, axis=2)
        grads[n] = g.reshape(w[n].shape)
    wp = _pack([w[n] for n in small_names])
    gp = _pack([grads[n] for n in small_names])
    mp = _pack([mom[n] for n in small_names])
    vp = _pack([var[n] for n in small_names])
    shapes = [w[n].shape for n in small_names]
    for dst, packed in zip((deltas, new_m, new_v), _small_adam(wp, gp, mp, vp)):
        for n, a in zip(small_names, _unpack(packed, shapes)):
            dst[n] = a

    return (loss, grad_x, *[grads[n] for n in WEIGHTS], *[deltas[n] for n in WEIGHTS],
            *[new_m[n] for n in WEIGHTS], *[new_v[n] for n in WEIGHTS])
```

```python
import functools

import jax
import jax.numpy as jnp
import numpy as np
from jax import lax
from jax.experimental import pallas as pl
from jax.experimental.pallas import tpu as pltpu

F32 = jnp.float32
BF16 = jnp.bfloat16
S = jax.ShapeDtypeStruct

N_DEV = 8
DEPTH = 2
D = 1024
F = 2816
D_IN = 2048
HEAD_DIM = 64
N_Q_HEADS = 8
N_KV_HEADS = 2
GROUP = 4
D_SC = 256
D_ATT = 512
D_CC = 256
CC_W = 31
SC_W = 3
BLOCK = 128
ROPE_THETA = 10000.0
LN_EPS = 1e-5
ALPHA = (2.0 * DEPTH) ** 0.25
ADAM_LR = 0.001
ADAM_B1 = 0.9
ADAM_B2 = 0.999
ADAM_EPS = 1e-08
ADAM_WD = 0.01
ADAM_STEP = 10

O_SCB, O_SCC, O_SCH, O_Q, O_K, O_V, O_CCA, O_CCG = 0, 256, 512, 768, 1280, 1408, 1536, 1792

V7X_VMEM_BYTES = 64 * 1024 * 1024
VMEM_LIMIT = V7X_VMEM_BYTES - 8 * 1024 * 1024
TOKEN_TILE = 256
WIDE_TILE = 512
WGRAD_TOKENS = 2048
FFN_CHUNKS = (0, 768, 1536, 2176, 2816)
MIX_BWD_TILE = 128
HALO_FWD = 16
HALO_BWD = 16
CONV_ROWS = 128
SUBLANES = 8

NT = (((1,), (1,)), ((), ()))
TN = (((0,), (0,)), ((), ()))
MESH = pl.DeviceIdType.MESH


def _params(sem=None):
    return pltpu.CompilerParams(dimension_semantics=sem, vmem_limit_bytes=VMEM_LIMIT)


def _sigmoid(v):
    return 1.0 / (1.0 + jnp.exp(-v))


def _ln_stats(r):
    mu = jnp.mean(r, axis=-1, keepdims=True)
    d = r - mu
    var = jnp.mean(d * d, axis=-1, keepdims=True)
    rstd = lax.rsqrt(var + LN_EPS)
    return d * rstd, rstd


def _ln_bwd(dn, xhat, rstd, gam):
    dxh = dn * gam
    return rstd * (dxh - jnp.mean(dxh, axis=-1, keepdims=True) - xhat * jnp.mean(dxh * xhat, axis=-1, keepdims=True))


def _swap_halves(v):
    n = v.shape[-1]
    lane = lax.broadcasted_iota(jnp.int32, v.shape, v.ndim - 1) % HEAD_DIM
    return jnp.where(lane < HEAD_DIM // 2, pltpu.roll(v, n - HEAD_DIM // 2, v.ndim - 1), pltpu.roll(v, HEAD_DIM // 2, v.ndim - 1))


def _wide(tab, n):
    return tab if n == 128 else jnp.concatenate([tab] * (n // 128), axis=1)


def _me():
    x, y, c = lax.axis_index("x"), lax.axis_index("y"), lax.axis_index("c")
    return x, y, c


def _peer(rel):
    x, y, c = _me()
    px = 1 - x if rel & 4 else x
    py = 1 - y if rel & 2 else y
    pc = 1 - c if rel & 1 else c
    return (px, py, pc), 4 * px + 2 * py + pc


def _exchange(srcs, dsts_shape, dst_index, src_of, dst_of, name):
    n = len(srcs)

    def body(*refs):
        ins = refs[:n]
        outs = [refs[n + dst_index[k]] for k in range(n)]
        send, recv, lsem = refs[n + len(dsts_shape):]
        x, y, c = _me()
        me = 4 * x + 2 * y + c
        local = [pltpu.make_async_copy(src_of(ins[k], k, me), dst_of(outs[k], k, me), lsem.at[k]) for k in range(n)]
        for cp in local:
            cp.start()
        sends, recvs = [], []
        for k in range(n):
            for rel in range(1, N_DEV):
                peer, pidx = _peer(rel)
                sends.append(pltpu.make_async_remote_copy(
                    src_ref=src_of(ins[k], k, pidx), dst_ref=dst_of(outs[k], k, me),
                    send_sem=send.at[k, rel - 1], recv_sem=recv.at[k, rel - 1], device_id=peer, device_id_type=MESH))
                recvs.append(pltpu.make_async_remote_copy(
                    src_ref=src_of(ins[k], k, pidx), dst_ref=dst_of(outs[k], k, pidx),
                    send_sem=send.at[k, rel - 1], recv_sem=recv.at[k, rel - 1], device_id=peer, device_id_type=MESH))
        for cp in sends:
            cp.start()
        for cp in recvs:
            cp.wait_recv()
        for cp in sends:
            cp.wait_send()
        for cp in local:
            cp.wait()

    hbm = pl.BlockSpec(memory_space=pltpu.HBM)
    return pl.pallas_call(
        body, name=name, in_specs=[hbm] * n, out_specs=[hbm] * len(dsts_shape), out_shape=dsts_shape,
        scratch_shapes=[pltpu.SemaphoreType.DMA((n, N_DEV - 1)), pltpu.SemaphoreType.DMA((n, N_DEV - 1)),
                        pltpu.SemaphoreType.DMA((n,))],
    )(*srcs)


def _all_gather(blocks, name):
    shapes = [S((N_DEV,) + b.shape, b.dtype) for b in blocks]
    return _exchange(blocks, shapes, list(range(len(blocks))), lambda ref, k, idx: ref, lambda ref, k, idx: ref.at[idx], name)


HBM_SPEC = pl.BlockSpec(memory_space=pltpu.HBM)
SEM_SPEC = pl.BlockSpec(memory_space=pltpu.SEMAPHORE)
ANY_SPEC = pl.BlockSpec(memory_space=pl.ANY)
EFFECT = pltpu.SideEffectType.DATAFLOW_SIDE_EFFECTING
N_PEERS = N_DEV - 1


def _own_slot(block):
    x, y, c = _me()
    return lax.dynamic_update_index_in_dim(lax.empty((N_DEV,) + block.shape, block.dtype), block, 4 * x + 2 * y + c, 0)


def _follow(body, n_in, in_specs, operands, after):
    if after is None:
        return body, list(in_specs), list(operands)

    def tail(*refs):
        return body(*refs[:n_in], *refs[n_in + 1:])

    return tail, list(in_specs) + [ANY_SPEC], list(operands) + [after]


def _send_start(srcs, lands, src_of, name, after):
    n = len(srcs)

    def body(*refs):
        ins, zones = refs[:n], refs[n:2 * n]
        send, recv = refs[2 * n + 1], refs[2 * n + 2]
        token = refs[-1]
        x, y, c = _me()
        me = 4 * x + 2 * y + c
        for k in range(n):
            for rel in range(1, N_DEV):
                peer, pidx = _peer(rel)
                pltpu.make_async_remote_copy(
                    src_ref=src_of(ins[k], pidx), dst_ref=zones[k].at[me],
                    send_sem=send.at[k * N_PEERS + rel - 1], recv_sem=recv.at[k * N_PEERS + rel - 1],
                    device_id=peer, device_id_type=MESH).start()
        token[...] = jnp.zeros_like(token)

    outs = pl.pallas_call(
        body, name=name,
        out_shape=(pltpu.SemaphoreType.DMA((n * N_PEERS,)), pltpu.SemaphoreType.DMA((n * N_PEERS,)),
                   *[pltpu.HBM(a.shape, a.dtype) for a in srcs], *[pltpu.HBM(a.shape, a.dtype) for a in lands],
                   S((8, 128), F32)),
        in_specs=[HBM_SPEC] * (2 * n) + [ANY_SPEC],
        out_specs=(SEM_SPEC, SEM_SPEC, *[HBM_SPEC] * (2 * n), pl.BlockSpec(memory_space=pltpu.VMEM)),
        input_output_aliases={i: 2 + i for i in range(2 * n)},
        compiler_params=pltpu.CompilerParams(has_side_effects=EFFECT),
    )(*[pltpu.with_memory_space_constraint(a, pltpu.HBM) for a in list(srcs) + list(lands)], after)
    return outs[0], outs[1], list(outs[2:2 + n]), list(outs[2 + n:2 + 2 * n]), outs[-1]


def _recv_wait(send, recv, ks, srcs, lands, src_of, after, name):
    n = len(ks)

    def body(*refs):
        ins, zones = refs[:n], refs[n:2 * n]
        send_sems, recv_sems = refs[2 * n], refs[2 * n + 1]
        for j, k in enumerate(ks):
            for rel in range(1, N_DEV):
                peer, pidx = _peer(rel)
                cp = pltpu.make_async_remote_copy(
                    src_ref=src_of(ins[j], pidx), dst_ref=zones[j].at[pidx],
                    send_sem=send_sems.at[k * N_PEERS + rel - 1], recv_sem=recv_sems.at[k * N_PEERS + rel - 1],
                    device_id=peer, device_id_type=MESH)
                cp.wait_send()
                cp.wait_recv()

    outs = pl.pallas_call(
        body, name=name,
        out_shape=(*[pltpu.HBM(a.shape, a.dtype) for a in srcs], *[pltpu.HBM(a.shape, a.dtype) for a in lands]),
        in_specs=[HBM_SPEC] * (2 * n) + [SEM_SPEC, SEM_SPEC, ANY_SPEC], out_specs=[HBM_SPEC] * (2 * n),
        input_output_aliases={i: i for i in range(2 * n)},
        compiler_params=pltpu.CompilerParams(has_side_effects=EFFECT),
    )(*srcs, *lands, send, recv, after)
    return list(outs[n:])


def _whole(ref, idx):
    return ref


def _block_of(ref, idx):
    return ref.at[idx]


def _resident(shape):
    return pl.BlockSpec(shape, lambda i: (0,) * len(shape), pipeline_mode=pl.Buffered(1))


def _ffn_up(xb, wgut, after=None):
    t = xb.shape[0]
    tm = min(WIDE_TILE, t)
    half = F // 2

    def body(x_ref, w_ref, gu_ref, a_ref):
        x = x_ref[...]
        for ch in range(2):
            lo = ch * half
            g = lax.dot_general(x, w_ref[lo:lo + half, :], NT, preferred_element_type=F32)
            u = lax.dot_general(x, w_ref[F + lo:F + lo + half, :], NT, preferred_element_type=F32)
            gu_ref[:, lo:lo + half] = g.astype(BF16)
            gu_ref[:, F + lo:F + lo + half] = u.astype(BF16)
            a_ref[:, lo:lo + half] = (g * _sigmoid(g) * u).astype(BF16)

    body, in_specs, operands = _follow(
        body, 2, [pl.BlockSpec((tm, D), lambda i: (i, 0)), _resident((2 * F, D))], [xb, wgut], after)
    return pl.pallas_call(
        body, name="ffn_up", grid=(t // tm,), in_specs=in_specs,
        out_specs=[pl.BlockSpec((tm, 2 * F), lambda i: (i, 0)), pl.BlockSpec((tm, F), lambda i: (i, 0))],
        out_shape=[S((t, 2 * F), BF16), S((t, F), BF16)], compiler_params=_params(("parallel",)),
    )(*operands)


def _residual_ln_out(x_ref, f, scale, g_ref, b_ref, r_ref, y_ref, yb_ref):
    r = ALPHA * x_ref[...] + scale * f
    xhat, _ = _ln_stats(r)
    y = xhat * g_ref[...] + b_ref[...]
    r_ref[...] = r
    y_ref[...] = y
    yb_ref[...] = y.astype(BF16)


def _ffn_down_ln(a, wd, x, gam, bet, after=None):
    t = x.shape[0]
    tm = min(WIDE_TILE, t)

    def body(a_ref, w_ref, x_ref, g_ref, b_ref, r_ref, y_ref, yb_ref):
        f = jnp.dot(a_ref[...], w_ref[...], preferred_element_type=F32)
        _residual_ln_out(x_ref, f, 0.5, g_ref, b_ref, r_ref, y_ref, yb_ref)

    row = pl.BlockSpec((tm, D), lambda i: (i, 0))
    vec = pl.BlockSpec((1, D), lambda i: (0, 0))
    body, in_specs, operands = _follow(
        body, 5, [pl.BlockSpec((tm, F), lambda i: (i, 0)), _resident((F, D)), row, vec, vec],
        [a, wd, x, gam, bet], after)
    return pl.pallas_call(
        body, name="ffn_down_ln", grid=(t // tm,), in_specs=in_specs,
        out_specs=[row, row, row], out_shape=[S((t, D), F32), S((t, D), F32), S((t, D), BF16)],
        compiler_params=_params(("parallel",)),
    )(*operands)


def _proj_in(xb, wint):
    t = xb.shape[0]
    tm = min(WIDE_TILE, t)

    def body(x_ref, w_ref, z_ref):
        z_ref[...] = lax.dot_general(x_ref[...], w_ref[...], NT, preferred_element_type=F32)

    return pl.pallas_call(
        body, name="proj_in", grid=(t // tm,),
        in_specs=[pl.BlockSpec((tm, D), lambda i: (i, 0)), _resident((D_IN, D))],
        out_specs=pl.BlockSpec((tm, D_IN), lambda i: (i, 0)), out_shape=S((t, D_IN), F32),
        compiler_params=_params(("parallel",)),
    )(xb, wint)


def _halo_specs(t, tm, halo, width):
    per = tm // halo
    last = t // halo - 1
    return [pl.BlockSpec((tm, width), lambda i: (i, 0)),
            pl.BlockSpec((halo, width), lambda i: (jnp.maximum(i * per - 1, 0), 0)),
            pl.BlockSpec((halo, width), lambda i: (jnp.minimum((i + 1) * per, last), 0))]


def _taps_aligned(w_ref, x_ref, p_ref, offsets, rows):
    for r in range(SUBLANES):
        acc = jnp.zeros((rows + SUBLANES, x_ref.shape[1]), F32)
        for j, o in enumerate(offsets):
            if o % SUBLANES == r:
                acc = acc + w_ref[j:j + 1, :] * x_ref[pl.ds(o - r, rows + SUBLANES), :]
        p_ref[r] = acc
    out = p_ref[0, 0:rows, :]
    for r in range(1, SUBLANES):
        out = out + p_ref[r, pl.ds(r, rows), :]
    return out


def _mix_fwd(z, scw, ccw, ccb, ccg, ccbb, cos, sin):
    t = z.shape[0]
    tm = min(TOKEN_TILE, t)
    nt = t // tm
    h = HALO_FWD
    rc = min(CONV_ROWS, tm)

    def body(z_ref, zp_ref, zn_ref, scw_ref, ccw_ref, ccb_ref, ccg_ref, ccbb_ref, cos_ref, sin_ref,
             ysc_ref, ycc_ref, c_ref, q_ref, k_ref, v_ref, u_s, ch_s, p_s):
        i = pl.program_id(0)
        pz = jnp.where(i == 0, 0.0, zp_ref[...])
        nz = jnp.where(i == nt - 1, 0.0, zn_ref[...])

        def u_of(zz):
            return zz[:, O_CCA:O_CCA + D_CC] * _sigmoid(zz[:, O_CCG:O_CCG + D_CC])

        def ch_of(zz):
            return zz[:, O_SCC:O_SCC + D_SC] * zz[:, O_SCH:O_SCH + D_SC]

        u_s[0:h, :] = u_of(pz)
        u_s[h:h + tm, :] = z_ref[:, O_CCA:O_CCA + D_CC] * _sigmoid(z_ref[:, O_CCG:O_CCG + D_CC])
        u_s[h + tm:2 * h + tm, :] = u_of(nz)
        ch_s[0:h, :] = ch_of(pz)
        ch_s[h:h + tm, :] = z_ref[:, O_SCC:O_SCC + D_SC] * z_ref[:, O_SCH:O_SCH + D_SC]
        ch_s[h + tm:2 * h + tm, :] = ch_of(nz)
        for r0 in range(0, tm, rc):
            c = _taps_aligned(ccw_ref, u_s, p_s, [r0 + h + j - CC_W // 2 for j in range(CC_W)], rc) + ccb_ref[...]
            c_ref[r0:r0 + rc, :] = c
            xhat, _ = _ln_stats(c)
            n = xhat * ccg_ref[...] + ccbb_ref[...]
            ycc_ref[r0:r0 + rc, :] = (n * _sigmoid(n)).astype(BF16)
            acc = jnp.zeros((rc, D_SC), F32)
            for j in range(SC_W):
                acc = acc + scw_ref[j:j + 1, :] * ch_s[pl.ds(r0 + h + j - SC_W // 2, rc), :]
            ysc_ref[r0:r0 + rc, :] = (z_ref[r0:r0 + rc, O_SCB:O_SCB + D_SC] * acc).astype(BF16)
        q = z_ref[:, O_Q:O_Q + D_ATT]
        q_ref[...] = ((q * _wide(cos_ref[...], D_ATT) + _swap_halves(q) * _wide(sin_ref[...], D_ATT)) * (HEAD_DIM ** -0.5)).astype(BF16)
        k = z_ref[:, O_K:O_K + 128]
        k_ref[...] = (k * cos_ref[...] + _swap_halves(k) * sin_ref[...]).astype(BF16)
        v_ref[...] = z_ref[:, O_V:O_V + 128].astype(BF16)

    def full(a):
        return pl.BlockSpec(a.shape, lambda i: (0, 0))

    def rows(w):
        return pl.BlockSpec((tm, w), lambda i: (i, 0))

    return pl.pallas_call(
        body, name="mix_fwd", grid=(nt,),
        in_specs=_halo_specs(t, tm, h, D_IN) + [full(scw), full(ccw), full(ccb), full(ccg), full(ccbb), rows(128), rows(128)],
        out_specs=[rows(D_SC), rows(D_CC), rows(D_CC), rows(D_ATT), rows(128), rows(128)],
        out_shape=[S((t, D_SC), BF16), S((t, D_CC), BF16), S((t, D_CC), F32), S((t, D_ATT), BF16), S((t, 128), BF16),
                   S((t, 128), BF16)],
        scratch_shapes=[pltpu.VMEM((tm + 2 * h, D_CC), F32), pltpu.VMEM((tm + 2 * h, D_SC), F32),
                        pltpu.VMEM((SUBLANES, rc + SUBLANES, D_CC), F32)],
        compiler_params=_params(("parallel",)),
    )(z, z, z, scw, ccw, ccb, ccg, ccbb, cos, sin)


def _band_specs(nb, width):
    return [pl.BlockSpec((BLOCK, width), lambda n: (jnp.maximum(n - 1, 0), 0)),
            pl.BlockSpec((BLOCK, width), lambda n: (n, 0)),
            pl.BlockSpec((BLOCK, width), lambda n: (jnp.minimum(n + 1, nb - 1), 0))]


def _band_bias(n, nb, bias_s):
    qpos = lax.broadcasted_iota(jnp.int32, (BLOCK, 3 * BLOCK), 0)
    col = lax.broadcasted_iota(jnp.int32, (BLOCK, 3 * BLOCK), 1)
    ok = jnp.abs(qpos - (col - BLOCK)) <= BLOCK
    ok = jnp.logical_and(ok, jnp.logical_or(col >= BLOCK, n > 0))
    ok = jnp.logical_and(ok, jnp.logical_or(col < 2 * BLOCK, n < nb - 1))
    bias_s[...] = jnp.where(ok, 0.0, -1e30)


def _band_cat(refs, kvh):
    return jnp.concatenate([r[:, kvh * HEAD_DIM:(kvh + 1) * HEAD_DIM] for r in refs], axis=0)


def _head_scores(q_ref, kc, h, bias_s):
    qh = q_ref[:, h * HEAD_DIM:(h + 1) * HEAD_DIM]
    return qh, lax.dot_general(qh, kc, NT, preferred_element_type=F32) + bias_s[...]


def _softmax_parts(s, sk):
    m = jnp.maximum(jnp.max(s, axis=-1, keepdims=True), sk)
    p = jnp.exp(s - m)
    ps = jnp.exp(sk - m)
    return p, ps, jnp.sum(p, axis=-1, keepdims=True) + ps


def _head_probs(q_ref, kc, sink_ref, h, bias_s):
    qh, s = _head_scores(q_ref, kc, h, bias_s)
    return (qh,) + _softmax_parts(s, sink_ref[h])


def _attn_fwd(qr, kr, vv, sink, after=None):
    t = qr.shape[0]
    nb = t // BLOCK

    def body(q_ref, kp_ref, ko_ref, kn_ref, vp_ref, vo_ref, vn_ref, sink_ref, o_ref, bias_s):
        _band_bias(pl.program_id(0), nb, bias_s)
        kcs = [_band_cat((kp_ref, ko_ref, kn_ref), kvh) for kvh in range(N_KV_HEADS)]
        vcs = [_band_cat((vp_ref, vo_ref, vn_ref), kvh) for kvh in range(N_KV_HEADS)]
        s_next = _head_scores(q_ref, kcs[0], 0, bias_s)[1]
        for h in range(N_Q_HEADS):
            s = s_next
            if h + 1 < N_Q_HEADS:
                s_next = _head_scores(q_ref, kcs[(h + 1) // GROUP], h + 1, bias_s)[1]
            p, _, denom = _softmax_parts(s, sink_ref[h])
            o = jnp.dot(p.astype(BF16), vcs[h // GROUP], preferred_element_type=F32) / denom
            o_ref[:, h * HEAD_DIM:(h + 1) * HEAD_DIM] = o.astype(BF16)

    qspec = pl.BlockSpec((BLOCK, D_ATT), lambda n: (n, 0))
    body, in_specs, operands = _follow(
        body, 8, [qspec] + _band_specs(nb, 128) + _band_specs(nb, 128) + [pl.BlockSpec(memory_space=pltpu.SMEM)],
        [qr, kr, kr, kr, vv, vv, vv, sink], after)
    return pl.pallas_call(
        body, name="attn_fwd", grid=(nb,), in_specs=in_specs,
        out_specs=qspec, out_shape=S((t, D_ATT), BF16), scratch_shapes=[pltpu.VMEM((BLOCK, 3 * BLOCK), F32)],
        compiler_params=_params(("parallel",)),
    )(*operands)


def _out_ln(ysc, yatt, ycc, wout, x, gam, bet):
    t = x.shape[0]
    tm = min(WIDE_TILE, t)

    def body(sc_ref, at_ref, cc_ref, w_ref, x_ref, g_ref, b_ref, cat_ref, r_ref, y_ref, yb_ref):
        cat = jnp.concatenate([sc_ref[...], at_ref[...], cc_ref[...]], axis=1)
        cat_ref[...] = cat
        f = jnp.dot(cat, w_ref[...], preferred_element_type=F32)
        _residual_ln_out(x_ref, f, 1.0, g_ref, b_ref, r_ref, y_ref, yb_ref)

    def rows(w):
        return pl.BlockSpec((tm, w), lambda i: (i, 0))

    vec = pl.BlockSpec((1, D), lambda i: (0, 0))
    return pl.pallas_call(
        body, name="out_ln", grid=(t // tm,),
        in_specs=[rows(D_SC), rows(D_ATT), rows(D_CC), _resident((D, D)), rows(D), vec, vec],
        out_specs=[rows(D), rows(D), rows(D), rows(D)],
        out_shape=[S((t, D), BF16), S((t, D), F32), S((t, D), F32), S((t, D), BF16)],
        compiler_params=_params(("parallel",)),
    )(ysc, yatt, ycc, wout, x, gam, bet)


def _loss_head(y, target):
    t = y.shape[0]
    tm = min(TOKEN_TILE, t)

    def body(y_ref, t_ref, dy_ref, part_ref):
        e = y_ref[...] - t_ref[...]
        dy_ref[...] = e / D

        @pl.when(pl.program_id(0) == 0)
        def _():
            part_ref[...] = jnp.zeros_like(part_ref)

        part_ref[...] += jnp.sum(e * e, axis=0, keepdims=True)

    row = pl.BlockSpec((tm, D), lambda i: (i, 0))
    return pl.pallas_call(
        body, name="loss_head", grid=(t // tm,), in_specs=[row, row],
        out_specs=[row, pl.BlockSpec((1, D), lambda i: (0, 0))], out_shape=[S((t, D), F32), S((1, D), F32)],
        compiler_params=_params(("arbitrary",)),
    )(y, target)


def _ln_bwd_block(dy_ref, r_ref, g_ref, dgam_ref, dbet_ref):
    xhat, rstd = _ln_stats(r_ref[...])
    dy = dy_ref[...]

    @pl.when(pl.program_id(0) == 0)
    def _():
        dgam_ref[...] = jnp.zeros_like(dgam_ref)
        dbet_ref[...] = jnp.zeros_like(dbet_ref)

    dgam_ref[...] += jnp.sum(dy * xhat, axis=0, keepdims=True)
    dbet_ref[...] += jnp.sum(dy, axis=0, keepdims=True)
    return _ln_bwd(dy, xhat, rstd, g_ref[...])


def _ffn_bwd(dy, r, gam, wd, gu):
    t = dy.shape[0]
    tm = min(TOKEN_TILE, t)
    chunks = list(zip(FFN_CHUNKS[:-1], FFN_CHUNKS[1:]))

    def body(dy_ref, r_ref, g_ref, w_ref, gu_ref, dr_ref, df_ref, dh_ref, dgam_ref, dbet_ref):
        dr = _ln_bwd_block(dy_ref, r_ref, g_ref, dgam_ref, dbet_ref)
        dr_ref[...] = dr
        dfb = (0.5 * dr).astype(BF16)
        df_ref[...] = dfb
        for lo, hi in chunks:
            da = lax.dot_general(dfb, w_ref[lo:hi, :], NT, preferred_element_type=F32)
            g = gu_ref[:, lo:hi].astype(F32)
            u = gu_ref[:, F + lo:F + hi].astype(F32)
            sg = _sigmoid(g)
            dh_ref[:, lo:hi] = (da * u * (sg * (1.0 + g * (1.0 - sg)))).astype(BF16)
            dh_ref[:, F + lo:F + hi] = (da * (g * sg)).astype(BF16)

    row = pl.BlockSpec((tm, D), lambda i: (i, 0))
    vec = pl.BlockSpec((1, D), lambda i: (0, 0))
    wide = pl.BlockSpec((tm, 2 * F), lambda i: (i, 0))
    return pl.pallas_call(
        body, name="ffn_bwd", grid=(t // tm,),
        in_specs=[row, row, vec, _resident((F, D)), wide],
        out_specs=[row, row, wide, vec, vec],
        out_shape=[S((t, D), F32), S((t, D), BF16), S((t, 2 * F), BF16), S((1, D), F32), S((1, D), F32)],
        compiler_params=_params(("arbitrary",)),
    )(dy, r, gam, wd, gu)


def _ffn_bwd_dx(dy, r, gam, wd, gu, wgut, after=None):
    t = dy.shape[0]
    tm = min(TOKEN_TILE, t)
    n = t // tm
    chunks = list(zip(FFN_CHUNKS[:-1], FFN_CHUNKS[1:]))

    def body(dy_ref, r_ref, g_ref, w_ref, gu_ref, wg_ref, dx_ref, df_ref, dh_ref, dgam_ref, dbet_ref,
             keep_a, keep_b, dr_keep):
        i = pl.program_id(0)

        @pl.when(i == 0)
        def _():
            keep_a[...] = jnp.zeros_like(keep_a)
            keep_b[...] = jnp.zeros_like(keep_b)
            dr_keep[...] = jnp.zeros_like(dr_keep)
            dgam_ref[...] = jnp.zeros_like(dgam_ref)
            dbet_ref[...] = jnp.zeros_like(dbet_ref)

        def step(prev, cur):
            def to_dx(c):
                lo, hi = chunks[c]
                return (jnp.dot(prev[:, lo:hi], wg_ref[lo:hi, :], preferred_element_type=F32)
                        + jnp.dot(prev[:, F + lo:F + hi], wg_ref[F + lo:F + hi, :], preferred_element_type=F32))

            acc = ALPHA * dr_keep[...]
            xhat, rstd = _ln_stats(r_ref[...])
            dy_t = dy_ref[...]
            live = jnp.where(i < n, 1.0, 0.0)
            dgam_ref[...] += live * jnp.sum(dy_t * xhat, axis=0, keepdims=True)
            dbet_ref[...] += live * jnp.sum(dy_t, axis=0, keepdims=True)
            dr = _ln_bwd(dy_t, xhat, rstd, g_ref[...])
            dfb = (0.5 * dr).astype(BF16)
            df_ref[...] = dfb

            def down(c):
                return lax.dot_general(dfb, w_ref[chunks[c][0]:chunks[c][1], :], NT, preferred_element_type=F32)

            das = {0: down(0), 1: down(1)}
            for c, (lo, hi) in enumerate(chunks):
                if c + 2 < len(chunks):
                    das[c + 2] = down(c + 2)
                acc = acc + to_dx(c)
                da = das.pop(c)
                g = gu_ref[:, lo:hi].astype(F32)
                u = gu_ref[:, F + lo:F + hi].astype(F32)
                sg = _sigmoid(g)
                dg = (da * u * (sg * (1.0 + g * (1.0 - sg)))).astype(BF16)
                du = (da * (g * sg)).astype(BF16)
                dh_ref[:, lo:hi] = dg
                dh_ref[:, F + lo:F + hi] = du
                cur[:, lo:hi] = dg
                cur[:, F + lo:F + hi] = du
            dx_ref[...] = acc
            dr_keep[...] = dr

        @pl.when(i % 2 == 0)
        def _():
            step(keep_b, keep_a)

        @pl.when(i % 2 == 1)
        def _():
            step(keep_a, keep_b)

    cur_row = lambda i: (jnp.minimum(i, n - 1), 0)
    row = pl.BlockSpec((tm, D), cur_row)
    vec = pl.BlockSpec((1, D), lambda i: (0, 0))
    wide = pl.BlockSpec((tm, 2 * F), cur_row)
    body, in_specs, operands = _follow(
        body, 6, [row, row, vec, _resident((F, D)), wide, _resident((2 * F, D))], [dy, r, gam, wd, gu, wgut], after)
    return pl.pallas_call(
        body, name="ffn_bwd_dx", grid=(n + 1,), in_specs=in_specs,
        out_specs=[pl.BlockSpec((tm, D), lambda i: (jnp.maximum(i - 1, 0), 0)), row, wide, vec, vec],
        out_shape=[S((t, D), F32), S((t, D), BF16), S((t, 2 * F), BF16), S((1, D), F32), S((1, D), F32)],
        scratch_shapes=[pltpu.VMEM((tm, 2 * F), BF16), pltpu.VMEM((tm, 2 * F), BF16), pltpu.VMEM((tm, D), F32)],
        compiler_params=_params(("arbitrary",)),
    )(*operands)


def _dx(dr, dh, w, after=None):
    t = dr.shape[0]
    tm = min(WIDE_TILE, t)
    kk = dh.shape[1]

    def body(dr_ref, dh_ref, w_ref, o_ref):
        o_ref[...] = ALPHA * dr_ref[...] + jnp.dot(dh_ref[...], w_ref[...], preferred_element_type=F32)

    row = pl.BlockSpec((tm, D), lambda i: (i, 0))
    body, in_specs, operands = _follow(
        body, 3, [row, pl.BlockSpec((tm, kk), lambda i: (i, 0)), _resident((kk, D))], [dr, dh, w], after)
    return pl.pallas_call(
        body, name="dx", grid=(t // tm,), in_specs=in_specs,
        out_specs=row, out_shape=S((t, D), F32), compiler_params=_params(("parallel",)),
    )(*operands)


def _wgrad(a, b, ta, after=None):
    t, ka = a.shape
    tk = min(WGRAD_TOKENS, t)
    nk = t // tk

    def body(a_ref, b_ref, o_ref, acc):
        k = pl.program_id(1)

        @pl.when(k == 0)
        def _():
            acc[...] = jnp.zeros_like(acc)

        acc[...] += lax.dot_general(a_ref[...], b_ref[...], TN, preferred_element_type=F32)

        @pl.when(k == nk - 1)
        def _():
            o_ref[...] = acc[...].astype(BF16)

    body, in_specs, operands = _follow(
        body, 2, [pl.BlockSpec((tk, ta), lambda i, k: (k, i)), pl.BlockSpec((tk, D), lambda i, k: (k, 0))], [a, b], after)
    return pl.pallas_call(
        body, name="wgrad", grid=(ka // ta, nk), in_specs=in_specs,
        out_specs=pl.BlockSpec((ta, D), lambda i, k: (i, 0)), out_shape=S((ka, D), BF16),
        scratch_shapes=[pltpu.VMEM((ta, D), F32)], compiler_params=_params(("parallel", "arbitrary")),
    )(*operands)


def _out_bwd(dy, r, gam, wout, after=None):
    t = dy.shape[0]
    tm = min(TOKEN_TILE, t)

    def body(dy_ref, r_ref, g_ref, w_ref, dr_ref, dm_ref, dsc_ref, dat_ref, dcc_ref, dgam_ref, dbet_ref):
        dr = _ln_bwd_block(dy_ref, r_ref, g_ref, dgam_ref, dbet_ref)
        dr_ref[...] = dr
        dmb = dr.astype(BF16)
        dm_ref[...] = dmb
        dcat = lax.dot_general(dmb, w_ref[...], NT, preferred_element_type=F32)
        dsc_ref[...] = dcat[:, 0:D_SC]
        dat_ref[...] = dcat[:, D_SC:D_SC + D_ATT]
        dcc_ref[...] = dcat[:, D_SC + D_ATT:D]

    def rows(w):
        return pl.BlockSpec((tm, w), lambda i: (i, 0))

    vec = pl.BlockSpec((1, D), lambda i: (0, 0))
    body, in_specs, operands = _follow(body, 4, [rows(D), rows(D), vec, _resident((D, D))], [dy, r, gam, wout], after)
    return pl.pallas_call(
        body, name="out_bwd", grid=(t // tm,), in_specs=in_specs,
        out_specs=[rows(D), rows(D), rows(D_SC), rows(D_ATT), rows(D_CC), vec, vec],
        out_shape=[S((t, D), F32), S((t, D), BF16), S((t, D_SC), F32), S((t, D_ATT), F32), S((t, D_CC), F32),
                   S((1, D), F32), S((1, D), F32)],
        compiler_params=_params(("arbitrary",)),
    )(*operands)


def _attn_bwd(qr, kr, vv, sink, do, yatt):
    t = qr.shape[0]
    nb = t // BLOCK
    scale = HEAD_DIM ** -0.5

    def body(q_ref, kp_ref, ko_ref, kn_ref, vp_ref, vo_ref, vn_ref, sink_ref, do_ref, o_ref,
             dq_ref, dk_ref, dv_ref, dsink_ref, bias_s, ds_s, p_s, q_s, dou_s):
        n = pl.program_id(0)
        _band_bias(n, nb, bias_s)

        @pl.when(n == 0)
        def _():
            dsink_ref[...] = jnp.zeros_like(dsink_ref)

        kcs = [_band_cat((kp_ref, ko_ref, kn_ref), kvh) for kvh in range(N_KV_HEADS)]
        vcs = [_band_cat((vp_ref, vo_ref, vn_ref), kvh) for kvh in range(N_KV_HEADS)]

        def scores(h):
            return _head_scores(q_ref, kcs[h // GROUP], h, bias_s)

        def probs(h, qh, s):
            cols = slice(h * HEAD_DIM, (h + 1) * HEAD_DIM)
            p, ps, denom = _softmax_parts(s, sink_ref[h])
            doh = do_ref[:, cols]
            dd = jnp.sum(doh * o_ref[:, cols].astype(F32), axis=-1, keepdims=True) / denom
            dou = (doh / denom).astype(BF16)
            dp = lax.dot_general(dou, vcs[h // GROUP], NT, preferred_element_type=F32)
            dsink_ref[h:h + 1, :] += jnp.zeros((1, 128), F32) - jnp.sum(ps * dd)
            return qh, p, dd, dou, dp

        def grads(h, qh, p, dd, dou, dp):
            kvh, g = divmod(h, GROUP)
            cols = slice(h * HEAD_DIM, (h + 1) * HEAD_DIM)
            rows = slice(g * BLOCK, (g + 1) * BLOCK)
            ds = (p * (dp - dd)).astype(BF16)
            dq_ref[:, cols] = jnp.dot(ds, kcs[kvh], preferred_element_type=F32) * scale
            ds_s[rows, :] = ds
            p_s[rows, :] = p.astype(BF16)
            q_s[rows, :] = qh
            dou_s[rows, :] = dou
            if g == GROUP - 1:
                dk = lax.dot_general(ds_s[...], q_s[...], TN, preferred_element_type=F32)
                dv = lax.dot_general(p_s[...], dou_s[...], TN, preferred_element_type=F32)
                for j in range(3):
                    dk_ref[j, :, kvh * HEAD_DIM:(kvh + 1) * HEAD_DIM] = dk[j * BLOCK:(j + 1) * BLOCK, :]
                    dv_ref[j, :, kvh * HEAD_DIM:(kvh + 1) * HEAD_DIM] = dv[j * BLOCK:(j + 1) * BLOCK, :]

        sc = {0: scores(0), 1: scores(1)}
        pr = {0: probs(0, *sc.pop(0))}
        for h in range(N_Q_HEADS):
            if h + 2 < N_Q_HEADS:
                sc[h + 2] = scores(h + 2)
            if h + 1 < N_Q_HEADS:
                pr[h + 1] = probs(h + 1, *sc.pop(h + 1))
            grads(h, *pr.pop(h))

    qspec = pl.BlockSpec((BLOCK, D_ATT), lambda n: (n, 0))
    part = pl.BlockSpec((3, BLOCK, 128), lambda n: (0, n, 0))
    stacked = GROUP * BLOCK
    return pl.pallas_call(
        body, name="attn_bwd", grid=(nb,),
        in_specs=[qspec] + _band_specs(nb, 128) + _band_specs(nb, 128) + [pl.BlockSpec(memory_space=pltpu.SMEM), qspec, qspec],
        out_specs=[qspec, part, part, pl.BlockSpec((N_Q_HEADS, 128), lambda n: (0, 0))],
        out_shape=[S((t, D_ATT), F32), S((3, t, 128), F32), S((3, t, 128), F32), S((N_Q_HEADS, 128), F32)],
        scratch_shapes=[pltpu.VMEM((BLOCK, 3 * BLOCK), F32), pltpu.VMEM((stacked, 3 * BLOCK), BF16),
                        pltpu.VMEM((stacked, 3 * BLOCK), BF16), pltpu.VMEM((stacked, HEAD_DIM), BF16),
                        pltpu.VMEM((stacked, HEAD_DIM), BF16)],
        compiler_params=_params(("arbitrary",)),
    )(qr, kr, kr, kr, vv, vv, vv, sink, do, yatt)


def _tap_grads_aligned(d_own, x_ref, d_ref, offsets, out_ref):
    rows = d_own.shape[0]
    padded = jnp.concatenate([d_own, jnp.zeros((SUBLANES, d_own.shape[1]), F32)], axis=0)
    for r in range(SUBLANES):
        d_ref[r] = padded if r == 0 else pltpu.roll(padded, r, 0)
    for j, o in enumerate(offsets):
        r = o % SUBLANES
        out_ref[j:j + 1, :] += jnp.sum(d_ref[r] * x_ref[pl.ds(o - r, rows + SUBLANES), :], axis=0, keepdims=True)


def _mix_bwd(z, c, dysc, dycc, dqr, dkp, dvp, scw, ccw, ccb, ccg, ccbb, cos, sin):
    t = z.shape[0]
    tm = min(MIX_BWD_TILE, t)
    nt = t // tm
    h = HALO_BWD
    half = CC_W // 2
    ext = tm + 2 * h

    def body(z_ref, zp_ref, zn_ref, c_ref, cp_ref, cn_ref, dsc_ref, dscp_ref, dscn_ref, dcc_ref, dccp_ref, dccn_ref,
             dq_ref, dk0_ref, dk1_ref, dk2_ref, dv0_ref, dv1_ref, dv2_ref,
             scw_ref, ccw_ref, ccb_ref, ccg_ref, ccbb_ref, cos_ref, sin_ref,
             dz_ref, dscw_ref, dccw_ref, dvec_ref, u_s, dc_s, ch_s, g_s, p_s, d_s):
        i = pl.program_id(0)
        first, last = i == 0, i == nt - 1

        @pl.when(first)
        def _():
            dscw_ref[...] = jnp.zeros_like(dscw_ref)
            dccw_ref[...] = jnp.zeros_like(dccw_ref)
            dvec_ref[...] = jnp.zeros_like(dvec_ref)

        pz = jnp.where(first, 0.0, zp_ref[...])
        nz = jnp.where(last, 0.0, zn_ref[...])
        zo = z_ref[...]

        def u_of(zz):
            return zz[:, O_CCA:O_CCA + D_CC] * _sigmoid(zz[:, O_CCG:O_CCG + D_CC])

        u_s[0:h, :] = u_of(pz)
        u_s[h:h + tm, :] = u_of(zo)
        u_s[h + tm:ext, :] = u_of(nz)
        c_ext = jnp.concatenate([jnp.where(first, 0.0, cp_ref[...]), c_ref[...], jnp.where(last, 0.0, cn_ref[...])], axis=0)
        xhat, rstd = _ln_stats(c_ext)
        nn = xhat * ccg_ref[...] + ccbb_ref[...]
        sg = _sigmoid(nn)
        dycc_ext = jnp.concatenate([jnp.where(first, 0.0, dccp_ref[...]), dcc_ref[...],
                                    jnp.where(last, 0.0, dccn_ref[...])], axis=0)
        dn = dycc_ext * (sg * (1.0 + nn * (1.0 - sg)))
        dc = _ln_bwd(dn, xhat, rstd, ccg_ref[...])
        dc_s[...] = dc
        dn_own = dn[h:h + tm, :]
        dc_own = dc[h:h + tm, :]
        dvec_ref[0:1, :] += jnp.sum(dc_own, axis=0, keepdims=True)
        dvec_ref[1:2, :] += jnp.sum(dn_own * xhat[h:h + tm, :], axis=0, keepdims=True)
        dvec_ref[2:3, :] += jnp.sum(dn_own, axis=0, keepdims=True)
        du = _taps_aligned(ccw_ref, dc_s, p_s, [h + half - j for j in range(CC_W)], tm)
        _tap_grads_aligned(dc_own, u_s, d_s, [h + j - half for j in range(CC_W)], dccw_ref)
        gate = _sigmoid(zo[:, O_CCG:O_CCG + D_CC])
        a_own = zo[:, O_CCA:O_CCA + D_CC]
        dz_ref[:, O_CCA:O_CCA + D_CC] = (du * gate).astype(BF16)
        dz_ref[:, O_CCG:O_CCG + D_CC] = (du * a_own * gate * (1.0 - gate)).astype(BF16)

        def ch_of(zz):
            return zz[:, O_SCC:O_SCC + D_SC] * zz[:, O_SCH:O_SCH + D_SC]

        ch_s[0:h, :] = ch_of(pz)
        ch_s[h:h + tm, :] = ch_of(zo)
        ch_s[h + tm:ext, :] = ch_of(nz)
        g_s[0:h, :] = jnp.where(first, 0.0, dscp_ref[...]) * pz[:, O_SCB:O_SCB + D_SC]
        g_s[h:h + tm, :] = dsc_ref[...] * zo[:, O_SCB:O_SCB + D_SC]
        g_s[h + tm:ext, :] = jnp.where(last, 0.0, dscn_ref[...]) * nz[:, O_SCB:O_SCB + D_SC]
        conv = jnp.zeros((tm, D_SC), F32)
        dch = jnp.zeros((tm, D_SC), F32)
        g_own = g_s[h:h + tm, :]
        for j in range(SC_W):
            chj = ch_s[pl.ds(h + j - SC_W // 2, tm), :]
            conv = conv + scw_ref[j:j + 1, :] * chj
            dch = dch + scw_ref[j:j + 1, :] * g_s[pl.ds(h + SC_W // 2 - j, tm), :]
            dscw_ref[j:j + 1, :] += jnp.sum(g_own * chj, axis=0, keepdims=True)
        dz_ref[:, O_SCB:O_SCB + D_SC] = (dsc_ref[...] * conv).astype(BF16)
        dz_ref[:, O_SCC:O_SCC + D_SC] = (dch * zo[:, O_SCH:O_SCH + D_SC]).astype(BF16)
        dz_ref[:, O_SCH:O_SCH + D_SC] = (dch * zo[:, O_SCC:O_SCC + D_SC]).astype(BF16)

        dq = dq_ref[...]
        dz_ref[:, O_Q:O_Q + D_ATT] = (dq * _wide(cos_ref[...], D_ATT) + _swap_halves(dq * _wide(sin_ref[...], D_ATT))).astype(BF16)
        dk = dk1_ref[0] + jnp.where(last, 0.0, dk0_ref[0]) + jnp.where(first, 0.0, dk2_ref[0])
        dz_ref[:, O_K:O_K + 128] = (dk * cos_ref[...] + _swap_halves(dk * sin_ref[...])).astype(BF16)
        dv = dv1_ref[0] + jnp.where(last, 0.0, dv0_ref[0]) + jnp.where(first, 0.0, dv2_ref[0])
        dz_ref[:, O_V:O_V + 128] = dv.astype(BF16)

    def full(a):
        return pl.BlockSpec(a.shape, lambda i: (0, 0))

    def rows(w):
        return pl.BlockSpec((tm, w), lambda i: (i, 0))

    parts = [pl.BlockSpec((1, tm, 128), lambda i: (0, jnp.minimum(i + 1, nt - 1), 0)),
             pl.BlockSpec((1, tm, 128), lambda i: (1, i, 0)),
             pl.BlockSpec((1, tm, 128), lambda i: (2, jnp.maximum(i - 1, 0), 0))]
    acc_spec = lambda r: pl.BlockSpec((r, D_CC), lambda i: (0, 0))
    return pl.pallas_call(
        body, name="mix_bwd", grid=(nt,),
        in_specs=(_halo_specs(t, tm, h, D_IN) + _halo_specs(t, tm, h, D_CC) + _halo_specs(t, tm, h, D_SC)
                  + _halo_specs(t, tm, h, D_CC) + [rows(D_ATT)] + parts + parts
                  + [full(scw), full(ccw), full(ccb), full(ccg), full(ccbb), rows(128), rows(128)]),
        out_specs=[rows(D_IN), acc_spec(SC_W), acc_spec(CC_W), acc_spec(3)],
        out_shape=[S((t, D_IN), BF16), S((SC_W, D_SC), F32), S((CC_W, D_CC), F32), S((3, D_CC), F32)],
        scratch_shapes=[pltpu.VMEM((ext, D_CC), F32), pltpu.VMEM((ext, D_CC), F32),
                        pltpu.VMEM((ext, D_SC), F32), pltpu.VMEM((ext, D_SC), F32),
                        pltpu.VMEM((SUBLANES, tm + SUBLANES, D_CC), F32), pltpu.VMEM((SUBLANES, tm + SUBLANES, D_CC), F32)],
        compiler_params=_params(("arbitrary",)),
    )(z, z, z, c, c, c, dysc, dysc, dysc, dycc, dycc, dycc, dqr, dkp, dkp, dkp, dvp, dvp, dvp,
      scw, ccw, ccb, ccg, ccbb, cos, sin)


def _adamw(w, g, m, v):
    m = ADAM_B1 * m + (1.0 - ADAM_B1) * g
    v = ADAM_B2 * v + (1.0 - ADAM_B2) * (g * g)
    m_hat = m / (1.0 - ADAM_B1 ** ADAM_STEP)
    v_hat = v / (1.0 - ADAM_B2 ** ADAM_STEP)
    delta = -ADAM_LR * (m_hat / (jnp.sqrt(v_hat) + ADAM_EPS) + ADAM_WD * w)
    return delta, m, v


def _row_tile(rows):
    for cand in (256, 176, 128):
        if rows % cand == 0:
            return cand
    return rows


def _sum_adam(recv, w, m, v, transposed):
    nl, rows = len(recv), recv[0].shape[1]
    tile = 256 if transposed else _row_tile(rows)
    nc = (D if transposed else rows) // tile

    def body(*refs):
        w_ref, m_ref, v_ref, g_ref, d_ref, mo_ref, vo_ref = refs[nl:]
        for layer in range(nl):
            @pl.when(pl.program_id(0) == layer)
            def _(r_ref=refs[layer]):
                g = r_ref[0].astype(F32)
                for s in range(1, N_DEV):
                    g = g + r_ref[s].astype(F32)
                if transposed:
                    g = g.T
                g_ref[0] = g
                d_ref[0], mo_ref[0], vo_ref[0] = _adamw(w_ref[0], g, m_ref[0], v_ref[0])

    def held(layer):
        def at(l, c):
            return jnp.where(l == layer, c, jnp.where(l < layer, 0, nc - 1))
        if transposed:
            return pl.BlockSpec((N_DEV, rows, tile), lambda l, c: (0, 0, at(l, c)))
        return pl.BlockSpec((N_DEV, tile, D), lambda l, c: (0, at(l, c), 0))

    if transposed:
        blk = pl.BlockSpec((1, tile, rows), lambda l, c: (l, c, 0))
    else:
        blk = pl.BlockSpec((1, tile, D), lambda l, c: (l, c, 0))
    out = S(w.shape, F32)
    return pl.pallas_call(
        body, name="sum_adam_t" if transposed else "sum_adam", grid=(nl, nc),
        in_specs=[held(layer) for layer in range(nl)] + [blk, blk, blk], out_specs=[blk] * 4, out_shape=[out] * 4,
        compiler_params=_params(("arbitrary", "arbitrary")),
    )(*recv, w, m, v)


def _small_sum(gathered):
    rows = gathered.shape[1]

    def body(g_ref, o_ref):
        acc = g_ref[0]
        for s in range(1, N_DEV):
            acc = acc + g_ref[s]
        o_ref[...] = acc

    return pl.pallas_call(
        body, name="small_sum", in_specs=[pl.BlockSpec(gathered.shape, lambda: (0, 0, 0))],
        out_specs=pl.BlockSpec((rows, 128), lambda: (0, 0)), out_shape=S((rows, 128), F32),
    )(gathered)


def _small_adam(w, g, m, v):
    def body(w_ref, g_ref, m_ref, v_ref, d_ref, mo_ref, vo_ref):
        d_ref[...], mo_ref[...], vo_ref[...] = _adamw(w_ref[...], g_ref[...], m_ref[...], v_ref[...])

    spec = pl.BlockSpec(w.shape, lambda: (0, 0))
    return pl.pallas_call(
        body, name="small_adam", in_specs=[spec] * 4, out_specs=[spec] * 3, out_shape=[S(w.shape, F32)] * 3,
    )(w, g, m, v)


def _pack(pieces):
    flat = jnp.concatenate([p.reshape(-1).astype(F32) for p in pieces])
    n = flat.shape[0]
    rows = -(-n // 1024) * 8
    return jnp.pad(flat, (0, rows * 128 - n)).reshape(rows, 128)


def _unpack(packed, shapes):
    flat = packed.reshape(-1)
    out, o = [], 0
    for shp in shapes:
        n = int(np.prod(shp))
        out.append(flat[o:o + n].reshape(shp))
        o += n
    return out


def _rope_tables(t):
    half = HEAD_DIM // 2
    inv_freq = ROPE_THETA ** (-jnp.arange(half, dtype=F32) / half)
    ang = jnp.arange(t).astype(F32)[:, None] * inv_freq[None, :]
    cos, sin = jnp.cos(ang), jnp.sin(ang)
    cos128 = jnp.concatenate([cos, cos, cos, cos], axis=1)
    sin128 = jnp.concatenate([-sin, sin, -sin, sin], axis=1)
    return cos128, sin128


BIG = ("ffn1_w_gu", "ffn1_w_down", "w_in", "w_out", "ffn2_w_gu", "ffn2_w_down")
BIG_T = {"ffn1_w_gu": True, "ffn1_w_down": False, "w_in": True, "w_out": False, "ffn2_w_gu": True, "ffn2_w_down": False}
SWAPPED = ("ffn1_w_gu", "ffn2_w_gu")
REPLICATED = ("ln1_g", "ln1_b", "attn_sink", "cc_conv_b", "cc_ln_g", "cc_ln_b", "ln2_g", "ln2_b", "ln3_g", "ln3_b")
CONVS = ("sc_conv_w", "cc_conv_w")
WEIGHTS = ("ffn1_w_gu", "ffn1_w_down", "ln1_g", "ln1_b", "w_in", "sc_conv_w", "attn_sink", "cc_conv_w", "cc_conv_b",
           "cc_ln_g", "cc_ln_b", "w_out", "ln2_g", "ln2_b", "ffn2_w_gu", "ffn2_w_down", "ln3_g", "ln3_b")


def kernel(x, ffn1_w_gu, ffn1_w_down, ln1_g, ln1_b, w_in, sc_conv_w, attn_sink, cc_conv_w, cc_conv_b, cc_ln_g, cc_ln_b, w_out, ln2_g, ln2_b, ffn2_w_gu, ffn2_w_down, ln3_g, ln3_b, loss_target, m_ffn1_w_gu, m_ffn1_w_down, m_ln1_g, m_ln1_b, m_w_in, m_sc_conv_w, m_attn_sink, m_cc_conv_w, m_cc_conv_b, m_cc_ln_g, m_cc_ln_b, m_w_out, m_ln2_g, m_ln2_b, m_ffn2_w_gu, m_ffn2_w_down, m_ln3_g, m_ln3_b, v_ffn1_w_gu, v_ffn1_w_down, v_ln1_g, v_ln1_b, v_w_in, v_sc_conv_w, v_attn_sink, v_cc_conv_w, v_cc_conv_b, v_cc_ln_g, v_cc_ln_b, v_w_out, v_ln2_g, v_ln2_b, v_ffn2_w_gu, v_ffn2_w_down, v_ln3_g, v_ln3_b):
    args = dict(locals())
    w = {n: args[n] for n in WEIGHTS}
    mom = {n: args["m_" + n] for n in WEIGHTS}
    var = {n: args["v_" + n] for n in WEIGHTS}
    x0 = x[0]
    target = loss_target[0]
    t = x0.shape[0]
    idx = 4 * lax.axis_index("x") + 2 * lax.axis_index("y") + lax.axis_index("c")

    blocks = {(n, l): (w[n][l].T if BIG_T[n] else w[n][l]).astype(BF16) for l in range(DEPTH) for n in BIG}
    where = {}

    def start_stage(tag, members, after, extra=()):
        srcs = list(extra) + [blocks[m] for m in members]
        started = _send_start(srcs, [_own_slot(s) for s in srcs], _whole, f"gather_start_{tag}", after)
        for j, m in enumerate(members):
            where[m] = (started, len(extra) + j)
        return started

    def wait_stage(started, k, after, name):
        send, rcv, srcs, lands, _ = started
        return _recv_wait(send, rcv, [k], [srcs[k]], [lands[k]], _whole, after, name)[0]

    def weight(n, l, after):
        g = wait_stage(*where[n, l], after, f"gather_wait_{n}_{l}")
        return g.reshape(N_DEV * g.shape[1], g.shape[2])

    xf, xb = x0, x0.astype(BF16)
    first = start_stage("a", [("ffn1_w_gu", 0)], xb, extra=[_pack([w["sc_conv_w"], w["cc_conv_w"]])])
    conv_all = wait_stage(first, 0, xb, "gather_wait_convs").reshape(N_DEV, -1)
    n_sc = DEPTH * SC_W * 32
    scw_full = conv_all[:, :n_sc].reshape(N_DEV, DEPTH, SC_W, 32).transpose(1, 2, 0, 3).reshape(DEPTH, SC_W, D_SC)
    ccw_full = conv_all[:, n_sc:n_sc + DEPTH * CC_W * 32].reshape(N_DEV, DEPTH, CC_W, 32).transpose(1, 2, 0, 3).reshape(DEPTH, CC_W, D_CC)

    cos, sin = _rope_tables(t)
    row = lambda a, l: a[l].reshape(1, -1)

    saved, full = [], {}
    for l in range(DEPTH):
        sv = {"x0b": xb}
        token = None
        full["ffn1_w_gu", l] = weight("ffn1_w_gu", l, xb)
        if l == 0:
            token = start_stage("b", [("ffn1_w_down", 0), ("w_in", 0), ("w_out", 0)], full["ffn1_w_gu", l])[-1]
        gu1, a1 = _ffn_up(xb, full["ffn1_w_gu", l], token)
        full["ffn1_w_down", l] = weight("ffn1_w_down", l, a1)
        if l == 0:
            token = start_stage("c", [("ffn2_w_gu", 0), ("ffn2_w_down", 0)], full["ffn1_w_down", l])[-1]
        r1, x1, x1b = _ffn_down_ln(a1, full["ffn1_w_down", l], xf, row(ln1_g, l), row(ln1_b, l), token)
        full["w_in", l] = weight("w_in", l, x1b)
        z = _proj_in(x1b, full["w_in", l])
        ysc, ycc, cpre, qr, kr, vv = _mix_fwd(z, scw_full[l], ccw_full[l], row(cc_conv_b, l), row(cc_ln_g, l), row(cc_ln_b, l), cos, sin)
        if l == 0:
            token = start_stage("d", [("ffn1_w_gu", 1), ("ffn1_w_down", 1)], ysc)[-1]
        yatt = _attn_fwd(qr, kr, vv, attn_sink[l], token)
        full["w_out", l] = weight("w_out", l, yatt)
        ycat, r2, x2, x2b = _out_ln(ysc, yatt, ycc, full["w_out", l], x1, row(ln2_g, l), row(ln2_b, l))
        full["ffn2_w_gu", l] = weight("ffn2_w_gu", l, x2b)
        if l == 0:
            token = start_stage("e", [("w_in", 1), ("w_out", 1), ("ffn2_w_gu", 1), ("ffn2_w_down", 1)], full["ffn2_w_gu", l])[-1]
        gu2, a2 = _ffn_up(x2b, full["ffn2_w_gu", l], token)
        full["ffn2_w_down", l] = weight("ffn2_w_down", l, a2)
        r3, x3, x3b = _ffn_down_ln(a2, full["ffn2_w_down", l], x2, row(ln3_g, l), row(ln3_b, l))
        sv.update(gu1=gu1, a1=a1, r1=r1, x1b=x1b, z=z, cpre=cpre, qr=qr, kr=kr, vv=vv, yatt=yatt, ycat=ycat, r2=r2, x2b=x2b,
                  gu2=gu2, a2=a2, r3=r3)
        saved.append(sv)
        xf, xb = x3, x3b

    dy, sq = _loss_head(xf, target)
    loss = lax.psum(0.5 * jnp.sum(sq) / D, ("x", "y", "c"))

    sent = []
    small = {n: [None] * DEPTH for n in REPLICATED + CONVS}

    def send_grads(names, l, gs):
        srcs = [g.reshape(N_DEV, g.shape[0] // N_DEV, g.shape[1]) for g in gs]
        lands = [_own_slot(lax.dynamic_index_in_dim(s3, idx, 0, keepdims=False)) for s3 in srcs]
        started = _send_start(srcs, lands, _block_of, f"grads_start_{names[0]}_{l}", gs[-1])
        sent.append((names, l, started))
        return started[-1]

    token = None
    for l in reversed(range(DEPTH)):
        sv = saved[l]
        dy, dfb, dh, dg, db = _ffn_bwd_dx(dy, sv["r3"], row(ln3_g, l), full["ffn2_w_down", l], sv["gu2"],
                                          full["ffn2_w_gu", l], token)
        small["ln3_g"][l], small["ln3_b"][l] = dg, db
        token = send_grads(("ffn2_w_down", "ffn2_w_gu"), l,
                           [_wgrad(sv["a2"], dfb, F // 2), _wgrad(dh, sv["x2b"], F // 2)])

        dr, dmb, dysc, dyatt, dycc, dg, db = _out_bwd(dy, sv["r2"], row(ln2_g, l), full["w_out", l], token)
        small["ln2_g"][l], small["ln2_b"][l] = dg, db
        g_out = _wgrad(sv["ycat"], dmb, D)
        dqr, dkp, dvp, dsink = _attn_bwd(sv["qr"], sv["kr"], sv["vv"], attn_sink[l], dyatt, sv["yatt"])
        small["attn_sink"][l] = dsink[:, 0]
        dz, dscw, dccw, dvec = _mix_bwd(sv["z"], sv["cpre"], dysc, dycc, dqr, dkp, dvp, scw_full[l], ccw_full[l],
                                        row(cc_conv_b, l), row(cc_ln_g, l), row(cc_ln_b, l), cos, sin)
        small["sc_conv_w"][l], small["cc_conv_w"][l] = dscw, dccw
        small["cc_conv_b"][l], small["cc_ln_g"][l], small["cc_ln_b"][l] = dvec[0], dvec[1], dvec[2]
        token = send_grads(("w_out", "w_in"), l, [g_out, _wgrad(dz, sv["x1b"], D)])
        dy = _dx(dr, dz, full["w_in", l], token)

        if l > 0:
            dy, dfb, dh, dg, db = _ffn_bwd_dx(dy, sv["r1"], row(ln1_g, l), full["ffn1_w_down", l], sv["gu1"],
                                              full["ffn1_w_gu", l])
            token = send_grads(("ffn1_w_down", "ffn1_w_gu"), l,
                               [_wgrad(sv["a1"], dfb, F // 2), _wgrad(dh, sv["x0b"], F // 2)])
        else:
            dr, dfb, dh, dg, db = _ffn_bwd(dy, sv["r1"], row(ln1_g, l), full["ffn1_w_down", l], sv["gu1"])
            token = send_grads(("ffn1_w_gu",), l, [_wgrad(dh, sv["x0b"], F // 2)])
            token = send_grads(("ffn1_w_down",), l, [_wgrad(sv["a1"], dfb, F // 2, token)])
            dy = _dx(dr, dh, full["ffn1_w_gu", l], token)
        small["ln1_g"][l], small["ln1_b"][l] = dg, db
    grad_x = dy[None]

    small_names = REPLICATED + CONVS
    small_shapes = [(DEPTH,) + tuple(np.shape(small[n][0].reshape(-1))) for n in small_names]
    small_pack = _pack([jnp.stack([small[n][l].reshape(-1) for l in range(DEPTH)]) for n in small_names])
    small_all = _all_gather([small_pack], "gather_small_grads")[0]

    recv = {n: [None] * DEPTH for n in BIG}
    grads, deltas, new_m, new_v = {}, {}, {}, {}

    def receive(upto, after):
        while len(sent) > upto:
            names, l, (send, rcv, srcs, lands, _) = sent.pop(0)
            got = _recv_wait(send, rcv, list(range(len(names))), srcs, lands, _block_of, after, f"grads_wait_{names[0]}_{l}")
            for n, g in zip(names, got):
                recv[n][l] = g

    def update(n):
        if n in SWAPPED:
            outs = _sum_adam(recv[n], *[jnp.swapaxes(a, 1, 2) for a in (w[n], mom[n], var[n])], False)
            grads[n], deltas[n], new_m[n], new_v[n] = [jnp.swapaxes(a, 1, 2) for a in outs]
        else:
            grads[n], deltas[n], new_m[n], new_v[n] = _sum_adam(recv[n], w[n], mom[n], var[n], BIG_T[n])

    receive(2, dy)
    for n in ("ffn2_w_down", "ffn2_w_gu", "w_out", "w_in"):
        update(n)
    receive(0, new_v["w_in"])
    update("ffn1_w_gu")
    update("ffn1_w_down")
    small_total = _unpack(_small_sum(small_all), small_shapes)
    for n, g in zip(small_names, small_total):
        if n in CONVS:
            taps = SC_W if n == "sc_conv_w" else CC_W
            g = lax.dynamic_slice_in_dim(g.reshape(DEPTH, taps, D_SC), idx * 32, 32, axis=2)
        grads[n] = g.reshape(w[n].shape)
    wp = _pack([w[n] for n in small_names])
    gp = _pack([grads[n] for n in small_names])
    mp = _pack([mom[n] for n in small_names])
    vp = _pack([var[n] for n in small_names])
    shapes = [w[n].shape for n in small_names]
    for dst, packed in zip((deltas, new_m, new_v), _small_adam(wp, gp, mp, vp)):
        for n, a in zip(small_names, _unpack(packed, shapes)):
            dst[n] = a

    return (loss, grad_x, *[grads[n] for n in WEIGHTS], *[deltas[n] for n in WEIGHTS],
            *[new_m[n] for n in WEIGHTS], *[new_v[n] for n in WEIGHTS])
```

```python
import functools

import jax
import jax.numpy as jnp
import numpy as np
from jax import lax
from jax.experimental import pallas as pl
from jax.experimental.pallas import tpu as pltpu

F32 = jnp.float32
BF16 = jnp.bfloat16
S = jax.ShapeDtypeStruct

N_DEV = 8
DEPTH = 2
D = 1024
F = 2816
D_IN = 2048
HEAD_DIM = 64
N_Q_HEADS = 8
N_KV_HEADS = 2
GROUP = 4
D_SC = 256
D_ATT = 512
D_CC = 256
CC_W = 31
SC_W = 3
BLOCK = 128
ROPE_THETA = 10000.0
LN_EPS = 1e-5
ALPHA = (2.0 * DEPTH) ** 0.25
ADAM_LR = 0.001
ADAM_B1 = 0.9
ADAM_B2 = 0.999
ADAM_EPS = 1e-08
ADAM_WD = 0.01
ADAM_STEP = 10

O_SCB, O_SCC, O_SCH, O_Q, O_K, O_V, O_CCA, O_CCG = 0, 256, 512, 768, 1280, 1408, 1536, 1792

V7X_VMEM_BYTES = 64 * 1024 * 1024
VMEM_LIMIT = V7X_VMEM_BYTES - 8 * 1024 * 1024
TOKEN_TILE = 256
WIDE_TILE = 512
WGRAD_TOKENS = 2048
FFN_CHUNKS = (0, 768, 1536, 2176, 2816)
MIX_BWD_TILE = 128
HALO_FWD = 16
HALO_BWD = 16
CONV_ROWS = 128
SUBLANES = 8

NT = (((1,), (1,)), ((), ()))
TN = (((0,), (0,)), ((), ()))
MESH = pl.DeviceIdType.MESH


def _params(sem=None):
    return pltpu.CompilerParams(dimension_semantics=sem, vmem_limit_bytes=VMEM_LIMIT)


def _sigmoid(v):
    return 1.0 / (1.0 + jnp.exp(-v))


def _ln_stats(r):
    mu = jnp.mean(r, axis=-1, keepdims=True)
    d = r - mu
    var = jnp.mean(d * d, axis=-1, keepdims=True)
    rstd = lax.rsqrt(var + LN_EPS)
    return d * rstd, rstd


def _ln_bwd(dn, xhat, rstd, gam):
    dxh = dn * gam
    return rstd * (dxh - jnp.mean(dxh, axis=-1, keepdims=True) - xhat * jnp.mean(dxh * xhat, axis=-1, keepdims=True))


def _swap_halves(v):
    n = v.shape[-1]
    lane = lax.broadcasted_iota(jnp.int32, v.shape, v.ndim - 1) % HEAD_DIM
    return jnp.where(lane < HEAD_DIM // 2, pltpu.roll(v, n - HEAD_DIM // 2, v.ndim - 1), pltpu.roll(v, HEAD_DIM // 2, v.ndim - 1))


def _wide(tab, n):
    return tab if n == 128 else jnp.concatenate([tab] * (n // 128), axis=1)


def _me():
    x, y, c = lax.axis_index("x"), lax.axis_index("y"), lax.axis_index("c")
    return x, y, c


def _peer(rel):
    x, y, c = _me()
    px = 1 - x if rel & 4 else x
    py = 1 - y if rel & 2 else y
    pc = 1 - c if rel & 1 else c
    return (px, py, pc), 4 * px + 2 * py + pc


def _exchange(srcs, dsts_shape, dst_index, src_of, dst_of, name):
    n = len(srcs)

    def body(*refs):
        ins = refs[:n]
        outs = [refs[n + dst_index[k]] for k in range(n)]
        send, recv, lsem = refs[n + len(dsts_shape):]
        x, y, c = _me()
        me = 4 * x + 2 * y + c
        local = [pltpu.make_async_copy(src_of(ins[k], k, me), dst_of(outs[k], k, me), lsem.at[k]) for k in range(n)]
        for cp in local:
            cp.start()
        sends, recvs = [], []
        for k in range(n):
            for rel in range(1, N_DEV):
                peer, pidx = _peer(rel)
                sends.append(pltpu.make_async_remote_copy(
                    src_ref=src_of(ins[k], k, pidx), dst_ref=dst_of(outs[k], k, me),
                    send_sem=send.at[k, rel - 1], recv_sem=recv.at[k, rel - 1], device_id=peer, device_id_type=MESH))
                recvs.append(pltpu.make_async_remote_copy(
                    src_ref=src_of(ins[k], k, pidx), dst_ref=dst_of(outs[k], k, pidx),
                    send_sem=send.at[k, rel - 1], recv_sem=recv.at[k, rel - 1], device_id=peer, device_id_type=MESH))
        for cp in sends:
            cp.start()
        for cp in recvs:
            cp.wait_recv()
        for cp in sends:
            cp.wait_send()
        for cp in local:
            cp.wait()

    hbm = pl.BlockSpec(memory_space=pltpu.HBM)
    return pl.pallas_call(
        body, name=name, in_specs=[hbm] * n, out_specs=[hbm] * len(dsts_shape), out_shape=dsts_shape,
        scratch_shapes=[pltpu.SemaphoreType.DMA((n, N_DEV - 1)), pltpu.SemaphoreType.DMA((n, N_DEV - 1)),
                        pltpu.SemaphoreType.DMA((n,))],
    )(*srcs)


def _all_gather(blocks, name):
    shapes = [S((N_DEV,) + b.shape, b.dtype) for b in blocks]
    return _exchange(blocks, shapes, list(range(len(blocks))), lambda ref, k, idx: ref, lambda ref, k, idx: ref.at[idx], name)


HBM_SPEC = pl.BlockSpec(memory_space=pltpu.HBM)
SEM_SPEC = pl.BlockSpec(memory_space=pltpu.SEMAPHORE)
ANY_SPEC = pl.BlockSpec(memory_space=pl.ANY)
EFFECT = pltpu.SideEffectType.DATAFLOW_SIDE_EFFECTING
N_PEERS = N_DEV - 1


def _own_slot(block):
    x, y, c = _me()
    return lax.dynamic_update_index_in_dim(lax.empty((N_DEV,) + block.shape, block.dtype), block, 4 * x + 2 * y + c, 0)


def _follow(body, n_in, in_specs, operands, after):
    if after is None:
        return body, list(in_specs), list(operands)

    def tail(*refs):
        return body(*refs[:n_in], *refs[n_in + 1:])

    return tail, list(in_specs) + [ANY_SPEC], list(operands) + [after]


def _send_start(srcs, lands, src_of, name, after):
    n = len(srcs)

    def body(*refs):
        ins, zones = refs[:n], refs[n:2 * n]
        send, recv = refs[2 * n + 1], refs[2 * n + 2]
        token = refs[-1]
        x, y, c = _me()
        me = 4 * x + 2 * y + c
        for k in range(n):
            for rel in range(1, N_DEV):
                peer, pidx = _peer(rel)
                pltpu.make_async_remote_copy(
                    src_ref=src_of(ins[k], pidx), dst_ref=zones[k].at[me],
                    send_sem=send.at[k * N_PEERS + rel - 1], recv_sem=recv.at[k * N_PEERS + rel - 1],
                    device_id=peer, device_id_type=MESH).start()
        token[...] = jnp.zeros_like(token)

    outs = pl.pallas_call(
        body, name=name,
        out_shape=(pltpu.SemaphoreType.DMA((n * N_PEERS,)), pltpu.SemaphoreType.DMA((n * N_PEERS,)),
                   *[pltpu.HBM(a.shape, a.dtype) for a in lands], S((8, 128), F32)),
        in_specs=[HBM_SPEC] * (2 * n) + [ANY_SPEC],
        out_specs=(SEM_SPEC, SEM_SPEC, *[HBM_SPEC] * n, pl.BlockSpec(memory_space=pltpu.VMEM)),
        input_output_aliases={n + i: 2 + i for i in range(n)},
        compiler_params=pltpu.CompilerParams(has_side_effects=EFFECT),
    )(*[pltpu.with_memory_space_constraint(a, pltpu.HBM) for a in list(srcs) + list(lands)], after)
    return outs[0], outs[1], list(srcs), list(outs[2:2 + n]), outs[-1]


def _recv_wait(send, recv, ks, srcs, lands, src_of, after, name):
    n = len(ks)
    after = after if isinstance(after, (tuple, list)) else (after,)

    def body(*refs):
        ins, zones = refs[:n], refs[n:2 * n]
        send_sems, recv_sems = refs[2 * n], refs[2 * n + 1]
        for j, k in enumerate(ks):
            for rel in range(1, N_DEV):
                peer, pidx = _peer(rel)
                cp = pltpu.make_async_remote_copy(
                    src_ref=src_of(ins[j], pidx), dst_ref=zones[j].at[pidx],
                    send_sem=send_sems.at[k * N_PEERS + rel - 1], recv_sem=recv_sems.at[k * N_PEERS + rel - 1],
                    device_id=peer, device_id_type=MESH)
                cp.wait_send()
                cp.wait_recv()

    outs = pl.pallas_call(
        body, name=name,
        out_shape=[pltpu.HBM(a.shape, a.dtype) for a in lands],
        in_specs=[HBM_SPEC] * (2 * n) + [SEM_SPEC, SEM_SPEC] + [ANY_SPEC] * len(after), out_specs=[HBM_SPEC] * n,
        input_output_aliases={n + i: i for i in range(n)},
        compiler_params=pltpu.CompilerParams(has_side_effects=EFFECT),
    )(*srcs, *lands, send, recv, *after)
    return list(outs)


def _whole(ref, idx):
    return ref


def _block_of(ref, idx):
    return ref.at[idx]


def _resident(shape):
    return pl.BlockSpec(shape, lambda i: (0,) * len(shape), pipeline_mode=pl.Buffered(1))


def _ffn_up(xb, wgut, after=None):
    t = xb.shape[0]
    tm = min(WIDE_TILE, t)
    half = F // 2

    def body(x_ref, w_ref, gu_ref, a_ref):
        x = x_ref[...]
        for ch in range(2):
            lo = ch * half
            g = lax.dot_general(x, w_ref[lo:lo + half, :], NT, preferred_element_type=F32)
            u = lax.dot_general(x, w_ref[F + lo:F + lo + half, :], NT, preferred_element_type=F32)
            gu_ref[:, lo:lo + half] = g.astype(BF16)
            gu_ref[:, F + lo:F + lo + half] = u.astype(BF16)
            a_ref[:, lo:lo + half] = (g * _sigmoid(g) * u).astype(BF16)

    body, in_specs, operands = _follow(
        body, 2, [pl.BlockSpec((tm, D), lambda i: (i, 0)), _resident((2 * F, D))], [xb, wgut], after)
    return pl.pallas_call(
        body, name="ffn_up", grid=(t // tm,), in_specs=in_specs,
        out_specs=[pl.BlockSpec((tm, 2 * F), lambda i: (i, 0)), pl.BlockSpec((tm, F), lambda i: (i, 0))],
        out_shape=[S((t, 2 * F), BF16), S((t, F), BF16)], compiler_params=_params(("parallel",)),
    )(*operands)


def _residual_ln_out(x_ref, f, scale, g_ref, b_ref, r_ref, y_ref, yb_ref):
    r = ALPHA * x_ref[...] + scale * f
    xhat, _ = _ln_stats(r)
    y = xhat * g_ref[...] + b_ref[...]
    r_ref[...] = r
    y_ref[...] = y
    yb_ref[...] = y.astype(BF16)


def _ffn_down_ln(a, wd, x, gam, bet, after=None):
    t = x.shape[0]
    tm = min(WIDE_TILE, t)

    def body(a_ref, w_ref, x_ref, g_ref, b_ref, r_ref, y_ref, yb_ref):
        f = jnp.dot(a_ref[...], w_ref[...], preferred_element_type=F32)
        _residual_ln_out(x_ref, f, 0.5, g_ref, b_ref, r_ref, y_ref, yb_ref)

    row = pl.BlockSpec((tm, D), lambda i: (i, 0))
    vec = pl.BlockSpec((1, D), lambda i: (0, 0))
    body, in_specs, operands = _follow(
        body, 5, [pl.BlockSpec((tm, F), lambda i: (i, 0)), _resident((F, D)), row, vec, vec],
        [a, wd, x, gam, bet], after)
    return pl.pallas_call(
        body, name="ffn_down_ln", grid=(t // tm,), in_specs=in_specs,
        out_specs=[row, row, row], out_shape=[S((t, D), F32), S((t, D), F32), S((t, D), BF16)],
        compiler_params=_params(("parallel",)),
    )(*operands)


def _ffn_down_ln_loss(a, wd, x, gam, bet, target):
    t = x.shape[0]
    tm = min(WIDE_TILE, t)

    def body(a_ref, w_ref, x_ref, g_ref, b_ref, t_ref, r_ref, dy_ref, part_ref):
        f = jnp.dot(a_ref[...], w_ref[...], preferred_element_type=F32)
        r = ALPHA * x_ref[...] + 0.5 * f
        r_ref[...] = r
        xhat, _ = _ln_stats(r)
        e = xhat * g_ref[...] + b_ref[...] - t_ref[...]
        dy_ref[...] = e / D

        @pl.when(pl.program_id(0) == 0)
        def _():
            part_ref[...] = jnp.zeros_like(part_ref)

        part_ref[...] += jnp.sum(e * e, axis=0, keepdims=True)

    row = pl.BlockSpec((tm, D), lambda i: (i, 0))
    vec = pl.BlockSpec((1, D), lambda i: (0, 0))
    return pl.pallas_call(
        body, name="ffn_down_ln_loss", grid=(t // tm,),
        in_specs=[pl.BlockSpec((tm, F), lambda i: (i, 0)), _resident((F, D)), row, vec, vec, row],
        out_specs=[row, row, vec], out_shape=[S((t, D), F32), S((t, D), F32), S((1, D), F32)],
        compiler_params=_params(("arbitrary",)),
    )(a, wd, x, gam, bet, target)


def _proj_in(xb, wint):
    t = xb.shape[0]
    tm = min(WIDE_TILE, t)

    def body(x_ref, w_ref, z_ref):
        z_ref[...] = lax.dot_general(x_ref[...], w_ref[...], NT, preferred_element_type=F32)

    return pl.pallas_call(
        body, name="proj_in", grid=(t // tm,),
        in_specs=[pl.BlockSpec((tm, D), lambda i: (i, 0)), _resident((D_IN, D))],
        out_specs=pl.BlockSpec((tm, D_IN), lambda i: (i, 0)), out_shape=S((t, D_IN), F32),
        compiler_params=_params(("parallel",)),
    )(xb, wint)


def _halo_specs(t, tm, halo, width):
    per = tm // halo
    last = t // halo - 1
    return [pl.BlockSpec((tm, width), lambda i: (i, 0)),
            pl.BlockSpec((halo, width), lambda i: (jnp.maximum(i * per - 1, 0), 0)),
            pl.BlockSpec((halo, width), lambda i: (jnp.minimum((i + 1) * per, last), 0))]


def _taps_aligned(w_ref, x_ref, p_ref, offsets, rows):
    for r in range(SUBLANES):
        acc = jnp.zeros((rows + SUBLANES, x_ref.shape[1]), F32)
        for j, o in enumerate(offsets):
            if o % SUBLANES == r:
                acc = acc + w_ref[j:j + 1, :] * x_ref[pl.ds(o - r, rows + SUBLANES), :]
        p_ref[r] = acc
    out = p_ref[0, 0:rows, :]
    for r in range(1, SUBLANES):
        out = out + p_ref[r, pl.ds(r, rows), :]
    return out


def _mix_fwd(z, scw, ccw, ccb, ccg, ccbb, cos, sin):
    t = z.shape[0]
    tm = min(TOKEN_TILE, t)
    nt = t // tm
    h = HALO_FWD
    rc = min(CONV_ROWS, tm)

    def body(z_ref, zp_ref, zn_ref, scw_ref, ccw_ref, ccb_ref, ccg_ref, ccbb_ref, cos_ref, sin_ref,
             ysc_ref, ycc_ref, c_ref, q_ref, k_ref, v_ref, u_s, ch_s, p_s):
        i = pl.program_id(0)
        pz = jnp.where(i == 0, 0.0, zp_ref[...])
        nz = jnp.where(i == nt - 1, 0.0, zn_ref[...])

        def u_of(zz):
            return zz[:, O_CCA:O_CCA + D_CC] * _sigmoid(zz[:, O_CCG:O_CCG + D_CC])

        def ch_of(zz):
            return zz[:, O_SCC:O_SCC + D_SC] * zz[:, O_SCH:O_SCH + D_SC]

        u_s[0:h, :] = u_of(pz)
        u_s[h:h + tm, :] = z_ref[:, O_CCA:O_CCA + D_CC] * _sigmoid(z_ref[:, O_CCG:O_CCG + D_CC])
        u_s[h + tm:2 * h + tm, :] = u_of(nz)
        ch_s[0:h, :] = ch_of(pz)
        ch_s[h:h + tm, :] = z_ref[:, O_SCC:O_SCC + D_SC] * z_ref[:, O_SCH:O_SCH + D_SC]
        ch_s[h + tm:2 * h + tm, :] = ch_of(nz)
        for r0 in range(0, tm, rc):
            c = _taps_aligned(ccw_ref, u_s, p_s, [r0 + h + j - CC_W // 2 for j in range(CC_W)], rc) + ccb_ref[...]
            c_ref[r0:r0 + rc, :] = c
            xhat, _ = _ln_stats(c)
            n = xhat * ccg_ref[...] + ccbb_ref[...]
            ycc_ref[r0:r0 + rc, :] = (n * _sigmoid(n)).astype(BF16)
            acc = jnp.zeros((rc, D_SC), F32)
            for j in range(SC_W):
                acc = acc + scw_ref[j:j + 1, :] * ch_s[pl.ds(r0 + h + j - SC_W // 2, rc), :]
            ysc_ref[r0:r0 + rc, :] = (z_ref[r0:r0 + rc, O_SCB:O_SCB + D_SC] * acc).astype(BF16)
        q = z_ref[:, O_Q:O_Q + D_ATT]
        q_ref[...] = ((q * _wide(cos_ref[...], D_ATT) + _swap_halves(q) * _wide(sin_ref[...], D_ATT)) * (HEAD_DIM ** -0.5)).astype(BF16)
        k = z_ref[:, O_K:O_K + 128]
        k_ref[...] = (k * cos_ref[...] + _swap_halves(k) * sin_ref[...]).astype(BF16)
        v_ref[...] = z_ref[:, O_V:O_V + 128].astype(BF16)

    def full(a):
        return pl.BlockSpec(a.shape, lambda i: (0, 0))

    def rows(w):
        return pl.BlockSpec((tm, w), lambda i: (i, 0))

    return pl.pallas_call(
        body, name="mix_fwd", grid=(nt,),
        in_specs=_halo_specs(t, tm, h, D_IN) + [full(scw), full(ccw), full(ccb), full(ccg), full(ccbb), rows(128), rows(128)],
        out_specs=[rows(D_SC), rows(D_CC), rows(D_CC), rows(D_ATT), rows(128), rows(128)],
        out_shape=[S((t, D_SC), BF16), S((t, D_CC), BF16), S((t, D_CC), F32), S((t, D_ATT), BF16), S((t, 128), BF16),
                   S((t, 128), BF16)],
        scratch_shapes=[pltpu.VMEM((tm + 2 * h, D_CC), F32), pltpu.VMEM((tm + 2 * h, D_SC), F32),
                        pltpu.VMEM((SUBLANES, rc + SUBLANES, D_CC), F32)],
        compiler_params=_params(("parallel",)),
    )(z, z, z, scw, ccw, ccb, ccg, ccbb, cos, sin)


def _band_specs(nb, width):
    return [pl.BlockSpec((BLOCK, width), lambda n: (jnp.maximum(n - 1, 0), 0)),
            pl.BlockSpec((BLOCK, width), lambda n: (n, 0)),
            pl.BlockSpec((BLOCK, width), lambda n: (jnp.minimum(n + 1, nb - 1), 0))]


def _band_bias(n, nb, bias_s):
    qpos = lax.broadcasted_iota(jnp.int32, (BLOCK, 3 * BLOCK), 0)
    col = lax.broadcasted_iota(jnp.int32, (BLOCK, 3 * BLOCK), 1)
    ok = jnp.abs(qpos - (col - BLOCK)) <= BLOCK
    ok = jnp.logical_and(ok, jnp.logical_or(col >= BLOCK, n > 0))
    ok = jnp.logical_and(ok, jnp.logical_or(col < 2 * BLOCK, n < nb - 1))
    bias_s[...] = jnp.where(ok, 0.0, -1e30)


def _band_cat(refs, kvh):
    return jnp.concatenate([r[:, kvh * HEAD_DIM:(kvh + 1) * HEAD_DIM] for r in refs], axis=0)


def _head_scores(q_ref, kc, h, bias_s):
    qh = q_ref[:, h * HEAD_DIM:(h + 1) * HEAD_DIM]
    return qh, lax.dot_general(qh, kc, NT, preferred_element_type=F32) + bias_s[...]


def _softmax_parts(s, sk):
    m = jnp.maximum(jnp.max(s, axis=-1, keepdims=True), sk)
    p = jnp.exp(s - m)
    ps = jnp.exp(sk - m)
    return p, ps, jnp.sum(p, axis=-1, keepdims=True) + ps


def _head_probs(q_ref, kc, sink_ref, h, bias_s):
    qh, s = _head_scores(q_ref, kc, h, bias_s)
    return (qh,) + _softmax_parts(s, sink_ref[h])


def _attn_fwd(qr, kr, vv, sink, after=None):
    t = qr.shape[0]
    nb = t // BLOCK

    def body(q_ref, kp_ref, ko_ref, kn_ref, vp_ref, vo_ref, vn_ref, sink_ref, o_ref, bias_s):
        _band_bias(pl.program_id(0), nb, bias_s)
        kcs = [_band_cat((kp_ref, ko_ref, kn_ref), kvh) for kvh in range(N_KV_HEADS)]
        vcs = [_band_cat((vp_ref, vo_ref, vn_ref), kvh) for kvh in range(N_KV_HEADS)]
        s_next = _head_scores(q_ref, kcs[0], 0, bias_s)[1]
        for h in range(N_Q_HEADS):
            s = s_next
            if h + 1 < N_Q_HEADS:
                s_next = _head_scores(q_ref, kcs[(h + 1) // GROUP], h + 1, bias_s)[1]
            p, _, denom = _softmax_parts(s, sink_ref[h])
            o = jnp.dot(p.astype(BF16), vcs[h // GROUP], preferred_element_type=F32) / denom
            o_ref[:, h * HEAD_DIM:(h + 1) * HEAD_DIM] = o.astype(BF16)

    qspec = pl.BlockSpec((BLOCK, D_ATT), lambda n: (n, 0))
    body, in_specs, operands = _follow(
        body, 8, [qspec] + _band_specs(nb, 128) + _band_specs(nb, 128) + [pl.BlockSpec(memory_space=pltpu.SMEM)],
        [qr, kr, kr, kr, vv, vv, vv, sink], after)
    return pl.pallas_call(
        body, name="attn_fwd", grid=(nb,), in_specs=in_specs,
        out_specs=qspec, out_shape=S((t, D_ATT), BF16), scratch_shapes=[pltpu.VMEM((BLOCK, 3 * BLOCK), F32)],
        compiler_params=_params(("parallel",)),
    )(*operands)


def _out_ln(ysc, yatt, ycc, wout, x, gam, bet):
    t = x.shape[0]
    tm = min(WIDE_TILE, t)

    def body(sc_ref, at_ref, cc_ref, w_ref, x_ref, g_ref, b_ref, cat_ref, r_ref, y_ref, yb_ref):
        cat = jnp.concatenate([sc_ref[...], at_ref[...], cc_ref[...]], axis=1)
        cat_ref[...] = cat
        f = jnp.dot(cat, w_ref[...], preferred_element_type=F32)
        _residual_ln_out(x_ref, f, 1.0, g_ref, b_ref, r_ref, y_ref, yb_ref)

    def rows(w):
        return pl.BlockSpec((tm, w), lambda i: (i, 0))

    vec = pl.BlockSpec((1, D), lambda i: (0, 0))
    return pl.pallas_call(
        body, name="out_ln", grid=(t // tm,),
        in_specs=[rows(D_SC), rows(D_ATT), rows(D_CC), _resident((D, D)), rows(D), vec, vec],
        out_specs=[rows(D), rows(D), rows(D), rows(D)],
        out_shape=[S((t, D), BF16), S((t, D), F32), S((t, D), F32), S((t, D), BF16)],
        compiler_params=_params(("parallel",)),
    )(ysc, yatt, ycc, wout, x, gam, bet)


def _ln_bwd_block(dy_ref, r_ref, g_ref, dgam_ref, dbet_ref):
    xhat, rstd = _ln_stats(r_ref[...])
    dy = dy_ref[...]

    @pl.when(pl.program_id(0) == 0)
    def _():
        dgam_ref[...] = jnp.zeros_like(dgam_ref)
        dbet_ref[...] = jnp.zeros_like(dbet_ref)

    dgam_ref[...] += jnp.sum(dy * xhat, axis=0, keepdims=True)
    dbet_ref[...] += jnp.sum(dy, axis=0, keepdims=True)
    return _ln_bwd(dy, xhat, rstd, g_ref[...])


def _ffn_bwd(dy, r, gam, wd, gu):
    t = dy.shape[0]
    tm = min(TOKEN_TILE, t)
    chunks = list(zip(FFN_CHUNKS[:-1], FFN_CHUNKS[1:]))

    def body(dy_ref, r_ref, g_ref, w_ref, gu_ref, dr_ref, df_ref, dh_ref, dgam_ref, dbet_ref):
        dr = _ln_bwd_block(dy_ref, r_ref, g_ref, dgam_ref, dbet_ref)
        dr_ref[...] = dr
        dfb = (0.5 * dr).astype(BF16)
        df_ref[...] = dfb
        for lo, hi in chunks:
            da = lax.dot_general(dfb, w_ref[lo:hi, :], NT, preferred_element_type=F32)
            g = gu_ref[:, lo:hi].astype(F32)
            u = gu_ref[:, F + lo:F + hi].astype(F32)
            sg = _sigmoid(g)
            dh_ref[:, lo:hi] = (da * u * (sg * (1.0 + g * (1.0 - sg)))).astype(BF16)
            dh_ref[:, F + lo:F + hi] = (da * (g * sg)).astype(BF16)

    row = pl.BlockSpec((tm, D), lambda i: (i, 0))
    vec = pl.BlockSpec((1, D), lambda i: (0, 0))
    wide = pl.BlockSpec((tm, 2 * F), lambda i: (i, 0))
    return pl.pallas_call(
        body, name="ffn_bwd", grid=(t // tm,),
        in_specs=[row, row, vec, _resident((F, D)), wide],
        out_specs=[row, row, wide, vec, vec],
        out_shape=[S((t, D), F32), S((t, D), BF16), S((t, 2 * F), BF16), S((1, D), F32), S((1, D), F32)],
        compiler_params=_params(("arbitrary",)),
    )(dy, r, gam, wd, gu)


def _ffn_bwd_dx(dy, r, gam, wd, gu, wgut, after=None):
    t = dy.shape[0]
    tm = min(TOKEN_TILE, t)
    n = t // tm
    chunks = list(zip(FFN_CHUNKS[:-1], FFN_CHUNKS[1:]))

    def body(dy_ref, r_ref, g_ref, w_ref, gu_ref, wg_ref, dx_ref, df_ref, dh_ref, dgam_ref, dbet_ref,
             keep_a, keep_b, dr_keep):
        i = pl.program_id(0)

        @pl.when(i == 0)
        def _():
            keep_a[...] = jnp.zeros_like(keep_a)
            keep_b[...] = jnp.zeros_like(keep_b)
            dr_keep[...] = jnp.zeros_like(dr_keep)
            dgam_ref[...] = jnp.zeros_like(dgam_ref)
            dbet_ref[...] = jnp.zeros_like(dbet_ref)

        def step(prev, cur):
            def to_dx(c):
                lo, hi = chunks[c]
                return (jnp.dot(prev[:, lo:hi], wg_ref[lo:hi, :], preferred_element_type=F32)
                        + jnp.dot(prev[:, F + lo:F + hi], wg_ref[F + lo:F + hi, :], preferred_element_type=F32))

            acc = ALPHA * dr_keep[...]
            xhat, rstd = _ln_stats(r_ref[...])
            dy_t = dy_ref[...]
            live = jnp.where(i < n, 1.0, 0.0)
            dgam_ref[...] += live * jnp.sum(dy_t * xhat, axis=0, keepdims=True)
            dbet_ref[...] += live * jnp.sum(dy_t, axis=0, keepdims=True)
            dr = _ln_bwd(dy_t, xhat, rstd, g_ref[...])
            dfb = (0.5 * dr).astype(BF16)
            df_ref[...] = dfb

            def down(c):
                return lax.dot_general(dfb, w_ref[chunks[c][0]:chunks[c][1], :], NT, preferred_element_type=F32)

            das = {0: down(0), 1: down(1)}
            for c, (lo, hi) in enumerate(chunks):
                if c + 2 < len(chunks):
                    das[c + 2] = down(c + 2)
                acc = acc + to_dx(c)
                da = das.pop(c)
                g = gu_ref[:, lo:hi].astype(F32)
                u = gu_ref[:, F + lo:F + hi].astype(F32)
                sg = _sigmoid(g)
                dg = (da * u * (sg * (1.0 + g * (1.0 - sg)))).astype(BF16)
                du = (da * (g * sg)).astype(BF16)
                dh_ref[:, lo:hi] = dg
                dh_ref[:, F + lo:F + hi] = du
                cur[:, lo:hi] = dg
                cur[:, F + lo:F + hi] = du
            dx_ref[...] = acc
            dr_keep[...] = dr

        @pl.when(i % 2 == 0)
        def _():
            step(keep_b, keep_a)

        @pl.when(i % 2 == 1)
        def _():
            step(keep_a, keep_b)

    cur_row = lambda i: (jnp.minimum(i, n - 1), 0)
    row = pl.BlockSpec((tm, D), cur_row)
    vec = pl.BlockSpec((1, D), lambda i: (0, 0))
    wide = pl.BlockSpec((tm, 2 * F), cur_row)
    body, in_specs, operands = _follow(
        body, 6, [row, row, vec, _resident((F, D)), wide, _resident((2 * F, D))], [dy, r, gam, wd, gu, wgut], after)
    return pl.pallas_call(
        body, name="ffn_bwd_dx", grid=(n + 1,), in_specs=in_specs,
        out_specs=[pl.BlockSpec((tm, D), lambda i: (jnp.maximum(i - 1, 0), 0)), row, wide, vec, vec],
        out_shape=[S((t, D), F32), S((t, D), BF16), S((t, 2 * F), BF16), S((1, D), F32), S((1, D), F32)],
        scratch_shapes=[pltpu.VMEM((tm, 2 * F), BF16), pltpu.VMEM((tm, 2 * F), BF16), pltpu.VMEM((tm, D), F32)],
        compiler_params=_params(("arbitrary",)),
    )(*operands)


def _dx(dr, dh, w, after=None):
    t = dr.shape[0]
    tm = min(WIDE_TILE, t)
    kk = dh.shape[1]

    def body(dr_ref, dh_ref, w_ref, o_ref):
        o_ref[...] = ALPHA * dr_ref[...] + jnp.dot(dh_ref[...], w_ref[...], preferred_element_type=F32)

    row = pl.BlockSpec((tm, D), lambda i: (i, 0))
    body, in_specs, operands = _follow(
        body, 3, [row, pl.BlockSpec((tm, kk), lambda i: (i, 0)), _resident((kk, D))], [dr, dh, w], after)
    return pl.pallas_call(
        body, name="dx", grid=(t // tm,), in_specs=in_specs,
        out_specs=row, out_shape=S((t, D), F32), compiler_params=_params(("parallel",)),
    )(*operands)


def _wgrad(a, b, ta, after=None):
    t, ka = a.shape
    tk = min(WGRAD_TOKENS, t)
    nk = t // tk

    def body(a_ref, b_ref, o_ref, acc):
        k = pl.program_id(1)

        @pl.when(k == 0)
        def _():
            acc[...] = jnp.zeros_like(acc)

        acc[...] += lax.dot_general(a_ref[...], b_ref[...], TN, preferred_element_type=F32)

        @pl.when(k == nk - 1)
        def _():
            o_ref[...] = acc[...].astype(BF16)

    body, in_specs, operands = _follow(
        body, 2, [pl.BlockSpec((tk, ta), lambda i, k: (k, i)), pl.BlockSpec((tk, D), lambda i, k: (k, 0))], [a, b], after)
    return pl.pallas_call(
        body, name="wgrad", grid=(ka // ta, nk), in_specs=in_specs,
        out_specs=pl.BlockSpec((ta, D), lambda i, k: (i, 0)), out_shape=S((ka, D), BF16),
        scratch_shapes=[pltpu.VMEM((ta, D), F32)], compiler_params=_params(("parallel", "arbitrary")),
    )(*operands)


def _out_bwd(dy, r, gam, wout, after=None):
    t = dy.shape[0]
    tm = min(TOKEN_TILE, t)

    def body(dy_ref, r_ref, g_ref, w_ref, dr_ref, dm_ref, dsc_ref, dat_ref, dcc_ref, dgam_ref, dbet_ref):
        dr = _ln_bwd_block(dy_ref, r_ref, g_ref, dgam_ref, dbet_ref)
        dr_ref[...] = dr
        dmb = dr.astype(BF16)
        dm_ref[...] = dmb
        dcat = lax.dot_general(dmb, w_ref[...], NT, preferred_element_type=F32)
        dsc_ref[...] = dcat[:, 0:D_SC]
        dat_ref[...] = dcat[:, D_SC:D_SC + D_ATT]
        dcc_ref[...] = dcat[:, D_SC + D_ATT:D]

    def rows(w):
        return pl.BlockSpec((tm, w), lambda i: (i, 0))

    vec = pl.BlockSpec((1, D), lambda i: (0, 0))
    body, in_specs, operands = _follow(body, 4, [rows(D), rows(D), vec, _resident((D, D))], [dy, r, gam, wout], after)
    return pl.pallas_call(
        body, name="out_bwd", grid=(t // tm,), in_specs=in_specs,
        out_specs=[rows(D), rows(D), rows(D_SC), rows(D_ATT), rows(D_CC), vec, vec],
        out_shape=[S((t, D), F32), S((t, D), BF16), S((t, D_SC), F32), S((t, D_ATT), F32), S((t, D_CC), F32),
                   S((1, D), F32), S((1, D), F32)],
        compiler_params=_params(("arbitrary",)),
    )(*operands)


def _attn_bwd(qr, kr, vv, sink, do, yatt):
    t = qr.shape[0]
    nb = t // BLOCK
    scale = HEAD_DIM ** -0.5

    def body(q_ref, kp_ref, ko_ref, kn_ref, vp_ref, vo_ref, vn_ref, sink_ref, do_ref, o_ref,
             dq_ref, dk_ref, dv_ref, dsink_ref, bias_s, ds_s, p_s, q_s, dou_s):
        n = pl.program_id(0)
        _band_bias(n, nb, bias_s)

        @pl.when(n == 0)
        def _():
            dsink_ref[...] = jnp.zeros_like(dsink_ref)

        kcs = [_band_cat((kp_ref, ko_ref, kn_ref), kvh) for kvh in range(N_KV_HEADS)]
        vcs = [_band_cat((vp_ref, vo_ref, vn_ref), kvh) for kvh in range(N_KV_HEADS)]

        def scores(h):
            return _head_scores(q_ref, kcs[h // GROUP], h, bias_s)

        def probs(h, qh, s):
            cols = slice(h * HEAD_DIM, (h + 1) * HEAD_DIM)
            p, ps, denom = _softmax_parts(s, sink_ref[h])
            doh = do_ref[:, cols]
            dd = jnp.sum(doh * o_ref[:, cols].astype(F32), axis=-1, keepdims=True) / denom
            dou = (doh / denom).astype(BF16)
            dp = lax.dot_general(dou, vcs[h // GROUP], NT, preferred_element_type=F32)
            dsink_ref[h:h + 1, :] += jnp.zeros((1, 128), F32) - jnp.sum(ps * dd)
            return qh, p, dd, dou, dp

        def grads(h, qh, p, dd, dou, dp):
            kvh, g = divmod(h, GROUP)
            cols = slice(h * HEAD_DIM, (h + 1) * HEAD_DIM)
            rows = slice(g * BLOCK, (g + 1) * BLOCK)
            ds = (p * (dp - dd)).astype(BF16)
            dq_ref[:, cols] = jnp.dot(ds, kcs[kvh], preferred_element_type=F32) * scale
            ds_s[rows, :] = ds
            p_s[rows, :] = p.astype(BF16)
            q_s[rows, :] = qh
            dou_s[rows, :] = dou
            if g == GROUP - 1:
                dk = lax.dot_general(ds_s[...], q_s[...], TN, preferred_element_type=F32)
                dv = lax.dot_general(p_s[...], dou_s[...], TN, preferred_element_type=F32)
                for j in range(3):
                    dk_ref[j, :, kvh * HEAD_DIM:(kvh + 1) * HEAD_DIM] = dk[j * BLOCK:(j + 1) * BLOCK, :]
                    dv_ref[j, :, kvh * HEAD_DIM:(kvh + 1) * HEAD_DIM] = dv[j * BLOCK:(j + 1) * BLOCK, :]

        sc = {0: scores(0), 1: scores(1)}
        pr = {0: probs(0, *sc.pop(0))}
        for h in range(N_Q_HEADS):
            if h + 2 < N_Q_HEADS:
                sc[h + 2] = scores(h + 2)
            if h + 1 < N_Q_HEADS:
                pr[h + 1] = probs(h + 1, *sc.pop(h + 1))
            grads(h, *pr.pop(h))

    qspec = pl.BlockSpec((BLOCK, D_ATT), lambda n: (n, 0))
    part = pl.BlockSpec((3, BLOCK, 128), lambda n: (0, n, 0))
    stacked = GROUP * BLOCK
    return pl.pallas_call(
        body, name="attn_bwd", grid=(nb,),
        in_specs=[qspec] + _band_specs(nb, 128) + _band_specs(nb, 128) + [pl.BlockSpec(memory_space=pltpu.SMEM), qspec, qspec],
        out_specs=[qspec, part, part, pl.BlockSpec((N_Q_HEADS, 128), lambda n: (0, 0))],
        out_shape=[S((t, D_ATT), F32), S((3, t, 128), F32), S((3, t, 128), F32), S((N_Q_HEADS, 128), F32)],
        scratch_shapes=[pltpu.VMEM((BLOCK, 3 * BLOCK), F32), pltpu.VMEM((stacked, 3 * BLOCK), BF16),
                        pltpu.VMEM((stacked, 3 * BLOCK), BF16), pltpu.VMEM((stacked, HEAD_DIM), BF16),
                        pltpu.VMEM((stacked, HEAD_DIM), BF16)],
        compiler_params=_params(("arbitrary",)),
    )(qr, kr, kr, kr, vv, vv, vv, sink, do, yatt)


def _tap_grads_aligned(d_own, x_ref, d_ref, offsets, out_ref):
    rows = d_own.shape[0]
    padded = jnp.concatenate([d_own, jnp.zeros((SUBLANES, d_own.shape[1]), F32)], axis=0)
    for r in range(SUBLANES):
        d_ref[r] = padded if r == 0 else pltpu.roll(padded, r, 0)
    for j, o in enumerate(offsets):
        r = o % SUBLANES
        out_ref[j:j + 1, :] += jnp.sum(d_ref[r] * x_ref[pl.ds(o - r, rows + SUBLANES), :], axis=0, keepdims=True)


def _mix_bwd(z, c, dysc, dycc, dqr, dkp, dvp, scw, ccw, ccb, ccg, ccbb, cos, sin):
    t = z.shape[0]
    tm = min(MIX_BWD_TILE, t)
    nt = t // tm
    h = HALO_BWD
    half = CC_W // 2
    ext = tm + 2 * h

    def body(z_ref, zp_ref, zn_ref, c_ref, cp_ref, cn_ref, dsc_ref, dscp_ref, dscn_ref, dcc_ref, dccp_ref, dccn_ref,
             dq_ref, dk0_ref, dk1_ref, dk2_ref, dv0_ref, dv1_ref, dv2_ref,
             scw_ref, ccw_ref, ccb_ref, ccg_ref, ccbb_ref, cos_ref, sin_ref,
             dz_ref, dscw_ref, dccw_ref, dvec_ref, u_s, dc_s, ch_s, g_s, p_s, d_s):
        i = pl.program_id(0)
        first, last = i == 0, i == nt - 1

        @pl.when(first)
        def _():
            dscw_ref[...] = jnp.zeros_like(dscw_ref)
            dccw_ref[...] = jnp.zeros_like(dccw_ref)
            dvec_ref[...] = jnp.zeros_like(dvec_ref)

        pz = jnp.where(first, 0.0, zp_ref[...])
        nz = jnp.where(last, 0.0, zn_ref[...])
        zo = z_ref[...]

        def u_of(zz):
            return zz[:, O_CCA:O_CCA + D_CC] * _sigmoid(zz[:, O_CCG:O_CCG + D_CC])

        u_s[0:h, :] = u_of(pz)
        u_s[h:h + tm, :] = u_of(zo)
        u_s[h + tm:ext, :] = u_of(nz)
        c_ext = jnp.concatenate([jnp.where(first, 0.0, cp_ref[...]), c_ref[...], jnp.where(last, 0.0, cn_ref[...])], axis=0)
        xhat, rstd = _ln_stats(c_ext)
        nn = xhat * ccg_ref[...] + ccbb_ref[...]
        sg = _sigmoid(nn)
        dycc_ext = jnp.concatenate([jnp.where(first, 0.0, dccp_ref[...]), dcc_ref[...],
                                    jnp.where(last, 0.0, dccn_ref[...])], axis=0)
        dn = dycc_ext * (sg * (1.0 + nn * (1.0 - sg)))
        dc = _ln_bwd(dn, xhat, rstd, ccg_ref[...])
        dc_s[...] = dc
        dn_own = dn[h:h + tm, :]
        dc_own = dc[h:h + tm, :]
        dvec_ref[0:1, :] += jnp.sum(dc_own, axis=0, keepdims=True)
        dvec_ref[1:2, :] += jnp.sum(dn_own * xhat[h:h + tm, :], axis=0, keepdims=True)
        dvec_ref[2:3, :] += jnp.sum(dn_own, axis=0, keepdims=True)
        du = _taps_aligned(ccw_ref, dc_s, p_s, [h + half - j for j in range(CC_W)], tm)
        _tap_grads_aligned(dc_own, u_s, d_s, [h + j - half for j in range(CC_W)], dccw_ref)
        gate = _sigmoid(zo[:, O_CCG:O_CCG + D_CC])
        a_own = zo[:, O_CCA:O_CCA + D_CC]
        dz_ref[:, O_CCA:O_CCA + D_CC] = (du * gate).astype(BF16)
        dz_ref[:, O_CCG:O_CCG + D_CC] = (du * a_own * gate * (1.0 - gate)).astype(BF16)

        def ch_of(zz):
            return zz[:, O_SCC:O_SCC + D_SC] * zz[:, O_SCH:O_SCH + D_SC]

        ch_s[0:h, :] = ch_of(pz)
        ch_s[h:h + tm, :] = ch_of(zo)
        ch_s[h + tm:ext, :] = ch_of(nz)
        g_s[0:h, :] = jnp.where(first, 0.0, dscp_ref[...]) * pz[:, O_SCB:O_SCB + D_SC]
        g_s[h:h + tm, :] = dsc_ref[...] * zo[:, O_SCB:O_SCB + D_SC]
        g_s[h + tm:ext, :] = jnp.where(last, 0.0, dscn_ref[...]) * nz[:, O_SCB:O_SCB + D_SC]
        conv = jnp.zeros((tm, D_SC), F32)
        dch = jnp.zeros((tm, D_SC), F32)
        g_own = g_s[h:h + tm, :]
        for j in range(SC_W):
            chj = ch_s[pl.ds(h + j - SC_W // 2, tm), :]
            conv = conv + scw_ref[j:j + 1, :] * chj
            dch = dch + scw_ref[j:j + 1, :] * g_s[pl.ds(h + SC_W // 2 - j, tm), :]
            dscw_ref[j:j + 1, :] += jnp.sum(g_own * chj, axis=0, keepdims=True)
        dz_ref[:, O_SCB:O_SCB + D_SC] = (dsc_ref[...] * conv).astype(BF16)
        dz_ref[:, O_SCC:O_SCC + D_SC] = (dch * zo[:, O_SCH:O_SCH + D_SC]).astype(BF16)
        dz_ref[:, O_SCH:O_SCH + D_SC] = (dch * zo[:, O_SCC:O_SCC + D_SC]).astype(BF16)

        dq = dq_ref[...]
        dz_ref[:, O_Q:O_Q + D_ATT] = (dq * _wide(cos_ref[...], D_ATT) + _swap_halves(dq * _wide(sin_ref[...], D_ATT))).astype(BF16)
        dk = dk1_ref[0] + jnp.where(last, 0.0, dk0_ref[0]) + jnp.where(first, 0.0, dk2_ref[0])
        dz_ref[:, O_K:O_K + 128] = (dk * cos_ref[...] + _swap_halves(dk * sin_ref[...])).astype(BF16)
        dv = dv1_ref[0] + jnp.where(last, 0.0, dv0_ref[0]) + jnp.where(first, 0.0, dv2_ref[0])
        dz_ref[:, O_V:O_V + 128] = dv.astype(BF16)

    def full(a):
        return pl.BlockSpec(a.shape, lambda i: (0, 0))

    def rows(w):
        return pl.BlockSpec((tm, w), lambda i: (i, 0))

    parts = [pl.BlockSpec((1, tm, 128), lambda i: (0, jnp.minimum(i + 1, nt - 1), 0)),
             pl.BlockSpec((1, tm, 128), lambda i: (1, i, 0)),
             pl.BlockSpec((1, tm, 128), lambda i: (2, jnp.maximum(i - 1, 0), 0))]
    acc_spec = lambda r: pl.BlockSpec((r, D_CC), lambda i: (0, 0))
    return pl.pallas_call(
        body, name="mix_bwd", grid=(nt,),
        in_specs=(_halo_specs(t, tm, h, D_IN) + _halo_specs(t, tm, h, D_CC) + _halo_specs(t, tm, h, D_SC)
                  + _halo_specs(t, tm, h, D_CC) + [rows(D_ATT)] + parts + parts
                  + [full(scw), full(ccw), full(ccb), full(ccg), full(ccbb), rows(128), rows(128)]),
        out_specs=[rows(D_IN), acc_spec(SC_W), acc_spec(CC_W), acc_spec(3)],
        out_shape=[S((t, D_IN), BF16), S((SC_W, D_SC), F32), S((CC_W, D_CC), F32), S((3, D_CC), F32)],
        scratch_shapes=[pltpu.VMEM((ext, D_CC), F32), pltpu.VMEM((ext, D_CC), F32),
                        pltpu.VMEM((ext, D_SC), F32), pltpu.VMEM((ext, D_SC), F32),
                        pltpu.VMEM((SUBLANES, tm + SUBLANES, D_CC), F32), pltpu.VMEM((SUBLANES, tm + SUBLANES, D_CC), F32)],
        compiler_params=_params(("arbitrary",)),
    )(z, z, z, c, c, c, dysc, dysc, dysc, dycc, dycc, dycc, dqr, dkp, dkp, dkp, dvp, dvp, dvp,
      scw, ccw, ccb, ccg, ccbb, cos, sin)


def _adamw(w, g, m, v):
    m = ADAM_B1 * m + (1.0 - ADAM_B1) * g
    v = ADAM_B2 * v + (1.0 - ADAM_B2) * (g * g)
    m_hat = m / (1.0 - ADAM_B1 ** ADAM_STEP)
    v_hat = v / (1.0 - ADAM_B2 ** ADAM_STEP)
    delta = -ADAM_LR * (m_hat / (jnp.sqrt(v_hat) + ADAM_EPS) + ADAM_WD * w)
    return delta, m, v


def _row_tile(rows):
    for cand in (256, 176, 128):
        if rows % cand == 0:
            return cand
    return rows


def _sum_adam(recv, w, m, v, transposed):
    nl, rows = len(recv), recv[0].shape[1]
    tile = 256 if transposed else _row_tile(rows)
    nc = (D if transposed else rows) // tile

    def body(*refs):
        w_ref, m_ref, v_ref, g_ref, d_ref, mo_ref, vo_ref = refs[nl:]
        for layer in range(nl):
            @pl.when(pl.program_id(0) == layer)
            def _(r_ref=refs[layer]):
                g = r_ref[0].astype(F32)
                for s in range(1, N_DEV):
                    g = g + r_ref[s].astype(F32)
                if transposed:
                    g = g.T
                g_ref[0] = g
                d_ref[0], mo_ref[0], vo_ref[0] = _adamw(w_ref[0], g, m_ref[0], v_ref[0])

    def held(layer):
        def at(l, c):
            return jnp.where(l == layer, c, jnp.where(l < layer, 0, nc - 1))
        if transposed:
            return pl.BlockSpec((N_DEV, rows, tile), lambda l, c: (0, 0, at(l, c)))
        return pl.BlockSpec((N_DEV, tile, D), lambda l, c: (0, at(l, c), 0))

    if transposed:
        blk = pl.BlockSpec((1, tile, rows), lambda l, c: (l, c, 0))
    else:
        blk = pl.BlockSpec((1, tile, D), lambda l, c: (l, c, 0))
    out = S(w.shape, F32)
    return pl.pallas_call(
        body, name="sum_adam_t" if transposed else "sum_adam", grid=(nl, nc),
        in_specs=[held(layer) for layer in range(nl)] + [blk, blk, blk], out_specs=[blk] * 4, out_shape=[out] * 4,
        compiler_params=_params(("arbitrary", "arbitrary")),
    )(*recv, w, m, v)


def _small_sum(gathered):
    rows = gathered.shape[1]

    def body(g_ref, o_ref):
        acc = g_ref[0]
        for s in range(1, N_DEV):
            acc = acc + g_ref[s]
        o_ref[...] = acc

    return pl.pallas_call(
        body, name="small_sum", in_specs=[pl.BlockSpec(gathered.shape, lambda: (0, 0, 0))],
        out_specs=pl.BlockSpec((rows, 128), lambda: (0, 0)), out_shape=S((rows, 128), F32),
    )(gathered)


def _small_adam(w, g, m, v):
    def body(w_ref, g_ref, m_ref, v_ref, d_ref, mo_ref, vo_ref):
        d_ref[...], mo_ref[...], vo_ref[...] = _adamw(w_ref[...], g_ref[...], m_ref[...], v_ref[...])

    spec = pl.BlockSpec(w.shape, lambda: (0, 0))
    return pl.pallas_call(
        body, name="small_adam", in_specs=[spec] * 4, out_specs=[spec] * 3, out_shape=[S(w.shape, F32)] * 3,
    )(w, g, m, v)


def _pack(pieces):
    flat = jnp.concatenate([p.reshape(-1).astype(F32) for p in pieces])
    n = flat.shape[0]
    rows = -(-n // 1024) * 8
    return jnp.pad(flat, (0, rows * 128 - n)).reshape(rows, 128)


def _unpack(packed, shapes):
    flat = packed.reshape(-1)
    out, o = [], 0
    for shp in shapes:
        n = int(np.prod(shp))
        out.append(flat[o:o + n].reshape(shp))
        o += n
    return out


def _rope_tables(t):
    half = HEAD_DIM // 2
    inv_freq = ROPE_THETA ** (-jnp.arange(half, dtype=F32) / half)
    ang = jnp.arange(t).astype(F32)[:, None] * jnp.tile(inv_freq, 128 // half)[None, :]
    sign = jnp.tile(jnp.concatenate([-jnp.ones((half,), F32), jnp.ones((half,), F32)]), 128 // HEAD_DIM)
    return jnp.cos(ang), jnp.sin(ang) * sign[None, :]


BIG = ("ffn1_w_gu", "ffn1_w_down", "w_in", "w_out", "ffn2_w_gu", "ffn2_w_down")
BIG_T = {"ffn1_w_gu": True, "ffn1_w_down": False, "w_in": True, "w_out": False, "ffn2_w_gu": True, "ffn2_w_down": False}
SWAPPED = ("ffn1_w_gu", "ffn2_w_gu")
REPLICATED = ("ln1_g", "ln1_b", "attn_sink", "cc_conv_b", "cc_ln_g", "cc_ln_b", "ln2_g", "ln2_b", "ln3_g", "ln3_b")
CONVS = ("sc_conv_w", "cc_conv_w")
WEIGHTS = ("ffn1_w_gu", "ffn1_w_down", "ln1_g", "ln1_b", "w_in", "sc_conv_w", "attn_sink", "cc_conv_w", "cc_conv_b",
           "cc_ln_g", "cc_ln_b", "w_out", "ln2_g", "ln2_b", "ffn2_w_gu", "ffn2_w_down", "ln3_g", "ln3_b")


def kernel(x, ffn1_w_gu, ffn1_w_down, ln1_g, ln1_b, w_in, sc_conv_w, attn_sink, cc_conv_w, cc_conv_b, cc_ln_g, cc_ln_b, w_out, ln2_g, ln2_b, ffn2_w_gu, ffn2_w_down, ln3_g, ln3_b, loss_target, m_ffn1_w_gu, m_ffn1_w_down, m_ln1_g, m_ln1_b, m_w_in, m_sc_conv_w, m_attn_sink, m_cc_conv_w, m_cc_conv_b, m_cc_ln_g, m_cc_ln_b, m_w_out, m_ln2_g, m_ln2_b, m_ffn2_w_gu, m_ffn2_w_down, m_ln3_g, m_ln3_b, v_ffn1_w_gu, v_ffn1_w_down, v_ln1_g, v_ln1_b, v_w_in, v_sc_conv_w, v_attn_sink, v_cc_conv_w, v_cc_conv_b, v_cc_ln_g, v_cc_ln_b, v_w_out, v_ln2_g, v_ln2_b, v_ffn2_w_gu, v_ffn2_w_down, v_ln3_g, v_ln3_b):
    args = dict(locals())
    w = {n: args[n] for n in WEIGHTS}
    mom = {n: args["m_" + n] for n in WEIGHTS}
    var = {n: args["v_" + n] for n in WEIGHTS}
    x0 = x[0]
    target = loss_target[0]
    t = x0.shape[0]
    idx = 4 * lax.axis_index("x") + 2 * lax.axis_index("y") + lax.axis_index("c")

    blocks = {(n, l): (w[n][l].T if BIG_T[n] else w[n][l]).astype(BF16) for l in range(DEPTH) for n in BIG}
    where = {}

    def start_stage(tag, members, after, extra=()):
        srcs = list(extra) + [blocks[m] for m in members]
        started = _send_start(srcs, [_own_slot(s) for s in srcs], _whole, f"gather_start_{tag}", after)
        for j, m in enumerate(members):
            where[m] = (started, len(extra) + j)
        return started

    def wait_stage(started, k, after, name):
        send, rcv, srcs, lands, _ = started
        return _recv_wait(send, rcv, [k], [srcs[k]], [lands[k]], _whole, after, name)[0]

    def weight(n, l, after):
        g = wait_stage(*where[n, l], after, f"gather_wait_{n}_{l}")
        return g.reshape(N_DEV * g.shape[1], g.shape[2])

    first = start_stage("a", [("ffn1_w_gu", 0)], x0, extra=[_pack([w["sc_conv_w"], w["cc_conv_w"]])])
    xf, xb = x0, x0.astype(BF16)
    cos, sin = _rope_tables(t)
    conv_all = wait_stage(first, 0, (xb, cos, sin), "gather_wait_convs").reshape(N_DEV, -1)
    n_sc = DEPTH * SC_W * 32
    scw_full = conv_all[:, :n_sc].reshape(N_DEV, DEPTH, SC_W, 32).transpose(1, 2, 0, 3).reshape(DEPTH, SC_W, D_SC)
    ccw_full = conv_all[:, n_sc:n_sc + DEPTH * CC_W * 32].reshape(N_DEV, DEPTH, CC_W, 32).transpose(1, 2, 0, 3).reshape(DEPTH, CC_W, D_CC)

    row = lambda a, l: a[l].reshape(1, -1)

    saved, full = [], {}
    for l in range(DEPTH):
        sv = {"x0b": xb}
        token = None
        full["ffn1_w_gu", l] = weight("ffn1_w_gu", l, (xb, scw_full, ccw_full) if l == 0 else xb)
        if l == 0:
            token = start_stage("b", [("ffn1_w_down", 0), ("w_in", 0), ("w_out", 0)], full["ffn1_w_gu", l])[-1]
        gu1, a1 = _ffn_up(xb, full["ffn1_w_gu", l], token)
        full["ffn1_w_down", l] = weight("ffn1_w_down", l, a1)
        if l == 0:
            token = start_stage("c", [("ffn2_w_gu", 0), ("ffn2_w_down", 0)], full["ffn1_w_down", l])[-1]
        r1, x1, x1b = _ffn_down_ln(a1, full["ffn1_w_down", l], xf, row(ln1_g, l), row(ln1_b, l), token)
        full["w_in", l] = weight("w_in", l, x1b)
        z = _proj_in(x1b, full["w_in", l])
        ysc, ycc, cpre, qr, kr, vv = _mix_fwd(z, scw_full[l], ccw_full[l], row(cc_conv_b, l), row(cc_ln_g, l), row(cc_ln_b, l), cos, sin)
        if l == 0:
            token = start_stage("d", [("ffn1_w_gu", 1), ("ffn1_w_down", 1)], ysc)[-1]
        yatt = _attn_fwd(qr, kr, vv, attn_sink[l], token)
        full["w_out", l] = weight("w_out", l, yatt)
        ycat, r2, x2, x2b = _out_ln(ysc, yatt, ycc, full["w_out", l], x1, row(ln2_g, l), row(ln2_b, l))
        full["ffn2_w_gu", l] = weight("ffn2_w_gu", l, x2b)
        if l == 0:
            token = start_stage("e", [("w_in", 1), ("w_out", 1), ("ffn2_w_gu", 1), ("ffn2_w_down", 1)], full["ffn2_w_gu", l])[-1]
        gu2, a2 = _ffn_up(x2b, full["ffn2_w_gu", l], token)
        full["ffn2_w_down", l] = weight("ffn2_w_down", l, a2)
        if l + 1 < DEPTH:
            r3, xf, xb = _ffn_down_ln(a2, full["ffn2_w_down", l], x2, row(ln3_g, l), row(ln3_b, l))
        else:
            r3, dy, sq = _ffn_down_ln_loss(a2, full["ffn2_w_down", l], x2, row(ln3_g, l), row(ln3_b, l), target)
        sv.update(gu1=gu1, a1=a1, r1=r1, x1b=x1b, z=z, cpre=cpre, qr=qr, kr=kr, vv=vv, yatt=yatt, ycat=ycat, r2=r2, x2b=x2b,
                  gu2=gu2, a2=a2, r3=r3)
        saved.append(sv)

    loss = lax.psum(0.5 * jnp.sum(sq) / D, ("x", "y", "c"))

    sent = []
    small = {n: [None] * DEPTH for n in REPLICATED + CONVS}

    def send_grads(names, l, gs):
        srcs = [g.reshape(N_DEV, g.shape[0] // N_DEV, g.shape[1]) for g in gs]
        lands = [_own_slot(lax.dynamic_index_in_dim(s3, idx, 0, keepdims=False)) for s3 in srcs]
        started = _send_start(srcs, lands, _block_of, f"grads_start_{names[0]}_{l}", gs[-1])
        sent.append((names, l, started))
        return started[-1]

    token = None
    for l in reversed(range(DEPTH)):
        sv = saved[l]
        dy, dfb, dh, dg, db = _ffn_bwd_dx(dy, sv["r3"], row(ln3_g, l), full["ffn2_w_down", l], sv["gu2"],
                                          full["ffn2_w_gu", l], token)
        small["ln3_g"][l], small["ln3_b"][l] = dg, db
        token = send_grads(("ffn2_w_down", "ffn2_w_gu"), l,
                           [_wgrad(sv["a2"], dfb, F // 2), _wgrad(dh, sv["x2b"], F // 2)])

        dr, dmb, dysc, dyatt, dycc, dg, db = _out_bwd(dy, sv["r2"], row(ln2_g, l), full["w_out", l], token)
        small["ln2_g"][l], small["ln2_b"][l] = dg, db
        g_out = _wgrad(sv["ycat"], dmb, D)
        dqr, dkp, dvp, dsink = _attn_bwd(sv["qr"], sv["kr"], sv["vv"], attn_sink[l], dyatt, sv["yatt"])
        small["attn_sink"][l] = dsink[:, 0]
        dz, dscw, dccw, dvec = _mix_bwd(sv["z"], sv["cpre"], dysc, dycc, dqr, dkp, dvp, scw_full[l], ccw_full[l],
                                        row(cc_conv_b, l), row(cc_ln_g, l), row(cc_ln_b, l), cos, sin)
        small["sc_conv_w"][l], small["cc_conv_w"][l] = dscw, dccw
        small["cc_conv_b"][l], small["cc_ln_g"][l], small["cc_ln_b"][l] = dvec[0], dvec[1], dvec[2]
        token = send_grads(("w_out", "w_in"), l, [g_out, _wgrad(dz, sv["x1b"], D)])
        dy = _dx(dr, dz, full["w_in", l], token)

        if l > 0:
            dy, dfb, dh, dg, db = _ffn_bwd_dx(dy, sv["r1"], row(ln1_g, l), full["ffn1_w_down", l], sv["gu1"],
                                              full["ffn1_w_gu", l])
            token = send_grads(("ffn1_w_down", "ffn1_w_gu"), l,
                               [_wgrad(sv["a1"], dfb, F // 2), _wgrad(dh, sv["x0b"], F // 2)])
        else:
            dr, dfb, dh, dg, db = _ffn_bwd(dy, sv["r1"], row(ln1_g, l), full["ffn1_w_down", l], sv["gu1"])
            token = send_grads(("ffn1_w_gu",), l, [_wgrad(dh, sv["x0b"], F // 2)])
            token = send_grads(("ffn1_w_down",), l, [_wgrad(sv["a1"], dfb, F // 2, token)])
            dy = _dx(dr, dh, full["ffn1_w_gu", l], token)
        small["ln1_g"][l], small["ln1_b"][l] = dg, db
    grad_x = dy[None]

    small_names = REPLICATED + CONVS
    small_shapes = [(DEPTH,) + tuple(np.shape(small[n][0].reshape(-1))) for n in small_names]
    small_pack = _pack([jnp.stack([small[n][l].reshape(-1) for l in range(DEPTH)]) for n in small_names])
    small_all = _all_gather([small_pack], "gather_small_grads")[0]

    recv = {n: [None] * DEPTH for n in BIG}
    grads, deltas, new_m, new_v = {}, {}, {}, {}

    def receive(upto, after):
        while len(sent) > upto:
            names, l, (send, rcv, srcs, lands, _) = sent.pop(0)
            got = _recv_wait(send, rcv, list(range(len(names))), srcs, lands, _block_of, after, f"grads_wait_{names[0]}_{l}")
            for n, g in zip(names, got):
                recv[n][l] = g

    def update(n):
        if n in SWAPPED:
            outs = _sum_adam(recv[n], *[jnp.swapaxes(a, 1, 2) for a in (w[n], mom[n], var[n])], False)
            grads[n], deltas[n], new_m[n], new_v[n] = [jnp.swapaxes(a, 1, 2) for a in outs]
        else:
            grads[n], deltas[n], new_m[n], new_v[n] = _sum_adam(recv[n], w[n], mom[n], var[n], BIG_T[n])

    receive(2, dy)
    for n in ("ffn2_w_down", "ffn2_w_gu", "w_out", "w_in"):
        update(n)
    receive(0, new_v["w_in"])
    update("ffn1_w_gu")
    update("ffn1_w_down")
    small_total = _unpack(_small_sum(small_all), small_shapes)
    for n, g in zip(small_names, small_total):
        if n in CONVS:
            taps = SC_W if n == "sc_conv_w" else CC_W
            g = lax.dynamic_slice_in_dim(g.reshape(DEPTH, taps, D_SC), idx * 32, 32, axis=2)
        grads[n] = g.reshape(w[n].shape)
    wp = _pack([w[n] for n in small_names])
    gp = _pack([grads[n] for n in small_names])
    mp = _pack([mom[n] for n in small_names])
    vp = _pack([var[n] for n in small_names])
    shapes = [w[n].shape for n in small_names]
    for dst, packed in zip((deltas, new_m, new_v), _small_adam(wp, gp, mp, vp)):
        for n, a in zip(small_names, _unpack(packed, shapes)):
            dst[n] = a

    return (loss, grad_x, *[grads[n] for n in WEIGHTS], *[deltas[n] for n in WEIGHTS],
            *[new_m[n] for n in WEIGHTS], *[new_v[n] for n in WEIGHTS])
```

```python
import functools

import jax
import jax.numpy as jnp
import numpy as np
from jax import lax
from jax.experimental import pallas as pl
from jax.experimental.pallas import tpu as pltpu

F32 = jnp.float32
BF16 = jnp.bfloat16
S = jax.ShapeDtypeStruct

N_DEV = 8
DEPTH = 2
D = 1024
F = 2816
D_IN = 2048
HEAD_DIM = 64
N_Q_HEADS = 8
N_KV_HEADS = 2
GROUP = 4
D_SC = 256
D_ATT = 512
D_CC = 256
CC_W = 31
SC_W = 3
BLOCK = 128
ROPE_THETA = 10000.0
LN_EPS = 1e-5
ALPHA = (2.0 * DEPTH) ** 0.25
ADAM_LR = 0.001
ADAM_B1 = 0.9
ADAM_B2 = 0.999
ADAM_EPS = 1e-08
ADAM_WD = 0.01
ADAM_STEP = 10

O_SCB, O_SCC, O_SCH, O_Q, O_K, O_V, O_CCA, O_CCG = 0, 256, 512, 768, 1280, 1408, 1536, 1792

V7X_VMEM_BYTES = 64 * 1024 * 1024
VMEM_LIMIT = V7X_VMEM_BYTES - 8 * 1024 * 1024
TOKEN_TILE = 256
WIDE_TILE = 512
WGRAD_TOKENS = 2048
FFN_CHUNKS = (0, 768, 1536, 2176, 2816)
DX_COLS = 256
MIX_BWD_TILE = 128
HALO_FWD = 16
HALO_BWD = 16
ATT_Q_BLOCKS = 4
CONV_ROWS = 128
SUBLANES = 8

NT = (((1,), (1,)), ((), ()))
TN = (((0,), (0,)), ((), ()))
MESH = pl.DeviceIdType.MESH


def _params(sem=None):
    return pltpu.CompilerParams(dimension_semantics=sem, vmem_limit_bytes=VMEM_LIMIT)


def _sigmoid(v):
    return 1.0 / (1.0 + jnp.exp(-v))


def _ln_stats(r):
    mu = jnp.mean(r, axis=-1, keepdims=True)
    d = r - mu
    var = jnp.mean(d * d, axis=-1, keepdims=True)
    rstd = lax.rsqrt(var + LN_EPS)
    return d * rstd, rstd


def _ln_bwd(dn, xhat, rstd, gam):
    dxh = dn * gam
    return rstd * (dxh - jnp.mean(dxh, axis=-1, keepdims=True) - xhat * jnp.mean(dxh * xhat, axis=-1, keepdims=True))


def _swap_halves(v):
    n = v.shape[-1]
    lane = lax.broadcasted_iota(jnp.int32, v.shape, v.ndim - 1) % HEAD_DIM
    return jnp.where(lane < HEAD_DIM // 2, pltpu.roll(v, n - HEAD_DIM // 2, v.ndim - 1), pltpu.roll(v, HEAD_DIM // 2, v.ndim - 1))


def _wide(tab, n):
    return tab if n == 128 else jnp.concatenate([tab] * (n // 128), axis=1)


def _me():
    x, y, c = lax.axis_index("x"), lax.axis_index("y"), lax.axis_index("c")
    return x, y, c


def _peer(rel):
    x, y, c = _me()
    px = 1 - x if rel & 4 else x
    py = 1 - y if rel & 2 else y
    pc = 1 - c if rel & 1 else c
    return (px, py, pc), 4 * px + 2 * py + pc


def _exchange(srcs, dsts_shape, dst_index, src_of, dst_of, name):
    n = len(srcs)

    def body(*refs):
        ins = refs[:n]
        outs = [refs[n + dst_index[k]] for k in range(n)]
        send, recv, lsem = refs[n + len(dsts_shape):]
        x, y, c = _me()
        me = 4 * x + 2 * y + c
        local = [pltpu.make_async_copy(src_of(ins[k], k, me), dst_of(outs[k], k, me), lsem.at[k]) for k in range(n)]
        for cp in local:
            cp.start()
        sends, recvs = [], []
        for k in range(n):
            for rel in range(1, N_DEV):
                peer, pidx = _peer(rel)
                sends.append(pltpu.make_async_remote_copy(
                    src_ref=src_of(ins[k], k, pidx), dst_ref=dst_of(outs[k], k, me),
                    send_sem=send.at[k, rel - 1], recv_sem=recv.at[k, rel - 1], device_id=peer, device_id_type=MESH))
                recvs.append(pltpu.make_async_remote_copy(
                    src_ref=src_of(ins[k], k, pidx), dst_ref=dst_of(outs[k], k, pidx),
                    send_sem=send.at[k, rel - 1], recv_sem=recv.at[k, rel - 1], device_id=peer, device_id_type=MESH))
        for cp in sends:
            cp.start()
        for cp in recvs:
            cp.wait_recv()
        for cp in sends:
            cp.wait_send()
        for cp in local:
            cp.wait()

    hbm = pl.BlockSpec(memory_space=pltpu.HBM)
    return pl.pallas_call(
        body, name=name, in_specs=[hbm] * n, out_specs=[hbm] * len(dsts_shape), out_shape=dsts_shape,
        scratch_shapes=[pltpu.SemaphoreType.DMA((n, N_DEV - 1)), pltpu.SemaphoreType.DMA((n, N_DEV - 1)),
                        pltpu.SemaphoreType.DMA((n,))],
    )(*srcs)


def _all_gather(blocks, name):
    shapes = [S((N_DEV,) + b.shape, b.dtype) for b in blocks]
    return _exchange(blocks, shapes, list(range(len(blocks))), lambda ref, k, idx: ref, lambda ref, k, idx: ref.at[idx], name)


HBM_SPEC = pl.BlockSpec(memory_space=pltpu.HBM)
SEM_SPEC = pl.BlockSpec(memory_space=pltpu.SEMAPHORE)
ANY_SPEC = pl.BlockSpec(memory_space=pl.ANY)
EFFECT = pltpu.SideEffectType.DATAFLOW_SIDE_EFFECTING
N_PEERS = N_DEV - 1


def _own_slot(block):
    x, y, c = _me()
    return lax.dynamic_update_index_in_dim(lax.empty((N_DEV,) + block.shape, block.dtype), block, 4 * x + 2 * y + c, 0)


def _follow(body, n_in, in_specs, operands, after):
    if after is None:
        return body, list(in_specs), list(operands)

    def tail(*refs):
        return body(*refs[:n_in], *refs[n_in + 1:])

    return tail, list(in_specs) + [ANY_SPEC], list(operands) + [after]


def _send_start(srcs, lands, src_of, name, after):
    n = len(srcs)

    def body(*refs):
        ins, zones = refs[:n], refs[n:2 * n]
        send, recv = refs[2 * n + 1], refs[2 * n + 2]
        token = refs[-1]
        x, y, c = _me()
        me = 4 * x + 2 * y + c
        for k in range(n):
            for rel in range(1, N_DEV):
                peer, pidx = _peer(rel)
                pltpu.make_async_remote_copy(
                    src_ref=src_of(ins[k], pidx), dst_ref=zones[k].at[me],
                    send_sem=send.at[k * N_PEERS + rel - 1], recv_sem=recv.at[k * N_PEERS + rel - 1],
                    device_id=peer, device_id_type=MESH).start()
        token[...] = jnp.zeros_like(token)

    outs = pl.pallas_call(
        body, name=name,
        out_shape=(pltpu.SemaphoreType.DMA((n * N_PEERS,)), pltpu.SemaphoreType.DMA((n * N_PEERS,)),
                   *[pltpu.HBM(a.shape, a.dtype) for a in lands], S((8, 128), F32)),
        in_specs=[HBM_SPEC] * (2 * n) + [ANY_SPEC],
        out_specs=(SEM_SPEC, SEM_SPEC, *[HBM_SPEC] * n, pl.BlockSpec(memory_space=pltpu.VMEM)),
        input_output_aliases={n + i: 2 + i for i in range(n)},
        compiler_params=pltpu.CompilerParams(has_side_effects=EFFECT),
    )(*[pltpu.with_memory_space_constraint(a, pltpu.HBM) for a in list(srcs) + list(lands)], after)
    return outs[0], outs[1], list(srcs), list(outs[2:2 + n]), outs[-1]


def _recv_wait(send, recv, ks, srcs, lands, src_of, after, name):
    n = len(ks)
    after = after if isinstance(after, (tuple, list)) else (after,)

    def body(*refs):
        ins, zones = refs[:n], refs[n:2 * n]
        send_sems, recv_sems = refs[2 * n], refs[2 * n + 1]
        for j, k in enumerate(ks):
            for rel in range(1, N_DEV):
                peer, pidx = _peer(rel)
                cp = pltpu.make_async_remote_copy(
                    src_ref=src_of(ins[j], pidx), dst_ref=zones[j].at[pidx],
                    send_sem=send_sems.at[k * N_PEERS + rel - 1], recv_sem=recv_sems.at[k * N_PEERS + rel - 1],
                    device_id=peer, device_id_type=MESH)
                cp.wait_send()
                cp.wait_recv()

    outs = pl.pallas_call(
        body, name=name,
        out_shape=[pltpu.HBM(a.shape, a.dtype) for a in lands],
        in_specs=[HBM_SPEC] * (2 * n) + [SEM_SPEC, SEM_SPEC] + [ANY_SPEC] * len(after), out_specs=[HBM_SPEC] * n,
        input_output_aliases={n + i: i for i in range(n)},
        compiler_params=pltpu.CompilerParams(has_side_effects=EFFECT),
    )(*srcs, *lands, send, recv, *after)
    return list(outs)


def _whole(ref, idx):
    return ref


def _block_of(ref, idx):
    return ref.at[idx]


def _resident(shape):
    return pl.BlockSpec(shape, lambda i: (0,) * len(shape), pipeline_mode=pl.Buffered(1))


def _ffn_up(xb, wgut, after=None):
    t = xb.shape[0]
    tm = min(WIDE_TILE, t)
    half = F // 2

    def body(x_ref, w_ref, gu_ref, a_ref):
        x = x_ref[...]
        for ch in range(2):
            lo = ch * half
            g = lax.dot_general(x, w_ref[lo:lo + half, :], NT, preferred_element_type=F32)
            u = lax.dot_general(x, w_ref[F + lo:F + lo + half, :], NT, preferred_element_type=F32)
            gu_ref[:, lo:lo + half] = g.astype(BF16)
            gu_ref[:, F + lo:F + lo + half] = u.astype(BF16)
            a_ref[:, lo:lo + half] = (g * _sigmoid(g) * u).astype(BF16)

    body, in_specs, operands = _follow(
        body, 2, [pl.BlockSpec((tm, D), lambda i: (i, 0)), _resident((2 * F, D))], [xb, wgut], after)
    return pl.pallas_call(
        body, name="ffn_up", grid=(t // tm,), in_specs=in_specs,
        out_specs=[pl.BlockSpec((tm, 2 * F), lambda i: (i, 0)), pl.BlockSpec((tm, F), lambda i: (i, 0))],
        out_shape=[S((t, 2 * F), BF16), S((t, F), BF16)], compiler_params=_params(("parallel",)),
    )(*operands)


def _residual_ln_out(x_ref, f, scale, g_ref, b_ref, r_ref, y_ref, yb_ref):
    r = ALPHA * x_ref[...] + scale * f
    xhat, _ = _ln_stats(r)
    y = xhat * g_ref[...] + b_ref[...]
    r_ref[...] = r
    y_ref[...] = y
    yb_ref[...] = y.astype(BF16)


def _ffn_down_ln(a, wd, x, gam, bet, after=None):
    t = x.shape[0]
    tm = min(WIDE_TILE, t)

    def body(a_ref, w_ref, x_ref, g_ref, b_ref, r_ref, y_ref, yb_ref):
        f = jnp.dot(a_ref[...], w_ref[...], preferred_element_type=F32)
        _residual_ln_out(x_ref, f, 0.5, g_ref, b_ref, r_ref, y_ref, yb_ref)

    row = pl.BlockSpec((tm, D), lambda i: (i, 0))
    vec = pl.BlockSpec((1, D), lambda i: (0, 0))
    body, in_specs, operands = _follow(
        body, 5, [pl.BlockSpec((tm, F), lambda i: (i, 0)), _resident((F, D)), row, vec, vec],
        [a, wd, x, gam, bet], after)
    return pl.pallas_call(
        body, name="ffn_down_ln", grid=(t // tm,), in_specs=in_specs,
        out_specs=[row, row, row], out_shape=[S((t, D), F32), S((t, D), F32), S((t, D), BF16)],
        compiler_params=_params(("parallel",)),
    )(*operands)


def _ffn_down_ln_loss(a, wd, x, gam, bet, target):
    t = x.shape[0]
    tm = min(WIDE_TILE, t)

    def body(a_ref, w_ref, x_ref, g_ref, b_ref, t_ref, r_ref, dy_ref, part_ref):
        f = jnp.dot(a_ref[...], w_ref[...], preferred_element_type=F32)
        r = ALPHA * x_ref[...] + 0.5 * f
        r_ref[...] = r
        xhat, _ = _ln_stats(r)
        e = xhat * g_ref[...] + b_ref[...] - t_ref[...]
        dy_ref[...] = e / D

        @pl.when(pl.program_id(0) == 0)
        def _():
            part_ref[...] = jnp.zeros_like(part_ref)

        part_ref[...] += jnp.sum(e * e, axis=0, keepdims=True)

    row = pl.BlockSpec((tm, D), lambda i: (i, 0))
    vec = pl.BlockSpec((1, D), lambda i: (0, 0))
    return pl.pallas_call(
        body, name="ffn_down_ln_loss", grid=(t // tm,),
        in_specs=[pl.BlockSpec((tm, F), lambda i: (i, 0)), _resident((F, D)), row, vec, vec, row],
        out_specs=[row, row, vec], out_shape=[S((t, D), F32), S((t, D), F32), S((1, D), F32)],
        compiler_params=_params(("arbitrary",)),
    )(a, wd, x, gam, bet, target)


def _proj_in(xb, wint):
    t = xb.shape[0]
    tm = min(WIDE_TILE, t)

    def body(x_ref, w_ref, z_ref):
        z_ref[...] = lax.dot_general(x_ref[...], w_ref[...], NT, preferred_element_type=F32)

    return pl.pallas_call(
        body, name="proj_in", grid=(t // tm,),
        in_specs=[pl.BlockSpec((tm, D), lambda i: (i, 0)), _resident((D_IN, D))],
        out_specs=pl.BlockSpec((tm, D_IN), lambda i: (i, 0)), out_shape=S((t, D_IN), F32),
        compiler_params=_params(("parallel",)),
    )(xb, wint)


def _halo_specs(t, tm, halo, width):
    per = tm // halo
    last = t // halo - 1
    return [pl.BlockSpec((tm, width), lambda i: (i, 0)),
            pl.BlockSpec((halo, width), lambda i: (jnp.maximum(i * per - 1, 0), 0)),
            pl.BlockSpec((halo, width), lambda i: (jnp.minimum((i + 1) * per, last), 0))]


def _taps_aligned(w_ref, x_ref, p_ref, offsets, rows):
    for r in range(SUBLANES):
        acc = jnp.zeros((rows + SUBLANES, x_ref.shape[1]), F32)
        for j, o in enumerate(offsets):
            if o % SUBLANES == r:
                acc = acc + w_ref[j:j + 1, :] * x_ref[pl.ds(o - r, rows + SUBLANES), :]
        p_ref[r] = acc
    out = p_ref[0, 0:rows, :]
    for r in range(1, SUBLANES):
        out = out + p_ref[r, pl.ds(r, rows), :]
    return out


def _mix_fwd(z, scw, ccw, ccb, ccg, ccbb, cos, sin):
    t = z.shape[0]
    tm = min(TOKEN_TILE, t)
    nt = t // tm
    h = HALO_FWD
    rc = min(CONV_ROWS, tm)

    def body(z_ref, zp_ref, zn_ref, scw_ref, ccw_ref, ccb_ref, ccg_ref, ccbb_ref, cos_ref, sin_ref,
             ysc_ref, ycc_ref, c_ref, q_ref, k_ref, v_ref, u_s, ch_s, p_s):
        i = pl.program_id(0)
        pz = jnp.where(i == 0, 0.0, zp_ref[...])
        nz = jnp.where(i == nt - 1, 0.0, zn_ref[...])

        def u_of(zz):
            return zz[:, O_CCA:O_CCA + D_CC] * _sigmoid(zz[:, O_CCG:O_CCG + D_CC])

        def ch_of(zz):
            return zz[:, O_SCC:O_SCC + D_SC] * zz[:, O_SCH:O_SCH + D_SC]

        u_s[0:h, :] = u_of(pz)
        u_s[h:h + tm, :] = z_ref[:, O_CCA:O_CCA + D_CC] * _sigmoid(z_ref[:, O_CCG:O_CCG + D_CC])
        u_s[h + tm:2 * h + tm, :] = u_of(nz)
        ch_s[0:h, :] = ch_of(pz)
        ch_s[h:h + tm, :] = z_ref[:, O_SCC:O_SCC + D_SC] * z_ref[:, O_SCH:O_SCH + D_SC]
        ch_s[h + tm:2 * h + tm, :] = ch_of(nz)
        for r0 in range(0, tm, rc):
            c = _taps_aligned(ccw_ref, u_s, p_s, [r0 + h + j - CC_W // 2 for j in range(CC_W)], rc) + ccb_ref[...]
            c_ref[r0:r0 + rc, :] = c
            xhat, _ = _ln_stats(c)
            n = xhat * ccg_ref[...] + ccbb_ref[...]
            ycc_ref[r0:r0 + rc, :] = (n * _sigmoid(n)).astype(BF16)
            acc = jnp.zeros((rc, D_SC), F32)
            for j in range(SC_W):
                acc = acc + scw_ref[j:j + 1, :] * ch_s[pl.ds(r0 + h + j - SC_W // 2, rc), :]
            ysc_ref[r0:r0 + rc, :] = (z_ref[r0:r0 + rc, O_SCB:O_SCB + D_SC] * acc).astype(BF16)
        q = z_ref[:, O_Q:O_Q + D_ATT]
        q_ref[...] = ((q * _wide(cos_ref[...], D_ATT) + _swap_halves(q) * _wide(sin_ref[...], D_ATT)) * (HEAD_DIM ** -0.5)).astype(BF16)
        k = z_ref[:, O_K:O_K + 128]
        k_ref[...] = (k * cos_ref[...] + _swap_halves(k) * sin_ref[...]).astype(BF16)
        v_ref[...] = z_ref[:, O_V:O_V + 128].astype(BF16)

    def full(a):
        return pl.BlockSpec(a.shape, lambda i: (0, 0))

    def rows(w):
        return pl.BlockSpec((tm, w), lambda i: (i, 0))

    return pl.pallas_call(
        body, name="mix_fwd", grid=(nt,),
        in_specs=_halo_specs(t, tm, h, D_IN) + [full(scw), full(ccw), full(ccb), full(ccg), full(ccbb), rows(128), rows(128)],
        out_specs=[rows(D_SC), rows(D_CC), rows(D_CC), rows(D_ATT), rows(128), rows(128)],
        out_shape=[S((t, D_SC), BF16), S((t, D_CC), BF16), S((t, D_CC), F32), S((t, D_ATT), BF16), S((t, 128), BF16),
                   S((t, 128), BF16)],
        scratch_shapes=[pltpu.VMEM((tm + 2 * h, D_CC), F32), pltpu.VMEM((tm + 2 * h, D_SC), F32),
                        pltpu.VMEM((SUBLANES, rc + SUBLANES, D_CC), F32)],
        compiler_params=_params(("parallel",)),
    )(z, z, z, scw, ccw, ccb, ccg, ccbb, cos, sin)


def _band_specs(nb, width):
    steps = nb // ATT_Q_BLOCKS
    return [pl.BlockSpec((BLOCK, width), lambda n: (jnp.maximum(n * ATT_Q_BLOCKS - 1, 0), 0)),
            pl.BlockSpec((ATT_Q_BLOCKS * BLOCK, width), lambda n: (n, 0)),
            pl.BlockSpec((BLOCK, width), lambda n: (jnp.minimum((n + 1) * ATT_Q_BLOCKS, nb - 1), 0))]


def _band_bias(b, nb, bias_s):
    qpos = lax.broadcasted_iota(jnp.int32, (BLOCK, 3 * BLOCK), 0)
    col = lax.broadcasted_iota(jnp.int32, (BLOCK, 3 * BLOCK), 1)
    ok = jnp.abs(qpos - (col - BLOCK)) <= BLOCK
    ok = jnp.logical_and(ok, jnp.logical_or(col >= BLOCK, b > 0))
    ok = jnp.logical_and(ok, jnp.logical_or(col < 2 * BLOCK, b < nb - 1))
    bias_s[...] = jnp.where(ok, 0.0, -1e30)


def _band_cats(before_ref, own_ref, after_ref):
    pieces = [(before_ref, 0)] + [(own_ref, j * BLOCK) for j in range(ATT_Q_BLOCKS)] + [(after_ref, 0)]
    return [[jnp.concatenate([r[r0:r0 + BLOCK, kvh * HEAD_DIM:(kvh + 1) * HEAD_DIM] for r, r0 in pieces[sub:sub + 3]], axis=0)
             for kvh in range(N_KV_HEADS)] for sub in range(ATT_Q_BLOCKS)]


def _head_scores(q_ref, kc, sub, h, bias_s):
    qh = q_ref[sub * BLOCK:(sub + 1) * BLOCK, h * HEAD_DIM:(h + 1) * HEAD_DIM]
    return qh, lax.dot_general(qh, kc, NT, preferred_element_type=F32) + bias_s[sub]


def _softmax_parts(s, sk):
    m = jnp.maximum(jnp.max(s, axis=-1, keepdims=True), sk)
    p = jnp.exp(s - m)
    ps = jnp.exp(sk - m)
    return p, ps, jnp.sum(p, axis=-1, keepdims=True) + ps


def _attn_fwd(qr, kr, vv, sink, after=None):
    t = qr.shape[0]
    nb = t // BLOCK
    units = [(sub, h) for sub in range(ATT_Q_BLOCKS) for h in range(N_Q_HEADS)]

    def body(q_ref, kp_ref, ko_ref, kn_ref, vp_ref, vo_ref, vn_ref, sink_ref, o_ref, bias_s):
        n = pl.program_id(0)
        for sub in range(ATT_Q_BLOCKS):
            _band_bias(n * ATT_Q_BLOCKS + sub, nb, bias_s.at[sub])
        kcs = _band_cats(kp_ref, ko_ref, kn_ref)
        vcs = _band_cats(vp_ref, vo_ref, vn_ref)

        def scores(u):
            sub, h = units[u]
            return _head_scores(q_ref, kcs[sub][h // GROUP], sub, h, bias_s)[1]

        s_next = scores(0)
        for u, (sub, h) in enumerate(units):
            s = s_next
            if u + 1 < len(units):
                s_next = scores(u + 1)
            p, _, denom = _softmax_parts(s, sink_ref[h])
            o = jnp.dot(p.astype(BF16), vcs[sub][h // GROUP], preferred_element_type=F32) / denom
            o_ref[sub * BLOCK:(sub + 1) * BLOCK, h * HEAD_DIM:(h + 1) * HEAD_DIM] = o.astype(BF16)

    qspec = pl.BlockSpec((ATT_Q_BLOCKS * BLOCK, D_ATT), lambda n: (n, 0))
    body, in_specs, operands = _follow(
        body, 8, [qspec] + _band_specs(nb, 128) + _band_specs(nb, 128) + [pl.BlockSpec(memory_space=pltpu.SMEM)],
        [qr, kr, kr, kr, vv, vv, vv, sink], after)
    return pl.pallas_call(
        body, name="attn_fwd", grid=(nb // ATT_Q_BLOCKS,), in_specs=in_specs,
        out_specs=qspec, out_shape=S((t, D_ATT), BF16),
        scratch_shapes=[pltpu.VMEM((ATT_Q_BLOCKS, BLOCK, 3 * BLOCK), F32)],
        compiler_params=_params(("parallel",)),
    )(*operands)


def _out_ln(ysc, yatt, ycc, wout, x, gam, bet):
    t = x.shape[0]
    tm = min(WIDE_TILE, t)

    def body(sc_ref, at_ref, cc_ref, w_ref, x_ref, g_ref, b_ref, cat_ref, r_ref, y_ref, yb_ref):
        cat = jnp.concatenate([sc_ref[...], at_ref[...], cc_ref[...]], axis=1)
        cat_ref[...] = cat
        f = jnp.dot(cat, w_ref[...], preferred_element_type=F32)
        _residual_ln_out(x_ref, f, 1.0, g_ref, b_ref, r_ref, y_ref, yb_ref)

    def rows(w):
        return pl.BlockSpec((tm, w), lambda i: (i, 0))

    vec = pl.BlockSpec((1, D), lambda i: (0, 0))
    return pl.pallas_call(
        body, name="out_ln", grid=(t // tm,),
        in_specs=[rows(D_SC), rows(D_ATT), rows(D_CC), _resident((D, D)), rows(D), vec, vec],
        out_specs=[rows(D), rows(D), rows(D), rows(D)],
        out_shape=[S((t, D), BF16), S((t, D), F32), S((t, D), F32), S((t, D), BF16)],
        compiler_params=_params(("parallel",)),
    )(ysc, yatt, ycc, wout, x, gam, bet)


def _ln_bwd_block(dy_ref, r_ref, g_ref, dgam_ref, dbet_ref):
    xhat, rstd = _ln_stats(r_ref[...])
    dy = dy_ref[...]

    @pl.when(pl.program_id(0) == 0)
    def _():
        dgam_ref[...] = jnp.zeros_like(dgam_ref)
        dbet_ref[...] = jnp.zeros_like(dbet_ref)

    dgam_ref[...] += jnp.sum(dy * xhat, axis=0, keepdims=True)
    dbet_ref[...] += jnp.sum(dy, axis=0, keepdims=True)
    return _ln_bwd(dy, xhat, rstd, g_ref[...])


def _ffn_bwd(dy, r, gam, wd, gu):
    t = dy.shape[0]
    tm = min(TOKEN_TILE, t)
    chunks = list(zip(FFN_CHUNKS[:-1], FFN_CHUNKS[1:]))

    def body(dy_ref, r_ref, g_ref, w_ref, gu_ref, dr_ref, df_ref, dh_ref, dgam_ref, dbet_ref):
        dr = _ln_bwd_block(dy_ref, r_ref, g_ref, dgam_ref, dbet_ref)
        dr_ref[...] = dr
        dfb = (0.5 * dr).astype(BF16)
        df_ref[...] = dfb
        for lo, hi in chunks:
            da = lax.dot_general(dfb, w_ref[lo:hi, :], NT, preferred_element_type=F32)
            g = gu_ref[:, lo:hi].astype(F32)
            u = gu_ref[:, F + lo:F + hi].astype(F32)
            sg = _sigmoid(g)
            dh_ref[:, lo:hi] = (da * u * (sg * (1.0 + g * (1.0 - sg)))).astype(BF16)
            dh_ref[:, F + lo:F + hi] = (da * (g * sg)).astype(BF16)

    row = pl.BlockSpec((tm, D), lambda i: (i, 0))
    vec = pl.BlockSpec((1, D), lambda i: (0, 0))
    wide = pl.BlockSpec((tm, 2 * F), lambda i: (i, 0))
    return pl.pallas_call(
        body, name="ffn_bwd", grid=(t // tm,),
        in_specs=[row, row, vec, _resident((F, D)), wide],
        out_specs=[row, row, wide, vec, vec],
        out_shape=[S((t, D), F32), S((t, D), BF16), S((t, 2 * F), BF16), S((1, D), F32), S((1, D), F32)],
        compiler_params=_params(("arbitrary",)),
    )(dy, r, gam, wd, gu)


def _ffn_bwd_dx(dy, r, gam, wd, gu, wgut, after=None):
    t = dy.shape[0]
    tm = min(TOKEN_TILE, t)
    n = t // tm
    chunks = list(zip(FFN_CHUNKS[:-1], FFN_CHUNKS[1:]))

    def body(dy_ref, r_ref, g_ref, w_ref, gu_ref, wg_ref, dx_ref, df_ref, dh_ref, dgam_ref, dbet_ref,
             keep_a, keep_b, dr_keep):
        i = pl.program_id(0)

        @pl.when(i == 0)
        def _():
            keep_a[...] = jnp.zeros_like(keep_a)
            keep_b[...] = jnp.zeros_like(keep_b)
            dr_keep[...] = jnp.zeros_like(dr_keep)
            dgam_ref[...] = jnp.zeros_like(dgam_ref)
            dbet_ref[...] = jnp.zeros_like(dbet_ref)

        def step(prev, cur):
            def to_dx(c):
                cols = slice(c * DX_COLS, (c + 1) * DX_COLS)
                dx_ref[:, cols] = ALPHA * dr_keep[:, cols] + jnp.dot(prev[...], wg_ref[:, cols], preferred_element_type=F32)

            to_dx(0)
            xhat, rstd = _ln_stats(r_ref[...])
            dy_t = dy_ref[...]
            live = jnp.where(i < n, 1.0, 0.0)
            dgam_ref[...] += live * jnp.sum(dy_t * xhat, axis=0, keepdims=True)
            dbet_ref[...] += live * jnp.sum(dy_t, axis=0, keepdims=True)
            dr = _ln_bwd(dy_t, xhat, rstd, g_ref[...])
            dfb = (0.5 * dr).astype(BF16)
            df_ref[...] = dfb

            def down(c):
                return lax.dot_general(dfb, w_ref[chunks[c][0]:chunks[c][1], :], NT, preferred_element_type=F32)

            das = {0: down(0), 1: down(1)}
            for c, (lo, hi) in enumerate(chunks):
                if c + 2 < len(chunks):
                    das[c + 2] = down(c + 2)
                if c + 1 < D // DX_COLS:
                    to_dx(c + 1)
                da = das.pop(c)
                g = gu_ref[:, lo:hi].astype(F32)
                u = gu_ref[:, F + lo:F + hi].astype(F32)
                sg = _sigmoid(g)
                dg = (da * u * (sg * (1.0 + g * (1.0 - sg)))).astype(BF16)
                du = (da * (g * sg)).astype(BF16)
                dh_ref[:, lo:hi] = dg
                dh_ref[:, F + lo:F + hi] = du
                cur[:, lo:hi] = dg
                cur[:, F + lo:F + hi] = du
            dr_keep[...] = dr

        @pl.when(i % 2 == 0)
        def _():
            step(keep_b, keep_a)

        @pl.when(i % 2 == 1)
        def _():
            step(keep_a, keep_b)

    cur_row = lambda i: (jnp.minimum(i, n - 1), 0)
    row = pl.BlockSpec((tm, D), cur_row)
    vec = pl.BlockSpec((1, D), lambda i: (0, 0))
    wide = pl.BlockSpec((tm, 2 * F), cur_row)
    body, in_specs, operands = _follow(
        body, 6, [row, row, vec, _resident((F, D)), wide, _resident((2 * F, D))], [dy, r, gam, wd, gu, wgut], after)
    return pl.pallas_call(
        body, name="ffn_bwd_dx", grid=(n + 1,), in_specs=in_specs,
        out_specs=[pl.BlockSpec((tm, D), lambda i: (jnp.maximum(i - 1, 0), 0)), row, wide, vec, vec],
        out_shape=[S((t, D), F32), S((t, D), BF16), S((t, 2 * F), BF16), S((1, D), F32), S((1, D), F32)],
        scratch_shapes=[pltpu.VMEM((tm, 2 * F), BF16), pltpu.VMEM((tm, 2 * F), BF16), pltpu.VMEM((tm, D), F32)],
        compiler_params=_params(("arbitrary",)),
    )(*operands)


def _dx(dr, dh, w, after=None):
    t = dr.shape[0]
    tm = min(WIDE_TILE, t)
    kk = dh.shape[1]

    def body(dr_ref, dh_ref, w_ref, o_ref):
        o_ref[...] = ALPHA * dr_ref[...] + jnp.dot(dh_ref[...], w_ref[...], preferred_element_type=F32)

    row = pl.BlockSpec((tm, D), lambda i: (i, 0))
    body, in_specs, operands = _follow(
        body, 3, [row, pl.BlockSpec((tm, kk), lambda i: (i, 0)), _resident((kk, D))], [dr, dh, w], after)
    return pl.pallas_call(
        body, name="dx", grid=(t // tm,), in_specs=in_specs,
        out_specs=row, out_shape=S((t, D), F32), compiler_params=_params(("parallel",)),
    )(*operands)


def _wgrad(a, b, ta, after=None):
    t, ka = a.shape
    tk = min(WGRAD_TOKENS, t)
    nk = t // tk

    def body(a_ref, b_ref, o_ref, acc):
        k = pl.program_id(1)

        @pl.when(k == 0)
        def _():
            acc[...] = jnp.zeros_like(acc)

        acc[...] += lax.dot_general(a_ref[...], b_ref[...], TN, preferred_element_type=F32)

        @pl.when(k == nk - 1)
        def _():
            o_ref[...] = acc[...].astype(BF16)

    body, in_specs, operands = _follow(
        body, 2, [pl.BlockSpec((tk, ta), lambda i, k: (k, i)), pl.BlockSpec((tk, D), lambda i, k: (k, 0))], [a, b], after)
    return pl.pallas_call(
        body, name="wgrad", grid=(ka // ta, nk), in_specs=in_specs,
        out_specs=pl.BlockSpec((ta, D), lambda i, k: (i, 0)), out_shape=S((ka, D), BF16),
        scratch_shapes=[pltpu.VMEM((ta, D), F32)], compiler_params=_params(("parallel", "arbitrary")),
    )(*operands)


def _out_bwd(dy, r, gam, wout, after=None):
    t = dy.shape[0]
    tm = min(TOKEN_TILE, t)

    def body(dy_ref, r_ref, g_ref, w_ref, dr_ref, dm_ref, dsc_ref, dat_ref, dcc_ref, dgam_ref, dbet_ref):
        dr = _ln_bwd_block(dy_ref, r_ref, g_ref, dgam_ref, dbet_ref)
        dr_ref[...] = dr
        dmb = dr.astype(BF16)
        dm_ref[...] = dmb
        dcat = lax.dot_general(dmb, w_ref[...], NT, preferred_element_type=F32)
        dsc_ref[...] = dcat[:, 0:D_SC]
        dat_ref[...] = dcat[:, D_SC:D_SC + D_ATT]
        dcc_ref[...] = dcat[:, D_SC + D_ATT:D]

    def rows(w):
        return pl.BlockSpec((tm, w), lambda i: (i, 0))

    vec = pl.BlockSpec((1, D), lambda i: (0, 0))
    body, in_specs, operands = _follow(body, 4, [rows(D), rows(D), vec, _resident((D, D))], [dy, r, gam, wout], after)
    return pl.pallas_call(
        body, name="out_bwd", grid=(t // tm,), in_specs=in_specs,
        out_specs=[rows(D), rows(D), rows(D_SC), rows(D_ATT), rows(D_CC), vec, vec],
        out_shape=[S((t, D), F32), S((t, D), BF16), S((t, D_SC), F32), S((t, D_ATT), F32), S((t, D_CC), F32),
                   S((1, D), F32), S((1, D), F32)],
        compiler_params=_params(("arbitrary",)),
    )(*operands)


def _attn_bwd(qr, kr, vv, sink, do, yatt):
    t = qr.shape[0]
    nb = t // BLOCK
    scale = HEAD_DIM ** -0.5
    units = [(sub, h) for sub in range(ATT_Q_BLOCKS) for h in range(N_Q_HEADS)]

    def body(q_ref, kp_ref, ko_ref, kn_ref, vp_ref, vo_ref, vn_ref, sink_ref, do_ref, o_ref,
             dq_ref, dk_ref, dv_ref, dsink_ref, bias_s, ds_s, p_s, q_s, dou_s):
        n = pl.program_id(0)
        for sub in range(ATT_Q_BLOCKS):
            _band_bias(n * ATT_Q_BLOCKS + sub, nb, bias_s.at[sub])

        @pl.when(n == 0)
        def _():
            dsink_ref[...] = jnp.zeros_like(dsink_ref)

        kcs = _band_cats(kp_ref, ko_ref, kn_ref)
        vcs = _band_cats(vp_ref, vo_ref, vn_ref)

        def scores(u):
            sub, h = units[u]
            return _head_scores(q_ref, kcs[sub][h // GROUP], sub, h, bias_s)

        def probs(u, qh, s):
            sub, h = units[u]
            rows = slice(sub * BLOCK, (sub + 1) * BLOCK)
            cols = slice(h * HEAD_DIM, (h + 1) * HEAD_DIM)
            p, ps, denom = _softmax_parts(s, sink_ref[h])
            doh = do_ref[rows, cols]
            dd = jnp.sum(doh * o_ref[rows, cols].astype(F32), axis=-1, keepdims=True) / denom
            dou = (doh / denom).astype(BF16)
            dp = lax.dot_general(dou, vcs[sub][h // GROUP], NT, preferred_element_type=F32)
            dsink_ref[h:h + 1, :] += jnp.zeros((1, 128), F32) - jnp.sum(ps * dd)
            return qh, p, dd, dou, dp

        def grads(u, qh, p, dd, dou, dp):
            sub, h = units[u]
            kvh, g = divmod(h, GROUP)
            rows = slice(sub * BLOCK, (sub + 1) * BLOCK)
            cols = slice(h * HEAD_DIM, (h + 1) * HEAD_DIM)
            stack = slice(g * BLOCK, (g + 1) * BLOCK)
            ds = (p * (dp - dd)).astype(BF16)
            dq_ref[rows, cols] = jnp.dot(ds, kcs[sub][kvh], preferred_element_type=F32) * scale
            ds_s[stack, :] = ds
            p_s[stack, :] = p.astype(BF16)
            q_s[stack, :] = qh
            dou_s[stack, :] = dou
            if g == GROUP - 1:
                dk = lax.dot_general(ds_s[...], q_s[...], TN, preferred_element_type=F32)
                dv = lax.dot_general(p_s[...], dou_s[...], TN, preferred_element_type=F32)
                for j in range(3):
                    dk_ref[j, rows, kvh * HEAD_DIM:(kvh + 1) * HEAD_DIM] = dk[j * BLOCK:(j + 1) * BLOCK, :]
                    dv_ref[j, rows, kvh * HEAD_DIM:(kvh + 1) * HEAD_DIM] = dv[j * BLOCK:(j + 1) * BLOCK, :]

        sc = {0: scores(0), 1: scores(1)}
        pr = {0: probs(0, *sc.pop(0))}
        for u in range(len(units)):
            if u + 2 < len(units):
                sc[u + 2] = scores(u + 2)
            if u + 1 < len(units):
                pr[u + 1] = probs(u + 1, *sc.pop(u + 1))
            grads(u, *pr.pop(u))

    qspec = pl.BlockSpec((ATT_Q_BLOCKS * BLOCK, D_ATT), lambda n: (n, 0))
    part = pl.BlockSpec((3, ATT_Q_BLOCKS * BLOCK, 128), lambda n: (0, n, 0))
    stacked = GROUP * BLOCK
    return pl.pallas_call(
        body, name="attn_bwd", grid=(nb // ATT_Q_BLOCKS,),
        in_specs=[qspec] + _band_specs(nb, 128) + _band_specs(nb, 128) + [pl.BlockSpec(memory_space=pltpu.SMEM), qspec, qspec],
        out_specs=[qspec, part, part, pl.BlockSpec((N_Q_HEADS, 128), lambda n: (0, 0))],
        out_shape=[S((t, D_ATT), F32), S((3, t, 128), F32), S((3, t, 128), F32), S((N_Q_HEADS, 128), F32)],
        scratch_shapes=[pltpu.VMEM((ATT_Q_BLOCKS, BLOCK, 3 * BLOCK), F32), pltpu.VMEM((stacked, 3 * BLOCK), BF16),
                        pltpu.VMEM((stacked, 3 * BLOCK), BF16), pltpu.VMEM((stacked, HEAD_DIM), BF16),
                        pltpu.VMEM((stacked, HEAD_DIM), BF16)],
        compiler_params=_params(("arbitrary",)),
    )(qr, kr, kr, kr, vv, vv, vv, sink, do, yatt)


def _tap_grads_aligned(d_own, x_ref, d_ref, offsets, out_ref):
    rows = d_own.shape[0]
    padded = jnp.concatenate([d_own, jnp.zeros((SUBLANES, d_own.shape[1]), F32)], axis=0)
    for r in range(SUBLANES):
        d_ref[r] = padded if r == 0 else pltpu.roll(padded, r, 0)
    for j, o in enumerate(offsets):
        r = o % SUBLANES
        out_ref[j:j + 1, :] += jnp.sum(d_ref[r] * x_ref[pl.ds(o - r, rows + SUBLANES), :], axis=0, keepdims=True)


def _mix_bwd(z, c, dysc, dycc, dqr, dkp, dvp, scw, ccw, ccb, ccg, ccbb, cos, sin):
    t = z.shape[0]
    tm = min(MIX_BWD_TILE, t)
    nt = t // tm
    h = HALO_BWD
    half = CC_W // 2
    ext = tm + 2 * h

    def body(z_ref, zp_ref, zn_ref, c_ref, cp_ref, cn_ref, dsc_ref, dscp_ref, dscn_ref, dcc_ref, dccp_ref, dccn_ref,
             dq_ref, dk0_ref, dk1_ref, dk2_ref, dv0_ref, dv1_ref, dv2_ref,
             scw_ref, ccw_ref, ccb_ref, ccg_ref, ccbb_ref, cos_ref, sin_ref,
             dz_ref, dscw_ref, dccw_ref, dvec_ref, u_s, dc_s, ch_s, g_s, p_s, d_s):
        i = pl.program_id(0)
        first, last = i == 0, i == nt - 1

        @pl.when(first)
        def _():
            dscw_ref[...] = jnp.zeros_like(dscw_ref)
            dccw_ref[...] = jnp.zeros_like(dccw_ref)
            dvec_ref[...] = jnp.zeros_like(dvec_ref)

        pz = jnp.where(first, 0.0, zp_ref[...])
        nz = jnp.where(last, 0.0, zn_ref[...])
        zo = z_ref[...]

        def u_of(zz):
            return zz[:, O_CCA:O_CCA + D_CC] * _sigmoid(zz[:, O_CCG:O_CCG + D_CC])

        u_s[0:h, :] = u_of(pz)
        u_s[h:h + tm, :] = u_of(zo)
        u_s[h + tm:ext, :] = u_of(nz)
        c_ext = jnp.concatenate([jnp.where(first, 0.0, cp_ref[...]), c_ref[...], jnp.where(last, 0.0, cn_ref[...])], axis=0)
        xhat, rstd = _ln_stats(c_ext)
        nn = xhat * ccg_ref[...] + ccbb_ref[...]
        sg = _sigmoid(nn)
        dycc_ext = jnp.concatenate([jnp.where(first, 0.0, dccp_ref[...]), dcc_ref[...],
                                    jnp.where(last, 0.0, dccn_ref[...])], axis=0)
        dn = dycc_ext * (sg * (1.0 + nn * (1.0 - sg)))
        dc = _ln_bwd(dn, xhat, rstd, ccg_ref[...])
        dc_s[...] = dc
        dn_own = dn[h:h + tm, :]
        dc_own = dc[h:h + tm, :]
        dvec_ref[0:1, :] += jnp.sum(dc_own, axis=0, keepdims=True)
        dvec_ref[1:2, :] += jnp.sum(dn_own * xhat[h:h + tm, :], axis=0, keepdims=True)
        dvec_ref[2:3, :] += jnp.sum(dn_own, axis=0, keepdims=True)
        du = _taps_aligned(ccw_ref, dc_s, p_s, [h + half - j for j in range(CC_W)], tm)
        _tap_grads_aligned(dc_own, u_s, d_s, [h + j - half for j in range(CC_W)], dccw_ref)
        gate = _sigmoid(zo[:, O_CCG:O_CCG + D_CC])
        a_own = zo[:, O_CCA:O_CCA + D_CC]
        dz_ref[:, O_CCA:O_CCA + D_CC] = (du * gate).astype(BF16)
        dz_ref[:, O_CCG:O_CCG + D_CC] = (du * a_own * gate * (1.0 - gate)).astype(BF16)

        def ch_of(zz):
            return zz[:, O_SCC:O_SCC + D_SC] * zz[:, O_SCH:O_SCH + D_SC]

        ch_s[0:h, :] = ch_of(pz)
        ch_s[h:h + tm, :] = ch_of(zo)
        ch_s[h + tm:ext, :] = ch_of(nz)
        g_s[0:h, :] = jnp.where(first, 0.0, dscp_ref[...]) * pz[:, O_SCB:O_SCB + D_SC]
        g_s[h:h + tm, :] = dsc_ref[...] * zo[:, O_SCB:O_SCB + D_SC]
        g_s[h + tm:ext, :] = jnp.where(last, 0.0, dscn_ref[...]) * nz[:, O_SCB:O_SCB + D_SC]
        conv = jnp.zeros((tm, D_SC), F32)
        dch = jnp.zeros((tm, D_SC), F32)
        g_own = g_s[h:h + tm, :]
        for j in range(SC_W):
            chj = ch_s[pl.ds(h + j - SC_W // 2, tm), :]
            conv = conv + scw_ref[j:j + 1, :] * chj
            dch = dch + scw_ref[j:j + 1, :] * g_s[pl.ds(h + SC_W // 2 - j, tm), :]
            dscw_ref[j:j + 1, :] += jnp.sum(g_own * chj, axis=0, keepdims=True)
        dz_ref[:, O_SCB:O_SCB + D_SC] = (dsc_ref[...] * conv).astype(BF16)
        dz_ref[:, O_SCC:O_SCC + D_SC] = (dch * zo[:, O_SCH:O_SCH + D_SC]).astype(BF16)
        dz_ref[:, O_SCH:O_SCH + D_SC] = (dch * zo[:, O_SCC:O_SCC + D_SC]).astype(BF16)

        dq = dq_ref[...]
        dz_ref[:, O_Q:O_Q + D_ATT] = (dq * _wide(cos_ref[...], D_ATT) + _swap_halves(dq * _wide(sin_ref[...], D_ATT))).astype(BF16)
        dk = dk1_ref[0] + jnp.where(last, 0.0, dk0_ref[0]) + jnp.where(first, 0.0, dk2_ref[0])
        dz_ref[:, O_K:O_K + 128] = (dk * cos_ref[...] + _swap_halves(dk * sin_ref[...])).astype(BF16)
        dv = dv1_ref[0] + jnp.where(last, 0.0, dv0_ref[0]) + jnp.where(first, 0.0, dv2_ref[0])
        dz_ref[:, O_V:O_V + 128] = dv.astype(BF16)

    def full(a):
        return pl.BlockSpec(a.shape, lambda i: (0, 0))

    def rows(w):
        return pl.BlockSpec((tm, w), lambda i: (i, 0))

    parts = [pl.BlockSpec((1, tm, 128), lambda i: (0, jnp.minimum(i + 1, nt - 1), 0)),
             pl.BlockSpec((1, tm, 128), lambda i: (1, i, 0)),
             pl.BlockSpec((1, tm, 128), lambda i: (2, jnp.maximum(i - 1, 0), 0))]
    acc_spec = lambda r: pl.BlockSpec((r, D_CC), lambda i: (0, 0))
    return pl.pallas_call(
        body, name="mix_bwd", grid=(nt,),
        in_specs=(_halo_specs(t, tm, h, D_IN) + _halo_specs(t, tm, h, D_CC) + _halo_specs(t, tm, h, D_SC)
                  + _halo_specs(t, tm, h, D_CC) + [rows(D_ATT)] + parts + parts
                  + [full(scw), full(ccw), full(ccb), full(ccg), full(ccbb), rows(128), rows(128)]),
        out_specs=[rows(D_IN), acc_spec(SC_W), acc_spec(CC_W), acc_spec(3)],
        out_shape=[S((t, D_IN), BF16), S((SC_W, D_SC), F32), S((CC_W, D_CC), F32), S((3, D_CC), F32)],
        scratch_shapes=[pltpu.VMEM((ext, D_CC), F32), pltpu.VMEM((ext, D_CC), F32),
                        pltpu.VMEM((ext, D_SC), F32), pltpu.VMEM((ext, D_SC), F32),
                        pltpu.VMEM((SUBLANES, tm + SUBLANES, D_CC), F32), pltpu.VMEM((SUBLANES, tm + SUBLANES, D_CC), F32)],
        compiler_params=_params(("arbitrary",)),
    )(z, z, z, c, c, c, dysc, dysc, dysc, dycc, dycc, dycc, dqr, dkp, dkp, dkp, dvp, dvp, dvp,
      scw, ccw, ccb, ccg, ccbb, cos, sin)


def _adamw(w, g, m, v):
    m = ADAM_B1 * m + (1.0 - ADAM_B1) * g
    v = ADAM_B2 * v + (1.0 - ADAM_B2) * (g * g)
    m_hat = m / (1.0 - ADAM_B1 ** ADAM_STEP)
    v_hat = v / (1.0 - ADAM_B2 ** ADAM_STEP)
    delta = -ADAM_LR * (m_hat / (jnp.sqrt(v_hat) + ADAM_EPS) + ADAM_WD * w)
    return delta, m, v


def _row_tile(rows):
    for cand in (256, 176, 128):
        if rows % cand == 0:
            return cand
    return rows


def _sum_adam(recv, w, m, v, transposed):
    nl, rows = len(recv), recv[0].shape[1]
    tile = 256 if transposed else _row_tile(rows)
    nc = (D if transposed else rows) // tile

    def body(*refs):
        w_ref, m_ref, v_ref, g_ref, d_ref, mo_ref, vo_ref = refs[nl:]
        for layer in range(nl):
            @pl.when(pl.program_id(0) == layer)
            def _(r_ref=refs[layer]):
                g = r_ref[0].astype(F32)
                for s in range(1, N_DEV):
                    g = g + r_ref[s].astype(F32)
                if transposed:
                    g = g.T
                g_ref[0] = g
                d_ref[0], mo_ref[0], vo_ref[0] = _adamw(w_ref[0], g, m_ref[0], v_ref[0])

    def held(layer):
        def at(l, c):
            return jnp.where(l == layer, c, jnp.where(l < layer, 0, nc - 1))
        if transposed:
            return pl.BlockSpec((N_DEV, rows, tile), lambda l, c: (0, 0, at(l, c)))
        return pl.BlockSpec((N_DEV, tile, D), lambda l, c: (0, at(l, c), 0))

    if transposed:
        blk = pl.BlockSpec((1, tile, rows), lambda l, c: (l, c, 0))
    else:
        blk = pl.BlockSpec((1, tile, D), lambda l, c: (l, c, 0))
    out = S(w.shape, F32)
    return pl.pallas_call(
        body, name="sum_adam_t" if transposed else "sum_adam", grid=(nl, nc),
        in_specs=[held(layer) for layer in range(nl)] + [blk, blk, blk], out_specs=[blk] * 4, out_shape=[out] * 4,
        compiler_params=_params(("arbitrary", "arbitrary")),
    )(*recv, w, m, v)


def _small_sum(gathered):
    rows = gathered.shape[1]

    def body(g_ref, o_ref):
        acc = g_ref[0]
        for s in range(1, N_DEV):
            acc = acc + g_ref[s]
        o_ref[...] = acc

    return pl.pallas_call(
        body, name="small_sum", in_specs=[pl.BlockSpec(gathered.shape, lambda: (0, 0, 0))],
        out_specs=pl.BlockSpec((rows, 128), lambda: (0, 0)), out_shape=S((rows, 128), F32),
    )(gathered)


def _small_adam(w, g, m, v):
    def body(w_ref, g_ref, m_ref, v_ref, d_ref, mo_ref, vo_ref):
        d_ref[...], mo_ref[...], vo_ref[...] = _adamw(w_ref[...], g_ref[...], m_ref[...], v_ref[...])

    spec = pl.BlockSpec(w.shape, lambda: (0, 0))
    return pl.pallas_call(
        body, name="small_adam", in_specs=[spec] * 4, out_specs=[spec] * 3, out_shape=[S(w.shape, F32)] * 3,
    )(w, g, m, v)


def _pack(pieces):
    flat = jnp.concatenate([p.reshape(-1).astype(F32) for p in pieces])
    n = flat.shape[0]
    rows = -(-n // 1024) * 8
    return jnp.pad(flat, (0, rows * 128 - n)).reshape(rows, 128)


def _unpack(packed, shapes):
    flat = packed.reshape(-1)
    out, o = [], 0
    for shp in shapes:
        n = int(np.prod(shp))
        out.append(flat[o:o + n].reshape(shp))
        o += n
    return out


def _rope_tables(t):
    half = HEAD_DIM // 2
    inv_freq = ROPE_THETA ** (-jnp.arange(half, dtype=F32) / half)
    ang = jnp.arange(t).astype(F32)[:, None] * jnp.tile(inv_freq, 128 // half)[None, :]
    sign = jnp.tile(jnp.concatenate([-jnp.ones((half,), F32), jnp.ones((half,), F32)]), 128 // HEAD_DIM)
    return jnp.cos(ang), jnp.sin(ang) * sign[None, :]


BIG = ("ffn1_w_gu", "ffn1_w_down", "w_in", "w_out", "ffn2_w_gu", "ffn2_w_down")
BIG_T = {"ffn1_w_gu": True, "ffn1_w_down": False, "w_in": True, "w_out": False, "ffn2_w_gu": True, "ffn2_w_down": False}
SWAPPED = ("ffn1_w_gu", "ffn2_w_gu")
REPLICATED = ("ln1_g", "ln1_b", "attn_sink", "cc_conv_b", "cc_ln_g", "cc_ln_b", "ln2_g", "ln2_b", "ln3_g", "ln3_b")
CONVS = ("sc_conv_w", "cc_conv_w")
WEIGHTS = ("ffn1_w_gu", "ffn1_w_down", "ln1_g", "ln1_b", "w_in", "sc_conv_w", "attn_sink", "cc_conv_w", "cc_conv_b",
           "cc_ln_g", "cc_ln_b", "w_out", "ln2_g", "ln2_b", "ffn2_w_gu", "ffn2_w_down", "ln3_g", "ln3_b")


def kernel(x, ffn1_w_gu, ffn1_w_down, ln1_g, ln1_b, w_in, sc_conv_w, attn_sink, cc_conv_w, cc_conv_b, cc_ln_g, cc_ln_b, w_out, ln2_g, ln2_b, ffn2_w_gu, ffn2_w_down, ln3_g, ln3_b, loss_target, m_ffn1_w_gu, m_ffn1_w_down, m_ln1_g, m_ln1_b, m_w_in, m_sc_conv_w, m_attn_sink, m_cc_conv_w, m_cc_conv_b, m_cc_ln_g, m_cc_ln_b, m_w_out, m_ln2_g, m_ln2_b, m_ffn2_w_gu, m_ffn2_w_down, m_ln3_g, m_ln3_b, v_ffn1_w_gu, v_ffn1_w_down, v_ln1_g, v_ln1_b, v_w_in, v_sc_conv_w, v_attn_sink, v_cc_conv_w, v_cc_conv_b, v_cc_ln_g, v_cc_ln_b, v_w_out, v_ln2_g, v_ln2_b, v_ffn2_w_gu, v_ffn2_w_down, v_ln3_g, v_ln3_b):
    args = dict(locals())
    w = {n: args[n] for n in WEIGHTS}
    mom = {n: args["m_" + n] for n in WEIGHTS}
    var = {n: args["v_" + n] for n in WEIGHTS}
    x0 = x[0]
    target = loss_target[0]
    t = x0.shape[0]
    idx = 4 * lax.axis_index("x") + 2 * lax.axis_index("y") + lax.axis_index("c")

    blocks = {(n, l): (w[n][l].T if BIG_T[n] else w[n][l]).astype(BF16) for l in range(DEPTH) for n in BIG}
    where = {}

    def start_stage(tag, members, after, extra=()):
        srcs = list(extra) + [blocks[m] for m in members]
        started = _send_start(srcs, [_own_slot(s) for s in srcs], _whole, f"gather_start_{tag}", after)
        for j, m in enumerate(members):
            where[m] = (started, len(extra) + j)
        return started

    def wait_stage(started, k, after, name):
        send, rcv, srcs, lands, _ = started
        return _recv_wait(send, rcv, [k], [srcs[k]], [lands[k]], _whole, after, name)[0]

    def weight(n, l, after):
        g = wait_stage(*where[n, l], after, f"gather_wait_{n}_{l}")
        return g.reshape(N_DEV * g.shape[1], g.shape[2])

    first = start_stage("a", [("ffn1_w_gu", 0)], x0, extra=[_pack([w["sc_conv_w"], w["cc_conv_w"]])])
    xf, xb = x0, x0.astype(BF16)
    cos, sin = _rope_tables(t)
    conv_all = wait_stage(first, 0, (xb, cos, sin), "gather_wait_convs").reshape(N_DEV, -1)
    n_sc = DEPTH * SC_W * 32
    scw_full = conv_all[:, :n_sc].reshape(N_DEV, DEPTH, SC_W, 32).transpose(1, 2, 0, 3).reshape(DEPTH, SC_W, D_SC)
    ccw_full = conv_all[:, n_sc:n_sc + DEPTH * CC_W * 32].reshape(N_DEV, DEPTH, CC_W, 32).transpose(1, 2, 0, 3).reshape(DEPTH, CC_W, D_CC)

    row = lambda a, l: a[l].reshape(1, -1)

    saved, full = [], {}
    for l in range(DEPTH):
        sv = {"x0b": xb}
        token = None
        full["ffn1_w_gu", l] = weight("ffn1_w_gu", l, (xb, scw_full, ccw_full) if l == 0 else xb)
        if l == 0:
            token = start_stage("b", [("ffn1_w_down", 0), ("w_in", 0), ("w_out", 0)], full["ffn1_w_gu", l])[-1]
        gu1, a1 = _ffn_up(xb, full["ffn1_w_gu", l], token)
        full["ffn1_w_down", l] = weight("ffn1_w_down", l, a1)
        if l == 0:
            token = start_stage("c", [("ffn2_w_gu", 0), ("ffn2_w_down", 0)], full["ffn1_w_down", l])[-1]
        r1, x1, x1b = _ffn_down_ln(a1, full["ffn1_w_down", l], xf, row(ln1_g, l), row(ln1_b, l), token)
        full["w_in", l] = weight("w_in", l, x1b)
        z = _proj_in(x1b, full["w_in", l])
        ysc, ycc, cpre, qr, kr, vv = _mix_fwd(z, scw_full[l], ccw_full[l], row(cc_conv_b, l), row(cc_ln_g, l), row(cc_ln_b, l), cos, sin)
        if l == 0:
            token = start_stage("d", [("ffn1_w_gu", 1), ("ffn1_w_down", 1)], ysc)[-1]
        yatt = _attn_fwd(qr, kr, vv, attn_sink[l], token)
        full["w_out", l] = weight("w_out", l, yatt)
        ycat, r2, x2, x2b = _out_ln(ysc, yatt, ycc, full["w_out", l], x1, row(ln2_g, l), row(ln2_b, l))
        full["ffn2_w_gu", l] = weight("ffn2_w_gu", l, x2b)
        if l == 0:
            token = start_stage("e", [("w_in", 1), ("w_out", 1), ("ffn2_w_gu", 1), ("ffn2_w_down", 1)], full["ffn2_w_gu", l])[-1]
        gu2, a2 = _ffn_up(x2b, full["ffn2_w_gu", l], token)
        full["ffn2_w_down", l] = weight("ffn2_w_down", l, a2)
        if l + 1 < DEPTH:
            r3, xf, xb = _ffn_down_ln(a2, full["ffn2_w_down", l], x2, row(ln3_g, l), row(ln3_b, l))
        else:
            r3, dy, sq = _ffn_down_ln_loss(a2, full["ffn2_w_down", l], x2, row(ln3_g, l), row(ln3_b, l), target)
        sv.update(gu1=gu1, a1=a1, r1=r1, x1b=x1b, z=z, cpre=cpre, qr=qr, kr=kr, vv=vv, yatt=yatt, ycat=ycat, r2=r2, x2b=x2b,
                  gu2=gu2, a2=a2, r3=r3)
        saved.append(sv)

    loss = lax.psum(0.5 * jnp.sum(sq) / D, ("x", "y", "c"))

    sent = []
    small = {n: [None] * DEPTH for n in REPLICATED + CONVS}

    def send_grads(names, l, gs):
        srcs = [g.reshape(N_DEV, g.shape[0] // N_DEV, g.shape[1]) for g in gs]
        lands = [_own_slot(lax.dynamic_index_in_dim(s3, idx, 0, keepdims=False)) for s3 in srcs]
        started = _send_start(srcs, lands, _block_of, f"grads_start_{names[0]}_{l}", gs[-1])
        sent.append((names, l, started))
        return started[-1]

    token = None
    for l in reversed(range(DEPTH)):
        sv = saved[l]
        dy, dfb, dh, dg, db = _ffn_bwd_dx(dy, sv["r3"], row(ln3_g, l), full["ffn2_w_down", l], sv["gu2"],
                                          full["ffn2_w_gu", l], token)
        small["ln3_g"][l], small["ln3_b"][l] = dg, db
        token = send_grads(("ffn2_w_down", "ffn2_w_gu"), l,
                           [_wgrad(sv["a2"], dfb, F // 2), _wgrad(dh, sv["x2b"], F // 2)])

        dr, dmb, dysc, dyatt, dycc, dg, db = _out_bwd(dy, sv["r2"], row(ln2_g, l), full["w_out", l], token)
        small["ln2_g"][l], small["ln2_b"][l] = dg, db
        g_out = _wgrad(sv["ycat"], dmb, D)
        dqr, dkp, dvp, dsink = _attn_bwd(sv["qr"], sv["kr"], sv["vv"], attn_sink[l], dyatt, sv["yatt"])
        small["attn_sink"][l] = dsink[:, 0]
        dz, dscw, dccw, dvec = _mix_bwd(sv["z"], sv["cpre"], dysc, dycc, dqr, dkp, dvp, scw_full[l], ccw_full[l],
                                        row(cc_conv_b, l), row(cc_ln_g, l), row(cc_ln_b, l), cos, sin)
        small["sc_conv_w"][l], small["cc_conv_w"][l] = dscw, dccw
        small["cc_conv_b"][l], small["cc_ln_g"][l], small["cc_ln_b"][l] = dvec[0], dvec[1], dvec[2]
        token = send_grads(("w_out", "w_in"), l, [g_out, _wgrad(dz, sv["x1b"], D)])
        dy = _dx(dr, dz, full["w_in", l], token)

        if l > 0:
            dy, dfb, dh, dg, db = _ffn_bwd_dx(dy, sv["r1"], row(ln1_g, l), full["ffn1_w_down", l], sv["gu1"],
                                              full["ffn1_w_gu", l])
            token = send_grads(("ffn1_w_down", "ffn1_w_gu"), l,
                               [_wgrad(sv["a1"], dfb, F // 2), _wgrad(dh, sv["x0b"], F // 2)])
        else:
            dr, dfb, dh, dg, db = _ffn_bwd(dy, sv["r1"], row(ln1_g, l), full["ffn1_w_down", l], sv["gu1"])
            token = send_grads(("ffn1_w_gu",), l, [_wgrad(dh, sv["x0b"], F // 2)])
            token = send_grads(("ffn1_w_down",), l, [_wgrad(sv["a1"], dfb, F // 2, token)])
            dy = _dx(dr, dh, full["ffn1_w_gu", l], token)
        small["ln1_g"][l], small["ln1_b"][l] = dg, db
    grad_x = dy[None]

    small_names = REPLICATED + CONVS
    small_shapes = [(DEPTH,) + tuple(np.shape(small[n][0].reshape(-1))) for n in small_names]
    small_pack = _pack([jnp.stack([small[n][l].reshape(-1) for l in range(DEPTH)]) for n in small_names])
    small_all = _all_gather([small_pack], "gather_small_grads")[0]

    recv = {n: [None] * DEPTH for n in BIG}
    grads, deltas, new_m, new_v = {}, {}, {}, {}

    def receive(upto, after):
        while len(sent) > upto:
            names, l, (send, rcv, srcs, lands, _) = sent.pop(0)
            got = _recv_wait(send, rcv, list(range(len(names))), srcs, lands, _block_of, after, f"grads_wait_{names[0]}_{l}")
            for n, g in zip(names, got):
                recv[n][l] = g

    def update(n):
        if n in SWAPPED:
            outs = _sum_adam(recv[n], *[jnp.swapaxes(a, 1, 2) for a in (w[n], mom[n], var[n])], False)
            grads[n], deltas[n], new_m[n], new_v[n] = [jnp.swapaxes(a, 1, 2) for a in outs]
        else:
            grads[n], deltas[n], new_m[n], new_v[n] = _sum_adam(recv[n], w[n], mom[n], var[n], BIG_T[n])

    receive(2, dy)
    for n in ("ffn2_w_down", "ffn2_w_gu", "w_out", "w_in"):
        update(n)
    receive(0, new_v["w_in"])
    update("ffn1_w_gu")
    update("ffn1_w_down")
    small_total = _unpack(_small_sum(small_all), small_shapes)
    for n, g in zip(small_names, small_total):
        if n in CONVS:
            taps = SC_W if n == "sc_conv_w" else CC_W
            g = lax.dynamic_slice_in_dim(g.reshape(DEPTH, taps, D_SC), idx * 32, 32, axis=2)
        grads[n] = g.reshape(w[n].shape)
    wp = _pack([w[n] for n in small_names])
    gp = _pack([grads[n] for n in small_names])
    mp = _pack([mom[n] for n in small_names])
    vp = _pack([var[n] for n in small_names])
    shapes = [w[n].shape for n in small_names]
    for dst, packed in zip((deltas, new_m, new_v), _small_adam(wp, gp, mp, vp)):
        for n, a in zip(small_names, _unpack(packed, shapes)):
            dst[n] = a

    return (loss, grad_x, *[grads[n] for n in WEIGHTS], *[deltas[n] for n in WEIGHTS],
            *[new_m[n] for n in WEIGHTS], *[new_v[n] for n in WEIGHTS])
```

```python
import functools

import jax
import jax.numpy as jnp
import numpy as np
from jax import lax
from jax.experimental import pallas as pl
from jax.experimental.pallas import tpu as pltpu

F32 = jnp.float32
BF16 = jnp.bfloat16
S = jax.ShapeDtypeStruct

N_DEV = 8
DEPTH = 2
D = 1024
F = 2816
D_IN = 2048
HEAD_DIM = 64
N_Q_HEADS = 8
N_KV_HEADS = 2
GROUP = 4
D_SC = 256
D_ATT = 512
D_CC = 256
CC_W = 31
SC_W = 3
BLOCK = 128
ROPE_THETA = 10000.0
LN_EPS = 1e-5
ALPHA = (2.0 * DEPTH) ** 0.25
ADAM_LR = 0.001
ADAM_B1 = 0.9
ADAM_B2 = 0.999
ADAM_EPS = 1e-08
ADAM_WD = 0.01
ADAM_STEP = 10

O_SCB, O_SCC, O_SCH, O_Q, O_K, O_V, O_CCA, O_CCG = 0, 256, 512, 768, 1280, 1408, 1536, 1792

V7X_VMEM_BYTES = 64 * 1024 * 1024
VMEM_LIMIT = V7X_VMEM_BYTES - 8 * 1024 * 1024
TOKEN_TILE = 256
WIDE_TILE = 512
BIG_TILE = 1024
WGRAD_TOKENS = 2048
FFN_CHUNKS = (0, 768, 1536, 2176, 2816)
DX_COLS = 256
MIX_BWD_TILE = 128
HALO_FWD = 16
HALO_BWD = 16
ATT_Q_BLOCKS = 4
CONV_ROWS = 128
SUBLANES = 8

NT = (((1,), (1,)), ((), ()))
TN = (((0,), (0,)), ((), ()))
MESH = pl.DeviceIdType.MESH


def _params(sem=None):
    return pltpu.CompilerParams(dimension_semantics=sem, vmem_limit_bytes=VMEM_LIMIT)


def _sigmoid(v):
    return 1.0 / (1.0 + jnp.exp(-v))


def _ln_stats(r):
    mu = jnp.mean(r, axis=-1, keepdims=True)
    d = r - mu
    var = jnp.mean(d * d, axis=-1, keepdims=True)
    rstd = lax.rsqrt(var + LN_EPS)
    return d * rstd, rstd


def _ln_bwd(dn, xhat, rstd, gam):
    dxh = dn * gam
    return rstd * (dxh - jnp.mean(dxh, axis=-1, keepdims=True) - xhat * jnp.mean(dxh * xhat, axis=-1, keepdims=True))


def _swap_halves(v):
    n = v.shape[-1]
    lane = lax.broadcasted_iota(jnp.int32, v.shape, v.ndim - 1) % HEAD_DIM
    return jnp.where(lane < HEAD_DIM // 2, pltpu.roll(v, n - HEAD_DIM // 2, v.ndim - 1), pltpu.roll(v, HEAD_DIM // 2, v.ndim - 1))


def _wide(tab, n):
    return tab if n == 128 else jnp.concatenate([tab] * (n // 128), axis=1)


def _me():
    x, y, c = lax.axis_index("x"), lax.axis_index("y"), lax.axis_index("c")
    return x, y, c


def _peer(rel):
    x, y, c = _me()
    px = 1 - x if rel & 4 else x
    py = 1 - y if rel & 2 else y
    pc = 1 - c if rel & 1 else c
    return (px, py, pc), 4 * px + 2 * py + pc


def _exchange(srcs, dsts_shape, dst_index, src_of, dst_of, name):
    n = len(srcs)

    def body(*refs):
        ins = refs[:n]
        outs = [refs[n + dst_index[k]] for k in range(n)]
        send, recv, lsem = refs[n + len(dsts_shape):]
        x, y, c = _me()
        me = 4 * x + 2 * y + c
        local = [pltpu.make_async_copy(src_of(ins[k], k, me), dst_of(outs[k], k, me), lsem.at[k]) for k in range(n)]
        for cp in local:
            cp.start()
        sends, recvs = [], []
        for k in range(n):
            for rel in range(1, N_DEV):
                peer, pidx = _peer(rel)
                sends.append(pltpu.make_async_remote_copy(
                    src_ref=src_of(ins[k], k, pidx), dst_ref=dst_of(outs[k], k, me),
                    send_sem=send.at[k, rel - 1], recv_sem=recv.at[k, rel - 1], device_id=peer, device_id_type=MESH))
                recvs.append(pltpu.make_async_remote_copy(
                    src_ref=src_of(ins[k], k, pidx), dst_ref=dst_of(outs[k], k, pidx),
                    send_sem=send.at[k, rel - 1], recv_sem=recv.at[k, rel - 1], device_id=peer, device_id_type=MESH))
        for cp in sends:
            cp.start()
        for cp in recvs:
            cp.wait_recv()
        for cp in sends:
            cp.wait_send()
        for cp in local:
            cp.wait()

    hbm = pl.BlockSpec(memory_space=pltpu.HBM)
    return pl.pallas_call(
        body, name=name, in_specs=[hbm] * n, out_specs=[hbm] * len(dsts_shape), out_shape=dsts_shape,
        scratch_shapes=[pltpu.SemaphoreType.DMA((n, N_DEV - 1)), pltpu.SemaphoreType.DMA((n, N_DEV - 1)),
                        pltpu.SemaphoreType.DMA((n,))],
    )(*srcs)


def _all_gather(blocks, name):
    shapes = [S((N_DEV,) + b.shape, b.dtype) for b in blocks]
    return _exchange(blocks, shapes, list(range(len(blocks))), lambda ref, k, idx: ref, lambda ref, k, idx: ref.at[idx], name)


HBM_SPEC = pl.BlockSpec(memory_space=pltpu.HBM)
SEM_SPEC = pl.BlockSpec(memory_space=pltpu.SEMAPHORE)
ANY_SPEC = pl.BlockSpec(memory_space=pl.ANY)
EFFECT = pltpu.SideEffectType.DATAFLOW_SIDE_EFFECTING
N_PEERS = N_DEV - 1


def _own_slot(block):
    x, y, c = _me()
    return lax.dynamic_update_index_in_dim(lax.empty((N_DEV,) + block.shape, block.dtype), block, 4 * x + 2 * y + c, 0)


def _follow(body, n_in, in_specs, operands, after):
    if after is None:
        return body, list(in_specs), list(operands)

    def tail(*refs):
        return body(*refs[:n_in], *refs[n_in + 1:])

    return tail, list(in_specs) + [ANY_SPEC], list(operands) + [after]


def _send_start(srcs, lands, src_of, name, after):
    n = len(srcs)

    def body(*refs):
        ins, zones = refs[:n], refs[n:2 * n]
        send, recv = refs[2 * n + 1], refs[2 * n + 2]
        token = refs[-1]
        x, y, c = _me()
        me = 4 * x + 2 * y + c
        for k in range(n):
            for rel in range(1, N_DEV):
                peer, pidx = _peer(rel)
                pltpu.make_async_remote_copy(
                    src_ref=src_of(ins[k], pidx), dst_ref=zones[k].at[me],
                    send_sem=send.at[k * N_PEERS + rel - 1], recv_sem=recv.at[k * N_PEERS + rel - 1],
                    device_id=peer, device_id_type=MESH).start()
        token[...] = jnp.zeros_like(token)

    outs = pl.pallas_call(
        body, name=name,
        out_shape=(pltpu.SemaphoreType.DMA((n * N_PEERS,)), pltpu.SemaphoreType.DMA((n * N_PEERS,)),
                   *[pltpu.HBM(a.shape, a.dtype) for a in lands], S((8, 128), F32)),
        in_specs=[HBM_SPEC] * (2 * n) + [ANY_SPEC],
        out_specs=(SEM_SPEC, SEM_SPEC, *[HBM_SPEC] * n, pl.BlockSpec(memory_space=pltpu.VMEM)),
        input_output_aliases={n + i: 2 + i for i in range(n)},
        compiler_params=pltpu.CompilerParams(has_side_effects=EFFECT),
    )(*[pltpu.with_memory_space_constraint(a, pltpu.HBM) for a in list(srcs) + list(lands)], after)
    return outs[0], outs[1], list(srcs), list(outs[2:2 + n]), outs[-1]


def _recv_wait(send, recv, ks, srcs, lands, src_of, after, name):
    n = len(ks)
    after = after if isinstance(after, (tuple, list)) else (after,)

    def body(*refs):
        ins, zones = refs[:n], refs[n:2 * n]
        send_sems, recv_sems = refs[2 * n], refs[2 * n + 1]
        for j, k in enumerate(ks):
            for rel in range(1, N_DEV):
                peer, pidx = _peer(rel)
                cp = pltpu.make_async_remote_copy(
                    src_ref=src_of(ins[j], pidx), dst_ref=zones[j].at[pidx],
                    send_sem=send_sems.at[k * N_PEERS + rel - 1], recv_sem=recv_sems.at[k * N_PEERS + rel - 1],
                    device_id=peer, device_id_type=MESH)
                cp.wait_send()
                cp.wait_recv()

    outs = pl.pallas_call(
        body, name=name,
        out_shape=[pltpu.HBM(a.shape, a.dtype) for a in lands],
        in_specs=[HBM_SPEC] * (2 * n) + [SEM_SPEC, SEM_SPEC] + [ANY_SPEC] * len(after), out_specs=[HBM_SPEC] * n,
        input_output_aliases={n + i: i for i in range(n)},
        compiler_params=pltpu.CompilerParams(has_side_effects=EFFECT),
    )(*srcs, *lands, send, recv, *after)
    return list(outs)


def _whole(ref, idx):
    return ref


def _block_of(ref, idx):
    return ref.at[idx]


def _resident(shape):
    return pl.BlockSpec(shape, lambda i: (0,) * len(shape), pipeline_mode=pl.Buffered(1))


def _ffn_up(xb, wgut, after=None):
    t = xb.shape[0]
    tm = min(WIDE_TILE, t)
    half = F // 2

    def body(x_ref, w_ref, gu_ref, a_ref):
        x = x_ref[...]
        for ch in range(2):
            lo = ch * half
            g = lax.dot_general(x, w_ref[lo:lo + half, :], NT, preferred_element_type=F32)
            u = lax.dot_general(x, w_ref[F + lo:F + lo + half, :], NT, preferred_element_type=F32)
            gu_ref[:, lo:lo + half] = g.astype(BF16)
            gu_ref[:, F + lo:F + lo + half] = u.astype(BF16)
            a_ref[:, lo:lo + half] = (g * _sigmoid(g) * u).astype(BF16)

    body, in_specs, operands = _follow(
        body, 2, [pl.BlockSpec((tm, D), lambda i: (i, 0)), _resident((2 * F, D))], [xb, wgut], after)
    return pl.pallas_call(
        body, name="ffn_up", grid=(t // tm,), in_specs=in_specs,
        out_specs=[pl.BlockSpec((tm, 2 * F), lambda i: (i, 0)), pl.BlockSpec((tm, F), lambda i: (i, 0))],
        out_shape=[S((t, 2 * F), BF16), S((t, F), BF16)], compiler_params=_params(("parallel",)),
    )(*operands)


def _residual(x_ref, src_ln):
    if src_ln is None:
        return x_ref[...]
    xhat, _ = _ln_stats(x_ref[...])
    return xhat * src_ln[0][...] + src_ln[1][...]


def _ffn_down_ln(a, wd, x, gam, bet, src_ln=None, after=None):
    t = x.shape[0]
    tm = min(BIG_TILE, t)
    n_ln = 0 if src_ln is None else 2

    def body(*refs):
        a_ref, w_ref, x_ref, g_ref, b_ref = refs[:5]
        r_ref, yb_ref = refs[5 + n_ln:]
        f = jnp.dot(a_ref[...], w_ref[...], preferred_element_type=F32)
        r = ALPHA * _residual(x_ref, refs[5:5 + n_ln] or None) + 0.5 * f
        r_ref[...] = r
        xhat, _ = _ln_stats(r)
        yb_ref[...] = (xhat * g_ref[...] + b_ref[...]).astype(BF16)

    row = pl.BlockSpec((tm, D), lambda i: (i, 0))
    vec = pl.BlockSpec((1, D), lambda i: (0, 0))
    body, in_specs, operands = _follow(
        body, 5 + n_ln, [pl.BlockSpec((tm, F), lambda i: (i, 0)), _resident((F, D)), row, vec, vec] + [vec] * n_ln,
        [a, wd, x, gam, bet] + list(src_ln or ()), after)
    return pl.pallas_call(
        body, name="ffn_down_ln", grid=(t // tm,), in_specs=in_specs,
        out_specs=[row, row], out_shape=[S((t, D), F32), S((t, D), BF16)],
        compiler_params=_params(("parallel",)),
    )(*operands)


def _ffn_down_ln_loss(a, wd, x, gam, bet, src_ln, target):
    t = x.shape[0]
    tm = min(BIG_TILE, t)

    def body(a_ref, w_ref, x_ref, g_ref, b_ref, sg_ref, sb_ref, t_ref, r_ref, dy_ref, part_ref):
        f = jnp.dot(a_ref[...], w_ref[...], preferred_element_type=F32)
        r = ALPHA * _residual(x_ref, (sg_ref, sb_ref)) + 0.5 * f
        r_ref[...] = r
        xhat, _ = _ln_stats(r)
        e = xhat * g_ref[...] + b_ref[...] - t_ref[...]
        dy_ref[...] = e / D

        @pl.when(pl.program_id(0) == 0)
        def _():
            part_ref[...] = jnp.zeros_like(part_ref)

        part_ref[...] += jnp.sum(e * e, axis=0, keepdims=True)

    row = pl.BlockSpec((tm, D), lambda i: (i, 0))
    vec = pl.BlockSpec((1, D), lambda i: (0, 0))
    return pl.pallas_call(
        body, name="ffn_down_ln_loss", grid=(t // tm,),
        in_specs=[pl.BlockSpec((tm, F), lambda i: (i, 0)), _resident((F, D)), row, vec, vec, vec, vec, row],
        out_specs=[row, row, vec], out_shape=[S((t, D), F32), S((t, D), F32), S((1, D), F32)],
        compiler_params=_params(("arbitrary",)),
    )(a, wd, x, gam, bet, *src_ln, target)


def _proj_in(xb, wint):
    t = xb.shape[0]
    tm = min(BIG_TILE, t)

    def body(x_ref, w_ref, z_ref):
        z_ref[...] = lax.dot_general(x_ref[...], w_ref[...], NT, preferred_element_type=F32)

    return pl.pallas_call(
        body, name="proj_in", grid=(t // tm,),
        in_specs=[pl.BlockSpec((tm, D), lambda i: (i, 0)), _resident((D_IN, D))],
        out_specs=pl.BlockSpec((tm, D_IN), lambda i: (i, 0)), out_shape=S((t, D_IN), F32),
        compiler_params=_params(("parallel",)),
    )(xb, wint)


def _halo_specs(t, tm, halo, width):
    per = tm // halo
    last = t // halo - 1
    return [pl.BlockSpec((tm, width), lambda i: (i, 0)),
            pl.BlockSpec((halo, width), lambda i: (jnp.maximum(i * per - 1, 0), 0)),
            pl.BlockSpec((halo, width), lambda i: (jnp.minimum((i + 1) * per, last), 0))]


def _taps_aligned(w_ref, x_ref, p_ref, offsets, rows):
    for r in range(SUBLANES):
        acc = jnp.zeros((rows + SUBLANES, x_ref.shape[1]), F32)
        for j, o in enumerate(offsets):
            if o % SUBLANES == r:
                acc = acc + w_ref[j:j + 1, :] * x_ref[pl.ds(o - r, rows + SUBLANES), :]
        p_ref[r] = acc
    out = p_ref[0, 0:rows, :]
    for r in range(1, SUBLANES):
        out = out + p_ref[r, pl.ds(r, rows), :]
    return out


def _mix_fwd(z, scw, ccw, ccb, ccg, ccbb, cos, sin):
    t = z.shape[0]
    tm = min(WIDE_TILE, t)
    nt = t // tm
    h = HALO_FWD
    rc = min(CONV_ROWS, tm)

    def body(z_ref, zp_ref, zn_ref, scw_ref, ccw_ref, ccb_ref, ccg_ref, ccbb_ref, cos_ref, sin_ref,
             ysc_ref, ycc_ref, c_ref, q_ref, k_ref, v_ref, u_s, ch_s, p_s):
        i = pl.program_id(0)
        pz = jnp.where(i == 0, 0.0, zp_ref[...])
        nz = jnp.where(i == nt - 1, 0.0, zn_ref[...])

        def u_of(zz):
            return zz[:, O_CCA:O_CCA + D_CC] * _sigmoid(zz[:, O_CCG:O_CCG + D_CC])

        def ch_of(zz):
            return zz[:, O_SCC:O_SCC + D_SC] * zz[:, O_SCH:O_SCH + D_SC]

        u_s[0:h, :] = u_of(pz)
        u_s[h:h + tm, :] = z_ref[:, O_CCA:O_CCA + D_CC] * _sigmoid(z_ref[:, O_CCG:O_CCG + D_CC])
        u_s[h + tm:2 * h + tm, :] = u_of(nz)
        ch_s[0:h, :] = ch_of(pz)
        ch_s[h:h + tm, :] = z_ref[:, O_SCC:O_SCC + D_SC] * z_ref[:, O_SCH:O_SCH + D_SC]
        ch_s[h + tm:2 * h + tm, :] = ch_of(nz)
        for r0 in range(0, tm, rc):
            c = _taps_aligned(ccw_ref, u_s, p_s, [r0 + h + j - CC_W // 2 for j in range(CC_W)], rc) + ccb_ref[...]
            c_ref[r0:r0 + rc, :] = c
            xhat, _ = _ln_stats(c)
            n = xhat * ccg_ref[...] + ccbb_ref[...]
            ycc_ref[r0:r0 + rc, :] = (n * _sigmoid(n)).astype(BF16)
            acc = jnp.zeros((rc, D_SC), F32)
            for j in range(SC_W):
                acc = acc + scw_ref[j:j + 1, :] * ch_s[pl.ds(r0 + h + j - SC_W // 2, rc), :]
            ysc_ref[r0:r0 + rc, :] = (z_ref[r0:r0 + rc, O_SCB:O_SCB + D_SC] * acc).astype(BF16)
        q = z_ref[:, O_Q:O_Q + D_ATT]
        q_ref[...] = ((q * _wide(cos_ref[...], D_ATT) + _swap_halves(q) * _wide(sin_ref[...], D_ATT)) * (HEAD_DIM ** -0.5)).astype(BF16)
        k = z_ref[:, O_K:O_K + 128]
        k_ref[...] = (k * cos_ref[...] + _swap_halves(k) * sin_ref[...]).astype(BF16)
        v_ref[...] = z_ref[:, O_V:O_V + 128].astype(BF16)

    def full(a):
        return pl.BlockSpec(a.shape, lambda i: (0, 0))

    def rows(w):
        return pl.BlockSpec((tm, w), lambda i: (i, 0))

    return pl.pallas_call(
        body, name="mix_fwd", grid=(nt,),
        in_specs=_halo_specs(t, tm, h, D_IN) + [full(scw), full(ccw), full(ccb), full(ccg), full(ccbb), rows(128), rows(128)],
        out_specs=[rows(D_SC), rows(D_CC), rows(D_CC), rows(D_ATT), rows(128), rows(128)],
        out_shape=[S((t, D_SC), BF16), S((t, D_CC), BF16), S((t, D_CC), F32), S((t, D_ATT), BF16), S((t, 128), BF16),
                   S((t, 128), BF16)],
        scratch_shapes=[pltpu.VMEM((tm + 2 * h, D_CC), F32), pltpu.VMEM((tm + 2 * h, D_SC), F32),
                        pltpu.VMEM((SUBLANES, rc + SUBLANES, D_CC), F32)],
        compiler_params=_params(("parallel",)),
    )(z, z, z, scw, ccw, ccb, ccg, ccbb, cos, sin)


def _band_specs(nb, width):
    return [pl.BlockSpec((BLOCK, width), lambda n: (jnp.maximum(n * ATT_Q_BLOCKS - 1, 0), 0)),
            pl.BlockSpec((ATT_Q_BLOCKS * BLOCK, width), lambda n: (n, 0)),
            pl.BlockSpec((BLOCK, width), lambda n: (jnp.minimum((n + 1) * ATT_Q_BLOCKS, nb - 1), 0))]


def _band_bias(b, nb, bias_s):
    qpos = lax.broadcasted_iota(jnp.int32, (BLOCK, 3 * BLOCK), 0)
    col = lax.broadcasted_iota(jnp.int32, (BLOCK, 3 * BLOCK), 1)
    ok = jnp.abs(qpos - (col - BLOCK)) <= BLOCK
    ok = jnp.logical_and(ok, jnp.logical_or(col >= BLOCK, b > 0))
    ok = jnp.logical_and(ok, jnp.logical_or(col < 2 * BLOCK, b < nb - 1))
    bias_s[...] = jnp.where(ok, 0.0, -1e30)


def _band_cats(before_ref, own_ref, after_ref):
    pieces = [(before_ref, 0)] + [(own_ref, j * BLOCK) for j in range(ATT_Q_BLOCKS)] + [(after_ref, 0)]
    return [[jnp.concatenate([r[r0:r0 + BLOCK, kvh * HEAD_DIM:(kvh + 1) * HEAD_DIM] for r, r0 in pieces[sub:sub + 3]], axis=0)
             for kvh in range(N_KV_HEADS)] for sub in range(ATT_Q_BLOCKS)]


def _head_scores(q_ref, kc, sub, h, bias_s):
    qh = q_ref[sub * BLOCK:(sub + 1) * BLOCK, h * HEAD_DIM:(h + 1) * HEAD_DIM]
    return qh, lax.dot_general(qh, kc, NT, preferred_element_type=F32) + bias_s[sub]


def _softmax_parts(s, sk):
    m = jnp.maximum(jnp.max(s, axis=-1, keepdims=True), sk)
    p = jnp.exp(s - m)
    ps = jnp.exp(sk - m)
    return p, ps, jnp.sum(p, axis=-1, keepdims=True) + ps


def _attn_fwd(qr, kr, vv, sink, after=None):
    t = qr.shape[0]
    nb = t // BLOCK
    units = [(sub, h) for sub in range(ATT_Q_BLOCKS) for h in range(N_Q_HEADS)]

    def body(q_ref, kp_ref, ko_ref, kn_ref, vp_ref, vo_ref, vn_ref, sink_ref, o_ref, bias_s):
        n = pl.program_id(0)
        for sub in range(ATT_Q_BLOCKS):
            _band_bias(n * ATT_Q_BLOCKS + sub, nb, bias_s.at[sub])
        kcs = _band_cats(kp_ref, ko_ref, kn_ref)
        vcs = _band_cats(vp_ref, vo_ref, vn_ref)

        def scores(u):
            sub, h = units[u]
            return _head_scores(q_ref, kcs[sub][h // GROUP], sub, h, bias_s)[1]

        s_next = scores(0)
        for u, (sub, h) in enumerate(units):
            s = s_next
            if u + 1 < len(units):
                s_next = scores(u + 1)
            p, _, denom = _softmax_parts(s, sink_ref[h])
            o = jnp.dot(p.astype(BF16), vcs[sub][h // GROUP], preferred_element_type=F32) * (1.0 / denom)
            o_ref[sub * BLOCK:(sub + 1) * BLOCK, h * HEAD_DIM:(h + 1) * HEAD_DIM] = o.astype(BF16)

    qspec = pl.BlockSpec((ATT_Q_BLOCKS * BLOCK, D_ATT), lambda n: (n, 0))
    body, in_specs, operands = _follow(
        body, 8, [qspec] + _band_specs(nb, 128) + _band_specs(nb, 128) + [pl.BlockSpec(memory_space=pltpu.SMEM)],
        [qr, kr, kr, kr, vv, vv, vv, sink], after)
    return pl.pallas_call(
        body, name="attn_fwd", grid=(nb // ATT_Q_BLOCKS,), in_specs=in_specs,
        out_specs=qspec, out_shape=S((t, D_ATT), BF16),
        scratch_shapes=[pltpu.VMEM((ATT_Q_BLOCKS, BLOCK, 3 * BLOCK), F32)],
        compiler_params=_params(("parallel",)),
    )(*operands)


def _out_ln(ysc, yatt, ycc, wout, x, gam, bet, src_ln):
    t = x.shape[0]
    tm = min(BIG_TILE, t)

    def body(sc_ref, at_ref, cc_ref, w_ref, x_ref, g_ref, b_ref, sg_ref, sb_ref, cat_ref, r_ref, yb_ref):
        cat = jnp.concatenate([sc_ref[...], at_ref[...], cc_ref[...]], axis=1)
        cat_ref[...] = cat
        f = jnp.dot(cat, w_ref[...], preferred_element_type=F32)
        r = ALPHA * _residual(x_ref, (sg_ref, sb_ref)) + f
        r_ref[...] = r
        xhat, _ = _ln_stats(r)
        yb_ref[...] = (xhat * g_ref[...] + b_ref[...]).astype(BF16)

    def rows(w):
        return pl.BlockSpec((tm, w), lambda i: (i, 0))

    vec = pl.BlockSpec((1, D), lambda i: (0, 0))
    return pl.pallas_call(
        body, name="out_ln", grid=(t // tm,),
        in_specs=[rows(D_SC), rows(D_ATT), rows(D_CC), _resident((D, D)), rows(D), vec, vec, vec, vec],
        out_specs=[rows(D), rows(D), rows(D)],
        out_shape=[S((t, D), BF16), S((t, D), F32), S((t, D), BF16)],
        compiler_params=_params(("parallel",)),
    )(ysc, yatt, ycc, wout, x, gam, bet, *src_ln)


def _ln_bwd_block(dy_ref, r_ref, g_ref, dgam_ref, dbet_ref):
    xhat, rstd = _ln_stats(r_ref[...])
    dy = dy_ref[...]

    @pl.when(pl.program_id(0) == 0)
    def _():
        dgam_ref[...] = jnp.zeros_like(dgam_ref)
        dbet_ref[...] = jnp.zeros_like(dbet_ref)

    dgam_ref[...] += jnp.sum(dy * xhat, axis=0, keepdims=True)
    dbet_ref[...] += jnp.sum(dy, axis=0, keepdims=True)
    return _ln_bwd(dy, xhat, rstd, g_ref[...])


def _ffn_bwd(dy, r, gam, wd, gu):
    t = dy.shape[0]
    tm = min(TOKEN_TILE, t)
    chunks = list(zip(FFN_CHUNKS[:-1], FFN_CHUNKS[1:]))

    def body(dy_ref, r_ref, g_ref, w_ref, gu_ref, dr_ref, df_ref, dh_ref, dgam_ref, dbet_ref):
        dr = _ln_bwd_block(dy_ref, r_ref, g_ref, dgam_ref, dbet_ref)
        dr_ref[...] = dr
        dfb = (0.5 * dr).astype(BF16)
        df_ref[...] = dfb
        for lo, hi in chunks:
            da = lax.dot_general(dfb, w_ref[lo:hi, :], NT, preferred_element_type=F32)
            g = gu_ref[:, lo:hi].astype(F32)
            u = gu_ref[:, F + lo:F + hi].astype(F32)
            sg = _sigmoid(g)
            dh_ref[:, lo:hi] = (da * u * (sg * (1.0 + g * (1.0 - sg)))).astype(BF16)
            dh_ref[:, F + lo:F + hi] = (da * (g * sg)).astype(BF16)

    row = pl.BlockSpec((tm, D), lambda i: (i, 0))
    vec = pl.BlockSpec((1, D), lambda i: (0, 0))
    wide = pl.BlockSpec((tm, 2 * F), lambda i: (i, 0))
    return pl.pallas_call(
        body, name="ffn_bwd", grid=(t // tm,),
        in_specs=[row, row, vec, _resident((F, D)), wide],
        out_specs=[row, row, wide, vec, vec],
        out_shape=[S((t, D), F32), S((t, D), BF16), S((t, 2 * F), BF16), S((1, D), F32), S((1, D), F32)],
        compiler_params=_params(("arbitrary",)),
    )(dy, r, gam, wd, gu)


def _ffn_bwd_dx(dy, r, gam, wd, gu, wgut, after=None):
    t = dy.shape[0]
    tm = min(TOKEN_TILE, t)
    n = t // tm
    chunks = list(zip(FFN_CHUNKS[:-1], FFN_CHUNKS[1:]))

    def body(dy_ref, r_ref, g_ref, w_ref, gu_ref, wg_ref, dx_ref, df_ref, dh_ref, dgam_ref, dbet_ref,
             keep_a, keep_b, dr_keep):
        i = pl.program_id(0)

        @pl.when(i == 0)
        def _():
            keep_a[...] = jnp.zeros_like(keep_a)
            keep_b[...] = jnp.zeros_like(keep_b)
            dr_keep[...] = jnp.zeros_like(dr_keep)
            dgam_ref[...] = jnp.zeros_like(dgam_ref)
            dbet_ref[...] = jnp.zeros_like(dbet_ref)

        def step(prev, cur):
            def to_dx(c):
                cols = slice(c * DX_COLS, (c + 1) * DX_COLS)
                dx_ref[:, cols] = ALPHA * dr_keep[:, cols] + jnp.dot(prev[...], wg_ref[:, cols], preferred_element_type=F32)

            to_dx(0)
            xhat, rstd = _ln_stats(r_ref[...])
            dy_t = dy_ref[...]
            live = jnp.where(i < n, 1.0, 0.0)
            dgam_ref[...] += live * jnp.sum(dy_t * xhat, axis=0, keepdims=True)
            dbet_ref[...] += live * jnp.sum(dy_t, axis=0, keepdims=True)
            dr = _ln_bwd(dy_t, xhat, rstd, g_ref[...])
            dfb = (0.5 * dr).astype(BF16)
            df_ref[...] = dfb

            def down(c):
                return lax.dot_general(dfb, w_ref[chunks[c][0]:chunks[c][1], :], NT, preferred_element_type=F32)

            das = {0: down(0), 1: down(1)}
            for c, (lo, hi) in enumerate(chunks):
                if c + 2 < len(chunks):
                    das[c + 2] = down(c + 2)
                if c + 1 < D // DX_COLS:
                    to_dx(c + 1)
                da = das.pop(c)
                g = gu_ref[:, lo:hi].astype(F32)
                u = gu_ref[:, F + lo:F + hi].astype(F32)
                sg = _sigmoid(g)
                dg = (da * u * (sg * (1.0 + g * (1.0 - sg)))).astype(BF16)
                du = (da * (g * sg)).astype(BF16)
                dh_ref[:, lo:hi] = dg
                dh_ref[:, F + lo:F + hi] = du
                cur[:, lo:hi] = dg
                cur[:, F + lo:F + hi] = du
            dr_keep[...] = dr

        @pl.when(i % 2 == 0)
        def _():
            step(keep_b, keep_a)

        @pl.when(i % 2 == 1)
        def _():
            step(keep_a, keep_b)

    cur_row = lambda i: (jnp.minimum(i, n - 1), 0)
    row = pl.BlockSpec((tm, D), cur_row)
    vec = pl.BlockSpec((1, D), lambda i: (0, 0))
    wide = pl.BlockSpec((tm, 2 * F), cur_row)
    body, in_specs, operands = _follow(
        body, 6, [row, row, vec, _resident((F, D)), wide, _resident((2 * F, D))], [dy, r, gam, wd, gu, wgut], after)
    return pl.pallas_call(
        body, name="ffn_bwd_dx", grid=(n + 1,), in_specs=in_specs,
        out_specs=[pl.BlockSpec((tm, D), lambda i: (jnp.maximum(i - 1, 0), 0)), row, wide, vec, vec],
        out_shape=[S((t, D), F32), S((t, D), BF16), S((t, 2 * F), BF16), S((1, D), F32), S((1, D), F32)],
        scratch_shapes=[pltpu.VMEM((tm, 2 * F), BF16), pltpu.VMEM((tm, 2 * F), BF16), pltpu.VMEM((tm, D), F32)],
        compiler_params=_params(("arbitrary",)),
    )(*operands)


def _dx(dr, dh, w, after=None):
    t = dr.shape[0]
    tm = min(BIG_TILE, t)
    kk = dh.shape[1]

    def body(dr_ref, dh_ref, w_ref, o_ref):
        o_ref[...] = ALPHA * dr_ref[...] + jnp.dot(dh_ref[...], w_ref[...], preferred_element_type=F32)

    row = pl.BlockSpec((tm, D), lambda i: (i, 0))
    body, in_specs, operands = _follow(
        body, 3, [row, pl.BlockSpec((tm, kk), lambda i: (i, 0)), _resident((kk, D))], [dr, dh, w], after)
    return pl.pallas_call(
        body, name="dx", grid=(t // tm,), in_specs=in_specs,
        out_specs=row, out_shape=S((t, D), F32), compiler_params=_params(("parallel",)),
    )(*operands)


def _wgrad(a, b, ta, after=None):
    t, ka = a.shape
    tk = min(WGRAD_TOKENS, t)
    nk = t // tk

    def body(a_ref, b_ref, o_ref, acc):
        k = pl.program_id(1)

        @pl.when(k == 0)
        def _():
            acc[...] = jnp.zeros_like(acc)

        acc[...] += lax.dot_general(a_ref[...], b_ref[...], TN, preferred_element_type=F32)

        @pl.when(k == nk - 1)
        def _():
            o_ref[...] = acc[...].astype(BF16)

    body, in_specs, operands = _follow(
        body, 2, [pl.BlockSpec((tk, ta), lambda i, k: (k, i)), pl.BlockSpec((tk, D), lambda i, k: (k, 0))], [a, b], after)
    return pl.pallas_call(
        body, name="wgrad", grid=(ka // ta, nk), in_specs=in_specs,
        out_specs=pl.BlockSpec((ta, D), lambda i, k: (i, 0)), out_shape=S((ka, D), BF16),
        scratch_shapes=[pltpu.VMEM((ta, D), F32)], compiler_params=_params(("parallel", "arbitrary")),
    )(*operands)


def _out_bwd(dy, r, gam, wout, after=None):
    t = dy.shape[0]
    tm = min(BIG_TILE, t)

    def body(dy_ref, r_ref, g_ref, w_ref, dr_ref, dm_ref, dsc_ref, dat_ref, dcc_ref, dgam_ref, dbet_ref):
        dr = _ln_bwd_block(dy_ref, r_ref, g_ref, dgam_ref, dbet_ref)
        dr_ref[...] = dr
        dmb = dr.astype(BF16)
        dm_ref[...] = dmb
        dcat = lax.dot_general(dmb, w_ref[...], NT, preferred_element_type=F32)
        dsc_ref[...] = dcat[:, 0:D_SC]
        dat_ref[...] = dcat[:, D_SC:D_SC + D_ATT]
        dcc_ref[...] = dcat[:, D_SC + D_ATT:D]

    def rows(w):
        return pl.BlockSpec((tm, w), lambda i: (i, 0))

    vec = pl.BlockSpec((1, D), lambda i: (0, 0))
    body, in_specs, operands = _follow(body, 4, [rows(D), rows(D), vec, _resident((D, D))], [dy, r, gam, wout], after)
    return pl.pallas_call(
        body, name="out_bwd", grid=(t // tm,), in_specs=in_specs,
        out_specs=[rows(D), rows(D), rows(D_SC), rows(D_ATT), rows(D_CC), vec, vec],
        out_shape=[S((t, D), F32), S((t, D), BF16), S((t, D_SC), F32), S((t, D_ATT), F32), S((t, D_CC), F32),
                   S((1, D), F32), S((1, D), F32)],
        compiler_params=_params(("arbitrary",)),
    )(*operands)


def _attn_bwd(qr, kr, vv, sink, do, yatt):
    t = qr.shape[0]
    nb = t // BLOCK
    scale = HEAD_DIM ** -0.5
    units = [(sub, h) for sub in range(ATT_Q_BLOCKS) for h in range(N_Q_HEADS)]

    def body(q_ref, kp_ref, ko_ref, kn_ref, vp_ref, vo_ref, vn_ref, sink_ref, do_ref, o_ref,
             dq_ref, dk_ref, dv_ref, dsink_ref, bias_s, ds_s, p_s, q_s, dou_s, sink_s):
        n = pl.program_id(0)
        for sub in range(ATT_Q_BLOCKS):
            _band_bias(n * ATT_Q_BLOCKS + sub, nb, bias_s.at[sub])

        @pl.when(n == 0)
        def _():
            sink_s[...] = jnp.zeros_like(sink_s)

        kcs = _band_cats(kp_ref, ko_ref, kn_ref)
        vcs = _band_cats(vp_ref, vo_ref, vn_ref)

        def scores(u):
            sub, h = units[u]
            return _head_scores(q_ref, kcs[sub][h // GROUP], sub, h, bias_s)

        def probs(u, qh, s):
            sub, h = units[u]
            rows = slice(sub * BLOCK, (sub + 1) * BLOCK)
            cols = slice(h * HEAD_DIM, (h + 1) * HEAD_DIM)
            p, ps, denom = _softmax_parts(s, sink_ref[h])
            doh = do_ref[rows, cols]
            inv = 1.0 / denom
            dd = jnp.sum(doh * o_ref[rows, cols].astype(F32), axis=-1, keepdims=True) * inv
            dou = (doh * inv).astype(BF16)
            dp = lax.dot_general(dou, vcs[sub][h // GROUP], NT, preferred_element_type=F32)
            sink_s[h] -= ps * dd
            return qh, p, dd, dou, dp

        def grads(u, qh, p, dd, dou, dp):
            sub, h = units[u]
            kvh, g = divmod(h, GROUP)
            rows = slice(sub * BLOCK, (sub + 1) * BLOCK)
            cols = slice(h * HEAD_DIM, (h + 1) * HEAD_DIM)
            stack = slice(g * BLOCK, (g + 1) * BLOCK)
            ds = (p * (dp - dd)).astype(BF16)
            dq_ref[rows, cols] = jnp.dot(ds, kcs[sub][kvh], preferred_element_type=F32) * scale
            ds_s[stack, :] = ds
            p_s[stack, :] = p.astype(BF16)
            q_s[stack, :] = qh
            dou_s[stack, :] = dou
            if g == GROUP - 1:
                dk = lax.dot_general(ds_s[...], q_s[...], TN, preferred_element_type=F32)
                dv = lax.dot_general(p_s[...], dou_s[...], TN, preferred_element_type=F32)
                for j in range(3):
                    dk_ref[j, rows, kvh * HEAD_DIM:(kvh + 1) * HEAD_DIM] = dk[j * BLOCK:(j + 1) * BLOCK, :]
                    dv_ref[j, rows, kvh * HEAD_DIM:(kvh + 1) * HEAD_DIM] = dv[j * BLOCK:(j + 1) * BLOCK, :]

        sc = {0: scores(0), 1: scores(1)}
        pr = {0: probs(0, *sc.pop(0))}
        for u in range(len(units)):
            if u + 2 < len(units):
                sc[u + 2] = scores(u + 2)
            if u + 1 < len(units):
                pr[u + 1] = probs(u + 1, *sc.pop(u + 1))
            grads(u, *pr.pop(u))

        @pl.when(n == nb // ATT_Q_BLOCKS - 1)
        def _():
            for h in range(N_Q_HEADS):
                dsink_ref[h:h + 1, :] = jnp.zeros((1, 128), F32) + jnp.sum(sink_s[h])

    qspec = pl.BlockSpec((ATT_Q_BLOCKS * BLOCK, D_ATT), lambda n: (n, 0))
    part = pl.BlockSpec((3, ATT_Q_BLOCKS * BLOCK, 128), lambda n: (0, n, 0))
    stacked = GROUP * BLOCK
    return pl.pallas_call(
        body, name="attn_bwd", grid=(nb // ATT_Q_BLOCKS,),
        in_specs=[qspec] + _band_specs(nb, 128) + _band_specs(nb, 128) + [pl.BlockSpec(memory_space=pltpu.SMEM), qspec, qspec],
        out_specs=[qspec, part, part, pl.BlockSpec((N_Q_HEADS, 128), lambda n: (0, 0))],
        out_shape=[S((t, D_ATT), F32), S((3, t, 128), F32), S((3, t, 128), F32), S((N_Q_HEADS, 128), F32)],
        scratch_shapes=[pltpu.VMEM((ATT_Q_BLOCKS, BLOCK, 3 * BLOCK), F32), pltpu.VMEM((stacked, 3 * BLOCK), BF16),
                        pltpu.VMEM((stacked, 3 * BLOCK), BF16), pltpu.VMEM((stacked, HEAD_DIM), BF16),
                        pltpu.VMEM((stacked, HEAD_DIM), BF16), pltpu.VMEM((N_Q_HEADS, BLOCK, 1), F32)],
        compiler_params=_params(("arbitrary",)),
    )(qr, kr, kr, kr, vv, vv, vv, sink, do, yatt)


def _tap_grads_aligned(d_own, x_ref, d_ref, offsets, out_ref):
    rows = d_own.shape[0]
    padded = jnp.concatenate([d_own, jnp.zeros((SUBLANES, d_own.shape[1]), F32)], axis=0)
    for r in range(SUBLANES):
        d_ref[r] = padded if r == 0 else pltpu.roll(padded, r, 0)
    for j, o in enumerate(offsets):
        r = o % SUBLANES
        out_ref[j:j + 1, :] += jnp.sum(d_ref[r] * x_ref[pl.ds(o - r, rows + SUBLANES), :], axis=0, keepdims=True)


def _mix_bwd(z, c, dysc, dycc, dqr, dkp, dvp, scw, ccw, ccb, ccg, ccbb, cos, sin):
    t = z.shape[0]
    tm = min(MIX_BWD_TILE, t)
    nt = t // tm
    h = HALO_BWD
    half = CC_W // 2
    ext = tm + 2 * h

    def body(z_ref, zp_ref, zn_ref, c_ref, cp_ref, cn_ref, dsc_ref, dscp_ref, dscn_ref, dcc_ref, dccp_ref, dccn_ref,
             dq_ref, dk0_ref, dk1_ref, dk2_ref, dv0_ref, dv1_ref, dv2_ref,
             scw_ref, ccw_ref, ccb_ref, ccg_ref, ccbb_ref, cos_ref, sin_ref,
             dz_ref, dscw_ref, dccw_ref, dvec_ref, u_s, dc_s, ch_s, g_s, p_s, d_s):
        i = pl.program_id(0)
        first, last = i == 0, i == nt - 1

        @pl.when(first)
        def _():
            dscw_ref[...] = jnp.zeros_like(dscw_ref)
            dccw_ref[...] = jnp.zeros_like(dccw_ref)
            dvec_ref[...] = jnp.zeros_like(dvec_ref)

        pz = jnp.where(first, 0.0, zp_ref[...])
        nz = jnp.where(last, 0.0, zn_ref[...])
        zo = z_ref[...]

        def u_of(zz):
            return zz[:, O_CCA:O_CCA + D_CC] * _sigmoid(zz[:, O_CCG:O_CCG + D_CC])

        u_s[0:h, :] = u_of(pz)
        u_s[h:h + tm, :] = u_of(zo)
        u_s[h + tm:ext, :] = u_of(nz)
        c_ext = jnp.concatenate([jnp.where(first, 0.0, cp_ref[...]), c_ref[...], jnp.where(last, 0.0, cn_ref[...])], axis=0)
        xhat, rstd = _ln_stats(c_ext)
        nn = xhat * ccg_ref[...] + ccbb_ref[...]
        sg = _sigmoid(nn)
        dycc_ext = jnp.concatenate([jnp.where(first, 0.0, dccp_ref[...]), dcc_ref[...],
                                    jnp.where(last, 0.0, dccn_ref[...])], axis=0)
        dn = dycc_ext * (sg * (1.0 + nn * (1.0 - sg)))
        dc = _ln_bwd(dn, xhat, rstd, ccg_ref[...])
        dc_s[...] = dc
        dn_own = dn[h:h + tm, :]
        dc_own = dc[h:h + tm, :]
        dvec_ref[0:1, :] += jnp.sum(dc_own, axis=0, keepdims=True)
        dvec_ref[1:2, :] += jnp.sum(dn_own * xhat[h:h + tm, :], axis=0, keepdims=True)
        dvec_ref[2:3, :] += jnp.sum(dn_own, axis=0, keepdims=True)
        du = _taps_aligned(ccw_ref, dc_s, p_s, [h + half - j for j in range(CC_W)], tm)
        _tap_grads_aligned(dc_own, u_s, d_s, [h + j - half for j in range(CC_W)], dccw_ref)
        gate = _sigmoid(zo[:, O_CCG:O_CCG + D_CC])
        a_own = zo[:, O_CCA:O_CCA + D_CC]
        dz_ref[:, O_CCA:O_CCA + D_CC] = (du * gate).astype(BF16)
        dz_ref[:, O_CCG:O_CCG + D_CC] = (du * a_own * gate * (1.0 - gate)).astype(BF16)

        def ch_of(zz):
            return zz[:, O_SCC:O_SCC + D_SC] * zz[:, O_SCH:O_SCH + D_SC]

        ch_s[0:h, :] = ch_of(pz)
        ch_s[h:h + tm, :] = ch_of(zo)
        ch_s[h + tm:ext, :] = ch_of(nz)
        g_s[0:h, :] = jnp.where(first, 0.0, dscp_ref[...]) * pz[:, O_SCB:O_SCB + D_SC]
        g_s[h:h + tm, :] = dsc_ref[...] * zo[:, O_SCB:O_SCB + D_SC]
        g_s[h + tm:ext, :] = jnp.where(last, 0.0, dscn_ref[...]) * nz[:, O_SCB:O_SCB + D_SC]
        conv = jnp.zeros((tm, D_SC), F32)
        dch = jnp.zeros((tm, D_SC), F32)
        g_own = g_s[h:h + tm, :]
        for j in range(SC_W):
            chj = ch_s[pl.ds(h + j - SC_W // 2, tm), :]
            conv = conv + scw_ref[j:j + 1, :] * chj
            dch = dch + scw_ref[j:j + 1, :] * g_s[pl.ds(h + SC_W // 2 - j, tm), :]
            dscw_ref[j:j + 1, :] += jnp.sum(g_own * chj, axis=0, keepdims=True)
        dz_ref[:, O_SCB:O_SCB + D_SC] = (dsc_ref[...] * conv).astype(BF16)
        dz_ref[:, O_SCC:O_SCC + D_SC] = (dch * zo[:, O_SCH:O_SCH + D_SC]).astype(BF16)
        dz_ref[:, O_SCH:O_SCH + D_SC] = (dch * zo[:, O_SCC:O_SCC + D_SC]).astype(BF16)

        dq = dq_ref[...]
        dz_ref[:, O_Q:O_Q + D_ATT] = (dq * _wide(cos_ref[...], D_ATT) + _swap_halves(dq * _wide(sin_ref[...], D_ATT))).astype(BF16)
        dk = dk1_ref[0] + jnp.where(last, 0.0, dk0_ref[0]) + jnp.where(first, 0.0, dk2_ref[0])
        dz_ref[:, O_K:O_K + 128] = (dk * cos_ref[...] + _swap_halves(dk * sin_ref[...])).astype(BF16)
        dv = dv1_ref[0] + jnp.where(last, 0.0, dv0_ref[0]) + jnp.where(first, 0.0, dv2_ref[0])
        dz_ref[:, O_V:O_V + 128] = dv.astype(BF16)

    def full(a):
        return pl.BlockSpec(a.shape, lambda i: (0, 0))

    def rows(w):
        return pl.BlockSpec((tm, w), lambda i: (i, 0))

    parts = [pl.BlockSpec((1, tm, 128), lambda i: (0, jnp.minimum(i + 1, nt - 1), 0)),
             pl.BlockSpec((1, tm, 128), lambda i: (1, i, 0)),
             pl.BlockSpec((1, tm, 128), lambda i: (2, jnp.maximum(i - 1, 0), 0))]
    acc_spec = lambda r: pl.BlockSpec((r, D_CC), lambda i: (0, 0))
    return pl.pallas_call(
        body, name="mix_bwd", grid=(nt,),
        in_specs=(_halo_specs(t, tm, h, D_IN) + _halo_specs(t, tm, h, D_CC) + _halo_specs(t, tm, h, D_SC)
                  + _halo_specs(t, tm, h, D_CC) + [rows(D_ATT)] + parts + parts
                  + [full(scw), full(ccw), full(ccb), full(ccg), full(ccbb), rows(128), rows(128)]),
        out_specs=[rows(D_IN), acc_spec(SC_W), acc_spec(CC_W), acc_spec(3)],
        out_shape=[S((t, D_IN), BF16), S((SC_W, D_SC), F32), S((CC_W, D_CC), F32), S((3, D_CC), F32)],
        scratch_shapes=[pltpu.VMEM((ext, D_CC), F32), pltpu.VMEM((ext, D_CC), F32),
                        pltpu.VMEM((ext, D_SC), F32), pltpu.VMEM((ext, D_SC), F32),
                        pltpu.VMEM((SUBLANES, tm + SUBLANES, D_CC), F32), pltpu.VMEM((SUBLANES, tm + SUBLANES, D_CC), F32)],
        compiler_params=_params(("arbitrary",)),
    )(z, z, z, c, c, c, dysc, dysc, dysc, dycc, dycc, dycc, dqr, dkp, dkp, dkp, dvp, dvp, dvp,
      scw, ccw, ccb, ccg, ccbb, cos, sin)


def _adamw(w, g, m, v):
    m = ADAM_B1 * m + (1.0 - ADAM_B1) * g
    v = ADAM_B2 * v + (1.0 - ADAM_B2) * (g * g)
    m_hat = m / (1.0 - ADAM_B1 ** ADAM_STEP)
    v_hat = v / (1.0 - ADAM_B2 ** ADAM_STEP)
    delta = -ADAM_LR * (m_hat / (jnp.sqrt(v_hat) + ADAM_EPS) + ADAM_WD * w)
    return delta, m, v


def _row_tile(rows):
    for cand in (256, 176, 128):
        if rows % cand == 0:
            return cand
    return rows


def _sum_adam(recv, w, m, v, transposed):
    nl, rows = len(recv), recv[0].shape[1]
    tile = 256 if transposed else _row_tile(rows)
    nc = (D if transposed else rows) // tile

    def body(*refs):
        w_ref, m_ref, v_ref, g_ref, d_ref, mo_ref, vo_ref = refs[nl:]
        for layer in range(nl):
            @pl.when(pl.program_id(0) == layer)
            def _(r_ref=refs[layer]):
                g = r_ref[0].astype(F32)
                for s in range(1, N_DEV):
                    g = g + r_ref[s].astype(F32)
                if transposed:
                    g = g.T
                g_ref[0] = g
                d_ref[0], mo_ref[0], vo_ref[0] = _adamw(w_ref[0], g, m_ref[0], v_ref[0])

    def held(layer):
        def at(l, c):
            return jnp.where(l == layer, c, jnp.where(l < layer, 0, nc - 1))
        if transposed:
            return pl.BlockSpec((N_DEV, rows, tile), lambda l, c: (0, 0, at(l, c)))
        return pl.BlockSpec((N_DEV, tile, D), lambda l, c: (0, at(l, c), 0))

    if transposed:
        blk = pl.BlockSpec((1, tile, rows), lambda l, c: (l, c, 0))
    else:
        blk = pl.BlockSpec((1, tile, D), lambda l, c: (l, c, 0))
    out = S(w.shape, F32)
    return pl.pallas_call(
        body, name="sum_adam_t" if transposed else "sum_adam", grid=(nl, nc),
        in_specs=[held(layer) for layer in range(nl)] + [blk, blk, blk], out_specs=[blk] * 4, out_shape=[out] * 4,
        compiler_params=_params(("arbitrary", "arbitrary")),
    )(*recv, w, m, v)


def _small_sum(gathered):
    rows = gathered.shape[1]

    def body(g_ref, o_ref):
        acc = g_ref[0]
        for s in range(1, N_DEV):
            acc = acc + g_ref[s]
        o_ref[...] = acc

    return pl.pallas_call(
        body, name="small_sum", in_specs=[pl.BlockSpec(gathered.shape, lambda: (0, 0, 0))],
        out_specs=pl.BlockSpec((rows, 128), lambda: (0, 0)), out_shape=S((rows, 128), F32),
    )(gathered)


def _small_adam(w, g, m, v):
    def body(w_ref, g_ref, m_ref, v_ref, d_ref, mo_ref, vo_ref):
        d_ref[...], mo_ref[...], vo_ref[...] = _adamw(w_ref[...], g_ref[...], m_ref[...], v_ref[...])

    spec = pl.BlockSpec(w.shape, lambda: (0, 0))
    return pl.pallas_call(
        body, name="small_adam", in_specs=[spec] * 4, out_specs=[spec] * 3, out_shape=[S(w.shape, F32)] * 3,
    )(w, g, m, v)


def _pack(pieces):
    flat = jnp.concatenate([p.reshape(-1).astype(F32) for p in pieces])
    n = flat.shape[0]
    rows = -(-n // 1024) * 8
    return jnp.pad(flat, (0, rows * 128 - n)).reshape(rows, 128)


def _unpack(packed, shapes):
    flat = packed.reshape(-1)
    out, o = [], 0
    for shp in shapes:
        n = int(np.prod(shp))
        out.append(flat[o:o + n].reshape(shp))
        o += n
    return out


def _rope_tables(t):
    half = HEAD_DIM // 2
    inv_freq = ROPE_THETA ** (-jnp.arange(half, dtype=F32) / half)
    ang = jnp.arange(t).astype(F32)[:, None] * jnp.tile(inv_freq, 128 // half)[None, :]
    sign = jnp.tile(jnp.concatenate([-jnp.ones((half,), F32), jnp.ones((half,), F32)]), 128 // HEAD_DIM)
    return jnp.cos(ang), jnp.sin(ang) * sign[None, :]


BIG = ("ffn1_w_gu", "ffn1_w_down", "w_in", "w_out", "ffn2_w_gu", "ffn2_w_down")
BIG_T = {"ffn1_w_gu": True, "ffn1_w_down": False, "w_in": True, "w_out": False, "ffn2_w_gu": True, "ffn2_w_down": False}
SWAPPED = ("ffn1_w_gu", "ffn2_w_gu")
REPLICATED = ("ln1_g", "ln1_b", "attn_sink", "cc_conv_b", "cc_ln_g", "cc_ln_b", "ln2_g", "ln2_b", "ln3_g", "ln3_b")
CONVS = ("sc_conv_w", "cc_conv_w")
WEIGHTS = ("ffn1_w_gu", "ffn1_w_down", "ln1_g", "ln1_b", "w_in", "sc_conv_w", "attn_sink", "cc_conv_w", "cc_conv_b",
           "cc_ln_g", "cc_ln_b", "w_out", "ln2_g", "ln2_b", "ffn2_w_gu", "ffn2_w_down", "ln3_g", "ln3_b")


def kernel(x, ffn1_w_gu, ffn1_w_down, ln1_g, ln1_b, w_in, sc_conv_w, attn_sink, cc_conv_w, cc_conv_b, cc_ln_g, cc_ln_b, w_out, ln2_g, ln2_b, ffn2_w_gu, ffn2_w_down, ln3_g, ln3_b, loss_target, m_ffn1_w_gu, m_ffn1_w_down, m_ln1_g, m_ln1_b, m_w_in, m_sc_conv_w, m_attn_sink, m_cc_conv_w, m_cc_conv_b, m_cc_ln_g, m_cc_ln_b, m_w_out, m_ln2_g, m_ln2_b, m_ffn2_w_gu, m_ffn2_w_down, m_ln3_g, m_ln3_b, v_ffn1_w_gu, v_ffn1_w_down, v_ln1_g, v_ln1_b, v_w_in, v_sc_conv_w, v_attn_sink, v_cc_conv_w, v_cc_conv_b, v_cc_ln_g, v_cc_ln_b, v_w_out, v_ln2_g, v_ln2_b, v_ffn2_w_gu, v_ffn2_w_down, v_ln3_g, v_ln3_b):
    args = dict(locals())
    w = {n: args[n] for n in WEIGHTS}
    mom = {n: args["m_" + n] for n in WEIGHTS}
    var = {n: args["v_" + n] for n in WEIGHTS}
    x0 = x[0]
    target = loss_target[0]
    t = x0.shape[0]
    idx = 4 * lax.axis_index("x") + 2 * lax.axis_index("y") + lax.axis_index("c")

    blocks = {(n, l): (w[n][l].T if BIG_T[n] else w[n][l]).astype(BF16) for l in range(DEPTH) for n in BIG}
    where = {}

    def start_stage(tag, members, after, extra=()):
        srcs = list(extra) + [blocks[m] for m in members]
        started = _send_start(srcs, [_own_slot(s) for s in srcs], _whole, f"gather_start_{tag}", after)
        for j, m in enumerate(members):
            where[m] = (started, len(extra) + j)
        return started

    def wait_stage(started, k, after, name):
        send, rcv, srcs, lands, _ = started
        return _recv_wait(send, rcv, [k], [srcs[k]], [lands[k]], _whole, after, name)[0]

    def weight(n, l, after):
        g = wait_stage(*where[n, l], after, f"gather_wait_{n}_{l}")
        return g.reshape(N_DEV * g.shape[1], g.shape[2])

    first = start_stage("a", [("ffn1_w_gu", 0)], x0, extra=[_pack([w["sc_conv_w"], w["cc_conv_w"]])])
    res, xb = (x0, None), x0.astype(BF16)
    cos, sin = _rope_tables(t)
    conv_all = wait_stage(first, 0, (xb, cos, sin), "gather_wait_convs").reshape(N_DEV, -1)
    n_sc = DEPTH * SC_W * 32
    scw_full = conv_all[:, :n_sc].reshape(N_DEV, DEPTH, SC_W, 32).transpose(1, 2, 0, 3).reshape(DEPTH, SC_W, D_SC)
    ccw_full = conv_all[:, n_sc:n_sc + DEPTH * CC_W * 32].reshape(N_DEV, DEPTH, CC_W, 32).transpose(1, 2, 0, 3).reshape(DEPTH, CC_W, D_CC)

    row = lambda a, l: a[l].reshape(1, -1)

    saved, full = [], {}
    for l in range(DEPTH):
        sv = {"x0b": xb}
        token = None
        full["ffn1_w_gu", l] = weight("ffn1_w_gu", l, (xb, scw_full, ccw_full) if l == 0 else xb)
        if l == 0:
            token = start_stage("b", [("ffn1_w_down", 0), ("w_in", 0), ("w_out", 0)], full["ffn1_w_gu", l])[-1]
        gu1, a1 = _ffn_up(xb, full["ffn1_w_gu", l], token)
        full["ffn1_w_down", l] = weight("ffn1_w_down", l, a1)
        if l == 0:
            token = start_stage("c", [("ffn2_w_gu", 0), ("ffn2_w_down", 0)], full["ffn1_w_down", l])[-1]
        r1, x1b = _ffn_down_ln(a1, full["ffn1_w_down", l], res[0], row(ln1_g, l), row(ln1_b, l), res[1], token)
        full["w_in", l] = weight("w_in", l, x1b)
        z = _proj_in(x1b, full["w_in", l])
        ysc, ycc, cpre, qr, kr, vv = _mix_fwd(z, scw_full[l], ccw_full[l], row(cc_conv_b, l), row(cc_ln_g, l), row(cc_ln_b, l), cos, sin)
        if l == 0:
            token = start_stage("d", [("ffn1_w_gu", 1), ("ffn1_w_down", 1)], ysc)[-1]
        yatt = _attn_fwd(qr, kr, vv, attn_sink[l], token)
        full["w_out", l] = weight("w_out", l, yatt)
        ycat, r2, x2b = _out_ln(ysc, yatt, ycc, full["w_out", l], r1, row(ln2_g, l), row(ln2_b, l),
                                (row(ln1_g, l), row(ln1_b, l)))
        full["ffn2_w_gu", l] = weight("ffn2_w_gu", l, x2b)
        if l == 0:
            token = start_stage("e", [("w_in", 1), ("w_out", 1), ("ffn2_w_gu", 1), ("ffn2_w_down", 1)], full["ffn2_w_gu", l])[-1]
        gu2, a2 = _ffn_up(x2b, full["ffn2_w_gu", l], token)
        full["ffn2_w_down", l] = weight("ffn2_w_down", l, a2)
        if l + 1 < DEPTH:
            r3, xb = _ffn_down_ln(a2, full["ffn2_w_down", l], r2, row(ln3_g, l), row(ln3_b, l),
                                  (row(ln2_g, l), row(ln2_b, l)))
            res = (r3, (row(ln3_g, l), row(ln3_b, l)))
        else:
            r3, dy, sq = _ffn_down_ln_loss(a2, full["ffn2_w_down", l], r2, row(ln3_g, l), row(ln3_b, l),
                                           (row(ln2_g, l), row(ln2_b, l)), target)
        sv.update(gu1=gu1, a1=a1, r1=r1, x1b=x1b, z=z, cpre=cpre, qr=qr, kr=kr, vv=vv, yatt=yatt, ycat=ycat, r2=r2, x2b=x2b,
                  gu2=gu2, a2=a2, r3=r3)
        saved.append(sv)

    loss = lax.psum(0.5 * jnp.sum(sq) / D, ("x", "y", "c"))

    sent = []
    small = {n: [None] * DEPTH for n in REPLICATED + CONVS}

    def send_grads(names, l, gs):
        srcs = [g.reshape(N_DEV, g.shape[0] // N_DEV, g.shape[1]) for g in gs]
        lands = [_own_slot(lax.dynamic_index_in_dim(s3, idx, 0, keepdims=False)) for s3 in srcs]
        started = _send_start(srcs, lands, _block_of, f"grads_start_{names[0]}_{l}", gs[-1])
        sent.append((names, l, started))
        return started[-1]

    token = None
    for l in reversed(range(DEPTH)):
        sv = saved[l]
        dy, dfb, dh, dg, db = _ffn_bwd_dx(dy, sv["r3"], row(ln3_g, l), full["ffn2_w_down", l], sv["gu2"],
                                          full["ffn2_w_gu", l], token)
        small["ln3_g"][l], small["ln3_b"][l] = dg, db
        token = send_grads(("ffn2_w_down", "ffn2_w_gu"), l,
                           [_wgrad(sv["a2"], dfb, F // 2), _wgrad(dh, sv["x2b"], F // 2)])

        dr, dmb, dysc, dyatt, dycc, dg, db = _out_bwd(dy, sv["r2"], row(ln2_g, l), full["w_out", l], token)
        small["ln2_g"][l], small["ln2_b"][l] = dg, db
        g_out = _wgrad(sv["ycat"], dmb, D)
        dqr, dkp, dvp, dsink = _attn_bwd(sv["qr"], sv["kr"], sv["vv"], attn_sink[l], dyatt, sv["yatt"])
        small["attn_sink"][l] = dsink[:, 0]
        dz, dscw, dccw, dvec = _mix_bwd(sv["z"], sv["cpre"], dysc, dycc, dqr, dkp, dvp, scw_full[l], ccw_full[l],
                                        row(cc_conv_b, l), row(cc_ln_g, l), row(cc_ln_b, l), cos, sin)
        small["sc_conv_w"][l], small["cc_conv_w"][l] = dscw, dccw
        small["cc_conv_b"][l], small["cc_ln_g"][l], small["cc_ln_b"][l] = dvec[0], dvec[1], dvec[2]
        token = send_grads(("w_out", "w_in"), l, [g_out, _wgrad(dz, sv["x1b"], D)])
        dy = _dx(dr, dz, full["w_in", l], token)

        if l > 0:
            dy, dfb, dh, dg, db = _ffn_bwd_dx(dy, sv["r1"], row(ln1_g, l), full["ffn1_w_down", l], sv["gu1"],
                                              full["ffn1_w_gu", l])
            token = send_grads(("ffn1_w_down", "ffn1_w_gu"), l,
                               [_wgrad(sv["a1"], dfb, F // 2), _wgrad(dh, sv["x0b"], F // 2)])
        else:
            dr, dfb, dh, dg, db = _ffn_bwd(dy, sv["r1"], row(ln1_g, l), full["ffn1_w_down", l], sv["gu1"])
            token = send_grads(("ffn1_w_gu",), l, [_wgrad(dh, sv["x0b"], F // 2)])
            token = send_grads(("ffn1_w_down",), l, [_wgrad(sv["a1"], dfb, F // 2, token)])
            dy = _dx(dr, dh, full["ffn1_w_gu", l], token)
        small["ln1_g"][l], small["ln1_b"][l] = dg, db
    grad_x = dy[None]

    small_names = REPLICATED + CONVS
    small_shapes = [(DEPTH,) + tuple(np.shape(small[n][0].reshape(-1))) for n in small_names]
    small_pack = _pack([jnp.stack([small[n][l].reshape(-1) for l in range(DEPTH)]) for n in small_names])
    small_all = _all_gather([small_pack], "gather_small_grads")[0]

    recv = {n: [None] * DEPTH for n in BIG}
    grads, deltas, new_m, new_v = {}, {}, {}, {}

    def receive(upto, after):
        while len(sent) > upto:
            names, l, (send, rcv, srcs, lands, _) = sent.pop(0)
            got = _recv_wait(send, rcv, list(range(len(names))), srcs, lands, _block_of, after, f"grads_wait_{names[0]}_{l}")
            for n, g in zip(names, got):
                recv[n][l] = g

    def update(n):
        if n in SWAPPED:
            outs = _sum_adam(recv[n], *[jnp.swapaxes(a, 1, 2) for a in (w[n], mom[n], var[n])], False)
            grads[n], deltas[n], new_m[n], new_v[n] = [jnp.swapaxes(a, 1, 2) for a in outs]
        else:
            grads[n], deltas[n], new_m[n], new_v[n] = _sum_adam(recv[n], w[n], mom[n], var[n], BIG_T[n])

    receive(2, dy)
    for n in ("ffn2_w_down", "ffn2_w_gu", "w_out", "w_in"):
        update(n)
    receive(0, new_v["w_in"])
    update("ffn1_w_gu")
    update("ffn1_w_down")
    small_total = _unpack(_small_sum(small_all), small_shapes)
    for n, g in zip(small_names, small_total):
        if n in CONVS:
            taps = SC_W if n == "sc_conv_w" else CC_W
            g = lax.dynamic_slice_in_dim(g.reshape(DEPTH, taps, D_SC), idx * 32, 32, axis=2)
        grads[n] = g.reshape(w[n].shape)
    wp = _pack([w[n] for n in small_names])
    gp = _pack([grads[n] for n in small_names])
    mp = _pack([mom[n] for n in small_names])
    vp = _pack([var[n] for n in small_names])
    shapes = [w[n].shape for n in small_names]
    for dst, packed in zip((deltas, new_m, new_v), _small_adam(wp, gp, mp, vp)):
        for n, a in zip(small_names, _unpack(packed, shapes)):
            dst[n] = a

    return (loss, grad_x, *[grads[n] for n in WEIGHTS], *[deltas[n] for n in WEIGHTS],
            *[new_m[n] for n in WEIGHTS], *[new_v[n] for n in WEIGHTS])
```

```python
import functools

import jax
import jax.numpy as jnp
import numpy as np
from jax import lax
from jax.experimental import pallas as pl
from jax.experimental.pallas import tpu as pltpu

F32 = jnp.float32
BF16 = jnp.bfloat16
S = jax.ShapeDtypeStruct

N_DEV = 8
DEPTH = 2
D = 1024
F = 2816
D_IN = 2048
HEAD_DIM = 64
N_Q_HEADS = 8
N_KV_HEADS = 2
GROUP = 4
D_SC = 256
D_ATT = 512
D_CC = 256
CC_W = 31
SC_W = 3
BLOCK = 128
ROPE_THETA = 10000.0
LN_EPS = 1e-5
ALPHA = (2.0 * DEPTH) ** 0.25
ADAM_LR = 0.001
ADAM_B1 = 0.9
ADAM_B2 = 0.999
ADAM_EPS = 1e-08
ADAM_WD = 0.01
ADAM_STEP = 10

O_SCB, O_SCC, O_SCH, O_Q, O_K, O_V, O_CCA, O_CCG = 0, 256, 512, 768, 1280, 1408, 1536, 1792

V7X_VMEM_BYTES = 64 * 1024 * 1024
VMEM_LIMIT = V7X_VMEM_BYTES - 8 * 1024 * 1024
TOKEN_TILE = 256
WIDE_TILE = 512
BIG_TILE = 1024
WGRAD_TOKENS = 2048
FFN_CHUNKS = (0, 768, 1536, 2176, 2816)
DX_COLS = 256
MIX_BWD_TILE = 128
HALO_FWD = 16
HALO_BWD = 16
ATT_Q_BLOCKS = 4
CONV_ROWS = 128
SUBLANES = 8

NT = (((1,), (1,)), ((), ()))
TN = (((0,), (0,)), ((), ()))
MESH = pl.DeviceIdType.MESH


def _params(sem=None):
    return pltpu.CompilerParams(dimension_semantics=sem, vmem_limit_bytes=VMEM_LIMIT)


def _sigmoid(v):
    return 1.0 / (1.0 + jnp.exp(-v))


def _ln_stats(r):
    mu = jnp.mean(r, axis=-1, keepdims=True)
    d = r - mu
    var = jnp.mean(d * d, axis=-1, keepdims=True)
    rstd = lax.rsqrt(var + LN_EPS)
    return d * rstd, rstd


def _ln_bwd(dn, xhat, rstd, gam):
    dxh = dn * gam
    return rstd * (dxh - jnp.mean(dxh, axis=-1, keepdims=True) - xhat * jnp.mean(dxh * xhat, axis=-1, keepdims=True))


def _swap_halves(v):
    n = v.shape[-1]
    lane = lax.broadcasted_iota(jnp.int32, v.shape, v.ndim - 1) % HEAD_DIM
    return jnp.where(lane < HEAD_DIM // 2, pltpu.roll(v, n - HEAD_DIM // 2, v.ndim - 1), pltpu.roll(v, HEAD_DIM // 2, v.ndim - 1))


def _wide(tab, n):
    return tab if n == 128 else jnp.concatenate([tab] * (n // 128), axis=1)


def _me():
    x, y, c = lax.axis_index("x"), lax.axis_index("y"), lax.axis_index("c")
    return x, y, c


def _peer(rel):
    x, y, c = _me()
    px = 1 - x if rel & 4 else x
    py = 1 - y if rel & 2 else y
    pc = 1 - c if rel & 1 else c
    return (px, py, pc), 4 * px + 2 * py + pc


def _exchange(srcs, dsts_shape, dst_index, src_of, dst_of, name):
    n = len(srcs)

    def body(*refs):
        ins = refs[:n]
        outs = [refs[n + dst_index[k]] for k in range(n)]
        send, recv, lsem = refs[n + len(dsts_shape):]
        x, y, c = _me()
        me = 4 * x + 2 * y + c
        local = [pltpu.make_async_copy(src_of(ins[k], k, me), dst_of(outs[k], k, me), lsem.at[k]) for k in range(n)]
        for cp in local:
            cp.start()
        sends, recvs = [], []
        for k in range(n):
            for rel in range(1, N_DEV):
                peer, pidx = _peer(rel)
                sends.append(pltpu.make_async_remote_copy(
                    src_ref=src_of(ins[k], k, pidx), dst_ref=dst_of(outs[k], k, me),
                    send_sem=send.at[k, rel - 1], recv_sem=recv.at[k, rel - 1], device_id=peer, device_id_type=MESH))
                recvs.append(pltpu.make_async_remote_copy(
                    src_ref=src_of(ins[k], k, pidx), dst_ref=dst_of(outs[k], k, pidx),
                    send_sem=send.at[k, rel - 1], recv_sem=recv.at[k, rel - 1], device_id=peer, device_id_type=MESH))
        for cp in sends:
            cp.start()
        for cp in recvs:
            cp.wait_recv()
        for cp in sends:
            cp.wait_send()
        for cp in local:
            cp.wait()

    hbm = pl.BlockSpec(memory_space=pltpu.HBM)
    return pl.pallas_call(
        body, name=name, in_specs=[hbm] * n, out_specs=[hbm] * len(dsts_shape), out_shape=dsts_shape,
        scratch_shapes=[pltpu.SemaphoreType.DMA((n, N_DEV - 1)), pltpu.SemaphoreType.DMA((n, N_DEV - 1)),
                        pltpu.SemaphoreType.DMA((n,))],
    )(*srcs)


def _all_gather(blocks, name):
    shapes = [S((N_DEV,) + b.shape, b.dtype) for b in blocks]
    return _exchange(blocks, shapes, list(range(len(blocks))), lambda ref, k, idx: ref, lambda ref, k, idx: ref.at[idx], name)


HBM_SPEC = pl.BlockSpec(memory_space=pltpu.HBM)
SEM_SPEC = pl.BlockSpec(memory_space=pltpu.SEMAPHORE)
ANY_SPEC = pl.BlockSpec(memory_space=pl.ANY)
EFFECT = pltpu.SideEffectType.DATAFLOW_SIDE_EFFECTING
N_PEERS = N_DEV - 1


def _own_slot(block):
    x, y, c = _me()
    return lax.dynamic_update_index_in_dim(lax.empty((N_DEV,) + block.shape, block.dtype), block, 4 * x + 2 * y + c, 0)


def _follow(body, n_in, in_specs, operands, after):
    if after is None:
        return body, list(in_specs), list(operands)

    def tail(*refs):
        return body(*refs[:n_in], *refs[n_in + 1:])

    return tail, list(in_specs) + [ANY_SPEC], list(operands) + [after]


def _send_start(srcs, lands, src_of, name, after):
    n = len(srcs)

    def body(*refs):
        ins, zones = refs[:n], refs[n:2 * n]
        send, recv = refs[2 * n + 1], refs[2 * n + 2]
        token = refs[-1]
        x, y, c = _me()
        me = 4 * x + 2 * y + c
        for k in range(n):
            for rel in range(1, N_DEV):
                peer, pidx = _peer(rel)
                pltpu.make_async_remote_copy(
                    src_ref=src_of(ins[k], pidx), dst_ref=zones[k].at[me],
                    send_sem=send.at[k * N_PEERS + rel - 1], recv_sem=recv.at[k * N_PEERS + rel - 1],
                    device_id=peer, device_id_type=MESH).start()
        token[...] = jnp.zeros_like(token)

    outs = pl.pallas_call(
        body, name=name,
        out_shape=(pltpu.SemaphoreType.DMA((n * N_PEERS,)), pltpu.SemaphoreType.DMA((n * N_PEERS,)),
                   *[pltpu.HBM(a.shape, a.dtype) for a in lands], S((8, 128), F32)),
        in_specs=[HBM_SPEC] * (2 * n) + [ANY_SPEC],
        out_specs=(SEM_SPEC, SEM_SPEC, *[HBM_SPEC] * n, pl.BlockSpec(memory_space=pltpu.VMEM)),
        input_output_aliases={n + i: 2 + i for i in range(n)},
        compiler_params=pltpu.CompilerParams(has_side_effects=EFFECT),
    )(*[pltpu.with_memory_space_constraint(a, pltpu.HBM) for a in list(srcs) + list(lands)], after)
    return outs[0], outs[1], list(srcs), list(outs[2:2 + n]), outs[-1]


def _recv_wait(send, recv, ks, srcs, lands, src_of, after, name):
    n = len(ks)
    after = after if isinstance(after, (tuple, list)) else (after,)

    def body(*refs):
        ins, zones = refs[:n], refs[n:2 * n]
        send_sems, recv_sems = refs[2 * n], refs[2 * n + 1]
        for j, k in enumerate(ks):
            for rel in range(1, N_DEV):
                peer, pidx = _peer(rel)
                cp = pltpu.make_async_remote_copy(
                    src_ref=src_of(ins[j], pidx), dst_ref=zones[j].at[pidx],
                    send_sem=send_sems.at[k * N_PEERS + rel - 1], recv_sem=recv_sems.at[k * N_PEERS + rel - 1],
                    device_id=peer, device_id_type=MESH)
                cp.wait_send()
                cp.wait_recv()

    outs = pl.pallas_call(
        body, name=name,
        out_shape=[pltpu.HBM(a.shape, a.dtype) for a in lands],
        in_specs=[HBM_SPEC] * (2 * n) + [SEM_SPEC, SEM_SPEC] + [ANY_SPEC] * len(after), out_specs=[HBM_SPEC] * n,
        input_output_aliases={n + i: i for i in range(n)},
        compiler_params=pltpu.CompilerParams(has_side_effects=EFFECT),
    )(*srcs, *lands, send, recv, *after)
    return list(outs)


def _whole(ref, idx):
    return ref


def _block_of(ref, idx):
    return ref.at[idx]


def _cast(x, after):
    t, d = x.shape
    tm = min(BIG_TILE, t)

    def body(x_ref, after_ref, o_ref):
        o_ref[...] = x_ref[...].astype(BF16)

    row = pl.BlockSpec((tm, d), lambda i: (i, 0))
    return pl.pallas_call(
        body, name="cast", grid=(t // tm,), in_specs=[row, ANY_SPEC], out_specs=row, out_shape=S((t, d), BF16),
        compiler_params=_params(("parallel",)),
    )(x, after)


def _resident(shape):
    return pl.BlockSpec(shape, lambda i: (0,) * len(shape), pipeline_mode=pl.Buffered(1))


def _ffn_up(xb, wgut, after=None):
    t = xb.shape[0]
    tm = min(WIDE_TILE, t)
    half = F // 2

    def body(x_ref, w_ref, gu_ref, a_ref):
        x = x_ref[...]
        for ch in range(2):
            lo = ch * half
            g = lax.dot_general(x, w_ref[lo:lo + half, :], NT, preferred_element_type=F32)
            u = lax.dot_general(x, w_ref[F + lo:F + lo + half, :], NT, preferred_element_type=F32)
            gu_ref[:, lo:lo + half] = g.astype(BF16)
            gu_ref[:, F + lo:F + lo + half] = u.astype(BF16)
            a_ref[:, lo:lo + half] = (g * _sigmoid(g) * u).astype(BF16)

    body, in_specs, operands = _follow(
        body, 2, [pl.BlockSpec((tm, D), lambda i: (i, 0)), _resident((2 * F, D))], [xb, wgut], after)
    return pl.pallas_call(
        body, name="ffn_up", grid=(t // tm,), in_specs=in_specs,
        out_specs=[pl.BlockSpec((tm, 2 * F), lambda i: (i, 0)), pl.BlockSpec((tm, F), lambda i: (i, 0))],
        out_shape=[S((t, 2 * F), BF16), S((t, F), BF16)], compiler_params=_params(("parallel",)),
    )(*operands)


def _residual(x_ref, src_ln):
    if src_ln is None:
        return x_ref[...]
    xhat, _ = _ln_stats(x_ref[...])
    return xhat * src_ln[0][...] + src_ln[1][...]


def _ffn_down_ln(a, wd, x, gam, bet, src_ln=None, after=None):
    t = x.shape[0]
    tm = min(BIG_TILE, t)
    n_ln = 0 if src_ln is None else 2

    def body(*refs):
        a_ref, w_ref, x_ref, g_ref, b_ref = refs[:5]
        r_ref, yb_ref = refs[5 + n_ln:]
        f = jnp.dot(a_ref[...], w_ref[...], preferred_element_type=F32)
        r = ALPHA * _residual(x_ref, refs[5:5 + n_ln] or None) + 0.5 * f
        r_ref[...] = r
        xhat, _ = _ln_stats(r)
        yb_ref[...] = (xhat * g_ref[...] + b_ref[...]).astype(BF16)

    row = pl.BlockSpec((tm, D), lambda i: (i, 0))
    vec = pl.BlockSpec((1, D), lambda i: (0, 0))
    body, in_specs, operands = _follow(
        body, 5 + n_ln, [pl.BlockSpec((tm, F), lambda i: (i, 0)), _resident((F, D)), row, vec, vec] + [vec] * n_ln,
        [a, wd, x, gam, bet] + list(src_ln or ()), after)
    return pl.pallas_call(
        body, name="ffn_down_ln", grid=(t // tm,), in_specs=in_specs,
        out_specs=[row, row], out_shape=[S((t, D), F32), S((t, D), BF16)],
        compiler_params=_params(("parallel",)),
    )(*operands)


def _ffn_down_ln_loss(a, wd, x, gam, bet, src_ln, target):
    t = x.shape[0]
    tm = min(BIG_TILE, t)

    def body(a_ref, w_ref, x_ref, g_ref, b_ref, sg_ref, sb_ref, t_ref, r_ref, dy_ref, part_ref):
        f = jnp.dot(a_ref[...], w_ref[...], preferred_element_type=F32)
        r = ALPHA * _residual(x_ref, (sg_ref, sb_ref)) + 0.5 * f
        r_ref[...] = r
        xhat, _ = _ln_stats(r)
        e = xhat * g_ref[...] + b_ref[...] - t_ref[...]
        dy_ref[...] = e / D

        @pl.when(pl.program_id(0) == 0)
        def _():
            part_ref[...] = jnp.zeros_like(part_ref)

        part_ref[...] += jnp.sum(e * e, axis=0, keepdims=True)

    row = pl.BlockSpec((tm, D), lambda i: (i, 0))
    vec = pl.BlockSpec((1, D), lambda i: (0, 0))
    return pl.pallas_call(
        body, name="ffn_down_ln_loss", grid=(t // tm,),
        in_specs=[pl.BlockSpec((tm, F), lambda i: (i, 0)), _resident((F, D)), row, vec, vec, vec, vec, row],
        out_specs=[row, row, vec], out_shape=[S((t, D), F32), S((t, D), F32), S((1, D), F32)],
        compiler_params=_params(("arbitrary",)),
    )(a, wd, x, gam, bet, *src_ln, target)


def _proj_in(xb, wint):
    t = xb.shape[0]
    tm = min(BIG_TILE, t)

    def body(x_ref, w_ref, z_ref):
        z_ref[...] = lax.dot_general(x_ref[...], w_ref[...], NT, preferred_element_type=F32)

    return pl.pallas_call(
        body, name="proj_in", grid=(t // tm,),
        in_specs=[pl.BlockSpec((tm, D), lambda i: (i, 0)), _resident((D_IN, D))],
        out_specs=pl.BlockSpec((tm, D_IN), lambda i: (i, 0)), out_shape=S((t, D_IN), F32),
        compiler_params=_params(("parallel",)),
    )(xb, wint)


def _halo_specs(t, tm, halo, width):
    per = tm // halo
    last = t // halo - 1
    return [pl.BlockSpec((tm, width), lambda i: (i, 0)),
            pl.BlockSpec((halo, width), lambda i: (jnp.maximum(i * per - 1, 0), 0)),
            pl.BlockSpec((halo, width), lambda i: (jnp.minimum((i + 1) * per, last), 0))]


def _taps_aligned(w_ref, x_ref, p_ref, offsets, rows):
    for r in range(SUBLANES):
        acc = jnp.zeros((rows + SUBLANES, x_ref.shape[1]), F32)
        for j, o in enumerate(offsets):
            if o % SUBLANES == r:
                acc = acc + w_ref[j:j + 1, :] * x_ref[pl.ds(o - r, rows + SUBLANES), :]
        p_ref[r] = acc
    out = p_ref[0, 0:rows, :]
    for r in range(1, SUBLANES):
        out = out + p_ref[r, pl.ds(r, rows), :]
    return out


def _mix_fwd(z, scw, ccw, ccb, ccg, ccbb, cos, sin):
    t = z.shape[0]
    tm = min(WIDE_TILE, t)
    nt = t // tm
    h = HALO_FWD
    rc = min(CONV_ROWS, tm)

    def body(z_ref, zp_ref, zn_ref, scw_ref, ccw_ref, ccb_ref, ccg_ref, ccbb_ref, cos_ref, sin_ref,
             ysc_ref, ycc_ref, c_ref, q_ref, k_ref, v_ref, u_s, ch_s, p_s):
        i = pl.program_id(0)
        pz = jnp.where(i == 0, 0.0, zp_ref[...])
        nz = jnp.where(i == nt - 1, 0.0, zn_ref[...])

        def u_of(zz):
            return zz[:, O_CCA:O_CCA + D_CC] * _sigmoid(zz[:, O_CCG:O_CCG + D_CC])

        def ch_of(zz):
            return zz[:, O_SCC:O_SCC + D_SC] * zz[:, O_SCH:O_SCH + D_SC]

        u_s[0:h, :] = u_of(pz)
        u_s[h:h + tm, :] = z_ref[:, O_CCA:O_CCA + D_CC] * _sigmoid(z_ref[:, O_CCG:O_CCG + D_CC])
        u_s[h + tm:2 * h + tm, :] = u_of(nz)
        ch_s[0:h, :] = ch_of(pz)
        ch_s[h:h + tm, :] = z_ref[:, O_SCC:O_SCC + D_SC] * z_ref[:, O_SCH:O_SCH + D_SC]
        ch_s[h + tm:2 * h + tm, :] = ch_of(nz)
        for r0 in range(0, tm, rc):
            c = _taps_aligned(ccw_ref, u_s, p_s, [r0 + h + j - CC_W // 2 for j in range(CC_W)], rc) + ccb_ref[...]
            c_ref[r0:r0 + rc, :] = c
            xhat, _ = _ln_stats(c)
            n = xhat * ccg_ref[...] + ccbb_ref[...]
            ycc_ref[r0:r0 + rc, :] = (n * _sigmoid(n)).astype(BF16)
            acc = jnp.zeros((rc, D_SC), F32)
            for j in range(SC_W):
                acc = acc + scw_ref[j:j + 1, :] * ch_s[pl.ds(r0 + h + j - SC_W // 2, rc), :]
            ysc_ref[r0:r0 + rc, :] = (z_ref[r0:r0 + rc, O_SCB:O_SCB + D_SC] * acc).astype(BF16)
        q = z_ref[:, O_Q:O_Q + D_ATT]
        q_ref[...] = ((q * _wide(cos_ref[...], D_ATT) + _swap_halves(q) * _wide(sin_ref[...], D_ATT)) * (HEAD_DIM ** -0.5)).astype(BF16)
        k = z_ref[:, O_K:O_K + 128]
        k_ref[...] = (k * cos_ref[...] + _swap_halves(k) * sin_ref[...]).astype(BF16)
        v_ref[...] = z_ref[:, O_V:O_V + 128].astype(BF16)

    def full(a):
        return pl.BlockSpec(a.shape, lambda i: (0, 0))

    def rows(w):
        return pl.BlockSpec((tm, w), lambda i: (i, 0))

    return pl.pallas_call(
        body, name="mix_fwd", grid=(nt,),
        in_specs=_halo_specs(t, tm, h, D_IN) + [full(scw), full(ccw), full(ccb), full(ccg), full(ccbb), rows(128), rows(128)],
        out_specs=[rows(D_SC), rows(D_CC), rows(D_CC), rows(D_ATT), rows(128), rows(128)],
        out_shape=[S((t, D_SC), BF16), S((t, D_CC), BF16), S((t, D_CC), F32), S((t, D_ATT), BF16), S((t, 128), BF16),
                   S((t, 128), BF16)],
        scratch_shapes=[pltpu.VMEM((tm + 2 * h, D_CC), F32), pltpu.VMEM((tm + 2 * h, D_SC), F32),
                        pltpu.VMEM((SUBLANES, rc + SUBLANES, D_CC), F32)],
        compiler_params=_params(("parallel",)),
    )(z, z, z, scw, ccw, ccb, ccg, ccbb, cos, sin)


def _band_specs(nb, width):
    return [pl.BlockSpec((BLOCK, width), lambda n: (jnp.maximum(n * ATT_Q_BLOCKS - 1, 0), 0)),
            pl.BlockSpec((ATT_Q_BLOCKS * BLOCK, width), lambda n: (n, 0)),
            pl.BlockSpec((BLOCK, width), lambda n: (jnp.minimum((n + 1) * ATT_Q_BLOCKS, nb - 1), 0))]


def _band_bias(b, nb, bias_s):
    qpos = lax.broadcasted_iota(jnp.int32, (BLOCK, 3 * BLOCK), 0)
    col = lax.broadcasted_iota(jnp.int32, (BLOCK, 3 * BLOCK), 1)
    ok = jnp.abs(qpos - (col - BLOCK)) <= BLOCK
    ok = jnp.logical_and(ok, jnp.logical_or(col >= BLOCK, b > 0))
    ok = jnp.logical_and(ok, jnp.logical_or(col < 2 * BLOCK, b < nb - 1))
    bias_s[...] = jnp.where(ok, 0.0, -1e30)


def _band_cats(before_ref, own_ref, after_ref):
    pieces = [(before_ref, 0)] + [(own_ref, j * BLOCK) for j in range(ATT_Q_BLOCKS)] + [(after_ref, 0)]
    return [[jnp.concatenate([r[r0:r0 + BLOCK, kvh * HEAD_DIM:(kvh + 1) * HEAD_DIM] for r, r0 in pieces[sub:sub + 3]], axis=0)
             for kvh in range(N_KV_HEADS)] for sub in range(ATT_Q_BLOCKS)]


def _head_scores(q_ref, kc, sub, h, bias_s):
    qh = q_ref[sub * BLOCK:(sub + 1) * BLOCK, h * HEAD_DIM:(h + 1) * HEAD_DIM]
    return qh, lax.dot_general(qh, kc, NT, preferred_element_type=F32) + bias_s[sub]


def _softmax_parts(s, sk):
    m = jnp.maximum(jnp.max(s, axis=-1, keepdims=True), sk)
    p = jnp.exp(s - m)
    ps = jnp.exp(sk - m)
    return p, ps, jnp.sum(p, axis=-1, keepdims=True) + ps


def _attn_fwd(qr, kr, vv, sink, after=None):
    t = qr.shape[0]
    nb = t // BLOCK
    units = [(sub, h) for sub in range(ATT_Q_BLOCKS) for h in range(N_Q_HEADS)]

    def body(q_ref, kp_ref, ko_ref, kn_ref, vp_ref, vo_ref, vn_ref, sink_ref, o_ref, bias_s):
        n = pl.program_id(0)
        for sub in range(ATT_Q_BLOCKS):
            _band_bias(n * ATT_Q_BLOCKS + sub, nb, bias_s.at[sub])
        kcs = _band_cats(kp_ref, ko_ref, kn_ref)
        vcs = _band_cats(vp_ref, vo_ref, vn_ref)

        def scores(u):
            sub, h = units[u]
            return _head_scores(q_ref, kcs[sub][h // GROUP], sub, h, bias_s)[1]

        s_next = scores(0)
        for u, (sub, h) in enumerate(units):
            s = s_next
            if u + 1 < len(units):
                s_next = scores(u + 1)
            p, _, denom = _softmax_parts(s, sink_ref[h])
            o = jnp.dot(p.astype(BF16), vcs[sub][h // GROUP], preferred_element_type=F32) * (1.0 / denom)
            o_ref[sub * BLOCK:(sub + 1) * BLOCK, h * HEAD_DIM:(h + 1) * HEAD_DIM] = o.astype(BF16)

    qspec = pl.BlockSpec((ATT_Q_BLOCKS * BLOCK, D_ATT), lambda n: (n, 0))
    body, in_specs, operands = _follow(
        body, 8, [qspec] + _band_specs(nb, 128) + _band_specs(nb, 128) + [pl.BlockSpec(memory_space=pltpu.SMEM)],
        [qr, kr, kr, kr, vv, vv, vv, sink], after)
    return pl.pallas_call(
        body, name="attn_fwd", grid=(nb // ATT_Q_BLOCKS,), in_specs=in_specs,
        out_specs=qspec, out_shape=S((t, D_ATT), BF16),
        scratch_shapes=[pltpu.VMEM((ATT_Q_BLOCKS, BLOCK, 3 * BLOCK), F32)],
        compiler_params=_params(("parallel",)),
    )(*operands)


def _out_ln(ysc, yatt, ycc, wout, x, gam, bet, src_ln):
    t = x.shape[0]
    tm = min(BIG_TILE, t)

    def body(sc_ref, at_ref, cc_ref, w_ref, x_ref, g_ref, b_ref, sg_ref, sb_ref, cat_ref, r_ref, yb_ref):
        cat = jnp.concatenate([sc_ref[...], at_ref[...], cc_ref[...]], axis=1)
        cat_ref[...] = cat
        f = jnp.dot(cat, w_ref[...], preferred_element_type=F32)
        r = ALPHA * _residual(x_ref, (sg_ref, sb_ref)) + f
        r_ref[...] = r
        xhat, _ = _ln_stats(r)
        yb_ref[...] = (xhat * g_ref[...] + b_ref[...]).astype(BF16)

    def rows(w):
        return pl.BlockSpec((tm, w), lambda i: (i, 0))

    vec = pl.BlockSpec((1, D), lambda i: (0, 0))
    return pl.pallas_call(
        body, name="out_ln", grid=(t // tm,),
        in_specs=[rows(D_SC), rows(D_ATT), rows(D_CC), _resident((D, D)), rows(D), vec, vec, vec, vec],
        out_specs=[rows(D), rows(D), rows(D)],
        out_shape=[S((t, D), BF16), S((t, D), F32), S((t, D), BF16)],
        compiler_params=_params(("parallel",)),
    )(ysc, yatt, ycc, wout, x, gam, bet, *src_ln)


def _ln_bwd_block(dy_ref, r_ref, g_ref, dgam_ref, dbet_ref):
    xhat, rstd = _ln_stats(r_ref[...])
    dy = dy_ref[...]

    @pl.when(pl.program_id(0) == 0)
    def _():
        dgam_ref[...] = jnp.zeros_like(dgam_ref)
        dbet_ref[...] = jnp.zeros_like(dbet_ref)

    dgam_ref[...] += jnp.sum(dy * xhat, axis=0, keepdims=True)
    dbet_ref[...] += jnp.sum(dy, axis=0, keepdims=True)
    return _ln_bwd(dy, xhat, rstd, g_ref[...])


def _ffn_bwd(dy, r, gam, wd, gu):
    t = dy.shape[0]
    tm = min(TOKEN_TILE, t)
    chunks = list(zip(FFN_CHUNKS[:-1], FFN_CHUNKS[1:]))

    def body(dy_ref, r_ref, g_ref, w_ref, gu_ref, dr_ref, df_ref, dh_ref, dgam_ref, dbet_ref):
        dr = _ln_bwd_block(dy_ref, r_ref, g_ref, dgam_ref, dbet_ref)
        dr_ref[...] = dr
        dfb = (0.5 * dr).astype(BF16)
        df_ref[...] = dfb
        for lo, hi in chunks:
            da = lax.dot_general(dfb, w_ref[lo:hi, :], NT, preferred_element_type=F32)
            g = gu_ref[:, lo:hi].astype(F32)
            u = gu_ref[:, F + lo:F + hi].astype(F32)
            sg = _sigmoid(g)
            dh_ref[:, lo:hi] = (da * u * (sg * (1.0 + g * (1.0 - sg)))).astype(BF16)
            dh_ref[:, F + lo:F + hi] = (da * (g * sg)).astype(BF16)

    row = pl.BlockSpec((tm, D), lambda i: (i, 0))
    vec = pl.BlockSpec((1, D), lambda i: (0, 0))
    wide = pl.BlockSpec((tm, 2 * F), lambda i: (i, 0))
    return pl.pallas_call(
        body, name="ffn_bwd", grid=(t // tm,),
        in_specs=[row, row, vec, _resident((F, D)), wide],
        out_specs=[row, row, wide, vec, vec],
        out_shape=[S((t, D), F32), S((t, D), BF16), S((t, 2 * F), BF16), S((1, D), F32), S((1, D), F32)],
        compiler_params=_params(("arbitrary",)),
    )(dy, r, gam, wd, gu)


def _ffn_bwd_dx(dy, r, gam, wd, gu, wgut, after=None):
    t = dy.shape[0]
    tm = min(TOKEN_TILE, t)
    n = t // tm
    chunks = list(zip(FFN_CHUNKS[:-1], FFN_CHUNKS[1:]))

    def body(dy_ref, r_ref, g_ref, w_ref, gu_ref, wg_ref, dx_ref, df_ref, dh_ref, dgam_ref, dbet_ref,
             keep_a, keep_b, dr_keep):
        i = pl.program_id(0)

        @pl.when(i == 0)
        def _():
            keep_a[...] = jnp.zeros_like(keep_a)
            keep_b[...] = jnp.zeros_like(keep_b)
            dr_keep[...] = jnp.zeros_like(dr_keep)
            dgam_ref[...] = jnp.zeros_like(dgam_ref)
            dbet_ref[...] = jnp.zeros_like(dbet_ref)

        def step(prev, cur):
            def to_dx(c):
                cols = slice(c * DX_COLS, (c + 1) * DX_COLS)
                dx_ref[:, cols] = ALPHA * dr_keep[:, cols] + jnp.dot(prev[...], wg_ref[:, cols], preferred_element_type=F32)

            to_dx(0)
            xhat, rstd = _ln_stats(r_ref[...])
            dy_t = dy_ref[...]
            live = jnp.where(i < n, 1.0, 0.0)
            dgam_ref[...] += live * jnp.sum(dy_t * xhat, axis=0, keepdims=True)
            dbet_ref[...] += live * jnp.sum(dy_t, axis=0, keepdims=True)
            dr = _ln_bwd(dy_t, xhat, rstd, g_ref[...])
            dfb = (0.5 * dr).astype(BF16)
            df_ref[...] = dfb

            def down(c):
                return lax.dot_general(dfb, w_ref[chunks[c][0]:chunks[c][1], :], NT, preferred_element_type=F32)

            das = {0: down(0), 1: down(1)}
            for c, (lo, hi) in enumerate(chunks):
                if c + 2 < len(chunks):
                    das[c + 2] = down(c + 2)
                if c + 1 < D // DX_COLS:
                    to_dx(c + 1)
                da = das.pop(c)
                g = gu_ref[:, lo:hi].astype(F32)
                u = gu_ref[:, F + lo:F + hi].astype(F32)
                sg = _sigmoid(g)
                dg = (da * u * (sg * (1.0 + g * (1.0 - sg)))).astype(BF16)
                du = (da * (g * sg)).astype(BF16)
                dh_ref[:, lo:hi] = dg
                dh_ref[:, F + lo:F + hi] = du
                cur[:, lo:hi] = dg
                cur[:, F + lo:F + hi] = du
            dr_keep[...] = dr

        @pl.when(i % 2 == 0)
        def _():
            step(keep_b, keep_a)

        @pl.when(i % 2 == 1)
        def _():
            step(keep_a, keep_b)

    cur_row = lambda i: (jnp.minimum(i, n - 1), 0)
    row = pl.BlockSpec((tm, D), cur_row)
    vec = pl.BlockSpec((1, D), lambda i: (0, 0))
    wide = pl.BlockSpec((tm, 2 * F), cur_row)
    body, in_specs, operands = _follow(
        body, 6, [row, row, vec, _resident((F, D)), wide, _resident((2 * F, D))], [dy, r, gam, wd, gu, wgut], after)
    return pl.pallas_call(
        body, name="ffn_bwd_dx", grid=(n + 1,), in_specs=in_specs,
        out_specs=[pl.BlockSpec((tm, D), lambda i: (jnp.maximum(i - 1, 0), 0)), row, wide, vec, vec],
        out_shape=[S((t, D), F32), S((t, D), BF16), S((t, 2 * F), BF16), S((1, D), F32), S((1, D), F32)],
        scratch_shapes=[pltpu.VMEM((tm, 2 * F), BF16), pltpu.VMEM((tm, 2 * F), BF16), pltpu.VMEM((tm, D), F32)],
        compiler_params=_params(("arbitrary",)),
    )(*operands)


def _dx(dr, dh, w, after=None):
    t = dr.shape[0]
    tm = min(BIG_TILE, t)
    kk = dh.shape[1]

    def body(dr_ref, dh_ref, w_ref, o_ref):
        o_ref[...] = ALPHA * dr_ref[...] + jnp.dot(dh_ref[...], w_ref[...], preferred_element_type=F32)

    row = pl.BlockSpec((tm, D), lambda i: (i, 0))
    body, in_specs, operands = _follow(
        body, 3, [row, pl.BlockSpec((tm, kk), lambda i: (i, 0)), _resident((kk, D))], [dr, dh, w], after)
    return pl.pallas_call(
        body, name="dx", grid=(t // tm,), in_specs=in_specs,
        out_specs=row, out_shape=S((t, D), F32), compiler_params=_params(("parallel",)),
    )(*operands)


def _wgrad(a, b, ta, after=None):
    t, ka = a.shape
    tk = min(WGRAD_TOKENS, t)
    nk = t // tk

    def body(a_ref, b_ref, o_ref, acc):
        k = pl.program_id(1)

        @pl.when(k == 0)
        def _():
            acc[...] = jnp.zeros_like(acc)

        acc[...] += lax.dot_general(a_ref[...], b_ref[...], TN, preferred_element_type=F32)

        @pl.when(k == nk - 1)
        def _():
            o_ref[...] = acc[...].astype(BF16)

    body, in_specs, operands = _follow(
        body, 2, [pl.BlockSpec((tk, ta), lambda i, k: (k, i)), pl.BlockSpec((tk, D), lambda i, k: (k, 0))], [a, b], after)
    return pl.pallas_call(
        body, name="wgrad", grid=(ka // ta, nk), in_specs=in_specs,
        out_specs=pl.BlockSpec((ta, D), lambda i, k: (i, 0)), out_shape=S((ka, D), BF16),
        scratch_shapes=[pltpu.VMEM((ta, D), F32)], compiler_params=_params(("parallel", "arbitrary")),
    )(*operands)


def _out_bwd(dy, r, gam, wout, after=None):
    t = dy.shape[0]
    tm = min(BIG_TILE, t)

    def body(dy_ref, r_ref, g_ref, w_ref, dr_ref, dm_ref, dsc_ref, dat_ref, dcc_ref, dgam_ref, dbet_ref):
        dr = _ln_bwd_block(dy_ref, r_ref, g_ref, dgam_ref, dbet_ref)
        dr_ref[...] = dr
        dmb = dr.astype(BF16)
        dm_ref[...] = dmb
        dcat = lax.dot_general(dmb, w_ref[...], NT, preferred_element_type=F32)
        dsc_ref[...] = dcat[:, 0:D_SC]
        dat_ref[...] = dcat[:, D_SC:D_SC + D_ATT]
        dcc_ref[...] = dcat[:, D_SC + D_ATT:D]

    def rows(w):
        return pl.BlockSpec((tm, w), lambda i: (i, 0))

    vec = pl.BlockSpec((1, D), lambda i: (0, 0))
    body, in_specs, operands = _follow(body, 4, [rows(D), rows(D), vec, _resident((D, D))], [dy, r, gam, wout], after)
    return pl.pallas_call(
        body, name="out_bwd", grid=(t // tm,), in_specs=in_specs,
        out_specs=[rows(D), rows(D), rows(D_SC), rows(D_ATT), rows(D_CC), vec, vec],
        out_shape=[S((t, D), F32), S((t, D), BF16), S((t, D_SC), F32), S((t, D_ATT), F32), S((t, D_CC), F32),
                   S((1, D), F32), S((1, D), F32)],
        compiler_params=_params(("arbitrary",)),
    )(*operands)


def _attn_bwd(qr, kr, vv, sink, do, yatt):
    t = qr.shape[0]
    nb = t // BLOCK
    scale = HEAD_DIM ** -0.5
    units = [(sub, h) for sub in range(ATT_Q_BLOCKS) for h in range(N_Q_HEADS)]

    def body(q_ref, kp_ref, ko_ref, kn_ref, vp_ref, vo_ref, vn_ref, sink_ref, do_ref, o_ref,
             dq_ref, dk_ref, dv_ref, dsink_ref, bias_s, ds_s, p_s, q_s, dou_s, sink_s):
        n = pl.program_id(0)
        for sub in range(ATT_Q_BLOCKS):
            _band_bias(n * ATT_Q_BLOCKS + sub, nb, bias_s.at[sub])

        @pl.when(n == 0)
        def _():
            sink_s[...] = jnp.zeros_like(sink_s)

        kcs = _band_cats(kp_ref, ko_ref, kn_ref)
        vcs = _band_cats(vp_ref, vo_ref, vn_ref)

        def scores(u):
            sub, h = units[u]
            return _head_scores(q_ref, kcs[sub][h // GROUP], sub, h, bias_s)

        def probs(u, qh, s):
            sub, h = units[u]
            rows = slice(sub * BLOCK, (sub + 1) * BLOCK)
            cols = slice(h * HEAD_DIM, (h + 1) * HEAD_DIM)
            p, ps, denom = _softmax_parts(s, sink_ref[h])
            doh = do_ref[rows, cols]
            inv = 1.0 / denom
            dd = jnp.sum(doh * o_ref[rows, cols].astype(F32), axis=-1, keepdims=True) * inv
            dou = (doh * inv).astype(BF16)
            dp = lax.dot_general(dou, vcs[sub][h // GROUP], NT, preferred_element_type=F32)
            sink_s[h] -= ps * dd
            return qh, p, dd, dou, dp

        def grads(u, qh, p, dd, dou, dp):
            sub, h = units[u]
            kvh, g = divmod(h, GROUP)
            rows = slice(sub * BLOCK, (sub + 1) * BLOCK)
            cols = slice(h * HEAD_DIM, (h + 1) * HEAD_DIM)
            stack = slice(g * BLOCK, (g + 1) * BLOCK)
            ds = (p * (dp - dd)).astype(BF16)
            dq_ref[rows, cols] = jnp.dot(ds, kcs[sub][kvh], preferred_element_type=F32) * scale
            ds_s[stack, :] = ds
            p_s[stack, :] = p.astype(BF16)
            q_s[stack, :] = qh
            dou_s[stack, :] = dou
            if g == GROUP - 1:
                dk = lax.dot_general(ds_s[...], q_s[...], TN, preferred_element_type=F32)
                dv = lax.dot_general(p_s[...], dou_s[...], TN, preferred_element_type=F32)
                for j in range(3):
                    dk_ref[j, rows, kvh * HEAD_DIM:(kvh + 1) * HEAD_DIM] = dk[j * BLOCK:(j + 1) * BLOCK, :]
                    dv_ref[j, rows, kvh * HEAD_DIM:(kvh + 1) * HEAD_DIM] = dv[j * BLOCK:(j + 1) * BLOCK, :]

        sc = {0: scores(0), 1: scores(1)}
        pr = {0: probs(0, *sc.pop(0))}
        for u in range(len(units)):
            if u + 2 < len(units):
                sc[u + 2] = scores(u + 2)
            if u + 1 < len(units):
                pr[u + 1] = probs(u + 1, *sc.pop(u + 1))
            grads(u, *pr.pop(u))

        @pl.when(n == nb // ATT_Q_BLOCKS - 1)
        def _():
            for h in range(N_Q_HEADS):
                dsink_ref[h:h + 1, :] = jnp.zeros((1, 128), F32) + jnp.sum(sink_s[h])

    qspec = pl.BlockSpec((ATT_Q_BLOCKS * BLOCK, D_ATT), lambda n: (n, 0))
    part = pl.BlockSpec((3, ATT_Q_BLOCKS * BLOCK, 128), lambda n: (0, n, 0))
    stacked = GROUP * BLOCK
    return pl.pallas_call(
        body, name="attn_bwd", grid=(nb // ATT_Q_BLOCKS,),
        in_specs=[qspec] + _band_specs(nb, 128) + _band_specs(nb, 128) + [pl.BlockSpec(memory_space=pltpu.SMEM), qspec, qspec],
        out_specs=[qspec, part, part, pl.BlockSpec((N_Q_HEADS, 128), lambda n: (0, 0))],
        out_shape=[S((t, D_ATT), F32), S((3, t, 128), F32), S((3, t, 128), F32), S((N_Q_HEADS, 128), F32)],
        scratch_shapes=[pltpu.VMEM((ATT_Q_BLOCKS, BLOCK, 3 * BLOCK), F32), pltpu.VMEM((stacked, 3 * BLOCK), BF16),
                        pltpu.VMEM((stacked, 3 * BLOCK), BF16), pltpu.VMEM((stacked, HEAD_DIM), BF16),
                        pltpu.VMEM((stacked, HEAD_DIM), BF16), pltpu.VMEM((N_Q_HEADS, BLOCK, 1), F32)],
        compiler_params=_params(("arbitrary",)),
    )(qr, kr, kr, kr, vv, vv, vv, sink, do, yatt)


def _tap_grads_aligned(d_own, x_ref, d_ref, offsets, out_ref):
    rows = d_own.shape[0]
    padded = jnp.concatenate([d_own, jnp.zeros((SUBLANES, d_own.shape[1]), F32)], axis=0)
    for r in range(SUBLANES):
        d_ref[r] = padded if r == 0 else pltpu.roll(padded, r, 0)
    for j, o in enumerate(offsets):
        r = o % SUBLANES
        out_ref[j:j + 1, :] += jnp.sum(d_ref[r] * x_ref[pl.ds(o - r, rows + SUBLANES), :], axis=0, keepdims=True)


def _mix_bwd(z, c, dysc, dycc, dqr, dkp, dvp, scw, ccw, ccb, ccg, ccbb, cos, sin):
    t = z.shape[0]
    tm = min(MIX_BWD_TILE, t)
    nt = t // tm
    h = HALO_BWD
    half = CC_W // 2
    ext = tm + 2 * h

    def body(z_ref, zp_ref, zn_ref, c_ref, cp_ref, cn_ref, dsc_ref, dscp_ref, dscn_ref, dcc_ref, dccp_ref, dccn_ref,
             dq_ref, dk0_ref, dk1_ref, dk2_ref, dv0_ref, dv1_ref, dv2_ref,
             scw_ref, ccw_ref, ccb_ref, ccg_ref, ccbb_ref, cos_ref, sin_ref,
             dz_ref, dscw_ref, dccw_ref, dvec_ref, u_s, dc_s, ch_s, g_s, p_s, d_s):
        i = pl.program_id(0)
        first, last = i == 0, i == nt - 1

        @pl.when(first)
        def _():
            dscw_ref[...] = jnp.zeros_like(dscw_ref)
            dccw_ref[...] = jnp.zeros_like(dccw_ref)
            dvec_ref[...] = jnp.zeros_like(dvec_ref)

        pz = jnp.where(first, 0.0, zp_ref[...])
        nz = jnp.where(last, 0.0, zn_ref[...])
        zo = z_ref[...]

        def u_of(zz):
            return zz[:, O_CCA:O_CCA + D_CC] * _sigmoid(zz[:, O_CCG:O_CCG + D_CC])

        u_s[0:h, :] = u_of(pz)
        u_s[h:h + tm, :] = u_of(zo)
        u_s[h + tm:ext, :] = u_of(nz)
        c_ext = jnp.concatenate([jnp.where(first, 0.0, cp_ref[...]), c_ref[...], jnp.where(last, 0.0, cn_ref[...])], axis=0)
        xhat, rstd = _ln_stats(c_ext)
        nn = xhat * ccg_ref[...] + ccbb_ref[...]
        sg = _sigmoid(nn)
        dycc_ext = jnp.concatenate([jnp.where(first, 0.0, dccp_ref[...]), dcc_ref[...],
                                    jnp.where(last, 0.0, dccn_ref[...])], axis=0)
        dn = dycc_ext * (sg * (1.0 + nn * (1.0 - sg)))
        dc = _ln_bwd(dn, xhat, rstd, ccg_ref[...])
        dc_s[...] = dc
        dn_own = dn[h:h + tm, :]
        dc_own = dc[h:h + tm, :]
        dvec_ref[0:1, :] += jnp.sum(dc_own, axis=0, keepdims=True)
        dvec_ref[1:2, :] += jnp.sum(dn_own * xhat[h:h + tm, :], axis=0, keepdims=True)
        dvec_ref[2:3, :] += jnp.sum(dn_own, axis=0, keepdims=True)
        du = _taps_aligned(ccw_ref, dc_s, p_s, [h + half - j for j in range(CC_W)], tm)
        _tap_grads_aligned(dc_own, u_s, d_s, [h + j - half for j in range(CC_W)], dccw_ref)
        gate = _sigmoid(zo[:, O_CCG:O_CCG + D_CC])
        a_own = zo[:, O_CCA:O_CCA + D_CC]
        dz_ref[:, O_CCA:O_CCA + D_CC] = (du * gate).astype(BF16)
        dz_ref[:, O_CCG:O_CCG + D_CC] = (du * a_own * gate * (1.0 - gate)).astype(BF16)

        def ch_of(zz):
            return zz[:, O_SCC:O_SCC + D_SC] * zz[:, O_SCH:O_SCH + D_SC]

        ch_s[0:h, :] = ch_of(pz)
        ch_s[h:h + tm, :] = ch_of(zo)
        ch_s[h + tm:ext, :] = ch_of(nz)
        g_s[0:h, :] = jnp.where(first, 0.0, dscp_ref[...]) * pz[:, O_SCB:O_SCB + D_SC]
        g_s[h:h + tm, :] = dsc_ref[...] * zo[:, O_SCB:O_SCB + D_SC]
        g_s[h + tm:ext, :] = jnp.where(last, 0.0, dscn_ref[...]) * nz[:, O_SCB:O_SCB + D_SC]
        conv = jnp.zeros((tm, D_SC), F32)
        dch = jnp.zeros((tm, D_SC), F32)
        g_own = g_s[h:h + tm, :]
        for j in range(SC_W):
            chj = ch_s[pl.ds(h + j - SC_W // 2, tm), :]
            conv = conv + scw_ref[j:j + 1, :] * chj
            dch = dch + scw_ref[j:j + 1, :] * g_s[pl.ds(h + SC_W // 2 - j, tm), :]
            dscw_ref[j:j + 1, :] += jnp.sum(g_own * chj, axis=0, keepdims=True)
        dz_ref[:, O_SCB:O_SCB + D_SC] = (dsc_ref[...] * conv).astype(BF16)
        dz_ref[:, O_SCC:O_SCC + D_SC] = (dch * zo[:, O_SCH:O_SCH + D_SC]).astype(BF16)
        dz_ref[:, O_SCH:O_SCH + D_SC] = (dch * zo[:, O_SCC:O_SCC + D_SC]).astype(BF16)

        dq = dq_ref[...]
        dz_ref[:, O_Q:O_Q + D_ATT] = (dq * _wide(cos_ref[...], D_ATT) + _swap_halves(dq * _wide(sin_ref[...], D_ATT))).astype(BF16)
        dk = dk1_ref[0] + jnp.where(last, 0.0, dk0_ref[0]) + jnp.where(first, 0.0, dk2_ref[0])
        dz_ref[:, O_K:O_K + 128] = (dk * cos_ref[...] + _swap_halves(dk * sin_ref[...])).astype(BF16)
        dv = dv1_ref[0] + jnp.where(last, 0.0, dv0_ref[0]) + jnp.where(first, 0.0, dv2_ref[0])
        dz_ref[:, O_V:O_V + 128] = dv.astype(BF16)

    def full(a):
        return pl.BlockSpec(a.shape, lambda i: (0, 0))

    def rows(w):
        return pl.BlockSpec((tm, w), lambda i: (i, 0))

    parts = [pl.BlockSpec((1, tm, 128), lambda i: (0, jnp.minimum(i + 1, nt - 1), 0)),
             pl.BlockSpec((1, tm, 128), lambda i: (1, i, 0)),
             pl.BlockSpec((1, tm, 128), lambda i: (2, jnp.maximum(i - 1, 0), 0))]
    acc_spec = lambda r: pl.BlockSpec((r, D_CC), lambda i: (0, 0))
    return pl.pallas_call(
        body, name="mix_bwd", grid=(nt,),
        in_specs=(_halo_specs(t, tm, h, D_IN) + _halo_specs(t, tm, h, D_CC) + _halo_specs(t, tm, h, D_SC)
                  + _halo_specs(t, tm, h, D_CC) + [rows(D_ATT)] + parts + parts
                  + [full(scw), full(ccw), full(ccb), full(ccg), full(ccbb), rows(128), rows(128)]),
        out_specs=[rows(D_IN), acc_spec(SC_W), acc_spec(CC_W), acc_spec(3)],
        out_shape=[S((t, D_IN), BF16), S((SC_W, D_SC), F32), S((CC_W, D_CC), F32), S((3, D_CC), F32)],
        scratch_shapes=[pltpu.VMEM((ext, D_CC), F32), pltpu.VMEM((ext, D_CC), F32),
                        pltpu.VMEM((ext, D_SC), F32), pltpu.VMEM((ext, D_SC), F32),
                        pltpu.VMEM((SUBLANES, tm + SUBLANES, D_CC), F32), pltpu.VMEM((SUBLANES, tm + SUBLANES, D_CC), F32)],
        compiler_params=_params(("arbitrary",)),
    )(z, z, z, c, c, c, dysc, dysc, dysc, dycc, dycc, dycc, dqr, dkp, dkp, dkp, dvp, dvp, dvp,
      scw, ccw, ccb, ccg, ccbb, cos, sin)


def _adamw(w, g, m, v):
    m = ADAM_B1 * m + (1.0 - ADAM_B1) * g
    v = ADAM_B2 * v + (1.0 - ADAM_B2) * (g * g)
    m_hat = m / (1.0 - ADAM_B1 ** ADAM_STEP)
    v_hat = v / (1.0 - ADAM_B2 ** ADAM_STEP)
    delta = -ADAM_LR * (m_hat / (jnp.sqrt(v_hat) + ADAM_EPS) + ADAM_WD * w)
    return delta, m, v


def _row_tile(rows):
    for cand in (256, 176, 128):
        if rows % cand == 0:
            return cand
    return rows


def _sum_adam(recv, w, m, v, transposed):
    nl, rows = len(recv), recv[0].shape[1]
    tile = 256 if transposed else _row_tile(rows)
    nc = (D if transposed else rows) // tile

    def body(*refs):
        w_ref, m_ref, v_ref, g_ref, d_ref, mo_ref, vo_ref = refs[nl:]
        for layer in range(nl):
            @pl.when(pl.program_id(0) == layer)
            def _(r_ref=refs[layer]):
                g = r_ref[0].astype(F32)
                for s in range(1, N_DEV):
                    g = g + r_ref[s].astype(F32)
                if transposed:
                    g = g.T
                g_ref[0] = g
                d_ref[0], mo_ref[0], vo_ref[0] = _adamw(w_ref[0], g, m_ref[0], v_ref[0])

    def held(layer):
        def at(l, c):
            return jnp.where(l == layer, c, jnp.where(l < layer, 0, nc - 1))
        if transposed:
            return pl.BlockSpec((N_DEV, rows, tile), lambda l, c: (0, 0, at(l, c)))
        return pl.BlockSpec((N_DEV, tile, D), lambda l, c: (0, at(l, c), 0))

    if transposed:
        blk = pl.BlockSpec((1, tile, rows), lambda l, c: (l, c, 0))
    else:
        blk = pl.BlockSpec((1, tile, D), lambda l, c: (l, c, 0))
    out = S(w.shape, F32)
    return pl.pallas_call(
        body, name="sum_adam_t" if transposed else "sum_adam", grid=(nl, nc),
        in_specs=[held(layer) for layer in range(nl)] + [blk, blk, blk], out_specs=[blk] * 4, out_shape=[out] * 4,
        compiler_params=_params(("arbitrary", "arbitrary")),
    )(*recv, w, m, v)


def _small_sum(gathered):
    rows = gathered.shape[1]

    def body(g_ref, o_ref):
        acc = g_ref[0]
        for s in range(1, N_DEV):
            acc = acc + g_ref[s]
        o_ref[...] = acc

    return pl.pallas_call(
        body, name="small_sum", in_specs=[pl.BlockSpec(gathered.shape, lambda: (0, 0, 0))],
        out_specs=pl.BlockSpec((rows, 128), lambda: (0, 0)), out_shape=S((rows, 128), F32),
    )(gathered)


def _small_adam(w, g, m, v):
    def body(w_ref, g_ref, m_ref, v_ref, d_ref, mo_ref, vo_ref):
        d_ref[...], mo_ref[...], vo_ref[...] = _adamw(w_ref[...], g_ref[...], m_ref[...], v_ref[...])

    spec = pl.BlockSpec(w.shape, lambda: (0, 0))
    return pl.pallas_call(
        body, name="small_adam", in_specs=[spec] * 4, out_specs=[spec] * 3, out_shape=[S(w.shape, F32)] * 3,
    )(w, g, m, v)


def _pack(pieces):
    flat = jnp.concatenate([p.reshape(-1).astype(F32) for p in pieces])
    n = flat.shape[0]
    rows = -(-n // 1024) * 8
    return jnp.pad(flat, (0, rows * 128 - n)).reshape(rows, 128)


def _unpack(packed, shapes):
    flat = packed.reshape(-1)
    out, o = [], 0
    for shp in shapes:
        n = int(np.prod(shp))
        out.append(flat[o:o + n].reshape(shp))
        o += n
    return out


def _rope_tables(t):
    half = HEAD_DIM // 2
    inv_freq = ROPE_THETA ** (-jnp.arange(half, dtype=F32) / half)
    ang = jnp.arange(t).astype(F32)[:, None] * jnp.tile(inv_freq, 128 // half)[None, :]
    sign = jnp.tile(jnp.concatenate([-jnp.ones((half,), F32), jnp.ones((half,), F32)]), 128 // HEAD_DIM)
    return jnp.cos(ang), jnp.sin(ang) * sign[None, :]


BIG = ("ffn1_w_gu", "ffn1_w_down", "w_in", "w_out", "ffn2_w_gu", "ffn2_w_down")
BIG_T = {"ffn1_w_gu": True, "ffn1_w_down": False, "w_in": True, "w_out": False, "ffn2_w_gu": True, "ffn2_w_down": False}
SWAPPED = ("ffn1_w_gu", "ffn2_w_gu")
REPLICATED = ("ln1_g", "ln1_b", "attn_sink", "cc_conv_b", "cc_ln_g", "cc_ln_b", "ln2_g", "ln2_b", "ln3_g", "ln3_b")
CONVS = ("sc_conv_w", "cc_conv_w")
WEIGHTS = ("ffn1_w_gu", "ffn1_w_down", "ln1_g", "ln1_b", "w_in", "sc_conv_w", "attn_sink", "cc_conv_w", "cc_conv_b",
           "cc_ln_g", "cc_ln_b", "w_out", "ln2_g", "ln2_b", "ffn2_w_gu", "ffn2_w_down", "ln3_g", "ln3_b")


def kernel(x, ffn1_w_gu, ffn1_w_down, ln1_g, ln1_b, w_in, sc_conv_w, attn_sink, cc_conv_w, cc_conv_b, cc_ln_g, cc_ln_b, w_out, ln2_g, ln2_b, ffn2_w_gu, ffn2_w_down, ln3_g, ln3_b, loss_target, m_ffn1_w_gu, m_ffn1_w_down, m_ln1_g, m_ln1_b, m_w_in, m_sc_conv_w, m_attn_sink, m_cc_conv_w, m_cc_conv_b, m_cc_ln_g, m_cc_ln_b, m_w_out, m_ln2_g, m_ln2_b, m_ffn2_w_gu, m_ffn2_w_down, m_ln3_g, m_ln3_b, v_ffn1_w_gu, v_ffn1_w_down, v_ln1_g, v_ln1_b, v_w_in, v_sc_conv_w, v_attn_sink, v_cc_conv_w, v_cc_conv_b, v_cc_ln_g, v_cc_ln_b, v_w_out, v_ln2_g, v_ln2_b, v_ffn2_w_gu, v_ffn2_w_down, v_ln3_g, v_ln3_b):
    args = dict(locals())
    w = {n: args[n] for n in WEIGHTS}
    mom = {n: args["m_" + n] for n in WEIGHTS}
    var = {n: args["v_" + n] for n in WEIGHTS}
    x0 = x[0]
    target = loss_target[0]
    t = x0.shape[0]
    idx = 4 * lax.axis_index("x") + 2 * lax.axis_index("y") + lax.axis_index("c")

    blocks = {(n, l): (w[n][l].T if BIG_T[n] else w[n][l]).astype(BF16) for l in range(DEPTH) for n in BIG}
    where = {}

    def start_stage(tag, members, after, extra=()):
        srcs = list(extra) + [blocks[m] for m in members]
        started = _send_start(srcs, [_own_slot(s) for s in srcs], _whole, f"gather_start_{tag}", after)
        for j, m in enumerate(members):
            where[m] = (started, len(extra) + j)
        return started

    def wait_stage(started, k, after, name):
        send, rcv, srcs, lands, _ = started
        return _recv_wait(send, rcv, [k], [srcs[k]], [lands[k]], _whole, after, name)[0]

    def weight(n, l, after):
        g = wait_stage(*where[n, l], after, f"gather_wait_{n}_{l}")
        return g.reshape(N_DEV * g.shape[1], g.shape[2])

    first = start_stage("a", [("ffn1_w_gu", 0)], x0, extra=[_pack([w["sc_conv_w"], w["cc_conv_w"]])])
    res, xb = (x0, None), _cast(x0, first[-1])
    cos, sin = _rope_tables(t)
    conv_all = wait_stage(first, 0, (xb, cos, sin), "gather_wait_convs").reshape(N_DEV, -1)
    n_sc = DEPTH * SC_W * 32
    scw_full = conv_all[:, :n_sc].reshape(N_DEV, DEPTH, SC_W, 32).transpose(1, 2, 0, 3).reshape(DEPTH, SC_W, D_SC)
    ccw_full = conv_all[:, n_sc:n_sc + DEPTH * CC_W * 32].reshape(N_DEV, DEPTH, CC_W, 32).transpose(1, 2, 0, 3).reshape(DEPTH, CC_W, D_CC)

    row = lambda a, l: a[l].reshape(1, -1)

    saved, full = [], {}
    for l in range(DEPTH):
        sv = {"x0b": xb}
        token = None
        full["ffn1_w_gu", l] = weight("ffn1_w_gu", l, (xb, scw_full, ccw_full, *blocks.values()) if l == 0 else xb)
        if l == 0:
            token = start_stage("b", [("ffn1_w_down", 0), ("w_in", 0), ("w_out", 0)], full["ffn1_w_gu", l])[-1]
        gu1, a1 = _ffn_up(xb, full["ffn1_w_gu", l], token)
        full["ffn1_w_down", l] = weight("ffn1_w_down", l, a1)
        if l == 0:
            token = start_stage("c", [("ffn2_w_gu", 0), ("ffn2_w_down", 0)], full["ffn1_w_down", l])[-1]
        r1, x1b = _ffn_down_ln(a1, full["ffn1_w_down", l], res[0], row(ln1_g, l), row(ln1_b, l), res[1], token)
        full["w_in", l] = weight("w_in", l, x1b)
        z = _proj_in(x1b, full["w_in", l])
        ysc, ycc, cpre, qr, kr, vv = _mix_fwd(z, scw_full[l], ccw_full[l], row(cc_conv_b, l), row(cc_ln_g, l), row(cc_ln_b, l), cos, sin)
        if l == 0:
            token = start_stage("d", [("ffn1_w_gu", 1), ("ffn1_w_down", 1)], ysc)[-1]
        yatt = _attn_fwd(qr, kr, vv, attn_sink[l], token)
        full["w_out", l] = weight("w_out", l, yatt)
        ycat, r2, x2b = _out_ln(ysc, yatt, ycc, full["w_out", l], r1, row(ln2_g, l), row(ln2_b, l),
                                (row(ln1_g, l), row(ln1_b, l)))
        full["ffn2_w_gu", l] = weight("ffn2_w_gu", l, x2b)
        if l == 0:
            token = start_stage("e", [("w_in", 1), ("w_out", 1), ("ffn2_w_gu", 1), ("ffn2_w_down", 1)], full["ffn2_w_gu", l])[-1]
        gu2, a2 = _ffn_up(x2b, full["ffn2_w_gu", l], token)
        full["ffn2_w_down", l] = weight("ffn2_w_down", l, a2)
        if l + 1 < DEPTH:
            r3, xb = _ffn_down_ln(a2, full["ffn2_w_down", l], r2, row(ln3_g, l), row(ln3_b, l),
                                  (row(ln2_g, l), row(ln2_b, l)))
            res = (r3, (row(ln3_g, l), row(ln3_b, l)))
        else:
            r3, dy, sq = _ffn_down_ln_loss(a2, full["ffn2_w_down", l], r2, row(ln3_g, l), row(ln3_b, l),
                                           (row(ln2_g, l), row(ln2_b, l)), target)
        sv.update(gu1=gu1, a1=a1, r1=r1, x1b=x1b, z=z, cpre=cpre, qr=qr, kr=kr, vv=vv, yatt=yatt, ycat=ycat, r2=r2, x2b=x2b,
                  gu2=gu2, a2=a2, r3=r3)
        saved.append(sv)

    loss = lax.psum(0.5 * jnp.sum(sq) / D, ("x", "y", "c"))

    sent = []
    small = {n: [None] * DEPTH for n in REPLICATED + CONVS}

    def send_grads(names, l, gs):
        srcs = [g.reshape(N_DEV, g.shape[0] // N_DEV, g.shape[1]) for g in gs]
        lands = [_own_slot(lax.dynamic_index_in_dim(s3, idx, 0, keepdims=False)) for s3 in srcs]
        started = _send_start(srcs, lands, _block_of, f"grads_start_{names[0]}_{l}", gs[-1])
        sent.append((names, l, started))
        return started[-1]

    token = None
    for l in reversed(range(DEPTH)):
        sv = saved[l]
        dy, dfb, dh, dg, db = _ffn_bwd_dx(dy, sv["r3"], row(ln3_g, l), full["ffn2_w_down", l], sv["gu2"],
                                          full["ffn2_w_gu", l], token)
        small["ln3_g"][l], small["ln3_b"][l] = dg, db
        token = send_grads(("ffn2_w_down", "ffn2_w_gu"), l,
                           [_wgrad(sv["a2"], dfb, F // 2), _wgrad(dh, sv["x2b"], F // 2)])

        dr, dmb, dysc, dyatt, dycc, dg, db = _out_bwd(dy, sv["r2"], row(ln2_g, l), full["w_out", l], token)
        small["ln2_g"][l], small["ln2_b"][l] = dg, db
        g_out = _wgrad(sv["ycat"], dmb, D)
        dqr, dkp, dvp, dsink = _attn_bwd(sv["qr"], sv["kr"], sv["vv"], attn_sink[l], dyatt, sv["yatt"])
        small["attn_sink"][l] = dsink[:, 0]
        dz, dscw, dccw, dvec = _mix_bwd(sv["z"], sv["cpre"], dysc, dycc, dqr, dkp, dvp, scw_full[l], ccw_full[l],
                                        row(cc_conv_b, l), row(cc_ln_g, l), row(cc_ln_b, l), cos, sin)
        small["sc_conv_w"][l], small["cc_conv_w"][l] = dscw, dccw
        small["cc_conv_b"][l], small["cc_ln_g"][l], small["cc_ln_b"][l] = dvec[0], dvec[1], dvec[2]
        token = send_grads(("w_out", "w_in"), l, [g_out, _wgrad(dz, sv["x1b"], D)])
        dy = _dx(dr, dz, full["w_in", l], token)

        if l > 0:
            dy, dfb, dh, dg, db = _ffn_bwd_dx(dy, sv["r1"], row(ln1_g, l), full["ffn1_w_down", l], sv["gu1"],
                                              full["ffn1_w_gu", l])
            token = send_grads(("ffn1_w_down", "ffn1_w_gu"), l,
                               [_wgrad(sv["a1"], dfb, F // 2), _wgrad(dh, sv["x0b"], F // 2)])
        else:
            dr, dfb, dh, dg, db = _ffn_bwd(dy, sv["r1"], row(ln1_g, l), full["ffn1_w_down", l], sv["gu1"])
            token = send_grads(("ffn1_w_gu",), l, [_wgrad(dh, sv["x0b"], F // 2)])
            token = send_grads(("ffn1_w_down",), l, [_wgrad(sv["a1"], dfb, F // 2, token)])
            dy = _dx(dr, dh, full["ffn1_w_gu", l], token)
        small["ln1_g"][l], small["ln1_b"][l] = dg, db
    grad_x = dy[None]

    small_names = REPLICATED + CONVS
    small_shapes = [(DEPTH,) + tuple(np.shape(small[n][0].reshape(-1))) for n in small_names]
    small_pack = _pack([jnp.stack([small[n][l].reshape(-1) for l in range(DEPTH)]) for n in small_names])
    small_all = _all_gather([small_pack], "gather_small_grads")[0]

    recv = {n: [None] * DEPTH for n in BIG}
    grads, deltas, new_m, new_v = {}, {}, {}, {}

    def receive(upto, after):
        while len(sent) > upto:
            names, l, (send, rcv, srcs, lands, _) = sent.pop(0)
            got = _recv_wait(send, rcv, list(range(len(names))), srcs, lands, _block_of, after, f"grads_wait_{names[0]}_{l}")
            for n, g in zip(names, got):
                recv[n][l] = g

    def update(n):
        if n in SWAPPED:
            outs = _sum_adam(recv[n], *[jnp.swapaxes(a, 1, 2) for a in (w[n], mom[n], var[n])], False)
            grads[n], deltas[n], new_m[n], new_v[n] = [jnp.swapaxes(a, 1, 2) for a in outs]
        else:
            grads[n], deltas[n], new_m[n], new_v[n] = _sum_adam(recv[n], w[n], mom[n], var[n], BIG_T[n])

    receive(2, dy)
    for n in ("ffn2_w_down", "ffn2_w_gu", "w_out", "w_in"):
        update(n)
    receive(0, new_v["w_in"])
    update("ffn1_w_gu")
    update("ffn1_w_down")
    small_total = _unpack(_small_sum(small_all), small_shapes)
    for n, g in zip(small_names, small_total):
        if n in CONVS:
            taps = SC_W if n == "sc_conv_w" else CC_W
            g = lax.dynamic_slice_in_dim(g.reshape(DEPTH, taps, D_SC), idx * 32, 32, axis=2)
        grads[n] = g.reshape(w[n].shape)
    wp = _pack([w[n] for n in small_names])
    gp = _pack([grads[n] for n in small_names])
    mp = _pack([mom[n] for n in small_names])
    vp = _pack([var[n] for n in small_names])
    shapes = [w[n].shape for n in small_names]
    for dst, packed in zip((deltas, new_m, new_v), _small_adam(wp, gp, mp, vp)):
        for n, a in zip(small_names, _unpack(packed, shapes)):
            dst[n] = a

    return (loss, grad_x, *[grads[n] for n in WEIGHTS], *[deltas[n] for n in WEIGHTS],
            *[new_m[n] for n in WEIGHTS], *[new_v[n] for n in WEIGHTS])
```

```python
import functools

import jax
import jax.numpy as jnp
import numpy as np
from jax import lax
from jax.experimental import pallas as pl
from jax.experimental.pallas import tpu as pltpu

F32 = jnp.float32
BF16 = jnp.bfloat16
S = jax.ShapeDtypeStruct

N_DEV = 8
DEPTH = 2
D = 1024
F = 2816
D_IN = 2048
HEAD_DIM = 64
N_Q_HEADS = 8
N_KV_HEADS = 2
GROUP = 4
D_SC = 256
D_ATT = 512
D_CC = 256
CC_W = 31
SC_W = 3
BLOCK = 128
ROPE_THETA = 10000.0
LN_EPS = 1e-5
ALPHA = (2.0 * DEPTH) ** 0.25
ADAM_LR = 0.001
ADAM_B1 = 0.9
ADAM_B2 = 0.999
ADAM_EPS = 1e-08
ADAM_WD = 0.01
ADAM_STEP = 10

O_SCB, O_SCC, O_SCH, O_Q, O_K, O_V, O_CCA, O_CCG = 0, 256, 512, 768, 1280, 1408, 1536, 1792

V7X_VMEM_BYTES = 64 * 1024 * 1024
VMEM_LIMIT = V7X_VMEM_BYTES - 8 * 1024 * 1024
TOKEN_TILE = 256
WIDE_TILE = 512
BIG_TILE = 1024
WGRAD_TOKENS = 2048
FFN_CHUNKS = (0, 768, 1536, 2176, 2816)
DX_COLS = 256
MIX_BWD_TILE = 128
HALO_FWD = 16
HALO_BWD = 16
ATT_Q_BLOCKS = 4
CONV_ROWS = 128
SUBLANES = 8

NT = (((1,), (1,)), ((), ()))
TN = (((0,), (0,)), ((), ()))
MESH = pl.DeviceIdType.MESH


def _params(sem=None):
    return pltpu.CompilerParams(dimension_semantics=sem, vmem_limit_bytes=VMEM_LIMIT)


def _sigmoid(v):
    return 1.0 / (1.0 + jnp.exp(-v))


def _ln_stats(r):
    mu = jnp.mean(r, axis=-1, keepdims=True)
    d = r - mu
    var = jnp.mean(d * d, axis=-1, keepdims=True)
    rstd = lax.rsqrt(var + LN_EPS)
    return d * rstd, rstd


def _ln_bwd(dn, xhat, rstd, gam):
    dxh = dn * gam
    return rstd * (dxh - jnp.mean(dxh, axis=-1, keepdims=True) - xhat * jnp.mean(dxh * xhat, axis=-1, keepdims=True))


def _swap_halves(v):
    n = v.shape[-1]
    lane = lax.broadcasted_iota(jnp.int32, v.shape, v.ndim - 1) % HEAD_DIM
    return jnp.where(lane < HEAD_DIM // 2, pltpu.roll(v, n - HEAD_DIM // 2, v.ndim - 1), pltpu.roll(v, HEAD_DIM // 2, v.ndim - 1))


def _wide(tab, n):
    return tab if n == 128 else jnp.concatenate([tab] * (n // 128), axis=1)


def _me():
    x, y, c = lax.axis_index("x"), lax.axis_index("y"), lax.axis_index("c")
    return x, y, c


def _peer(rel):
    x, y, c = _me()
    px = 1 - x if rel & 4 else x
    py = 1 - y if rel & 2 else y
    pc = 1 - c if rel & 1 else c
    return (px, py, pc), 4 * px + 2 * py + pc


def _exchange(srcs, dsts_shape, dst_index, src_of, dst_of, name):
    n = len(srcs)

    def body(*refs):
        ins = refs[:n]
        outs = [refs[n + dst_index[k]] for k in range(n)]
        send, recv, lsem = refs[n + len(dsts_shape):]
        x, y, c = _me()
        me = 4 * x + 2 * y + c
        local = [pltpu.make_async_copy(src_of(ins[k], k, me), dst_of(outs[k], k, me), lsem.at[k]) for k in range(n)]
        for cp in local:
            cp.start()
        sends, recvs = [], []
        for k in range(n):
            for rel in range(1, N_DEV):
                peer, pidx = _peer(rel)
                sends.append(pltpu.make_async_remote_copy(
                    src_ref=src_of(ins[k], k, pidx), dst_ref=dst_of(outs[k], k, me),
                    send_sem=send.at[k, rel - 1], recv_sem=recv.at[k, rel - 1], device_id=peer, device_id_type=MESH))
                recvs.append(pltpu.make_async_remote_copy(
                    src_ref=src_of(ins[k], k, pidx), dst_ref=dst_of(outs[k], k, pidx),
                    send_sem=send.at[k, rel - 1], recv_sem=recv.at[k, rel - 1], device_id=peer, device_id_type=MESH))
        for cp in sends:
            cp.start()
        for cp in recvs:
            cp.wait_recv()
        for cp in sends:
            cp.wait_send()
        for cp in local:
            cp.wait()

    hbm = pl.BlockSpec(memory_space=pltpu.HBM)
    return pl.pallas_call(
        body, name=name, in_specs=[hbm] * n, out_specs=[hbm] * len(dsts_shape), out_shape=dsts_shape,
        scratch_shapes=[pltpu.SemaphoreType.DMA((n, N_DEV - 1)), pltpu.SemaphoreType.DMA((n, N_DEV - 1)),
                        pltpu.SemaphoreType.DMA((n,))],
    )(*srcs)


def _all_gather(blocks, name):
    shapes = [S((N_DEV,) + b.shape, b.dtype) for b in blocks]
    return _exchange(blocks, shapes, list(range(len(blocks))), lambda ref, k, idx: ref, lambda ref, k, idx: ref.at[idx], name)


HBM_SPEC = pl.BlockSpec(memory_space=pltpu.HBM)
SEM_SPEC = pl.BlockSpec(memory_space=pltpu.SEMAPHORE)
ANY_SPEC = pl.BlockSpec(memory_space=pl.ANY)
EFFECT = pltpu.SideEffectType.DATAFLOW_SIDE_EFFECTING
N_PEERS = N_DEV - 1


def _own_slot(block):
    x, y, c = _me()
    return lax.dynamic_update_index_in_dim(lax.empty((N_DEV,) + block.shape, block.dtype), block, 4 * x + 2 * y + c, 0)


def _follow(body, n_in, in_specs, operands, after):
    if after is None:
        return body, list(in_specs), list(operands)

    def tail(*refs):
        return body(*refs[:n_in], *refs[n_in + 1:])

    return tail, list(in_specs) + [ANY_SPEC], list(operands) + [after]


def _send_start(srcs, lands, src_of, name, after):
    n = len(srcs)

    def body(*refs):
        ins, zones = refs[:n], refs[n:2 * n]
        send, recv = refs[2 * n + 1], refs[2 * n + 2]
        token = refs[-1]
        x, y, c = _me()
        me = 4 * x + 2 * y + c
        for k in range(n):
            for rel in range(1, N_DEV):
                peer, pidx = _peer(rel)
                pltpu.make_async_remote_copy(
                    src_ref=src_of(ins[k], pidx), dst_ref=zones[k].at[me],
                    send_sem=send.at[k * N_PEERS + rel - 1], recv_sem=recv.at[k * N_PEERS + rel - 1],
                    device_id=peer, device_id_type=MESH).start()
        token[...] = jnp.zeros_like(token)

    outs = pl.pallas_call(
        body, name=name,
        out_shape=(pltpu.SemaphoreType.DMA((n * N_PEERS,)), pltpu.SemaphoreType.DMA((n * N_PEERS,)),
                   *[pltpu.HBM(a.shape, a.dtype) for a in lands], S((8, 128), F32)),
        in_specs=[HBM_SPEC] * (2 * n) + [ANY_SPEC],
        out_specs=(SEM_SPEC, SEM_SPEC, *[HBM_SPEC] * n, pl.BlockSpec(memory_space=pltpu.VMEM)),
        input_output_aliases={n + i: 2 + i for i in range(n)},
        compiler_params=pltpu.CompilerParams(has_side_effects=EFFECT),
    )(*[pltpu.with_memory_space_constraint(a, pltpu.HBM) for a in list(srcs) + list(lands)], after)
    return outs[0], outs[1], list(srcs), list(outs[2:2 + n]), outs[-1]


def _recv_wait(send, recv, ks, srcs, lands, src_of, after, name):
    n = len(ks)
    after = after if isinstance(after, (tuple, list)) else (after,)

    def body(*refs):
        ins, zones = refs[:n], refs[n:2 * n]
        send_sems, recv_sems = refs[2 * n], refs[2 * n + 1]
        for j, k in enumerate(ks):
            for rel in range(1, N_DEV):
                peer, pidx = _peer(rel)
                cp = pltpu.make_async_remote_copy(
                    src_ref=src_of(ins[j], pidx), dst_ref=zones[j].at[pidx],
                    send_sem=send_sems.at[k * N_PEERS + rel - 1], recv_sem=recv_sems.at[k * N_PEERS + rel - 1],
                    device_id=peer, device_id_type=MESH)
                cp.wait_send()
                cp.wait_recv()

    outs = pl.pallas_call(
        body, name=name,
        out_shape=[pltpu.HBM(a.shape, a.dtype) for a in lands],
        in_specs=[HBM_SPEC] * (2 * n) + [SEM_SPEC, SEM_SPEC] + [ANY_SPEC] * len(after), out_specs=[HBM_SPEC] * n,
        input_output_aliases={n + i: i for i in range(n)},
        compiler_params=pltpu.CompilerParams(has_side_effects=EFFECT),
    )(*srcs, *lands, send, recv, *after)
    return list(outs)


def _whole(ref, idx):
    return ref


def _block_of(ref, idx):
    return ref.at[idx]


def _cast(x, after):
    t, d = x.shape
    tm = min(BIG_TILE, t)

    def body(x_ref, after_ref, o_ref):
        o_ref[...] = x_ref[...].astype(BF16)

    row = pl.BlockSpec((tm, d), lambda i: (i, 0))
    return pl.pallas_call(
        body, name="cast", grid=(t // tm,), in_specs=[row, ANY_SPEC], out_specs=row, out_shape=S((t, d), BF16),
        compiler_params=_params(("parallel",)),
    )(x, after)


def _resident(shape):
    return pl.BlockSpec(shape, lambda i: (0,) * len(shape), pipeline_mode=pl.Buffered(1))


def _ffn_up(xb, wgut, after=None):
    t = xb.shape[0]
    tm = min(WIDE_TILE, t)
    half = F // 2

    def body(x_ref, w_ref, gu_ref, a_ref):
        x = x_ref[...]
        for ch in range(2):
            lo = ch * half
            g = lax.dot_general(x, w_ref[lo:lo + half, :], NT, preferred_element_type=F32)
            u = lax.dot_general(x, w_ref[F + lo:F + lo + half, :], NT, preferred_element_type=F32)
            gu_ref[:, lo:lo + half] = g.astype(BF16)
            gu_ref[:, F + lo:F + lo + half] = u.astype(BF16)
            a_ref[:, lo:lo + half] = (g * _sigmoid(g) * u).astype(BF16)

    body, in_specs, operands = _follow(
        body, 2, [pl.BlockSpec((tm, D), lambda i: (i, 0)), _resident((2 * F, D))], [xb, wgut], after)
    return pl.pallas_call(
        body, name="ffn_up", grid=(t // tm,), in_specs=in_specs,
        out_specs=[pl.BlockSpec((tm, 2 * F), lambda i: (i, 0)), pl.BlockSpec((tm, F), lambda i: (i, 0))],
        out_shape=[S((t, 2 * F), BF16), S((t, F), BF16)], compiler_params=_params(("parallel",)),
    )(*operands)


def _residual(x_ref, src_ln):
    if src_ln is None:
        return x_ref[...]
    xhat, _ = _ln_stats(x_ref[...])
    return xhat * src_ln[0][...] + src_ln[1][...]


def _ffn_down_ln(a, wd, x, gam, bet, src_ln=None, after=None):
    t = x.shape[0]
    tm = min(BIG_TILE, t)
    n_ln = 0 if src_ln is None else 2

    def body(*refs):
        a_ref, w_ref, x_ref, g_ref, b_ref = refs[:5]
        r_ref, yb_ref = refs[5 + n_ln:]
        f = jnp.dot(a_ref[...], w_ref[...], preferred_element_type=F32)
        r = ALPHA * _residual(x_ref, refs[5:5 + n_ln] or None) + 0.5 * f
        r_ref[...] = r
        xhat, _ = _ln_stats(r)
        yb_ref[...] = (xhat * g_ref[...] + b_ref[...]).astype(BF16)

    row = pl.BlockSpec((tm, D), lambda i: (i, 0))
    vec = pl.BlockSpec((1, D), lambda i: (0, 0))
    body, in_specs, operands = _follow(
        body, 5 + n_ln, [pl.BlockSpec((tm, F), lambda i: (i, 0)), _resident((F, D)), row, vec, vec] + [vec] * n_ln,
        [a, wd, x, gam, bet] + list(src_ln or ()), after)
    return pl.pallas_call(
        body, name="ffn_down_ln", grid=(t // tm,), in_specs=in_specs,
        out_specs=[row, row], out_shape=[S((t, D), F32), S((t, D), BF16)],
        compiler_params=_params(("parallel",)),
    )(*operands)


def _ffn_down_ln_loss(a, wd, x, gam, bet, src_ln, target):
    t = x.shape[0]
    tm = min(BIG_TILE, t)

    def body(a_ref, w_ref, x_ref, g_ref, b_ref, sg_ref, sb_ref, t_ref, r_ref, dy_ref, part_ref):
        f = jnp.dot(a_ref[...], w_ref[...], preferred_element_type=F32)
        r = ALPHA * _residual(x_ref, (sg_ref, sb_ref)) + 0.5 * f
        r_ref[...] = r
        xhat, _ = _ln_stats(r)
        e = xhat * g_ref[...] + b_ref[...] - t_ref[...]
        dy_ref[...] = e / D

        @pl.when(pl.program_id(0) == 0)
        def _():
            part_ref[...] = jnp.zeros_like(part_ref)

        part_ref[...] += jnp.sum(e * e, axis=0, keepdims=True)

    row = pl.BlockSpec((tm, D), lambda i: (i, 0))
    vec = pl.BlockSpec((1, D), lambda i: (0, 0))
    return pl.pallas_call(
        body, name="ffn_down_ln_loss", grid=(t // tm,),
        in_specs=[pl.BlockSpec((tm, F), lambda i: (i, 0)), _resident((F, D)), row, vec, vec, vec, vec, row],
        out_specs=[row, row, vec], out_shape=[S((t, D), F32), S((t, D), F32), S((1, D), F32)],
        compiler_params=_params(("arbitrary",)),
    )(a, wd, x, gam, bet, *src_ln, target)


def _proj_in(xb, wint):
    t = xb.shape[0]
    tm = min(BIG_TILE, t)

    def body(x_ref, w_ref, z_ref):
        z_ref[...] = lax.dot_general(x_ref[...], w_ref[...], NT, preferred_element_type=F32)

    return pl.pallas_call(
        body, name="proj_in", grid=(t // tm,),
        in_specs=[pl.BlockSpec((tm, D), lambda i: (i, 0)), _resident((D_IN, D))],
        out_specs=pl.BlockSpec((tm, D_IN), lambda i: (i, 0)), out_shape=S((t, D_IN), F32),
        compiler_params=_params(("parallel",)),
    )(xb, wint)


def _halo_specs(t, tm, halo, width):
    per = tm // halo
    last = t // halo - 1
    return [pl.BlockSpec((tm, width), lambda i: (i, 0)),
            pl.BlockSpec((halo, width), lambda i: (jnp.maximum(i * per - 1, 0), 0)),
            pl.BlockSpec((halo, width), lambda i: (jnp.minimum((i + 1) * per, last), 0))]


def _taps_aligned(w_ref, x_ref, p_ref, offsets, rows):
    for r in range(SUBLANES):
        acc = jnp.zeros((rows + SUBLANES, x_ref.shape[1]), F32)
        for j, o in enumerate(offsets):
            if o % SUBLANES == r:
                acc = acc + w_ref[j:j + 1, :] * x_ref[pl.ds(o - r, rows + SUBLANES), :]
        p_ref[r] = acc
    out = p_ref[0, 0:rows, :]
    for r in range(1, SUBLANES):
        out = out + p_ref[r, pl.ds(r, rows), :]
    return out


def _mix_fwd(z, scw, ccw, ccb, ccg, ccbb, cos, sin):
    t = z.shape[0]
    tm = min(WIDE_TILE, t)
    nt = t // tm
    h = HALO_FWD
    rc = min(CONV_ROWS, tm)

    def body(z_ref, zp_ref, zn_ref, scw_ref, ccw_ref, ccb_ref, ccg_ref, ccbb_ref, cos_ref, sin_ref,
             ysc_ref, ycc_ref, c_ref, q_ref, k_ref, v_ref, u_s, ch_s, p_s):
        i = pl.program_id(0)
        pz = jnp.where(i == 0, 0.0, zp_ref[...])
        nz = jnp.where(i == nt - 1, 0.0, zn_ref[...])

        def u_of(zz):
            return zz[:, O_CCA:O_CCA + D_CC] * _sigmoid(zz[:, O_CCG:O_CCG + D_CC])

        def ch_of(zz):
            return zz[:, O_SCC:O_SCC + D_SC] * zz[:, O_SCH:O_SCH + D_SC]

        u_s[0:h, :] = u_of(pz)
        u_s[h:h + tm, :] = z_ref[:, O_CCA:O_CCA + D_CC] * _sigmoid(z_ref[:, O_CCG:O_CCG + D_CC])
        u_s[h + tm:2 * h + tm, :] = u_of(nz)
        ch_s[0:h, :] = ch_of(pz)
        ch_s[h:h + tm, :] = z_ref[:, O_SCC:O_SCC + D_SC] * z_ref[:, O_SCH:O_SCH + D_SC]
        ch_s[h + tm:2 * h + tm, :] = ch_of(nz)
        for r0 in range(0, tm, rc):
            c = _taps_aligned(ccw_ref, u_s, p_s, [r0 + h + j - CC_W // 2 for j in range(CC_W)], rc) + ccb_ref[...]
            c_ref[r0:r0 + rc, :] = c
            xhat, _ = _ln_stats(c)
            n = xhat * ccg_ref[...] + ccbb_ref[...]
            ycc_ref[r0:r0 + rc, :] = (n * _sigmoid(n)).astype(BF16)
            acc = jnp.zeros((rc, D_SC), F32)
            for j in range(SC_W):
                acc = acc + scw_ref[j:j + 1, :] * ch_s[pl.ds(r0 + h + j - SC_W // 2, rc), :]
            ysc_ref[r0:r0 + rc, :] = (z_ref[r0:r0 + rc, O_SCB:O_SCB + D_SC] * acc).astype(BF16)
        q = z_ref[:, O_Q:O_Q + D_ATT]
        q_ref[...] = ((q * _wide(cos_ref[...], D_ATT) + _swap_halves(q) * _wide(sin_ref[...], D_ATT)) * (HEAD_DIM ** -0.5)).astype(BF16)
        k = z_ref[:, O_K:O_K + 128]
        k_ref[...] = (k * cos_ref[...] + _swap_halves(k) * sin_ref[...]).astype(BF16)
        v_ref[...] = z_ref[:, O_V:O_V + 128].astype(BF16)

    def full(a):
        return pl.BlockSpec(a.shape, lambda i: (0, 0))

    def rows(w):
        return pl.BlockSpec((tm, w), lambda i: (i, 0))

    return pl.pallas_call(
        body, name="mix_fwd", grid=(nt,),
        in_specs=_halo_specs(t, tm, h, D_IN) + [full(scw), full(ccw), full(ccb), full(ccg), full(ccbb), rows(128), rows(128)],
        out_specs=[rows(D_SC), rows(D_CC), rows(D_CC), rows(D_ATT), rows(128), rows(128)],
        out_shape=[S((t, D_SC), BF16), S((t, D_CC), BF16), S((t, D_CC), F32), S((t, D_ATT), BF16), S((t, 128), BF16),
                   S((t, 128), BF16)],
        scratch_shapes=[pltpu.VMEM((tm + 2 * h, D_CC), F32), pltpu.VMEM((tm + 2 * h, D_SC), F32),
                        pltpu.VMEM((SUBLANES, rc + SUBLANES, D_CC), F32)],
        compiler_params=_params(("parallel",)),
    )(z, z, z, scw, ccw, ccb, ccg, ccbb, cos, sin)


def _band_specs(nb, width):
    return [pl.BlockSpec((BLOCK, width), lambda n: (jnp.maximum(n * ATT_Q_BLOCKS - 1, 0), 0)),
            pl.BlockSpec((ATT_Q_BLOCKS * BLOCK, width), lambda n: (n, 0)),
            pl.BlockSpec((BLOCK, width), lambda n: (jnp.minimum((n + 1) * ATT_Q_BLOCKS, nb - 1), 0))]


def _band_bias(b, nb, bias_s):
    qpos = lax.broadcasted_iota(jnp.int32, (BLOCK, 3 * BLOCK), 0)
    col = lax.broadcasted_iota(jnp.int32, (BLOCK, 3 * BLOCK), 1)
    ok = jnp.abs(qpos - (col - BLOCK)) <= BLOCK
    ok = jnp.logical_and(ok, jnp.logical_or(col >= BLOCK, b > 0))
    ok = jnp.logical_and(ok, jnp.logical_or(col < 2 * BLOCK, b < nb - 1))
    bias_s[...] = jnp.where(ok, 0.0, -1e30)


def _band_cats(before_ref, own_ref, after_ref):
    pieces = [(before_ref, 0)] + [(own_ref, j * BLOCK) for j in range(ATT_Q_BLOCKS)] + [(after_ref, 0)]
    return [[jnp.concatenate([r[r0:r0 + BLOCK, kvh * HEAD_DIM:(kvh + 1) * HEAD_DIM] for r, r0 in pieces[sub:sub + 3]], axis=0)
             for kvh in range(N_KV_HEADS)] for sub in range(ATT_Q_BLOCKS)]


def _head_scores(q_ref, kc, sub, h, bias_s):
    qh = q_ref[sub * BLOCK:(sub + 1) * BLOCK, h * HEAD_DIM:(h + 1) * HEAD_DIM]
    return qh, lax.dot_general(qh, kc, NT, preferred_element_type=F32) + bias_s[sub]


def _softmax_parts(s, sk):
    m = jnp.maximum(jnp.max(s, axis=-1, keepdims=True), sk)
    p = jnp.exp(s - m)
    ps = jnp.exp(sk - m)
    return p, ps, jnp.sum(p, axis=-1, keepdims=True) + ps


def _attn_fwd(qr, kr, vv, sink, after=None):
    t = qr.shape[0]
    nb = t // BLOCK
    units = [(sub, h) for sub in range(ATT_Q_BLOCKS) for h in range(N_Q_HEADS)]

    def body(q_ref, kp_ref, ko_ref, kn_ref, vp_ref, vo_ref, vn_ref, sink_ref, o_ref, bias_s):
        n = pl.program_id(0)
        for sub in range(ATT_Q_BLOCKS):
            _band_bias(n * ATT_Q_BLOCKS + sub, nb, bias_s.at[sub])
        kcs = _band_cats(kp_ref, ko_ref, kn_ref)
        vcs = _band_cats(vp_ref, vo_ref, vn_ref)

        def scores(u):
            sub, h = units[u]
            return _head_scores(q_ref, kcs[sub][h // GROUP], sub, h, bias_s)[1]

        s_next = scores(0)
        for u, (sub, h) in enumerate(units):
            s = s_next
            if u + 1 < len(units):
                s_next = scores(u + 1)
            p, _, denom = _softmax_parts(s, sink_ref[h])
            o = jnp.dot(p.astype(BF16), vcs[sub][h // GROUP], preferred_element_type=F32) * (1.0 / denom)
            o_ref[sub * BLOCK:(sub + 1) * BLOCK, h * HEAD_DIM:(h + 1) * HEAD_DIM] = o.astype(BF16)

    qspec = pl.BlockSpec((ATT_Q_BLOCKS * BLOCK, D_ATT), lambda n: (n, 0))
    body, in_specs, operands = _follow(
        body, 8, [qspec] + _band_specs(nb, 128) + _band_specs(nb, 128) + [pl.BlockSpec(memory_space=pltpu.SMEM)],
        [qr, kr, kr, kr, vv, vv, vv, sink], after)
    return pl.pallas_call(
        body, name="attn_fwd", grid=(nb // ATT_Q_BLOCKS,), in_specs=in_specs,
        out_specs=qspec, out_shape=S((t, D_ATT), BF16),
        scratch_shapes=[pltpu.VMEM((ATT_Q_BLOCKS, BLOCK, 3 * BLOCK), F32)],
        compiler_params=_params(("parallel",)),
    )(*operands)


def _out_ln(ysc, yatt, ycc, wout, x, gam, bet, src_ln):
    t = x.shape[0]
    tm = min(BIG_TILE, t)

    def body(sc_ref, at_ref, cc_ref, w_ref, x_ref, g_ref, b_ref, sg_ref, sb_ref, cat_ref, r_ref, yb_ref):
        cat = jnp.concatenate([sc_ref[...], at_ref[...], cc_ref[...]], axis=1)
        cat_ref[...] = cat
        f = jnp.dot(cat, w_ref[...], preferred_element_type=F32)
        r = ALPHA * _residual(x_ref, (sg_ref, sb_ref)) + f
        r_ref[...] = r
        xhat, _ = _ln_stats(r)
        yb_ref[...] = (xhat * g_ref[...] + b_ref[...]).astype(BF16)

    def rows(w):
        return pl.BlockSpec((tm, w), lambda i: (i, 0))

    vec = pl.BlockSpec((1, D), lambda i: (0, 0))
    return pl.pallas_call(
        body, name="out_ln", grid=(t // tm,),
        in_specs=[rows(D_SC), rows(D_ATT), rows(D_CC), _resident((D, D)), rows(D), vec, vec, vec, vec],
        out_specs=[rows(D), rows(D), rows(D)],
        out_shape=[S((t, D), BF16), S((t, D), F32), S((t, D), BF16)],
        compiler_params=_params(("parallel",)),
    )(ysc, yatt, ycc, wout, x, gam, bet, *src_ln)


def _ln_bwd_block(dy_ref, r_ref, g_ref, dgam_ref, dbet_ref):
    xhat, rstd = _ln_stats(r_ref[...])
    dy = dy_ref[...]

    @pl.when(pl.program_id(0) == 0)
    def _():
        dgam_ref[...] = jnp.zeros_like(dgam_ref)
        dbet_ref[...] = jnp.zeros_like(dbet_ref)

    dgam_ref[...] += jnp.sum(dy * xhat, axis=0, keepdims=True)
    dbet_ref[...] += jnp.sum(dy, axis=0, keepdims=True)
    return _ln_bwd(dy, xhat, rstd, g_ref[...])


def _ffn_bwd(dy, r, gam, wd, gu):
    t = dy.shape[0]
    tm = min(TOKEN_TILE, t)
    chunks = list(zip(FFN_CHUNKS[:-1], FFN_CHUNKS[1:]))

    def body(dy_ref, r_ref, g_ref, w_ref, gu_ref, dr_ref, df_ref, dh_ref, dgam_ref, dbet_ref):
        dr = _ln_bwd_block(dy_ref, r_ref, g_ref, dgam_ref, dbet_ref)
        dr_ref[...] = dr
        dfb = (0.5 * dr).astype(BF16)
        df_ref[...] = dfb
        for lo, hi in chunks:
            da = lax.dot_general(dfb, w_ref[lo:hi, :], NT, preferred_element_type=F32)
            g = gu_ref[:, lo:hi].astype(F32)
            u = gu_ref[:, F + lo:F + hi].astype(F32)
            sg = _sigmoid(g)
            dh_ref[:, lo:hi] = (da * u * (sg * (1.0 + g * (1.0 - sg)))).astype(BF16)
            dh_ref[:, F + lo:F + hi] = (da * (g * sg)).astype(BF16)

    row = pl.BlockSpec((tm, D), lambda i: (i, 0))
    vec = pl.BlockSpec((1, D), lambda i: (0, 0))
    wide = pl.BlockSpec((tm, 2 * F), lambda i: (i, 0))
    return pl.pallas_call(
        body, name="ffn_bwd", grid=(t // tm,),
        in_specs=[row, row, vec, _resident((F, D)), wide],
        out_specs=[row, row, wide, vec, vec],
        out_shape=[S((t, D), F32), S((t, D), BF16), S((t, 2 * F), BF16), S((1, D), F32), S((1, D), F32)],
        compiler_params=_params(("arbitrary",)),
    )(dy, r, gam, wd, gu)


def _ffn_bwd_dx(dy, r, gam, wd, gu, wgut, after=None):
    t = dy.shape[0]
    tm = min(TOKEN_TILE, t)
    n = t // tm
    chunks = list(zip(FFN_CHUNKS[:-1], FFN_CHUNKS[1:]))

    def body(dy_ref, r_ref, g_ref, w_ref, gu_ref, wg_ref, dx_ref, df_ref, dh_ref, dgam_ref, dbet_ref,
             keep_a, keep_b, dr_keep):
        i = pl.program_id(0)

        @pl.when(i == 0)
        def _():
            keep_a[...] = jnp.zeros_like(keep_a)
            keep_b[...] = jnp.zeros_like(keep_b)
            dr_keep[...] = jnp.zeros_like(dr_keep)
            dgam_ref[...] = jnp.zeros_like(dgam_ref)
            dbet_ref[...] = jnp.zeros_like(dbet_ref)

        def step(prev, cur):
            def to_dx(c):
                cols = slice(c * DX_COLS, (c + 1) * DX_COLS)
                dx_ref[:, cols] = ALPHA * dr_keep[:, cols] + jnp.dot(prev[...], wg_ref[:, cols], preferred_element_type=F32)

            to_dx(0)
            xhat, rstd = _ln_stats(r_ref[...])
            dy_t = dy_ref[...]
            live = jnp.where(i < n, 1.0, 0.0)
            dgam_ref[...] += live * jnp.sum(dy_t * xhat, axis=0, keepdims=True)
            dbet_ref[...] += live * jnp.sum(dy_t, axis=0, keepdims=True)
            dr = _ln_bwd(dy_t, xhat, rstd, g_ref[...])
            dfb = (0.5 * dr).astype(BF16)
            df_ref[...] = dfb

            def down(c):
                return lax.dot_general(dfb, w_ref[chunks[c][0]:chunks[c][1], :], NT, preferred_element_type=F32)

            das = {0: down(0), 1: down(1)}
            for c, (lo, hi) in enumerate(chunks):
                if c + 2 < len(chunks):
                    das[c + 2] = down(c + 2)
                if c + 1 < D // DX_COLS:
                    to_dx(c + 1)
                da = das.pop(c)
                g = gu_ref[:, lo:hi].astype(F32)
                u = gu_ref[:, F + lo:F + hi].astype(F32)
                sg = _sigmoid(g)
                dg = (da * u * (sg * (1.0 + g * (1.0 - sg)))).astype(BF16)
                du = (da * (g * sg)).astype(BF16)
                dh_ref[:, lo:hi] = dg
                dh_ref[:, F + lo:F + hi] = du
                cur[:, lo:hi] = dg
                cur[:, F + lo:F + hi] = du
            dr_keep[...] = dr

        @pl.when(i % 2 == 0)
        def _():
            step(keep_b, keep_a)

        @pl.when(i % 2 == 1)
        def _():
            step(keep_a, keep_b)

    cur_row = lambda i: (jnp.minimum(i, n - 1), 0)
    row = pl.BlockSpec((tm, D), cur_row)
    vec = pl.BlockSpec((1, D), lambda i: (0, 0))
    wide = pl.BlockSpec((tm, 2 * F), cur_row)
    body, in_specs, operands = _follow(
        body, 6, [row, row, vec, _resident((F, D)), wide, _resident((2 * F, D))], [dy, r, gam, wd, gu, wgut], after)
    return pl.pallas_call(
        body, name="ffn_bwd_dx", grid=(n + 1,), in_specs=in_specs,
        out_specs=[pl.BlockSpec((tm, D), lambda i: (jnp.maximum(i - 1, 0), 0)), row, wide, vec, vec],
        out_shape=[S((t, D), F32), S((t, D), BF16), S((t, 2 * F), BF16), S((1, D), F32), S((1, D), F32)],
        scratch_shapes=[pltpu.VMEM((tm, 2 * F), BF16), pltpu.VMEM((tm, 2 * F), BF16), pltpu.VMEM((tm, D), F32)],
        compiler_params=_params(("arbitrary",)),
    )(*operands)


def _dx(dr, dh, w, after=None):
    t = dr.shape[0]
    tm = min(BIG_TILE, t)
    kk = dh.shape[1]

    def body(dr_ref, dh_ref, w_ref, o_ref):
        o_ref[...] = ALPHA * dr_ref[...] + jnp.dot(dh_ref[...], w_ref[...], preferred_element_type=F32)

    row = pl.BlockSpec((tm, D), lambda i: (i, 0))
    body, in_specs, operands = _follow(
        body, 3, [row, pl.BlockSpec((tm, kk), lambda i: (i, 0)), _resident((kk, D))], [dr, dh, w], after)
    return pl.pallas_call(
        body, name="dx", grid=(t // tm,), in_specs=in_specs,
        out_specs=row, out_shape=S((t, D), F32), compiler_params=_params(("parallel",)),
    )(*operands)


def _wgrad(a, b, ta, after=None):
    t, ka = a.shape
    tk = min(WGRAD_TOKENS, t)
    nk = t // tk

    def body(a_ref, b_ref, o_ref, acc):
        k = pl.program_id(1)

        @pl.when(k == 0)
        def _():
            acc[...] = jnp.zeros_like(acc)

        acc[...] += lax.dot_general(a_ref[...], b_ref[...], TN, preferred_element_type=F32)

        @pl.when(k == nk - 1)
        def _():
            o_ref[...] = acc[...].astype(BF16)

    body, in_specs, operands = _follow(
        body, 2, [pl.BlockSpec((tk, ta), lambda i, k: (k, i)), pl.BlockSpec((tk, D), lambda i, k: (k, 0))], [a, b], after)
    return pl.pallas_call(
        body, name="wgrad", grid=(ka // ta, nk), in_specs=in_specs,
        out_specs=pl.BlockSpec((ta, D), lambda i, k: (i, 0)), out_shape=S((ka, D), BF16),
        scratch_shapes=[pltpu.VMEM((ta, D), F32)], compiler_params=_params(("parallel", "arbitrary")),
    )(*operands)


def _out_bwd(dy, r, gam, wout, after=None):
    t = dy.shape[0]
    tm = min(BIG_TILE, t)

    def body(dy_ref, r_ref, g_ref, w_ref, dr_ref, dm_ref, dsc_ref, dat_ref, dcc_ref, dgam_ref, dbet_ref):
        dr = _ln_bwd_block(dy_ref, r_ref, g_ref, dgam_ref, dbet_ref)
        dr_ref[...] = dr
        dmb = dr.astype(BF16)
        dm_ref[...] = dmb
        dcat = lax.dot_general(dmb, w_ref[...], NT, preferred_element_type=F32)
        dsc_ref[...] = dcat[:, 0:D_SC]
        dat_ref[...] = dcat[:, D_SC:D_SC + D_ATT]
        dcc_ref[...] = dcat[:, D_SC + D_ATT:D]

    def rows(w):
        return pl.BlockSpec((tm, w), lambda i: (i, 0))

    vec = pl.BlockSpec((1, D), lambda i: (0, 0))
    body, in_specs, operands = _follow(body, 4, [rows(D), rows(D), vec, _resident((D, D))], [dy, r, gam, wout], after)
    return pl.pallas_call(
        body, name="out_bwd", grid=(t // tm,), in_specs=in_specs,
        out_specs=[rows(D), rows(D), rows(D_SC), rows(D_ATT), rows(D_CC), vec, vec],
        out_shape=[S((t, D), F32), S((t, D), BF16), S((t, D_SC), F32), S((t, D_ATT), F32), S((t, D_CC), F32),
                   S((1, D), F32), S((1, D), F32)],
        compiler_params=_params(("arbitrary",)),
    )(*operands)


def _attn_bwd(qr, kr, vv, sink, do, yatt):
    t = qr.shape[0]
    nb = t // BLOCK
    scale = HEAD_DIM ** -0.5
    units = [(sub, h) for sub in range(ATT_Q_BLOCKS) for h in range(N_Q_HEADS)]

    def body(q_ref, kp_ref, ko_ref, kn_ref, vp_ref, vo_ref, vn_ref, sink_ref, do_ref, o_ref,
             dq_ref, dk_ref, dv_ref, dsink_ref, bias_s, ds_s, p_s, q_s, dou_s, sink_s):
        n = pl.program_id(0)
        for sub in range(ATT_Q_BLOCKS):
            _band_bias(n * ATT_Q_BLOCKS + sub, nb, bias_s.at[sub])

        @pl.when(n == 0)
        def _():
            sink_s[...] = jnp.zeros_like(sink_s)

        kcs = _band_cats(kp_ref, ko_ref, kn_ref)
        vcs = _band_cats(vp_ref, vo_ref, vn_ref)

        def scores(u):
            sub, h = units[u]
            return _head_scores(q_ref, kcs[sub][h // GROUP], sub, h, bias_s)

        def probs(u, qh, s):
            sub, h = units[u]
            rows = slice(sub * BLOCK, (sub + 1) * BLOCK)
            cols = slice(h * HEAD_DIM, (h + 1) * HEAD_DIM)
            p, ps, denom = _softmax_parts(s, sink_ref[h])
            doh = do_ref[rows, cols]
            inv = 1.0 / denom
            dd = jnp.sum(doh * o_ref[rows, cols].astype(F32), axis=-1, keepdims=True) * inv
            dou = (doh * inv).astype(BF16)
            dp = lax.dot_general(dou, vcs[sub][h // GROUP], NT, preferred_element_type=F32)
            sink_s[h] -= ps * dd
            return qh, p, dd, dou, dp

        def grads(u, qh, p, dd, dou, dp):
            sub, h = units[u]
            kvh, g = divmod(h, GROUP)
            rows = slice(sub * BLOCK, (sub + 1) * BLOCK)
            cols = slice(h * HEAD_DIM, (h + 1) * HEAD_DIM)
            stack = slice(g * BLOCK, (g + 1) * BLOCK)
            ds = (p * (dp - dd)).astype(BF16)
            dq_ref[rows, cols] = jnp.dot(ds, kcs[sub][kvh], preferred_element_type=F32) * scale
            ds_s[stack, :] = ds
            p_s[stack, :] = p.astype(BF16)
            q_s[stack, :] = qh
            dou_s[stack, :] = dou
            if g == GROUP - 1:
                dk = lax.dot_general(ds_s[...], q_s[...], TN, preferred_element_type=F32)
                dv = lax.dot_general(p_s[...], dou_s[...], TN, preferred_element_type=F32)
                for j in range(3):
                    dk_ref[j, rows, kvh * HEAD_DIM:(kvh + 1) * HEAD_DIM] = dk[j * BLOCK:(j + 1) * BLOCK, :]
                    dv_ref[j, rows, kvh * HEAD_DIM:(kvh + 1) * HEAD_DIM] = dv[j * BLOCK:(j + 1) * BLOCK, :]

        sc = {0: scores(0), 1: scores(1)}
        pr = {0: probs(0, *sc.pop(0))}
        for u in range(len(units)):
            if u + 2 < len(units):
                sc[u + 2] = scores(u + 2)
            if u + 1 < len(units):
                pr[u + 1] = probs(u + 1, *sc.pop(u + 1))
            grads(u, *pr.pop(u))

        @pl.when(n == nb // ATT_Q_BLOCKS - 1)
        def _():
            for h in range(N_Q_HEADS):
                dsink_ref[h:h + 1, :] = jnp.zeros((1, 128), F32) + jnp.sum(sink_s[h])

    qspec = pl.BlockSpec((ATT_Q_BLOCKS * BLOCK, D_ATT), lambda n: (n, 0))
    part = pl.BlockSpec((3, ATT_Q_BLOCKS * BLOCK, 128), lambda n: (0, n, 0))
    stacked = GROUP * BLOCK
    return pl.pallas_call(
        body, name="attn_bwd", grid=(nb // ATT_Q_BLOCKS,),
        in_specs=[qspec] + _band_specs(nb, 128) + _band_specs(nb, 128) + [pl.BlockSpec(memory_space=pltpu.SMEM), qspec, qspec],
        out_specs=[qspec, part, part, pl.BlockSpec((N_Q_HEADS, 128), lambda n: (0, 0))],
        out_shape=[S((t, D_ATT), F32), S((3, t, 128), F32), S((3, t, 128), F32), S((N_Q_HEADS, 128), F32)],
        scratch_shapes=[pltpu.VMEM((ATT_Q_BLOCKS, BLOCK, 3 * BLOCK), F32), pltpu.VMEM((stacked, 3 * BLOCK), BF16),
                        pltpu.VMEM((stacked, 3 * BLOCK), BF16), pltpu.VMEM((stacked, HEAD_DIM), BF16),
                        pltpu.VMEM((stacked, HEAD_DIM), BF16), pltpu.VMEM((N_Q_HEADS, BLOCK, 1), F32)],
        compiler_params=_params(("arbitrary",)),
    )(qr, kr, kr, kr, vv, vv, vv, sink, do, yatt)


def _tap_grads_aligned(d_own, x_ref, d_ref, offsets, out_ref):
    rows = d_own.shape[0]
    padded = jnp.concatenate([d_own, jnp.zeros((SUBLANES, d_own.shape[1]), F32)], axis=0)
    for r in range(SUBLANES):
        d_ref[r] = padded if r == 0 else pltpu.roll(padded, r, 0)
    for j, o in enumerate(offsets):
        r = o % SUBLANES
        out_ref[j:j + 1, :] += jnp.sum(d_ref[r] * x_ref[pl.ds(o - r, rows + SUBLANES), :], axis=0, keepdims=True)


def _mix_bwd(z, c, dysc, dycc, dqr, dkp, dvp, scw, ccw, ccb, ccg, ccbb, cos, sin):
    t = z.shape[0]
    tm = min(MIX_BWD_TILE, t)
    nt = t // tm
    h = HALO_BWD
    half = CC_W // 2
    ext = tm + 2 * h

    def body(z_ref, zp_ref, zn_ref, c_ref, cp_ref, cn_ref, dsc_ref, dscp_ref, dscn_ref, dcc_ref, dccp_ref, dccn_ref,
             dq_ref, dk0_ref, dk1_ref, dk2_ref, dv0_ref, dv1_ref, dv2_ref,
             scw_ref, ccw_ref, ccb_ref, ccg_ref, ccbb_ref, cos_ref, sin_ref,
             dz_ref, dscw_ref, dccw_ref, dvec_ref, u_s, dc_s, ch_s, g_s, p_s, d_s):
        i = pl.program_id(0)
        first, last = i == 0, i == nt - 1

        @pl.when(first)
        def _():
            dscw_ref[...] = jnp.zeros_like(dscw_ref)
            dccw_ref[...] = jnp.zeros_like(dccw_ref)
            dvec_ref[...] = jnp.zeros_like(dvec_ref)

        pz = jnp.where(first, 0.0, zp_ref[...])
        nz = jnp.where(last, 0.0, zn_ref[...])
        zo = z_ref[...]

        def u_of(zz):
            return zz[:, O_CCA:O_CCA + D_CC] * _sigmoid(zz[:, O_CCG:O_CCG + D_CC])

        u_s[0:h, :] = u_of(pz)
        u_s[h:h + tm, :] = u_of(zo)
        u_s[h + tm:ext, :] = u_of(nz)
        c_ext = jnp.concatenate([jnp.where(first, 0.0, cp_ref[...]), c_ref[...], jnp.where(last, 0.0, cn_ref[...])], axis=0)
        xhat, rstd = _ln_stats(c_ext)
        nn = xhat * ccg_ref[...] + ccbb_ref[...]
        sg = _sigmoid(nn)
        dycc_ext = jnp.concatenate([jnp.where(first, 0.0, dccp_ref[...]), dcc_ref[...],
                                    jnp.where(last, 0.0, dccn_ref[...])], axis=0)
        dn = dycc_ext * (sg * (1.0 + nn * (1.0 - sg)))
        dc = _ln_bwd(dn, xhat, rstd, ccg_ref[...])
        dc_s[...] = dc
        dn_own = dn[h:h + tm, :]
        dc_own = dc[h:h + tm, :]
        dvec_ref[0:1, :] += jnp.sum(dc_own, axis=0, keepdims=True)
        dvec_ref[1:2, :] += jnp.sum(dn_own * xhat[h:h + tm, :], axis=0, keepdims=True)
        dvec_ref[2:3, :] += jnp.sum(dn_own, axis=0, keepdims=True)
        du = _taps_aligned(ccw_ref, dc_s, p_s, [h + half - j for j in range(CC_W)], tm)
        _tap_grads_aligned(dc_own, u_s, d_s, [h + j - half for j in range(CC_W)], dccw_ref)
        gate = _sigmoid(zo[:, O_CCG:O_CCG + D_CC])
        a_own = zo[:, O_CCA:O_CCA + D_CC]
        dz_ref[:, O_CCA:O_CCA + D_CC] = (du * gate).astype(BF16)
        dz_ref[:, O_CCG:O_CCG + D_CC] = (du * a_own * gate * (1.0 - gate)).astype(BF16)

        def ch_of(zz):
            return zz[:, O_SCC:O_SCC + D_SC] * zz[:, O_SCH:O_SCH + D_SC]

        ch_s[0:h, :] = ch_of(pz)
        ch_s[h:h + tm, :] = ch_of(zo)
        ch_s[h + tm:ext, :] = ch_of(nz)
        g_s[0:h, :] = jnp.where(first, 0.0, dscp_ref[...]) * pz[:, O_SCB:O_SCB + D_SC]
        g_s[h:h + tm, :] = dsc_ref[...] * zo[:, O_SCB:O_SCB + D_SC]
        g_s[h + tm:ext, :] = jnp.where(last, 0.0, dscn_ref[...]) * nz[:, O_SCB:O_SCB + D_SC]
        conv = jnp.zeros((tm, D_SC), F32)
        dch = jnp.zeros((tm, D_SC), F32)
        g_own = g_s[h:h + tm, :]
        for j in range(SC_W):
            chj = ch_s[pl.ds(h + j - SC_W // 2, tm), :]
            conv = conv + scw_ref[j:j + 1, :] * chj
            dch = dch + scw_ref[j:j + 1, :] * g_s[pl.ds(h + SC_W // 2 - j, tm), :]
            dscw_ref[j:j + 1, :] += jnp.sum(g_own * chj, axis=0, keepdims=True)
        dz_ref[:, O_SCB:O_SCB + D_SC] = (dsc_ref[...] * conv).astype(BF16)
        dz_ref[:, O_SCC:O_SCC + D_SC] = (dch * zo[:, O_SCH:O_SCH + D_SC]).astype(BF16)
        dz_ref[:, O_SCH:O_SCH + D_SC] = (dch * zo[:, O_SCC:O_SCC + D_SC]).astype(BF16)

        dq = dq_ref[...]
        dz_ref[:, O_Q:O_Q + D_ATT] = (dq * _wide(cos_ref[...], D_ATT) + _swap_halves(dq * _wide(sin_ref[...], D_ATT))).astype(BF16)
        dk = dk1_ref[0] + jnp.where(last, 0.0, dk0_ref[0]) + jnp.where(first, 0.0, dk2_ref[0])
        dz_ref[:, O_K:O_K + 128] = (dk * cos_ref[...] + _swap_halves(dk * sin_ref[...])).astype(BF16)
        dv = dv1_ref[0] + jnp.where(last, 0.0, dv0_ref[0]) + jnp.where(first, 0.0, dv2_ref[0])
        dz_ref[:, O_V:O_V + 128] = dv.astype(BF16)

    def full(a):
        return pl.BlockSpec(a.shape, lambda i: (0, 0))

    def rows(w):
        return pl.BlockSpec((tm, w), lambda i: (i, 0))

    parts = [pl.BlockSpec((1, tm, 128), lambda i: (0, jnp.minimum(i + 1, nt - 1), 0)),
             pl.BlockSpec((1, tm, 128), lambda i: (1, i, 0)),
             pl.BlockSpec((1, tm, 128), lambda i: (2, jnp.maximum(i - 1, 0), 0))]
    acc_spec = lambda r: pl.BlockSpec((r, D_CC), lambda i: (0, 0))
    return pl.pallas_call(
        body, name="mix_bwd", grid=(nt,),
        in_specs=(_halo_specs(t, tm, h, D_IN) + _halo_specs(t, tm, h, D_CC) + _halo_specs(t, tm, h, D_SC)
                  + _halo_specs(t, tm, h, D_CC) + [rows(D_ATT)] + parts + parts
                  + [full(scw), full(ccw), full(ccb), full(ccg), full(ccbb), rows(128), rows(128)]),
        out_specs=[rows(D_IN), acc_spec(SC_W), acc_spec(CC_W), acc_spec(3)],
        out_shape=[S((t, D_IN), BF16), S((SC_W, D_SC), F32), S((CC_W, D_CC), F32), S((3, D_CC), F32)],
        scratch_shapes=[pltpu.VMEM((ext, D_CC), F32), pltpu.VMEM((ext, D_CC), F32),
                        pltpu.VMEM((ext, D_SC), F32), pltpu.VMEM((ext, D_SC), F32),
                        pltpu.VMEM((SUBLANES, tm + SUBLANES, D_CC), F32), pltpu.VMEM((SUBLANES, tm + SUBLANES, D_CC), F32)],
        compiler_params=_params(("arbitrary",)),
    )(z, z, z, c, c, c, dysc, dysc, dysc, dycc, dycc, dycc, dqr, dkp, dkp, dkp, dvp, dvp, dvp,
      scw, ccw, ccb, ccg, ccbb, cos, sin)


def _adamw(w, g, m, v):
    m = ADAM_B1 * m + (1.0 - ADAM_B1) * g
    v = ADAM_B2 * v + (1.0 - ADAM_B2) * (g * g)
    m_hat = m / (1.0 - ADAM_B1 ** ADAM_STEP)
    v_hat = v / (1.0 - ADAM_B2 ** ADAM_STEP)
    delta = -ADAM_LR * (m_hat / (jnp.sqrt(v_hat) + ADAM_EPS) + ADAM_WD * w)
    return delta, m, v


def _row_tile(rows):
    for cand in (256, 176, 128):
        if rows % cand == 0:
            return cand
    return rows


def _sum_adam(recv, w, m, v, transposed):
    nl, rows = len(recv), recv[0].shape[1]
    tile = 256 if transposed else _row_tile(rows)
    nc = (D if transposed else rows) // tile

    def body(*refs):
        w_ref, m_ref, v_ref, g_ref, d_ref, mo_ref, vo_ref = refs[nl:]
        for layer in range(nl):
            @pl.when(pl.program_id(0) == layer)
            def _(r_ref=refs[layer]):
                g = r_ref[0].astype(F32)
                for s in range(1, N_DEV):
                    g = g + r_ref[s].astype(F32)
                if transposed:
                    g = g.T
                g_ref[0] = g
                d_ref[0], mo_ref[0], vo_ref[0] = _adamw(w_ref[0], g, m_ref[0], v_ref[0])

    def held(layer):
        def at(l, c):
            return jnp.where(l == layer, c, jnp.where(l < layer, 0, nc - 1))
        if transposed:
            return pl.BlockSpec((N_DEV, rows, tile), lambda l, c: (0, 0, at(l, c)))
        return pl.BlockSpec((N_DEV, tile, D), lambda l, c: (0, at(l, c), 0))

    if transposed:
        blk = pl.BlockSpec((1, tile, rows), lambda l, c: (l, c, 0))
    else:
        blk = pl.BlockSpec((1, tile, D), lambda l, c: (l, c, 0))
    out = S(w.shape, F32)
    return pl.pallas_call(
        body, name="sum_adam_t" if transposed else "sum_adam", grid=(nl, nc),
        in_specs=[held(layer) for layer in range(nl)] + [blk, blk, blk], out_specs=[blk] * 4, out_shape=[out] * 4,
        compiler_params=_params(("arbitrary", "arbitrary")),
    )(*recv, w, m, v)


def _small_sum(gathered):
    rows = gathered.shape[1]

    def body(g_ref, o_ref):
        acc = g_ref[0]
        for s in range(1, N_DEV):
            acc = acc + g_ref[s]
        o_ref[...] = acc

    return pl.pallas_call(
        body, name="small_sum", in_specs=[pl.BlockSpec(gathered.shape, lambda: (0, 0, 0))],
        out_specs=pl.BlockSpec((rows, 128), lambda: (0, 0)), out_shape=S((rows, 128), F32),
    )(gathered)


def _small_adam(ws, gs, ms, vs):
    n = len(ws)

    def body(*refs):
        for k in range(n):
            delta, m, v = _adamw(refs[k][...], refs[n + k][...], refs[2 * n + k][...], refs[3 * n + k][...])
            refs[4 * n + k][...] = delta
            refs[5 * n + k][...] = m
            refs[6 * n + k][...] = v

    def whole(a):
        return pl.BlockSpec(a.shape, lambda: (0,) * a.ndim)

    return pl.pallas_call(
        body, name="small_adam", in_specs=[whole(a) for a in list(ws) + list(gs) + list(ms) + list(vs)],
        out_specs=[whole(a) for a in ws] * 3, out_shape=[S(a.shape, F32) for a in ws] * 3,
    )(*ws, *gs, *ms, *vs)


def _pack(pieces):
    flat = jnp.concatenate([p.reshape(-1).astype(F32) for p in pieces])
    n = flat.shape[0]
    rows = -(-n // 1024) * 8
    return jnp.pad(flat, (0, rows * 128 - n)).reshape(rows, 128)


def _unpack(packed, shapes):
    flat = packed.reshape(-1)
    out, o = [], 0
    for shp in shapes:
        n = int(np.prod(shp))
        out.append(flat[o:o + n].reshape(shp))
        o += n
    return out


def _rope_tables(t):
    half = HEAD_DIM // 2
    inv_freq = ROPE_THETA ** (-jnp.arange(half, dtype=F32) / half)
    ang = jnp.arange(t).astype(F32)[:, None] * jnp.tile(inv_freq, 128 // half)[None, :]
    sign = jnp.tile(jnp.concatenate([-jnp.ones((half,), F32), jnp.ones((half,), F32)]), 128 // HEAD_DIM)
    return jnp.cos(ang), jnp.sin(ang) * sign[None, :]


BIG = ("ffn1_w_gu", "ffn1_w_down", "w_in", "w_out", "ffn2_w_gu", "ffn2_w_down")
BIG_T = {"ffn1_w_gu": True, "ffn1_w_down": False, "w_in": True, "w_out": False, "ffn2_w_gu": True, "ffn2_w_down": False}
SWAPPED = ("ffn1_w_gu", "ffn2_w_gu")
REPLICATED = ("ln1_g", "ln1_b", "attn_sink", "cc_conv_b", "cc_ln_g", "cc_ln_b", "ln2_g", "ln2_b", "ln3_g", "ln3_b")
CONVS = ("sc_conv_w", "cc_conv_w")
WEIGHTS = ("ffn1_w_gu", "ffn1_w_down", "ln1_g", "ln1_b", "w_in", "sc_conv_w", "attn_sink", "cc_conv_w", "cc_conv_b",
           "cc_ln_g", "cc_ln_b", "w_out", "ln2_g", "ln2_b", "ffn2_w_gu", "ffn2_w_down", "ln3_g", "ln3_b")


def kernel(x, ffn1_w_gu, ffn1_w_down, ln1_g, ln1_b, w_in, sc_conv_w, attn_sink, cc_conv_w, cc_conv_b, cc_ln_g, cc_ln_b, w_out, ln2_g, ln2_b, ffn2_w_gu, ffn2_w_down, ln3_g, ln3_b, loss_target, m_ffn1_w_gu, m_ffn1_w_down, m_ln1_g, m_ln1_b, m_w_in, m_sc_conv_w, m_attn_sink, m_cc_conv_w, m_cc_conv_b, m_cc_ln_g, m_cc_ln_b, m_w_out, m_ln2_g, m_ln2_b, m_ffn2_w_gu, m_ffn2_w_down, m_ln3_g, m_ln3_b, v_ffn1_w_gu, v_ffn1_w_down, v_ln1_g, v_ln1_b, v_w_in, v_sc_conv_w, v_attn_sink, v_cc_conv_w, v_cc_conv_b, v_cc_ln_g, v_cc_ln_b, v_w_out, v_ln2_g, v_ln2_b, v_ffn2_w_gu, v_ffn2_w_down, v_ln3_g, v_ln3_b):
    args = dict(locals())
    w = {n: args[n] for n in WEIGHTS}
    mom = {n: args["m_" + n] for n in WEIGHTS}
    var = {n: args["v_" + n] for n in WEIGHTS}
    x0 = x[0]
    target = loss_target[0]
    t = x0.shape[0]
    idx = 4 * lax.axis_index("x") + 2 * lax.axis_index("y") + lax.axis_index("c")

    blocks = {(n, l): (w[n][l].T if BIG_T[n] else w[n][l]).astype(BF16) for l in range(DEPTH) for n in BIG}
    where = {}

    def start_stage(tag, members, after, extra=()):
        srcs = list(extra) + [blocks[m] for m in members]
        started = _send_start(srcs, [_own_slot(s) for s in srcs], _whole, f"gather_start_{tag}", after)
        for j, m in enumerate(members):
            where[m] = (started, len(extra) + j)
        return started

    def wait_stage(started, k, after, name):
        send, rcv, srcs, lands, _ = started
        return _recv_wait(send, rcv, [k], [srcs[k]], [lands[k]], _whole, after, name)[0]

    def weight(n, l, after):
        g = wait_stage(*where[n, l], after, f"gather_wait_{n}_{l}")
        return g.reshape(N_DEV * g.shape[1], g.shape[2])

    first = start_stage("a", [("ffn1_w_gu", 0)], x0, extra=[_pack([w["sc_conv_w"], w["cc_conv_w"]])])
    res, xb = (x0, None), _cast(x0, first[-1])
    cos, sin = _rope_tables(t)
    conv_all = wait_stage(first, 0, (xb, cos, sin), "gather_wait_convs").reshape(N_DEV, -1)
    n_sc = DEPTH * SC_W * 32
    scw_full = conv_all[:, :n_sc].reshape(N_DEV, DEPTH, SC_W, 32).transpose(1, 2, 0, 3).reshape(DEPTH, SC_W, D_SC)
    ccw_full = conv_all[:, n_sc:n_sc + DEPTH * CC_W * 32].reshape(N_DEV, DEPTH, CC_W, 32).transpose(1, 2, 0, 3).reshape(DEPTH, CC_W, D_CC)

    row = lambda a, l: a[l].reshape(1, -1)

    saved, full = [], {}
    for l in range(DEPTH):
        sv = {"x0b": xb}
        token = None
        full["ffn1_w_gu", l] = weight("ffn1_w_gu", l, (xb, scw_full, ccw_full, *blocks.values()) if l == 0 else xb)
        if l == 0:
            token = start_stage("b", [("ffn1_w_down", 0), ("w_in", 0), ("w_out", 0)], full["ffn1_w_gu", l])[-1]
        gu1, a1 = _ffn_up(xb, full["ffn1_w_gu", l], token)
        full["ffn1_w_down", l] = weight("ffn1_w_down", l, a1)
        if l == 0:
            token = start_stage("c", [("ffn2_w_gu", 0), ("ffn2_w_down", 0)], full["ffn1_w_down", l])[-1]
        r1, x1b = _ffn_down_ln(a1, full["ffn1_w_down", l], res[0], row(ln1_g, l), row(ln1_b, l), res[1], token)
        full["w_in", l] = weight("w_in", l, x1b)
        z = _proj_in(x1b, full["w_in", l])
        ysc, ycc, cpre, qr, kr, vv = _mix_fwd(z, scw_full[l], ccw_full[l], row(cc_conv_b, l), row(cc_ln_g, l), row(cc_ln_b, l), cos, sin)
        if l == 0:
            token = start_stage("d", [("ffn1_w_gu", 1), ("ffn1_w_down", 1)], ysc)[-1]
        yatt = _attn_fwd(qr, kr, vv, attn_sink[l], token)
        full["w_out", l] = weight("w_out", l, yatt)
        ycat, r2, x2b = _out_ln(ysc, yatt, ycc, full["w_out", l], r1, row(ln2_g, l), row(ln2_b, l),
                                (row(ln1_g, l), row(ln1_b, l)))
        full["ffn2_w_gu", l] = weight("ffn2_w_gu", l, x2b)
        if l == 0:
            token = start_stage("e", [("w_in", 1), ("w_out", 1), ("ffn2_w_gu", 1), ("ffn2_w_down", 1)], full["ffn2_w_gu", l])[-1]
        gu2, a2 = _ffn_up(x2b, full["ffn2_w_gu", l], token)
        full["ffn2_w_down", l] = weight("ffn2_w_down", l, a2)
        if l + 1 < DEPTH:
            r3, xb = _ffn_down_ln(a2, full["ffn2_w_down", l], r2, row(ln3_g, l), row(ln3_b, l),
                                  (row(ln2_g, l), row(ln2_b, l)))
            res = (r3, (row(ln3_g, l), row(ln3_b, l)))
        else:
            r3, dy, sq = _ffn_down_ln_loss(a2, full["ffn2_w_down", l], r2, row(ln3_g, l), row(ln3_b, l),
                                           (row(ln2_g, l), row(ln2_b, l)), target)
        sv.update(gu1=gu1, a1=a1, r1=r1, x1b=x1b, z=z, cpre=cpre, qr=qr, kr=kr, vv=vv, yatt=yatt, ycat=ycat, r2=r2, x2b=x2b,
                  gu2=gu2, a2=a2, r3=r3)
        saved.append(sv)

    loss = lax.psum(0.5 * jnp.sum(sq) / D, ("x", "y", "c"))

    sent = []
    small = {n: [None] * DEPTH for n in REPLICATED + CONVS}

    def send_grads(names, l, gs):
        srcs = [g.reshape(N_DEV, g.shape[0] // N_DEV, g.shape[1]) for g in gs]
        lands = [_own_slot(lax.dynamic_index_in_dim(s3, idx, 0, keepdims=False)) for s3 in srcs]
        started = _send_start(srcs, lands, _block_of, f"grads_start_{names[0]}_{l}", gs[-1])
        sent.append((names, l, started))
        return started[-1]

    token = None
    for l in reversed(range(DEPTH)):
        sv = saved[l]
        dy, dfb, dh, dg, db = _ffn_bwd_dx(dy, sv["r3"], row(ln3_g, l), full["ffn2_w_down", l], sv["gu2"],
                                          full["ffn2_w_gu", l], token)
        small["ln3_g"][l], small["ln3_b"][l] = dg, db
        token = send_grads(("ffn2_w_down", "ffn2_w_gu"), l,
                           [_wgrad(sv["a2"], dfb, F // 2), _wgrad(dh, sv["x2b"], F // 2)])

        dr, dmb, dysc, dyatt, dycc, dg, db = _out_bwd(dy, sv["r2"], row(ln2_g, l), full["w_out", l], token)
        small["ln2_g"][l], small["ln2_b"][l] = dg, db
        g_out = _wgrad(sv["ycat"], dmb, D)
        dqr, dkp, dvp, dsink = _attn_bwd(sv["qr"], sv["kr"], sv["vv"], attn_sink[l], dyatt, sv["yatt"])
        small["attn_sink"][l] = dsink[:, 0]
        dz, dscw, dccw, dvec = _mix_bwd(sv["z"], sv["cpre"], dysc, dycc, dqr, dkp, dvp, scw_full[l], ccw_full[l],
                                        row(cc_conv_b, l), row(cc_ln_g, l), row(cc_ln_b, l), cos, sin)
        small["sc_conv_w"][l], small["cc_conv_w"][l] = dscw, dccw
        small["cc_conv_b"][l], small["cc_ln_g"][l], small["cc_ln_b"][l] = dvec[0], dvec[1], dvec[2]
        token = send_grads(("w_out", "w_in"), l, [g_out, _wgrad(dz, sv["x1b"], D)])
        dy = _dx(dr, dz, full["w_in", l], token)

        if l > 0:
            dy, dfb, dh, dg, db = _ffn_bwd_dx(dy, sv["r1"], row(ln1_g, l), full["ffn1_w_down", l], sv["gu1"],
                                              full["ffn1_w_gu", l])
            token = send_grads(("ffn1_w_down", "ffn1_w_gu"), l,
                               [_wgrad(sv["a1"], dfb, F // 2), _wgrad(dh, sv["x0b"], F // 2)])
        else:
            dr, dfb, dh, dg, db = _ffn_bwd(dy, sv["r1"], row(ln1_g, l), full["ffn1_w_down", l], sv["gu1"])
            token = send_grads(("ffn1_w_gu",), l, [_wgrad(dh, sv["x0b"], F // 2)])
            token = send_grads(("ffn1_w_down",), l, [_wgrad(sv["a1"], dfb, F // 2, token)])
            dy = _dx(dr, dh, full["ffn1_w_gu", l], token)
        small["ln1_g"][l], small["ln1_b"][l] = dg, db
    grad_x = dy[None]

    small_names = REPLICATED + CONVS
    small_shapes = [(DEPTH,) + tuple(np.shape(small[n][0].reshape(-1))) for n in small_names]
    small_pack = _pack([jnp.stack([small[n][l].reshape(-1) for l in range(DEPTH)]) for n in small_names])
    small_all = _all_gather([small_pack], "gather_small_grads")[0]

    recv = {n: [None] * DEPTH for n in BIG}
    grads, deltas, new_m, new_v = {}, {}, {}, {}

    def receive(upto, after):
        while len(sent) > upto:
            names, l, (send, rcv, srcs, lands, _) = sent.pop(0)
            got = _recv_wait(send, rcv, list(range(len(names))), srcs, lands, _block_of, after, f"grads_wait_{names[0]}_{l}")
            for n, g in zip(names, got):
                recv[n][l] = g

    def update(n):
        if n in SWAPPED:
            outs = _sum_adam(recv[n], *[jnp.swapaxes(a, 1, 2) for a in (w[n], mom[n], var[n])], False)
            grads[n], deltas[n], new_m[n], new_v[n] = [jnp.swapaxes(a, 1, 2) for a in outs]
        else:
            grads[n], deltas[n], new_m[n], new_v[n] = _sum_adam(recv[n], w[n], mom[n], var[n], BIG_T[n])

    receive(2, dy)
    for n in ("ffn2_w_down", "ffn2_w_gu", "w_out", "w_in"):
        update(n)
    receive(0, new_v["w_in"])
    update("ffn1_w_gu")
    update("ffn1_w_down")
    small_total = _unpack(_small_sum(small_all), small_shapes)
    for n, g in zip(small_names, small_total):
        if n in CONVS:
            taps = SC_W if n == "sc_conv_w" else CC_W
            g = lax.dynamic_slice_in_dim(g.reshape(DEPTH, taps, D_SC), idx * 32, 32, axis=2)
        grads[n] = g.reshape(w[n].shape)
    outs = _small_adam(*[[src[n] for n in small_names] for src in (w, grads, mom, var)])
    for j, n in enumerate(small_names):
        deltas[n], new_m[n], new_v[n] = outs[j], outs[len(small_names) + j], outs[2 * len(small_names) + j]

    return (loss, grad_x, *[grads[n] for n in WEIGHTS], *[deltas[n] for n in WEIGHTS],
            *[new_m[n] for n in WEIGHTS], *[new_v[n] for n in WEIGHTS])
```

```python
import functools

import jax
import jax.numpy as jnp
import numpy as np
from jax import lax
from jax.experimental import pallas as pl
from jax.experimental.pallas import tpu as pltpu

F32 = jnp.float32
BF16 = jnp.bfloat16
S = jax.ShapeDtypeStruct

N_DEV = 8
DEPTH = 2
D = 1024
F = 2816
D_IN = 2048
HEAD_DIM = 64
N_Q_HEADS = 8
N_KV_HEADS = 2
GROUP = 4
D_SC = 256
D_ATT = 512
D_CC = 256
CC_W = 31
SC_W = 3
BLOCK = 128
ROPE_THETA = 10000.0
LN_EPS = 1e-5
ALPHA = (2.0 * DEPTH) ** 0.25
ADAM_LR = 0.001
ADAM_B1 = 0.9
ADAM_B2 = 0.999
ADAM_EPS = 1e-08
ADAM_WD = 0.01
ADAM_STEP = 10

O_SCB, O_SCC, O_SCH, O_Q, O_K, O_V, O_CCA, O_CCG = 0, 256, 512, 768, 1280, 1408, 1536, 1792

V7X_VMEM_BYTES = 64 * 1024 * 1024
VMEM_LIMIT = V7X_VMEM_BYTES - 8 * 1024 * 1024
TOKEN_TILE = 256
WIDE_TILE = 512
BIG_TILE = 1024
WGRAD_TOKENS = 2048
FFN_CHUNKS = (0, 768, 1536, 2176, 2816)
DX_COLS = 256
MIX_BWD_TILE = 128
HALO_FWD = 16
HALO_BWD = 16
ATT_Q_BLOCKS = 4
CONV_ROWS = 128
SUBLANES = 8

NT = (((1,), (1,)), ((), ()))
TN = (((0,), (0,)), ((), ()))
MESH = pl.DeviceIdType.MESH


def _params(sem=None):
    return pltpu.CompilerParams(dimension_semantics=sem, vmem_limit_bytes=VMEM_LIMIT)


def _sigmoid(v):
    return 1.0 / (1.0 + jnp.exp(-v))


def _ln_stats(r):
    mu = jnp.mean(r, axis=-1, keepdims=True)
    d = r - mu
    var = jnp.mean(d * d, axis=-1, keepdims=True)
    rstd = lax.rsqrt(var + LN_EPS)
    return d * rstd, rstd


def _ln_bwd(dn, xhat, rstd, gam):
    dxh = dn * gam
    return rstd * (dxh - jnp.mean(dxh, axis=-1, keepdims=True) - xhat * jnp.mean(dxh * xhat, axis=-1, keepdims=True))


def _swap_halves(v):
    n = v.shape[-1]
    lane = lax.broadcasted_iota(jnp.int32, v.shape, v.ndim - 1) % HEAD_DIM
    return jnp.where(lane < HEAD_DIM // 2, pltpu.roll(v, n - HEAD_DIM // 2, v.ndim - 1), pltpu.roll(v, HEAD_DIM // 2, v.ndim - 1))


def _wide(tab, n):
    return tab if n == 128 else jnp.concatenate([tab] * (n // 128), axis=1)


def _me():
    x, y, c = lax.axis_index("x"), lax.axis_index("y"), lax.axis_index("c")
    return x, y, c


def _peer(rel):
    x, y, c = _me()
    px = 1 - x if rel & 4 else x
    py = 1 - y if rel & 2 else y
    pc = 1 - c if rel & 1 else c
    return (px, py, pc), 4 * px + 2 * py + pc


def _exchange(srcs, dsts_shape, dst_index, src_of, dst_of, name):
    n = len(srcs)

    def body(*refs):
        ins = refs[:n]
        outs = [refs[n + dst_index[k]] for k in range(n)]
        send, recv, lsem = refs[n + len(dsts_shape):]
        x, y, c = _me()
        me = 4 * x + 2 * y + c
        local = [pltpu.make_async_copy(src_of(ins[k], k, me), dst_of(outs[k], k, me), lsem.at[k]) for k in range(n)]
        for cp in local:
            cp.start()
        sends, recvs = [], []
        for k in range(n):
            for rel in range(1, N_DEV):
                peer, pidx = _peer(rel)
                sends.append(pltpu.make_async_remote_copy(
                    src_ref=src_of(ins[k], k, pidx), dst_ref=dst_of(outs[k], k, me),
                    send_sem=send.at[k, rel - 1], recv_sem=recv.at[k, rel - 1], device_id=peer, device_id_type=MESH))
                recvs.append(pltpu.make_async_remote_copy(
                    src_ref=src_of(ins[k], k, pidx), dst_ref=dst_of(outs[k], k, pidx),
                    send_sem=send.at[k, rel - 1], recv_sem=recv.at[k, rel - 1], device_id=peer, device_id_type=MESH))
        for cp in sends:
            cp.start()
        for cp in recvs:
            cp.wait_recv()
        for cp in sends:
            cp.wait_send()
        for cp in local:
            cp.wait()

    hbm = pl.BlockSpec(memory_space=pltpu.HBM)
    return pl.pallas_call(
        body, name=name, in_specs=[hbm] * n, out_specs=[hbm] * len(dsts_shape), out_shape=dsts_shape,
        scratch_shapes=[pltpu.SemaphoreType.DMA((n, N_DEV - 1)), pltpu.SemaphoreType.DMA((n, N_DEV - 1)),
                        pltpu.SemaphoreType.DMA((n,))],
    )(*srcs)


def _all_gather(blocks, name):
    shapes = [S((N_DEV,) + b.shape, b.dtype) for b in blocks]
    return _exchange(blocks, shapes, list(range(len(blocks))), lambda ref, k, idx: ref, lambda ref, k, idx: ref.at[idx], name)


HBM_SPEC = pl.BlockSpec(memory_space=pltpu.HBM)
SEM_SPEC = pl.BlockSpec(memory_space=pltpu.SEMAPHORE)
ANY_SPEC = pl.BlockSpec(memory_space=pl.ANY)
EFFECT = pltpu.SideEffectType.DATAFLOW_SIDE_EFFECTING
N_PEERS = N_DEV - 1


def _own_slot(block):
    x, y, c = _me()
    return lax.dynamic_update_index_in_dim(lax.empty((N_DEV,) + block.shape, block.dtype), block, 4 * x + 2 * y + c, 0)


def _follow(body, n_in, in_specs, operands, after):
    if after is None:
        return body, list(in_specs), list(operands)

    def tail(*refs):
        return body(*refs[:n_in], *refs[n_in + 1:])

    return tail, list(in_specs) + [ANY_SPEC], list(operands) + [after]


def _send_start(srcs, lands, src_of, name, after):
    n = len(srcs)

    def body(*refs):
        ins, zones = refs[:n], refs[n:2 * n]
        send, recv = refs[2 * n + 1], refs[2 * n + 2]
        token = refs[-1]
        x, y, c = _me()
        me = 4 * x + 2 * y + c
        for k in range(n):
            for rel in range(1, N_DEV):
                peer, pidx = _peer(rel)
                pltpu.make_async_remote_copy(
                    src_ref=src_of(ins[k], pidx), dst_ref=zones[k].at[me],
                    send_sem=send.at[k * N_PEERS + rel - 1], recv_sem=recv.at[k * N_PEERS + rel - 1],
                    device_id=peer, device_id_type=MESH).start()
        token[...] = jnp.zeros_like(token)

    outs = pl.pallas_call(
        body, name=name,
        out_shape=(pltpu.SemaphoreType.DMA((n * N_PEERS,)), pltpu.SemaphoreType.DMA((n * N_PEERS,)),
                   *[pltpu.HBM(a.shape, a.dtype) for a in lands], S((8, 128), F32)),
        in_specs=[HBM_SPEC] * (2 * n) + [ANY_SPEC],
        out_specs=(SEM_SPEC, SEM_SPEC, *[HBM_SPEC] * n, pl.BlockSpec(memory_space=pltpu.VMEM)),
        input_output_aliases={n + i: 2 + i for i in range(n)},
        compiler_params=pltpu.CompilerParams(has_side_effects=EFFECT),
    )(*[pltpu.with_memory_space_constraint(a, pltpu.HBM) for a in list(srcs) + list(lands)], after)
    return outs[0], outs[1], list(srcs), list(outs[2:2 + n]), outs[-1]


def _recv_wait(send, recv, ks, srcs, lands, src_of, after, name):
    n = len(ks)
    after = after if isinstance(after, (tuple, list)) else (after,)

    def body(*refs):
        ins, zones = refs[:n], refs[n:2 * n]
        send_sems, recv_sems = refs[2 * n], refs[2 * n + 1]
        for j, k in enumerate(ks):
            for rel in range(1, N_DEV):
                peer, pidx = _peer(rel)
                cp = pltpu.make_async_remote_copy(
                    src_ref=src_of(ins[j], pidx), dst_ref=zones[j].at[pidx],
                    send_sem=send_sems.at[k * N_PEERS + rel - 1], recv_sem=recv_sems.at[k * N_PEERS + rel - 1],
                    device_id=peer, device_id_type=MESH)
                cp.wait_send()
                cp.wait_recv()

    outs = pl.pallas_call(
        body, name=name,
        out_shape=[pltpu.HBM(a.shape, a.dtype) for a in lands],
        in_specs=[HBM_SPEC] * (2 * n) + [SEM_SPEC, SEM_SPEC] + [ANY_SPEC] * len(after), out_specs=[HBM_SPEC] * n,
        input_output_aliases={n + i: i for i in range(n)},
        compiler_params=pltpu.CompilerParams(has_side_effects=EFFECT),
    )(*srcs, *lands, send, recv, *after)
    return list(outs)


def _whole(ref, idx):
    return ref


def _block_of(ref, idx):
    return ref.at[idx]


def _cast(x, after):
    t, d = x.shape
    tm = min(BIG_TILE, t)

    def body(x_ref, after_ref, o_ref):
        o_ref[...] = x_ref[...].astype(BF16)

    row = pl.BlockSpec((tm, d), lambda i: (i, 0))
    return pl.pallas_call(
        body, name="cast", grid=(t // tm,), in_specs=[row, ANY_SPEC], out_specs=row, out_shape=S((t, d), BF16),
        compiler_params=_params(("parallel",)),
    )(x, after)


def _resident(shape):
    return pl.BlockSpec(shape, lambda i: (0,) * len(shape), pipeline_mode=pl.Buffered(1))


def _ffn_up(xb, wgut, after=None):
    t = xb.shape[0]
    tm = min(WIDE_TILE, t)
    half = F // 2

    def body(x_ref, w_ref, gu_ref, a_ref):
        x = x_ref[...]
        for ch in range(2):
            lo = ch * half
            g = lax.dot_general(x, w_ref[lo:lo + half, :], NT, preferred_element_type=F32)
            u = lax.dot_general(x, w_ref[F + lo:F + lo + half, :], NT, preferred_element_type=F32)
            gu_ref[:, lo:lo + half] = g.astype(BF16)
            gu_ref[:, F + lo:F + lo + half] = u.astype(BF16)
            a_ref[:, lo:lo + half] = (g * _sigmoid(g) * u).astype(BF16)

    body, in_specs, operands = _follow(
        body, 2, [pl.BlockSpec((tm, D), lambda i: (i, 0)), _resident((2 * F, D))], [xb, wgut], after)
    return pl.pallas_call(
        body, name="ffn_up", grid=(t // tm,), in_specs=in_specs,
        out_specs=[pl.BlockSpec((tm, 2 * F), lambda i: (i, 0)), pl.BlockSpec((tm, F), lambda i: (i, 0))],
        out_shape=[S((t, 2 * F), BF16), S((t, F), BF16)], compiler_params=_params(("parallel",)),
    )(*operands)


def _residual(x_ref, src_ln):
    if src_ln is None:
        return x_ref[...]
    xhat, _ = _ln_stats(x_ref[...])
    return xhat * src_ln[0][...] + src_ln[1][...]


def _ffn_down_ln(a, wd, x, gam, bet, src_ln=None, after=None):
    t = x.shape[0]
    tm = min(BIG_TILE, t)
    n_ln = 0 if src_ln is None else 2

    def body(*refs):
        a_ref, w_ref, x_ref, g_ref, b_ref = refs[:5]
        r_ref, yb_ref = refs[5 + n_ln:]
        f = jnp.dot(a_ref[...], w_ref[...], preferred_element_type=F32)
        r = ALPHA * _residual(x_ref, refs[5:5 + n_ln] or None) + 0.5 * f
        r_ref[...] = r
        xhat, _ = _ln_stats(r)
        yb_ref[...] = (xhat * g_ref[...] + b_ref[...]).astype(BF16)

    row = pl.BlockSpec((tm, D), lambda i: (i, 0))
    vec = pl.BlockSpec((1, D), lambda i: (0, 0))
    body, in_specs, operands = _follow(
        body, 5 + n_ln, [pl.BlockSpec((tm, F), lambda i: (i, 0)), _resident((F, D)), row, vec, vec] + [vec] * n_ln,
        [a, wd, x, gam, bet] + list(src_ln or ()), after)
    return pl.pallas_call(
        body, name="ffn_down_ln", grid=(t // tm,), in_specs=in_specs,
        out_specs=[row, row], out_shape=[S((t, D), F32), S((t, D), BF16)],
        compiler_params=_params(("parallel",)),
    )(*operands)


def _ffn_down_ln_loss(a, wd, x, gam, bet, src_ln, target):
    t = x.shape[0]
    tm = min(BIG_TILE, t)

    def body(a_ref, w_ref, x_ref, g_ref, b_ref, sg_ref, sb_ref, t_ref, r_ref, dy_ref, part_ref):
        f = jnp.dot(a_ref[...], w_ref[...], preferred_element_type=F32)
        r = ALPHA * _residual(x_ref, (sg_ref, sb_ref)) + 0.5 * f
        r_ref[...] = r
        xhat, _ = _ln_stats(r)
        e = xhat * g_ref[...] + b_ref[...] - t_ref[...]
        dy_ref[...] = e / D

        @pl.when(pl.program_id(0) == 0)
        def _():
            part_ref[...] = jnp.zeros_like(part_ref)

        part_ref[...] += jnp.sum(e * e, axis=0, keepdims=True)

    row = pl.BlockSpec((tm, D), lambda i: (i, 0))
    vec = pl.BlockSpec((1, D), lambda i: (0, 0))
    return pl.pallas_call(
        body, name="ffn_down_ln_loss", grid=(t // tm,),
        in_specs=[pl.BlockSpec((tm, F), lambda i: (i, 0)), _resident((F, D)), row, vec, vec, vec, vec, row],
        out_specs=[row, row, vec], out_shape=[S((t, D), F32), S((t, D), F32), S((1, D), F32)],
        compiler_params=_params(("arbitrary",)),
    )(a, wd, x, gam, bet, *src_ln, target)


def _proj_in(xb, wint):
    t = xb.shape[0]
    tm = min(BIG_TILE, t)

    def body(x_ref, w_ref, z_ref):
        z_ref[...] = lax.dot_general(x_ref[...], w_ref[...], NT, preferred_element_type=F32)

    return pl.pallas_call(
        body, name="proj_in", grid=(t // tm,),
        in_specs=[pl.BlockSpec((tm, D), lambda i: (i, 0)), _resident((D_IN, D))],
        out_specs=pl.BlockSpec((tm, D_IN), lambda i: (i, 0)), out_shape=S((t, D_IN), F32),
        compiler_params=_params(("parallel",)),
    )(xb, wint)


def _halo_specs(t, tm, halo, width):
    per = tm // halo
    last = t // halo - 1
    return [pl.BlockSpec((tm, width), lambda i: (i, 0)),
            pl.BlockSpec((halo, width), lambda i: (jnp.maximum(i * per - 1, 0), 0)),
            pl.BlockSpec((halo, width), lambda i: (jnp.minimum((i + 1) * per, last), 0))]


def _taps_aligned(w_ref, x_ref, p_ref, offsets, rows):
    for r in range(SUBLANES):
        acc = jnp.zeros((rows + SUBLANES, x_ref.shape[1]), F32)
        for j, o in enumerate(offsets):
            if o % SUBLANES == r:
                acc = acc + w_ref[j:j + 1, :] * x_ref[pl.ds(o - r, rows + SUBLANES), :]
        p_ref[r] = acc
    out = p_ref[0, 0:rows, :]
    for r in range(1, SUBLANES):
        out = out + p_ref[r, pl.ds(r, rows), :]
    return out


def _mix_fwd(z, scw, ccw, ccb, ccg, ccbb, cos, sin):
    t = z.shape[0]
    tm = min(WIDE_TILE, t)
    nt = t // tm
    h = HALO_FWD
    rc = min(CONV_ROWS, tm)

    def body(z_ref, zp_ref, zn_ref, scw_ref, ccw_ref, ccb_ref, ccg_ref, ccbb_ref, cos_ref, sin_ref,
             ysc_ref, ycc_ref, c_ref, q_ref, k_ref, v_ref, u_s, ch_s, p_s):
        i = pl.program_id(0)
        pz = jnp.where(i == 0, 0.0, zp_ref[...])
        nz = jnp.where(i == nt - 1, 0.0, zn_ref[...])

        def u_of(zz):
            return zz[:, O_CCA:O_CCA + D_CC] * _sigmoid(zz[:, O_CCG:O_CCG + D_CC])

        def ch_of(zz):
            return zz[:, O_SCC:O_SCC + D_SC] * zz[:, O_SCH:O_SCH + D_SC]

        u_s[0:h, :] = u_of(pz)
        u_s[h:h + tm, :] = z_ref[:, O_CCA:O_CCA + D_CC] * _sigmoid(z_ref[:, O_CCG:O_CCG + D_CC])
        u_s[h + tm:2 * h + tm, :] = u_of(nz)
        ch_s[0:h, :] = ch_of(pz)
        ch_s[h:h + tm, :] = z_ref[:, O_SCC:O_SCC + D_SC] * z_ref[:, O_SCH:O_SCH + D_SC]
        ch_s[h + tm:2 * h + tm, :] = ch_of(nz)
        for r0 in range(0, tm, rc):
            c = _taps_aligned(ccw_ref, u_s, p_s, [r0 + h + j - CC_W // 2 for j in range(CC_W)], rc) + ccb_ref[...]
            c_ref[r0:r0 + rc, :] = c
            xhat, _ = _ln_stats(c)
            n = xhat * ccg_ref[...] + ccbb_ref[...]
            ycc_ref[r0:r0 + rc, :] = (n * _sigmoid(n)).astype(BF16)
            acc = jnp.zeros((rc, D_SC), F32)
            for j in range(SC_W):
                acc = acc + scw_ref[j:j + 1, :] * ch_s[pl.ds(r0 + h + j - SC_W // 2, rc), :]
            ysc_ref[r0:r0 + rc, :] = (z_ref[r0:r0 + rc, O_SCB:O_SCB + D_SC] * acc).astype(BF16)
        q = z_ref[:, O_Q:O_Q + D_ATT]
        q_ref[...] = ((q * _wide(cos_ref[...], D_ATT) + _swap_halves(q) * _wide(sin_ref[...], D_ATT)) * (HEAD_DIM ** -0.5)).astype(BF16)
        k = z_ref[:, O_K:O_K + 128]
        k_ref[...] = (k * cos_ref[...] + _swap_halves(k) * sin_ref[...]).astype(BF16)
        v_ref[...] = z_ref[:, O_V:O_V + 128].astype(BF16)

    def full(a):
        return pl.BlockSpec(a.shape, lambda i: (0, 0))

    def rows(w):
        return pl.BlockSpec((tm, w), lambda i: (i, 0))

    return pl.pallas_call(
        body, name="mix_fwd", grid=(nt,),
        in_specs=_halo_specs(t, tm, h, D_IN) + [full(scw), full(ccw), full(ccb), full(ccg), full(ccbb), rows(128), rows(128)],
        out_specs=[rows(D_SC), rows(D_CC), rows(D_CC), rows(D_ATT), rows(128), rows(128)],
        out_shape=[S((t, D_SC), BF16), S((t, D_CC), BF16), S((t, D_CC), F32), S((t, D_ATT), BF16), S((t, 128), BF16),
                   S((t, 128), BF16)],
        scratch_shapes=[pltpu.VMEM((tm + 2 * h, D_CC), F32), pltpu.VMEM((tm + 2 * h, D_SC), F32),
                        pltpu.VMEM((SUBLANES, rc + SUBLANES, D_CC), F32)],
        compiler_params=_params(("parallel",)),
    )(z, z, z, scw, ccw, ccb, ccg, ccbb, cos, sin)


def _band_specs(nb, width):
    return [pl.BlockSpec((BLOCK, width), lambda n: (jnp.maximum(n * ATT_Q_BLOCKS - 1, 0), 0)),
            pl.BlockSpec((ATT_Q_BLOCKS * BLOCK, width), lambda n: (n, 0)),
            pl.BlockSpec((BLOCK, width), lambda n: (jnp.minimum((n + 1) * ATT_Q_BLOCKS, nb - 1), 0))]


def _band_bias(b, nb, bias_s):
    qpos = lax.broadcasted_iota(jnp.int32, (BLOCK, 3 * BLOCK), 0)
    col = lax.broadcasted_iota(jnp.int32, (BLOCK, 3 * BLOCK), 1)
    ok = jnp.abs(qpos - (col - BLOCK)) <= BLOCK
    ok = jnp.logical_and(ok, jnp.logical_or(col >= BLOCK, b > 0))
    ok = jnp.logical_and(ok, jnp.logical_or(col < 2 * BLOCK, b < nb - 1))
    bias_s[...] = jnp.where(ok, 0.0, -1e30)


def _band_cats(before_ref, own_ref, after_ref):
    pieces = [(before_ref, 0)] + [(own_ref, j * BLOCK) for j in range(ATT_Q_BLOCKS)] + [(after_ref, 0)]
    return [[jnp.concatenate([r[r0:r0 + BLOCK, kvh * HEAD_DIM:(kvh + 1) * HEAD_DIM] for r, r0 in pieces[sub:sub + 3]], axis=0)
             for kvh in range(N_KV_HEADS)] for sub in range(ATT_Q_BLOCKS)]


def _head_scores(q_ref, kc, sub, h, bias_s):
    qh = q_ref[sub * BLOCK:(sub + 1) * BLOCK, h * HEAD_DIM:(h + 1) * HEAD_DIM]
    return qh, lax.dot_general(qh, kc, NT, preferred_element_type=F32) + bias_s[sub]


def _softmax_parts(s, sk):
    m = jnp.maximum(jnp.max(s, axis=-1, keepdims=True), sk)
    p = jnp.exp(s - m)
    ps = jnp.exp(sk - m)
    return p, ps, jnp.sum(p, axis=-1, keepdims=True) + ps


def _attn_fwd(qr, kr, vv, sink, after=None):
    t = qr.shape[0]
    nb = t // BLOCK
    units = [(sub, h) for sub in range(ATT_Q_BLOCKS) for h in range(N_Q_HEADS)]

    def body(q_ref, kp_ref, ko_ref, kn_ref, vp_ref, vo_ref, vn_ref, sink_ref, o_ref, bias_s):
        n = pl.program_id(0)
        for sub in range(ATT_Q_BLOCKS):
            _band_bias(n * ATT_Q_BLOCKS + sub, nb, bias_s.at[sub])
        kcs = _band_cats(kp_ref, ko_ref, kn_ref)
        vcs = _band_cats(vp_ref, vo_ref, vn_ref)

        def scores(u):
            sub, h = units[u]
            return _head_scores(q_ref, kcs[sub][h // GROUP], sub, h, bias_s)[1]

        s_next = scores(0)
        for u, (sub, h) in enumerate(units):
            s = s_next
            if u + 1 < len(units):
                s_next = scores(u + 1)
            p, _, denom = _softmax_parts(s, sink_ref[h])
            o = jnp.dot(p.astype(BF16), vcs[sub][h // GROUP], preferred_element_type=F32) * (1.0 / denom)
            o_ref[sub * BLOCK:(sub + 1) * BLOCK, h * HEAD_DIM:(h + 1) * HEAD_DIM] = o.astype(BF16)

    qspec = pl.BlockSpec((ATT_Q_BLOCKS * BLOCK, D_ATT), lambda n: (n, 0))
    body, in_specs, operands = _follow(
        body, 8, [qspec] + _band_specs(nb, 128) + _band_specs(nb, 128) + [pl.BlockSpec(memory_space=pltpu.SMEM)],
        [qr, kr, kr, kr, vv, vv, vv, sink], after)
    return pl.pallas_call(
        body, name="attn_fwd", grid=(nb // ATT_Q_BLOCKS,), in_specs=in_specs,
        out_specs=qspec, out_shape=S((t, D_ATT), BF16),
        scratch_shapes=[pltpu.VMEM((ATT_Q_BLOCKS, BLOCK, 3 * BLOCK), F32)],
        compiler_params=_params(("parallel",)),
    )(*operands)


def _out_ln(ysc, yatt, ycc, wout, x, gam, bet, src_ln):
    t = x.shape[0]
    tm = min(BIG_TILE, t)

    def body(sc_ref, at_ref, cc_ref, w_ref, x_ref, g_ref, b_ref, sg_ref, sb_ref, cat_ref, r_ref, yb_ref):
        cat = jnp.concatenate([sc_ref[...], at_ref[...], cc_ref[...]], axis=1)
        cat_ref[...] = cat
        f = jnp.dot(cat, w_ref[...], preferred_element_type=F32)
        r = ALPHA * _residual(x_ref, (sg_ref, sb_ref)) + f
        r_ref[...] = r
        xhat, _ = _ln_stats(r)
        yb_ref[...] = (xhat * g_ref[...] + b_ref[...]).astype(BF16)

    def rows(w):
        return pl.BlockSpec((tm, w), lambda i: (i, 0))

    vec = pl.BlockSpec((1, D), lambda i: (0, 0))
    return pl.pallas_call(
        body, name="out_ln", grid=(t // tm,),
        in_specs=[rows(D_SC), rows(D_ATT), rows(D_CC), _resident((D, D)), rows(D), vec, vec, vec, vec],
        out_specs=[rows(D), rows(D), rows(D)],
        out_shape=[S((t, D), BF16), S((t, D), F32), S((t, D), BF16)],
        compiler_params=_params(("parallel",)),
    )(ysc, yatt, ycc, wout, x, gam, bet, *src_ln)


def _ln_bwd_block(dy_ref, r_ref, g_ref, dgam_ref, dbet_ref):
    xhat, rstd = _ln_stats(r_ref[...])
    dy = dy_ref[...]

    @pl.when(pl.program_id(0) == 0)
    def _():
        dgam_ref[...] = jnp.zeros_like(dgam_ref)
        dbet_ref[...] = jnp.zeros_like(dbet_ref)

    dgam_ref[...] += jnp.sum(dy * xhat, axis=0, keepdims=True)
    dbet_ref[...] += jnp.sum(dy, axis=0, keepdims=True)
    return _ln_bwd(dy, xhat, rstd, g_ref[...])


def _ffn_bwd(dy, r, gam, wd, gu):
    t = dy.shape[0]
    tm = min(TOKEN_TILE, t)
    chunks = list(zip(FFN_CHUNKS[:-1], FFN_CHUNKS[1:]))

    def body(dy_ref, r_ref, g_ref, w_ref, gu_ref, dr_ref, df_ref, dh_ref, dgam_ref, dbet_ref):
        dr = _ln_bwd_block(dy_ref, r_ref, g_ref, dgam_ref, dbet_ref)
        dr_ref[...] = dr
        dfb = (0.5 * dr).astype(BF16)
        df_ref[...] = dfb
        for lo, hi in chunks:
            da = lax.dot_general(dfb, w_ref[lo:hi, :], NT, preferred_element_type=F32)
            g = gu_ref[:, lo:hi].astype(F32)
            u = gu_ref[:, F + lo:F + hi].astype(F32)
            sg = _sigmoid(g)
            dh_ref[:, lo:hi] = (da * u * (sg * (1.0 + g * (1.0 - sg)))).astype(BF16)
            dh_ref[:, F + lo:F + hi] = (da * (g * sg)).astype(BF16)

    row = pl.BlockSpec((tm, D), lambda i: (i, 0))
    vec = pl.BlockSpec((1, D), lambda i: (0, 0))
    wide = pl.BlockSpec((tm, 2 * F), lambda i: (i, 0))
    return pl.pallas_call(
        body, name="ffn_bwd", grid=(t // tm,),
        in_specs=[row, row, vec, _resident((F, D)), wide],
        out_specs=[row, row, wide, vec, vec],
        out_shape=[S((t, D), F32), S((t, D), BF16), S((t, 2 * F), BF16), S((1, D), F32), S((1, D), F32)],
        compiler_params=_params(("arbitrary",)),
    )(dy, r, gam, wd, gu)


def _ffn_bwd_dx(dy, r, gam, wd, gu, wgut, after=None):
    t = dy.shape[0]
    tm = min(TOKEN_TILE, t)
    n = t // tm
    chunks = list(zip(FFN_CHUNKS[:-1], FFN_CHUNKS[1:]))

    def body(dy_ref, r_ref, g_ref, w_ref, gu_ref, wg_ref, dx_ref, df_ref, dh_ref, dgam_ref, dbet_ref,
             keep_a, keep_b, dr_keep):
        i = pl.program_id(0)

        @pl.when(i == 0)
        def _():
            keep_a[...] = jnp.zeros_like(keep_a)
            keep_b[...] = jnp.zeros_like(keep_b)
            dr_keep[...] = jnp.zeros_like(dr_keep)
            dgam_ref[...] = jnp.zeros_like(dgam_ref)
            dbet_ref[...] = jnp.zeros_like(dbet_ref)

        def step(prev, cur):
            def to_dx(c):
                cols = slice(c * DX_COLS, (c + 1) * DX_COLS)
                dx_ref[:, cols] = ALPHA * dr_keep[:, cols] + jnp.dot(prev[...], wg_ref[:, cols], preferred_element_type=F32)

            to_dx(0)
            xhat, rstd = _ln_stats(r_ref[...])
            dy_t = dy_ref[...]
            live = jnp.where(i < n, 1.0, 0.0)
            dgam_ref[...] += live * jnp.sum(dy_t * xhat, axis=0, keepdims=True)
            dbet_ref[...] += live * jnp.sum(dy_t, axis=0, keepdims=True)
            dr = _ln_bwd(dy_t, xhat, rstd, g_ref[...])
            dfb = (0.5 * dr).astype(BF16)
            df_ref[...] = dfb

            def down(c):
                return lax.dot_general(dfb, w_ref[chunks[c][0]:chunks[c][1], :], NT, preferred_element_type=F32)

            das = {0: down(0), 1: down(1)}
            for c, (lo, hi) in enumerate(chunks):
                if c + 2 < len(chunks):
                    das[c + 2] = down(c + 2)
                if c + 1 < D // DX_COLS:
                    to_dx(c + 1)
                da = das.pop(c)
                g = gu_ref[:, lo:hi].astype(F32)
                u = gu_ref[:, F + lo:F + hi].astype(F32)
                sg = _sigmoid(g)
                dg = (da * u * (sg * (1.0 + g * (1.0 - sg)))).astype(BF16)
                du = (da * (g * sg)).astype(BF16)
                dh_ref[:, lo:hi] = dg
                dh_ref[:, F + lo:F + hi] = du
                cur[:, lo:hi] = dg
                cur[:, F + lo:F + hi] = du
            dr_keep[...] = dr

        @pl.when(i % 2 == 0)
        def _():
            step(keep_b, keep_a)

        @pl.when(i % 2 == 1)
        def _():
            step(keep_a, keep_b)

    cur_row = lambda i: (jnp.minimum(i, n - 1), 0)
    row = pl.BlockSpec((tm, D), cur_row)
    vec = pl.BlockSpec((1, D), lambda i: (0, 0))
    wide = pl.BlockSpec((tm, 2 * F), cur_row)
    body, in_specs, operands = _follow(
        body, 6, [row, row, vec, _resident((F, D)), wide, _resident((2 * F, D))], [dy, r, gam, wd, gu, wgut], after)
    return pl.pallas_call(
        body, name="ffn_bwd_dx", grid=(n + 1,), in_specs=in_specs,
        out_specs=[pl.BlockSpec((tm, D), lambda i: (jnp.maximum(i - 1, 0), 0)), row, wide, vec, vec],
        out_shape=[S((t, D), F32), S((t, D), BF16), S((t, 2 * F), BF16), S((1, D), F32), S((1, D), F32)],
        scratch_shapes=[pltpu.VMEM((tm, 2 * F), BF16), pltpu.VMEM((tm, 2 * F), BF16), pltpu.VMEM((tm, D), F32)],
        compiler_params=_params(("arbitrary",)),
    )(*operands)


def _dx(dr, dh, w, after=None):
    t = dr.shape[0]
    tm = min(BIG_TILE, t)
    kk = dh.shape[1]

    def body(dr_ref, dh_ref, w_ref, o_ref):
        o_ref[...] = ALPHA * dr_ref[...] + jnp.dot(dh_ref[...], w_ref[...], preferred_element_type=F32)

    row = pl.BlockSpec((tm, D), lambda i: (i, 0))
    body, in_specs, operands = _follow(
        body, 3, [row, pl.BlockSpec((tm, kk), lambda i: (i, 0)), _resident((kk, D))], [dr, dh, w], after)
    return pl.pallas_call(
        body, name="dx", grid=(t // tm,), in_specs=in_specs,
        out_specs=row, out_shape=S((t, D), F32), compiler_params=_params(("parallel",)),
    )(*operands)


def _wgrad(a, b, ta, after=None):
    t, ka = a.shape
    tk = min(WGRAD_TOKENS, t)
    nk = t // tk

    def body(a_ref, b_ref, o_ref, acc):
        k = pl.program_id(1)

        @pl.when(k == 0)
        def _():
            acc[...] = jnp.zeros_like(acc)

        acc[...] += lax.dot_general(a_ref[...], b_ref[...], TN, preferred_element_type=F32)

        @pl.when(k == nk - 1)
        def _():
            o_ref[...] = acc[...].astype(BF16)

    body, in_specs, operands = _follow(
        body, 2, [pl.BlockSpec((tk, ta), lambda i, k: (k, i)), pl.BlockSpec((tk, D), lambda i, k: (k, 0))], [a, b], after)
    return pl.pallas_call(
        body, name="wgrad", grid=(ka // ta, nk), in_specs=in_specs,
        out_specs=pl.BlockSpec((ta, D), lambda i, k: (i, 0)), out_shape=S((ka, D), BF16),
        scratch_shapes=[pltpu.VMEM((ta, D), F32)], compiler_params=_params(("parallel", "arbitrary")),
    )(*operands)


def _out_bwd(dy, r, gam, wout, after=None):
    t = dy.shape[0]
    tm = min(BIG_TILE, t)

    def body(dy_ref, r_ref, g_ref, w_ref, dr_ref, dm_ref, dsc_ref, dat_ref, dcc_ref, dgam_ref, dbet_ref):
        dr = _ln_bwd_block(dy_ref, r_ref, g_ref, dgam_ref, dbet_ref)
        dr_ref[...] = dr
        dmb = dr.astype(BF16)
        dm_ref[...] = dmb
        dcat = lax.dot_general(dmb, w_ref[...], NT, preferred_element_type=F32)
        dsc_ref[...] = dcat[:, 0:D_SC]
        dat_ref[...] = dcat[:, D_SC:D_SC + D_ATT]
        dcc_ref[...] = dcat[:, D_SC + D_ATT:D]

    def rows(w):
        return pl.BlockSpec((tm, w), lambda i: (i, 0))

    vec = pl.BlockSpec((1, D), lambda i: (0, 0))
    body, in_specs, operands = _follow(body, 4, [rows(D), rows(D), vec, _resident((D, D))], [dy, r, gam, wout], after)
    return pl.pallas_call(
        body, name="out_bwd", grid=(t // tm,), in_specs=in_specs,
        out_specs=[rows(D), rows(D), rows(D_SC), rows(D_ATT), rows(D_CC), vec, vec],
        out_shape=[S((t, D), F32), S((t, D), BF16), S((t, D_SC), F32), S((t, D_ATT), F32), S((t, D_CC), F32),
                   S((1, D), F32), S((1, D), F32)],
        compiler_params=_params(("arbitrary",)),
    )(*operands)


def _attn_bwd(qr, kr, vv, sink, do, yatt):
    t = qr.shape[0]
    nb = t // BLOCK
    scale = HEAD_DIM ** -0.5
    units = [(sub, h) for sub in range(ATT_Q_BLOCKS) for h in range(N_Q_HEADS)]

    def body(q_ref, kp_ref, ko_ref, kn_ref, vp_ref, vo_ref, vn_ref, sink_ref, do_ref, o_ref,
             dq_ref, dk_ref, dv_ref, dsink_ref, bias_s, ds_s, p_s, q_s, dou_s, sink_s):
        n = pl.program_id(0)
        for sub in range(ATT_Q_BLOCKS):
            _band_bias(n * ATT_Q_BLOCKS + sub, nb, bias_s.at[sub])

        @pl.when(n == 0)
        def _():
            sink_s[...] = jnp.zeros_like(sink_s)

        kcs = _band_cats(kp_ref, ko_ref, kn_ref)
        vcs = _band_cats(vp_ref, vo_ref, vn_ref)

        def scores(u):
            sub, h = units[u]
            return _head_scores(q_ref, kcs[sub][h // GROUP], sub, h, bias_s)

        def probs(u, qh, s):
            sub, h = units[u]
            rows = slice(sub * BLOCK, (sub + 1) * BLOCK)
            cols = slice(h * HEAD_DIM, (h + 1) * HEAD_DIM)
            p, ps, denom = _softmax_parts(s, sink_ref[h])
            doh = do_ref[rows, cols]
            inv = 1.0 / denom
            dd = jnp.sum(doh * o_ref[rows, cols].astype(F32), axis=-1, keepdims=True) * inv
            dou = (doh * inv).astype(BF16)
            dp = lax.dot_general(dou, vcs[sub][h // GROUP], NT, preferred_element_type=F32)
            sink_s[h] -= ps * dd
            return qh, p, dd, dou, dp

        def grads(u, qh, p, dd, dou, dp):
            sub, h = units[u]
            kvh, g = divmod(h, GROUP)
            rows = slice(sub * BLOCK, (sub + 1) * BLOCK)
            cols = slice(h * HEAD_DIM, (h + 1) * HEAD_DIM)
            stack = slice(g * BLOCK, (g + 1) * BLOCK)
            ds = (p * (dp - dd)).astype(BF16)
            dq_ref[rows, cols] = jnp.dot(ds, kcs[sub][kvh], preferred_element_type=F32) * scale
            ds_s[stack, :] = ds
            p_s[stack, :] = p.astype(BF16)
            q_s[stack, :] = qh
            dou_s[stack, :] = dou
            if g == GROUP - 1:
                dk = lax.dot_general(ds_s[...], q_s[...], TN, preferred_element_type=F32)
                dv = lax.dot_general(p_s[...], dou_s[...], TN, preferred_element_type=F32)
                for j in range(3):
                    dk_ref[j, rows, kvh * HEAD_DIM:(kvh + 1) * HEAD_DIM] = dk[j * BLOCK:(j + 1) * BLOCK, :]
                    dv_ref[j, rows, kvh * HEAD_DIM:(kvh + 1) * HEAD_DIM] = dv[j * BLOCK:(j + 1) * BLOCK, :]

        sc = {0: scores(0), 1: scores(1)}
        pr = {0: probs(0, *sc.pop(0))}
        for u in range(len(units)):
            if u + 2 < len(units):
                sc[u + 2] = scores(u + 2)
            if u + 1 < len(units):
                pr[u + 1] = probs(u + 1, *sc.pop(u + 1))
            grads(u, *pr.pop(u))

        @pl.when(n == nb // ATT_Q_BLOCKS - 1)
        def _():
            for h in range(N_Q_HEADS):
                dsink_ref[h:h + 1, :] = jnp.zeros((1, 128), F32) + jnp.sum(sink_s[h])

    qspec = pl.BlockSpec((ATT_Q_BLOCKS * BLOCK, D_ATT), lambda n: (n, 0))
    part = pl.BlockSpec((3, ATT_Q_BLOCKS * BLOCK, 128), lambda n: (0, n, 0))
    stacked = GROUP * BLOCK
    return pl.pallas_call(
        body, name="attn_bwd", grid=(nb // ATT_Q_BLOCKS,),
        in_specs=[qspec] + _band_specs(nb, 128) + _band_specs(nb, 128) + [pl.BlockSpec(memory_space=pltpu.SMEM), qspec, qspec],
        out_specs=[qspec, part, part, pl.BlockSpec((N_Q_HEADS, 128), lambda n: (0, 0))],
        out_shape=[S((t, D_ATT), F32), S((3, t, 128), F32), S((3, t, 128), F32), S((N_Q_HEADS, 128), F32)],
        scratch_shapes=[pltpu.VMEM((ATT_Q_BLOCKS, BLOCK, 3 * BLOCK), F32), pltpu.VMEM((stacked, 3 * BLOCK), BF16),
                        pltpu.VMEM((stacked, 3 * BLOCK), BF16), pltpu.VMEM((stacked, HEAD_DIM), BF16),
                        pltpu.VMEM((stacked, HEAD_DIM), BF16), pltpu.VMEM((N_Q_HEADS, BLOCK, 1), F32)],
        compiler_params=_params(("arbitrary",)),
    )(qr, kr, kr, kr, vv, vv, vv, sink, do, yatt)


def _tap_grads_aligned(d_own, x_ref, d_ref, offsets, out_ref):
    rows = d_own.shape[0]
    padded = jnp.concatenate([d_own, jnp.zeros((SUBLANES, d_own.shape[1]), F32)], axis=0)
    for r in range(SUBLANES):
        d_ref[r] = padded if r == 0 else pltpu.roll(padded, r, 0)
    for j, o in enumerate(offsets):
        r = o % SUBLANES
        out_ref[j:j + 1, :] += jnp.sum(d_ref[r] * x_ref[pl.ds(o - r, rows + SUBLANES), :], axis=0, keepdims=True)


def _mix_bwd(z, c, dysc, dycc, dqr, dkp, dvp, scw, ccw, ccb, ccg, ccbb, cos, sin):
    t = z.shape[0]
    tm = min(MIX_BWD_TILE, t)
    nt = t // tm
    h = HALO_BWD
    half = CC_W // 2
    ext = tm + 2 * h

    def body(z_ref, zp_ref, zn_ref, c_ref, cp_ref, cn_ref, dsc_ref, dscp_ref, dscn_ref, dcc_ref, dccp_ref, dccn_ref,
             dq_ref, dk0_ref, dk1_ref, dk2_ref, dv0_ref, dv1_ref, dv2_ref,
             scw_ref, ccw_ref, ccb_ref, ccg_ref, ccbb_ref, cos_ref, sin_ref,
             dz_ref, dscw_ref, dccw_ref, dvec_ref, u_s, dc_s, ch_s, g_s, p_s, d_s):
        i = pl.program_id(0)
        first, last = i == 0, i == nt - 1

        @pl.when(first)
        def _():
            dscw_ref[...] = jnp.zeros_like(dscw_ref)
            dccw_ref[...] = jnp.zeros_like(dccw_ref)
            dvec_ref[...] = jnp.zeros_like(dvec_ref)

        pz = jnp.where(first, 0.0, zp_ref[...])
        nz = jnp.where(last, 0.0, zn_ref[...])
        zo = z_ref[...]

        def u_of(zz):
            return zz[:, O_CCA:O_CCA + D_CC] * _sigmoid(zz[:, O_CCG:O_CCG + D_CC])

        u_s[0:h, :] = u_of(pz)
        u_s[h:h + tm, :] = u_of(zo)
        u_s[h + tm:ext, :] = u_of(nz)
        c_ext = jnp.concatenate([jnp.where(first, 0.0, cp_ref[...]), c_ref[...], jnp.where(last, 0.0, cn_ref[...])], axis=0)
        xhat, rstd = _ln_stats(c_ext)
        nn = xhat * ccg_ref[...] + ccbb_ref[...]
        sg = _sigmoid(nn)
        dycc_ext = jnp.concatenate([jnp.where(first, 0.0, dccp_ref[...]), dcc_ref[...],
                                    jnp.where(last, 0.0, dccn_ref[...])], axis=0)
        dn = dycc_ext * (sg * (1.0 + nn * (1.0 - sg)))
        dc = _ln_bwd(dn, xhat, rstd, ccg_ref[...])
        dc_s[...] = dc
        dn_own = dn[h:h + tm, :]
        dc_own = dc[h:h + tm, :]
        dvec_ref[0:1, :] += jnp.sum(dc_own, axis=0, keepdims=True)
        dvec_ref[1:2, :] += jnp.sum(dn_own * xhat[h:h + tm, :], axis=0, keepdims=True)
        dvec_ref[2:3, :] += jnp.sum(dn_own, axis=0, keepdims=True)
        du = _taps_aligned(ccw_ref, dc_s, p_s, [h + half - j for j in range(CC_W)], tm)
        _tap_grads_aligned(dc_own, u_s, d_s, [h + j - half for j in range(CC_W)], dccw_ref)
        gate = _sigmoid(zo[:, O_CCG:O_CCG + D_CC])
        a_own = zo[:, O_CCA:O_CCA + D_CC]
        dz_ref[:, O_CCA:O_CCA + D_CC] = (du * gate).astype(BF16)
        dz_ref[:, O_CCG:O_CCG + D_CC] = (du * a_own * gate * (1.0 - gate)).astype(BF16)

        def ch_of(zz):
            return zz[:, O_SCC:O_SCC + D_SC] * zz[:, O_SCH:O_SCH + D_SC]

        ch_s[0:h, :] = ch_of(pz)
        ch_s[h:h + tm, :] = ch_of(zo)
        ch_s[h + tm:ext, :] = ch_of(nz)
        g_s[0:h, :] = jnp.where(first, 0.0, dscp_ref[...]) * pz[:, O_SCB:O_SCB + D_SC]
        g_s[h:h + tm, :] = dsc_ref[...] * zo[:, O_SCB:O_SCB + D_SC]
        g_s[h + tm:ext, :] = jnp.where(last, 0.0, dscn_ref[...]) * nz[:, O_SCB:O_SCB + D_SC]
        conv = jnp.zeros((tm, D_SC), F32)
        dch = jnp.zeros((tm, D_SC), F32)
        g_own = g_s[h:h + tm, :]
        for j in range(SC_W):
            chj = ch_s[pl.ds(h + j - SC_W // 2, tm), :]
            conv = conv + scw_ref[j:j + 1, :] * chj
            dch = dch + scw_ref[j:j + 1, :] * g_s[pl.ds(h + SC_W // 2 - j, tm), :]
            dscw_ref[j:j + 1, :] += jnp.sum(g_own * chj, axis=0, keepdims=True)
        dz_ref[:, O_SCB:O_SCB + D_SC] = (dsc_ref[...] * conv).astype(BF16)
        dz_ref[:, O_SCC:O_SCC + D_SC] = (dch * zo[:, O_SCH:O_SCH + D_SC]).astype(BF16)
        dz_ref[:, O_SCH:O_SCH + D_SC] = (dch * zo[:, O_SCC:O_SCC + D_SC]).astype(BF16)

        dq = dq_ref[...]
        dz_ref[:, O_Q:O_Q + D_ATT] = (dq * _wide(cos_ref[...], D_ATT) + _swap_halves(dq * _wide(sin_ref[...], D_ATT))).astype(BF16)
        dk = dk1_ref[0] + jnp.where(last, 0.0, dk0_ref[0]) + jnp.where(first, 0.0, dk2_ref[0])
        dz_ref[:, O_K:O_K + 128] = (dk * cos_ref[...] + _swap_halves(dk * sin_ref[...])).astype(BF16)
        dv = dv1_ref[0] + jnp.where(last, 0.0, dv0_ref[0]) + jnp.where(first, 0.0, dv2_ref[0])
        dz_ref[:, O_V:O_V + 128] = dv.astype(BF16)

    def full(a):
        return pl.BlockSpec(a.shape, lambda i: (0, 0))

    def rows(w):
        return pl.BlockSpec((tm, w), lambda i: (i, 0))

    parts = [pl.BlockSpec((1, tm, 128), lambda i: (0, jnp.minimum(i + 1, nt - 1), 0)),
             pl.BlockSpec((1, tm, 128), lambda i: (1, i, 0)),
             pl.BlockSpec((1, tm, 128), lambda i: (2, jnp.maximum(i - 1, 0), 0))]
    acc_spec = lambda r: pl.BlockSpec((r, D_CC), lambda i: (0, 0))
    return pl.pallas_call(
        body, name="mix_bwd", grid=(nt,),
        in_specs=(_halo_specs(t, tm, h, D_IN) + _halo_specs(t, tm, h, D_CC) + _halo_specs(t, tm, h, D_SC)
                  + _halo_specs(t, tm, h, D_CC) + [rows(D_ATT)] + parts + parts
                  + [full(scw), full(ccw), full(ccb), full(ccg), full(ccbb), rows(128), rows(128)]),
        out_specs=[rows(D_IN), acc_spec(SC_W), acc_spec(CC_W), acc_spec(3)],
        out_shape=[S((t, D_IN), BF16), S((SC_W, D_SC), F32), S((CC_W, D_CC), F32), S((3, D_CC), F32)],
        scratch_shapes=[pltpu.VMEM((ext, D_CC), F32), pltpu.VMEM((ext, D_CC), F32),
                        pltpu.VMEM((ext, D_SC), F32), pltpu.VMEM((ext, D_SC), F32),
                        pltpu.VMEM((SUBLANES, tm + SUBLANES, D_CC), F32), pltpu.VMEM((SUBLANES, tm + SUBLANES, D_CC), F32)],
        compiler_params=_params(("arbitrary",)),
    )(z, z, z, c, c, c, dysc, dysc, dysc, dycc, dycc, dycc, dqr, dkp, dkp, dkp, dvp, dvp, dvp,
      scw, ccw, ccb, ccg, ccbb, cos, sin)


def _adamw(w, g, m, v):
    m = ADAM_B1 * m + (1.0 - ADAM_B1) * g
    v = ADAM_B2 * v + (1.0 - ADAM_B2) * (g * g)
    m_hat = m / (1.0 - ADAM_B1 ** ADAM_STEP)
    v_hat = v / (1.0 - ADAM_B2 ** ADAM_STEP)
    delta = -ADAM_LR * (m_hat / (jnp.sqrt(v_hat) + ADAM_EPS) + ADAM_WD * w)
    return delta, m, v


def _row_tile(rows):
    for cand in (256, 176, 128):
        if rows % cand == 0:
            return cand
    return rows


def _sum_adam(recv, own, me, w, m, v, transposed):
    nl, rows = len(recv), recv[0].shape[1]
    tile = 256 if transposed else _row_tile(rows)
    nc = (D if transposed else rows) // tile

    def body(me_ref, *refs):
        w_ref, m_ref, v_ref, g_ref, d_ref, mo_ref, vo_ref = refs[2 * nl:]
        for layer in range(nl):
            @pl.when(pl.program_id(0) == layer)
            def _(r_ref=refs[layer], own_ref=refs[nl + layer]):
                mine = own_ref[0].astype(F32)
                g = jnp.where(me_ref[0] == 0, mine, r_ref[0].astype(F32))
                for s in range(1, N_DEV):
                    g = g + jnp.where(me_ref[0] == s, mine, r_ref[s].astype(F32))
                if transposed:
                    g = g.T
                g_ref[0] = g
                d_ref[0], mo_ref[0], vo_ref[0] = _adamw(w_ref[0], g, m_ref[0], v_ref[0])

    def at(layer, l, c):
        return jnp.where(l == layer, c, jnp.where(l < layer, 0, nc - 1))

    def held(layer):
        if transposed:
            return pl.BlockSpec((N_DEV, rows, tile), lambda l, c, me_ref: (0, 0, at(layer, l, c)))
        return pl.BlockSpec((N_DEV, tile, D), lambda l, c, me_ref: (0, at(layer, l, c), 0))

    def mine(layer):
        if transposed:
            return pl.BlockSpec((1, rows, tile), lambda l, c, me_ref: (me_ref[0], 0, at(layer, l, c)))
        return pl.BlockSpec((1, tile, D), lambda l, c, me_ref: (me_ref[0], at(layer, l, c), 0))

    if transposed:
        blk = pl.BlockSpec((1, tile, rows), lambda l, c, me_ref: (l, c, 0))
    else:
        blk = pl.BlockSpec((1, tile, D), lambda l, c, me_ref: (l, c, 0))
    out = S(w.shape, F32)
    return pl.pallas_call(
        body, name="sum_adam_t" if transposed else "sum_adam",
        grid_spec=pltpu.PrefetchScalarGridSpec(
            num_scalar_prefetch=1, grid=(nl, nc),
            in_specs=[held(layer) for layer in range(nl)] + [mine(layer) for layer in range(nl)] + [blk, blk, blk],
            out_specs=[blk] * 4),
        out_shape=[out] * 4, compiler_params=_params(("arbitrary", "arbitrary")),
    )(me, *recv, *own, w, m, v)


def _small_sum(gathered):
    rows = gathered.shape[1]

    def body(g_ref, o_ref):
        acc = g_ref[0]
        for s in range(1, N_DEV):
            acc = acc + g_ref[s]
        o_ref[...] = acc

    return pl.pallas_call(
        body, name="small_sum", in_specs=[pl.BlockSpec(gathered.shape, lambda: (0, 0, 0))],
        out_specs=pl.BlockSpec((rows, 128), lambda: (0, 0)), out_shape=S((rows, 128), F32),
    )(gathered)


def _small_adam(ws, gs, ms, vs):
    n = len(ws)

    def body(*refs):
        for k in range(n):
            delta, m, v = _adamw(refs[k][...], refs[n + k][...], refs[2 * n + k][...], refs[3 * n + k][...])
            refs[4 * n + k][...] = delta
            refs[5 * n + k][...] = m
            refs[6 * n + k][...] = v

    def whole(a):
        return pl.BlockSpec(a.shape, lambda: (0,) * a.ndim)

    return pl.pallas_call(
        body, name="small_adam", in_specs=[whole(a) for a in list(ws) + list(gs) + list(ms) + list(vs)],
        out_specs=[whole(a) for a in ws] * 3, out_shape=[S(a.shape, F32) for a in ws] * 3,
    )(*ws, *gs, *ms, *vs)


def _pack(pieces):
    flat = jnp.concatenate([p.reshape(-1).astype(F32) for p in pieces])
    n = flat.shape[0]
    rows = -(-n // 1024) * 8
    return jnp.pad(flat, (0, rows * 128 - n)).reshape(rows, 128)


def _unpack(packed, shapes):
    flat = packed.reshape(-1)
    out, o = [], 0
    for shp in shapes:
        n = int(np.prod(shp))
        out.append(flat[o:o + n].reshape(shp))
        o += n
    return out


def _rope_tables(t):
    half = HEAD_DIM // 2
    inv_freq = ROPE_THETA ** (-jnp.arange(half, dtype=F32) / half)
    ang = jnp.arange(t).astype(F32)[:, None] * jnp.tile(inv_freq, 128 // half)[None, :]
    sign = jnp.tile(jnp.concatenate([-jnp.ones((half,), F32), jnp.ones((half,), F32)]), 128 // HEAD_DIM)
    return jnp.cos(ang), jnp.sin(ang) * sign[None, :]


BIG = ("ffn1_w_gu", "ffn1_w_down", "w_in", "w_out", "ffn2_w_gu", "ffn2_w_down")
BIG_T = {"ffn1_w_gu": True, "ffn1_w_down": False, "w_in": True, "w_out": False, "ffn2_w_gu": True, "ffn2_w_down": False}
SWAPPED = ("ffn1_w_gu", "ffn2_w_gu")
REPLICATED = ("ln1_g", "ln1_b", "attn_sink", "cc_conv_b", "cc_ln_g", "cc_ln_b", "ln2_g", "ln2_b", "ln3_g", "ln3_b")
CONVS = ("sc_conv_w", "cc_conv_w")
WEIGHTS = ("ffn1_w_gu", "ffn1_w_down", "ln1_g", "ln1_b", "w_in", "sc_conv_w", "attn_sink", "cc_conv_w", "cc_conv_b",
           "cc_ln_g", "cc_ln_b", "w_out", "ln2_g", "ln2_b", "ffn2_w_gu", "ffn2_w_down", "ln3_g", "ln3_b")


def kernel(x, ffn1_w_gu, ffn1_w_down, ln1_g, ln1_b, w_in, sc_conv_w, attn_sink, cc_conv_w, cc_conv_b, cc_ln_g, cc_ln_b, w_out, ln2_g, ln2_b, ffn2_w_gu, ffn2_w_down, ln3_g, ln3_b, loss_target, m_ffn1_w_gu, m_ffn1_w_down, m_ln1_g, m_ln1_b, m_w_in, m_sc_conv_w, m_attn_sink, m_cc_conv_w, m_cc_conv_b, m_cc_ln_g, m_cc_ln_b, m_w_out, m_ln2_g, m_ln2_b, m_ffn2_w_gu, m_ffn2_w_down, m_ln3_g, m_ln3_b, v_ffn1_w_gu, v_ffn1_w_down, v_ln1_g, v_ln1_b, v_w_in, v_sc_conv_w, v_attn_sink, v_cc_conv_w, v_cc_conv_b, v_cc_ln_g, v_cc_ln_b, v_w_out, v_ln2_g, v_ln2_b, v_ffn2_w_gu, v_ffn2_w_down, v_ln3_g, v_ln3_b):
    args = dict(locals())
    w = {n: args[n] for n in WEIGHTS}
    mom = {n: args["m_" + n] for n in WEIGHTS}
    var = {n: args["v_" + n] for n in WEIGHTS}
    x0 = x[0]
    target = loss_target[0]
    t = x0.shape[0]
    idx = 4 * lax.axis_index("x") + 2 * lax.axis_index("y") + lax.axis_index("c")

    blocks = {(n, l): (w[n][l].T if BIG_T[n] else w[n][l]).astype(BF16) for l in range(DEPTH) for n in BIG}
    where = {}

    def start_stage(tag, members, after, extra=()):
        srcs = list(extra) + [blocks[m] for m in members]
        started = _send_start(srcs, [_own_slot(s) for s in srcs], _whole, f"gather_start_{tag}", after)
        for j, m in enumerate(members):
            where[m] = (started, len(extra) + j)
        return started

    def wait_stage(started, k, after, name):
        send, rcv, srcs, lands, _ = started
        return _recv_wait(send, rcv, [k], [srcs[k]], [lands[k]], _whole, after, name)[0]

    def weight(n, l, after):
        g = wait_stage(*where[n, l], after, f"gather_wait_{n}_{l}")
        return g.reshape(N_DEV * g.shape[1], g.shape[2])

    first = start_stage("a", [("ffn1_w_gu", 0)], x0, extra=[_pack([w["sc_conv_w"], w["cc_conv_w"]])])
    res, xb = (x0, None), _cast(x0, first[-1])
    cos, sin = _rope_tables(t)
    conv_all = wait_stage(first, 0, (xb, cos, sin), "gather_wait_convs").reshape(N_DEV, -1)
    n_sc = DEPTH * SC_W * 32
    scw_full = conv_all[:, :n_sc].reshape(N_DEV, DEPTH, SC_W, 32).transpose(1, 2, 0, 3).reshape(DEPTH, SC_W, D_SC)
    ccw_full = conv_all[:, n_sc:n_sc + DEPTH * CC_W * 32].reshape(N_DEV, DEPTH, CC_W, 32).transpose(1, 2, 0, 3).reshape(DEPTH, CC_W, D_CC)

    row = lambda a, l: a[l].reshape(1, -1)

    saved, full = [], {}
    for l in range(DEPTH):
        sv = {"x0b": xb}
        token = None
        full["ffn1_w_gu", l] = weight("ffn1_w_gu", l, (xb, scw_full, ccw_full, *blocks.values()) if l == 0 else xb)
        if l == 0:
            token = start_stage("b", [("ffn1_w_down", 0), ("w_in", 0), ("w_out", 0)], full["ffn1_w_gu", l])[-1]
        gu1, a1 = _ffn_up(xb, full["ffn1_w_gu", l], token)
        full["ffn1_w_down", l] = weight("ffn1_w_down", l, a1)
        if l == 0:
            token = start_stage("c", [("ffn2_w_gu", 0), ("ffn2_w_down", 0)], full["ffn1_w_down", l])[-1]
        r1, x1b = _ffn_down_ln(a1, full["ffn1_w_down", l], res[0], row(ln1_g, l), row(ln1_b, l), res[1], token)
        full["w_in", l] = weight("w_in", l, x1b)
        z = _proj_in(x1b, full["w_in", l])
        ysc, ycc, cpre, qr, kr, vv = _mix_fwd(z, scw_full[l], ccw_full[l], row(cc_conv_b, l), row(cc_ln_g, l), row(cc_ln_b, l), cos, sin)
        if l == 0:
            token = start_stage("d", [("ffn1_w_gu", 1), ("ffn1_w_down", 1)], ysc)[-1]
        yatt = _attn_fwd(qr, kr, vv, attn_sink[l], token)
        full["w_out", l] = weight("w_out", l, yatt)
        ycat, r2, x2b = _out_ln(ysc, yatt, ycc, full["w_out", l], r1, row(ln2_g, l), row(ln2_b, l),
                                (row(ln1_g, l), row(ln1_b, l)))
        full["ffn2_w_gu", l] = weight("ffn2_w_gu", l, x2b)
        if l == 0:
            token = start_stage("e", [("w_in", 1), ("w_out", 1), ("ffn2_w_gu", 1), ("ffn2_w_down", 1)], full["ffn2_w_gu", l])[-1]
        gu2, a2 = _ffn_up(x2b, full["ffn2_w_gu", l], token)
        full["ffn2_w_down", l] = weight("ffn2_w_down", l, a2)
        if l + 1 < DEPTH:
            r3, xb = _ffn_down_ln(a2, full["ffn2_w_down", l], r2, row(ln3_g, l), row(ln3_b, l),
                                  (row(ln2_g, l), row(ln2_b, l)))
            res = (r3, (row(ln3_g, l), row(ln3_b, l)))
        else:
            r3, dy, sq = _ffn_down_ln_loss(a2, full["ffn2_w_down", l], r2, row(ln3_g, l), row(ln3_b, l),
                                           (row(ln2_g, l), row(ln2_b, l)), target)
        sv.update(gu1=gu1, a1=a1, r1=r1, x1b=x1b, z=z, cpre=cpre, qr=qr, kr=kr, vv=vv, yatt=yatt, ycat=ycat, r2=r2, x2b=x2b,
                  gu2=gu2, a2=a2, r3=r3)
        saved.append(sv)

    loss = lax.psum(0.5 * jnp.sum(sq) / D, ("x", "y", "c"))

    sent = []
    small = {n: [None] * DEPTH for n in REPLICATED + CONVS}

    def send_grads(names, l, gs):
        srcs = [g.reshape(N_DEV, g.shape[0] // N_DEV, g.shape[1]) for g in gs]
        lands = [lax.empty(s3.shape, s3.dtype) for s3 in srcs]
        started = _send_start(srcs, lands, _block_of, f"grads_start_{names[0]}_{l}", gs[-1])
        sent.append((names, l, started))
        return started[-1]

    token = None
    for l in reversed(range(DEPTH)):
        sv = saved[l]
        dy, dfb, dh, dg, db = _ffn_bwd_dx(dy, sv["r3"], row(ln3_g, l), full["ffn2_w_down", l], sv["gu2"],
                                          full["ffn2_w_gu", l], token)
        small["ln3_g"][l], small["ln3_b"][l] = dg, db
        token = send_grads(("ffn2_w_down", "ffn2_w_gu"), l,
                           [_wgrad(sv["a2"], dfb, F // 2), _wgrad(dh, sv["x2b"], F // 2)])

        dr, dmb, dysc, dyatt, dycc, dg, db = _out_bwd(dy, sv["r2"], row(ln2_g, l), full["w_out", l], token)
        small["ln2_g"][l], small["ln2_b"][l] = dg, db
        g_out = _wgrad(sv["ycat"], dmb, D)
        dqr, dkp, dvp, dsink = _attn_bwd(sv["qr"], sv["kr"], sv["vv"], attn_sink[l], dyatt, sv["yatt"])
        small["attn_sink"][l] = dsink[:, 0]
        dz, dscw, dccw, dvec = _mix_bwd(sv["z"], sv["cpre"], dysc, dycc, dqr, dkp, dvp, scw_full[l], ccw_full[l],
                                        row(cc_conv_b, l), row(cc_ln_g, l), row(cc_ln_b, l), cos, sin)
        small["sc_conv_w"][l], small["cc_conv_w"][l] = dscw, dccw
        small["cc_conv_b"][l], small["cc_ln_g"][l], small["cc_ln_b"][l] = dvec[0], dvec[1], dvec[2]
        token = send_grads(("w_out", "w_in"), l, [g_out, _wgrad(dz, sv["x1b"], D)])
        dy = _dx(dr, dz, full["w_in", l], token)

        if l > 0:
            dy, dfb, dh, dg, db = _ffn_bwd_dx(dy, sv["r1"], row(ln1_g, l), full["ffn1_w_down", l], sv["gu1"],
                                              full["ffn1_w_gu", l])
            token = send_grads(("ffn1_w_down", "ffn1_w_gu"), l,
                               [_wgrad(sv["a1"], dfb, F // 2), _wgrad(dh, sv["x0b"], F // 2)])
        else:
            dr, dfb, dh, dg, db = _ffn_bwd(dy, sv["r1"], row(ln1_g, l), full["ffn1_w_down", l], sv["gu1"])
            token = send_grads(("ffn1_w_gu",), l, [_wgrad(dh, sv["x0b"], F // 2)])
            token = send_grads(("ffn1_w_down",), l, [_wgrad(sv["a1"], dfb, F // 2, token)])
            dy = _dx(dr, dh, full["ffn1_w_gu", l], token)
        small["ln1_g"][l], small["ln1_b"][l] = dg, db
    grad_x = dy[None]

    small_names = REPLICATED + CONVS
    small_shapes = [(DEPTH,) + tuple(np.shape(small[n][0].reshape(-1))) for n in small_names]
    small_pack = _pack([jnp.stack([small[n][l].reshape(-1) for l in range(DEPTH)]) for n in small_names])
    small_all = _all_gather([small_pack], "gather_small_grads")[0]

    recv = {n: [None] * DEPTH for n in BIG}
    own = {n: [None] * DEPTH for n in BIG}
    me = idx.astype(jnp.int32).reshape(1)
    grads, deltas, new_m, new_v = {}, {}, {}, {}

    def receive(upto, after):
        while len(sent) > upto:
            names, l, (send, rcv, srcs, lands, _) = sent.pop(0)
            got = _recv_wait(send, rcv, list(range(len(names))), srcs, lands, _block_of, after, f"grads_wait_{names[0]}_{l}")
            for n, g, mine in zip(names, got, srcs):
                recv[n][l], own[n][l] = g, mine

    def update(n):
        if n in SWAPPED:
            outs = _sum_adam(recv[n], own[n], me, *[jnp.swapaxes(a, 1, 2) for a in (w[n], mom[n], var[n])], False)
            grads[n], deltas[n], new_m[n], new_v[n] = [jnp.swapaxes(a, 1, 2) for a in outs]
        else:
            grads[n], deltas[n], new_m[n], new_v[n] = _sum_adam(recv[n], own[n], me, w[n], mom[n], var[n], BIG_T[n])

    receive(2, dy)
    for n in ("ffn2_w_down", "ffn2_w_gu", "w_out", "w_in"):
        update(n)
    receive(0, new_v["w_in"])
    update("ffn1_w_gu")
    update("ffn1_w_down")
    small_total = _unpack(_small_sum(small_all), small_shapes)
    for n, g in zip(small_names, small_total):
        if n in CONVS:
            taps = SC_W if n == "sc_conv_w" else CC_W
            g = lax.dynamic_slice_in_dim(g.reshape(DEPTH, taps, D_SC), idx * 32, 32, axis=2)
        grads[n] = g.reshape(w[n].shape)
    outs = _small_adam(*[[src[n] for n in small_names] for src in (w, grads, mom, var)])
    for j, n in enumerate(small_names):
        deltas[n], new_m[n], new_v[n] = outs[j], outs[len(small_names) + j], outs[2 * len(small_names) + j]

    return (loss, grad_x, *[grads[n] for n in WEIGHTS], *[deltas[n] for n in WEIGHTS],
            *[new_m[n] for n in WEIGHTS], *[new_v[n] for n in WEIGHTS])
```

```python
import functools

import jax
import jax.numpy as jnp
import numpy as np
from jax import lax
from jax.experimental import pallas as pl
from jax.experimental.pallas import tpu as pltpu

F32 = jnp.float32
BF16 = jnp.bfloat16
S = jax.ShapeDtypeStruct

N_DEV = 8
DEPTH = 2
D = 1024
F = 2816
D_IN = 2048
HEAD_DIM = 64
N_Q_HEADS = 8
N_KV_HEADS = 2
GROUP = 4
D_SC = 256
D_ATT = 512
D_CC = 256
CC_W = 31
SC_W = 3
BLOCK = 128
ROPE_THETA = 10000.0
LN_EPS = 1e-5
ALPHA = (2.0 * DEPTH) ** 0.25
ADAM_LR = 0.001
ADAM_B1 = 0.9
ADAM_B2 = 0.999
ADAM_EPS = 1e-08
ADAM_WD = 0.01
ADAM_STEP = 10

O_SCB, O_SCC, O_SCH, O_Q, O_K, O_V, O_CCA, O_CCG = 0, 256, 512, 768, 1280, 1408, 1536, 1792

V7X_VMEM_BYTES = 64 * 1024 * 1024
VMEM_LIMIT = V7X_VMEM_BYTES - 8 * 1024 * 1024
TOKEN_TILE = 256
WIDE_TILE = 512
BIG_TILE = 1024
WGRAD_TOKENS = 2048
FFN_CHUNKS = (0, 768, 1536, 2176, 2816)
DX_COLS = 256
MIX_BWD_TILE = 128
HALO_FWD = 16
HALO_BWD = 16
ATT_Q_BLOCKS = 4
CONV_ROWS = 128
SUBLANES = 8

NT = (((1,), (1,)), ((), ()))
TN = (((0,), (0,)), ((), ()))
MESH = pl.DeviceIdType.MESH


def _params(sem=None):
    return pltpu.CompilerParams(dimension_semantics=sem, vmem_limit_bytes=VMEM_LIMIT)


def _sigmoid(v):
    return 1.0 / (1.0 + jnp.exp(-v))


def _ln_stats(r):
    mu = jnp.mean(r, axis=-1, keepdims=True)
    d = r - mu
    var = jnp.mean(d * d, axis=-1, keepdims=True)
    rstd = lax.rsqrt(var + LN_EPS)
    return d * rstd, rstd


def _ln_bwd(dn, xhat, rstd, gam):
    dxh = dn * gam
    return rstd * (dxh - jnp.mean(dxh, axis=-1, keepdims=True) - xhat * jnp.mean(dxh * xhat, axis=-1, keepdims=True))


def _swap_halves(v):
    n = v.shape[-1]
    lane = lax.broadcasted_iota(jnp.int32, v.shape, v.ndim - 1) % HEAD_DIM
    return jnp.where(lane < HEAD_DIM // 2, pltpu.roll(v, n - HEAD_DIM // 2, v.ndim - 1), pltpu.roll(v, HEAD_DIM // 2, v.ndim - 1))


def _wide(tab, n):
    return tab if n == 128 else jnp.concatenate([tab] * (n // 128), axis=1)


def _me():
    x, y, c = lax.axis_index("x"), lax.axis_index("y"), lax.axis_index("c")
    return x, y, c


def _peer(rel):
    x, y, c = _me()
    px = 1 - x if rel & 4 else x
    py = 1 - y if rel & 2 else y
    pc = 1 - c if rel & 1 else c
    return (px, py, pc), 4 * px + 2 * py + pc


def _exchange(srcs, dsts_shape, dst_index, src_of, dst_of, name):
    n = len(srcs)

    def body(*refs):
        ins = refs[:n]
        outs = [refs[n + dst_index[k]] for k in range(n)]
        send, recv, lsem = refs[n + len(dsts_shape):]
        x, y, c = _me()
        me = 4 * x + 2 * y + c
        local = [pltpu.make_async_copy(src_of(ins[k], k, me), dst_of(outs[k], k, me), lsem.at[k]) for k in range(n)]
        for cp in local:
            cp.start()
        sends, recvs = [], []
        for k in range(n):
            for rel in range(1, N_DEV):
                peer, pidx = _peer(rel)
                sends.append(pltpu.make_async_remote_copy(
                    src_ref=src_of(ins[k], k, pidx), dst_ref=dst_of(outs[k], k, me),
                    send_sem=send.at[k, rel - 1], recv_sem=recv.at[k, rel - 1], device_id=peer, device_id_type=MESH))
                recvs.append(pltpu.make_async_remote_copy(
                    src_ref=src_of(ins[k], k, pidx), dst_ref=dst_of(outs[k], k, pidx),
                    send_sem=send.at[k, rel - 1], recv_sem=recv.at[k, rel - 1], device_id=peer, device_id_type=MESH))
        for cp in sends:
            cp.start()
        for cp in recvs:
            cp.wait_recv()
        for cp in sends:
            cp.wait_send()
        for cp in local:
            cp.wait()

    hbm = pl.BlockSpec(memory_space=pltpu.HBM)
    return pl.pallas_call(
        body, name=name, in_specs=[hbm] * n, out_specs=[hbm] * len(dsts_shape), out_shape=dsts_shape,
        scratch_shapes=[pltpu.SemaphoreType.DMA((n, N_DEV - 1)), pltpu.SemaphoreType.DMA((n, N_DEV - 1)),
                        pltpu.SemaphoreType.DMA((n,))],
    )(*srcs)


def _all_gather(blocks, name):
    shapes = [S((N_DEV,) + b.shape, b.dtype) for b in blocks]
    return _exchange(blocks, shapes, list(range(len(blocks))), lambda ref, k, idx: ref, lambda ref, k, idx: ref.at[idx], name)


HBM_SPEC = pl.BlockSpec(memory_space=pltpu.HBM)
SEM_SPEC = pl.BlockSpec(memory_space=pltpu.SEMAPHORE)
ANY_SPEC = pl.BlockSpec(memory_space=pl.ANY)
EFFECT = pltpu.SideEffectType.DATAFLOW_SIDE_EFFECTING
N_PEERS = N_DEV - 1


def _own_slot(block):
    x, y, c = _me()
    return lax.dynamic_update_index_in_dim(lax.empty((N_DEV,) + block.shape, block.dtype), block, 4 * x + 2 * y + c, 0)


def _follow(body, n_in, in_specs, operands, after):
    if after is None:
        return body, list(in_specs), list(operands)

    def tail(*refs):
        return body(*refs[:n_in], *refs[n_in + 1:])

    return tail, list(in_specs) + [ANY_SPEC], list(operands) + [after]


def _send_start(srcs, lands, src_of, name, after):
    n = len(srcs)

    def body(*refs):
        ins, zones = refs[:n], refs[n:2 * n]
        send, recv = refs[2 * n + 1], refs[2 * n + 2]
        token = refs[-1]
        x, y, c = _me()
        me = 4 * x + 2 * y + c
        for k in range(n):
            for rel in range(1, N_DEV):
                peer, pidx = _peer(rel)
                pltpu.make_async_remote_copy(
                    src_ref=src_of(ins[k], pidx), dst_ref=zones[k].at[me],
                    send_sem=send.at[k * N_PEERS + rel - 1], recv_sem=recv.at[k * N_PEERS + rel - 1],
                    device_id=peer, device_id_type=MESH).start()
        token[...] = jnp.zeros_like(token)

    outs = pl.pallas_call(
        body, name=name,
        out_shape=(pltpu.SemaphoreType.DMA((n * N_PEERS,)), pltpu.SemaphoreType.DMA((n * N_PEERS,)),
                   *[pltpu.HBM(a.shape, a.dtype) for a in lands], S((8, 128), F32)),
        in_specs=[HBM_SPEC] * (2 * n) + [ANY_SPEC],
        out_specs=(SEM_SPEC, SEM_SPEC, *[HBM_SPEC] * n, pl.BlockSpec(memory_space=pltpu.VMEM)),
        input_output_aliases={n + i: 2 + i for i in range(n)},
        compiler_params=pltpu.CompilerParams(has_side_effects=EFFECT),
    )(*[pltpu.with_memory_space_constraint(a, pltpu.HBM) for a in list(srcs) + list(lands)], after)
    return outs[0], outs[1], list(srcs), list(outs[2:2 + n]), outs[-1]


def _recv_wait(send, recv, ks, srcs, lands, src_of, after, name):
    n = len(ks)
    after = after if isinstance(after, (tuple, list)) else (after,)

    def body(*refs):
        ins, zones = refs[:n], refs[n:2 * n]
        send_sems, recv_sems = refs[2 * n], refs[2 * n + 1]
        for j, k in enumerate(ks):
            for rel in range(1, N_DEV):
                peer, pidx = _peer(rel)
                cp = pltpu.make_async_remote_copy(
                    src_ref=src_of(ins[j], pidx), dst_ref=zones[j].at[pidx],
                    send_sem=send_sems.at[k * N_PEERS + rel - 1], recv_sem=recv_sems.at[k * N_PEERS + rel - 1],
                    device_id=peer, device_id_type=MESH)
                cp.wait_send()
                cp.wait_recv()

    outs = pl.pallas_call(
        body, name=name,
        out_shape=[pltpu.HBM(a.shape, a.dtype) for a in lands],
        in_specs=[HBM_SPEC] * (2 * n) + [SEM_SPEC, SEM_SPEC] + [ANY_SPEC] * len(after), out_specs=[HBM_SPEC] * n,
        input_output_aliases={n + i: i for i in range(n)},
        compiler_params=pltpu.CompilerParams(has_side_effects=EFFECT),
    )(*srcs, *lands, send, recv, *after)
    return list(outs)


def _whole(ref, idx):
    return ref


def _block_of(ref, idx):
    return ref.at[idx]


def _cast(x, after):
    t, d = x.shape
    tm = min(BIG_TILE, t)

    def body(x_ref, after_ref, o_ref):
        o_ref[...] = x_ref[...].astype(BF16)

    row = pl.BlockSpec((tm, d), lambda i: (i, 0))
    return pl.pallas_call(
        body, name="cast", grid=(t // tm,), in_specs=[row, ANY_SPEC], out_specs=row, out_shape=S((t, d), BF16),
        compiler_params=_params(("parallel",)),
    )(x, after)


def _resident(shape):
    return pl.BlockSpec(shape, lambda i: (0,) * len(shape), pipeline_mode=pl.Buffered(1))


def _ffn_up(xb, wgut, after=None):
    t = xb.shape[0]
    tm = min(WIDE_TILE, t)
    half = F // 2
    chunk_rows = [(0, half), (F, half), (half, half), (F + half, half)]

    def body(*refs):
        x_ref, w_hbm = refs[0], refs[1]
        gu_ref, a_ref, w_ref, sems = refs[-4:]
        first = pl.program_id(0) == 0
        copies = [pltpu.make_async_copy(w_hbm.at[pl.ds(r0, n)], w_ref.at[pl.ds(r0, n)], sems.at[k])
                  for k, (r0, n) in enumerate(chunk_rows)]

        def compute(arrived):
            x = x_ref[...]
            for ch in range(2):
                lo = ch * half
                arrived(2 * ch)
                g = lax.dot_general(x, w_ref[lo:lo + half, :], NT, preferred_element_type=F32)
                arrived(2 * ch + 1)
                u = lax.dot_general(x, w_ref[F + lo:F + lo + half, :], NT, preferred_element_type=F32)
                gu_ref[:, lo:lo + half] = g.astype(BF16)
                gu_ref[:, F + lo:F + lo + half] = u.astype(BF16)
                a_ref[:, lo:lo + half] = (g * _sigmoid(g) * u).astype(BF16)

        @pl.when(first)
        def _():
            for cp in copies:
                cp.start()
            compute(lambda k: copies[k].wait())

        @pl.when(jnp.logical_not(first))
        def _():
            compute(lambda k: None)

    body, in_specs, operands = _follow(
        body, 2, [pl.BlockSpec((tm, D), lambda i: (i, 0)), ANY_SPEC], [xb, wgut], after)
    return pl.pallas_call(
        body, name="ffn_up", grid=(t // tm,), in_specs=in_specs,
        out_specs=[pl.BlockSpec((tm, 2 * F), lambda i: (i, 0)), pl.BlockSpec((tm, F), lambda i: (i, 0))],
        out_shape=[S((t, 2 * F), BF16), S((t, F), BF16)],
        scratch_shapes=[pltpu.VMEM((2 * F, D), BF16), pltpu.SemaphoreType.DMA((4,))],
        compiler_params=_params(("arbitrary",)),
    )(*operands)


def _residual(x_ref, src_ln):
    if src_ln is None:
        return x_ref[...]
    xhat, _ = _ln_stats(x_ref[...])
    return xhat * src_ln[0][...] + src_ln[1][...]


def _ffn_down_ln(a, wd, x, gam, bet, src_ln=None, after=None):
    t = x.shape[0]
    tm = min(BIG_TILE, t)
    n_ln = 0 if src_ln is None else 2

    def body(*refs):
        a_ref, w_ref, x_ref, g_ref, b_ref = refs[:5]
        r_ref, yb_ref = refs[5 + n_ln:]
        f = jnp.dot(a_ref[...], w_ref[...], preferred_element_type=F32)
        r = ALPHA * _residual(x_ref, refs[5:5 + n_ln] or None) + 0.5 * f
        r_ref[...] = r
        xhat, _ = _ln_stats(r)
        yb_ref[...] = (xhat * g_ref[...] + b_ref[...]).astype(BF16)

    row = pl.BlockSpec((tm, D), lambda i: (i, 0))
    vec = pl.BlockSpec((1, D), lambda i: (0, 0))
    body, in_specs, operands = _follow(
        body, 5 + n_ln, [pl.BlockSpec((tm, F), lambda i: (i, 0)), _resident((F, D)), row, vec, vec] + [vec] * n_ln,
        [a, wd, x, gam, bet] + list(src_ln or ()), after)
    return pl.pallas_call(
        body, name="ffn_down_ln", grid=(t // tm,), in_specs=in_specs,
        out_specs=[row, row], out_shape=[S((t, D), F32), S((t, D), BF16)],
        compiler_params=_params(("parallel",)),
    )(*operands)


def _ffn_down_ln_loss(a, wd, x, gam, bet, src_ln, target):
    t = x.shape[0]
    tm = min(BIG_TILE, t)

    def body(a_ref, w_ref, x_ref, g_ref, b_ref, sg_ref, sb_ref, t_ref, r_ref, dy_ref, part_ref):
        f = jnp.dot(a_ref[...], w_ref[...], preferred_element_type=F32)
        r = ALPHA * _residual(x_ref, (sg_ref, sb_ref)) + 0.5 * f
        r_ref[...] = r
        xhat, _ = _ln_stats(r)
        e = xhat * g_ref[...] + b_ref[...] - t_ref[...]
        dy_ref[...] = e / D

        @pl.when(pl.program_id(0) == 0)
        def _():
            part_ref[...] = jnp.zeros_like(part_ref)

        part_ref[...] += jnp.sum(e * e, axis=0, keepdims=True)

    row = pl.BlockSpec((tm, D), lambda i: (i, 0))
    vec = pl.BlockSpec((1, D), lambda i: (0, 0))
    return pl.pallas_call(
        body, name="ffn_down_ln_loss", grid=(t // tm,),
        in_specs=[pl.BlockSpec((tm, F), lambda i: (i, 0)), _resident((F, D)), row, vec, vec, vec, vec, row],
        out_specs=[row, row, vec], out_shape=[S((t, D), F32), S((t, D), F32), S((1, D), F32)],
        compiler_params=_params(("arbitrary",)),
    )(a, wd, x, gam, bet, *src_ln, target)


def _proj_in(xb, wint):
    t = xb.shape[0]
    tm = min(BIG_TILE, t)

    def body(x_ref, w_ref, z_ref):
        z_ref[...] = lax.dot_general(x_ref[...], w_ref[...], NT, preferred_element_type=F32)

    return pl.pallas_call(
        body, name="proj_in", grid=(t // tm,),
        in_specs=[pl.BlockSpec((tm, D), lambda i: (i, 0)), _resident((D_IN, D))],
        out_specs=pl.BlockSpec((tm, D_IN), lambda i: (i, 0)), out_shape=S((t, D_IN), F32),
        compiler_params=_params(("parallel",)),
    )(xb, wint)


def _halo_specs(t, tm, halo, width):
    per = tm // halo
    last = t // halo - 1
    return [pl.BlockSpec((tm, width), lambda i: (i, 0)),
            pl.BlockSpec((halo, width), lambda i: (jnp.maximum(i * per - 1, 0), 0)),
            pl.BlockSpec((halo, width), lambda i: (jnp.minimum((i + 1) * per, last), 0))]


def _taps_aligned(w_ref, x_ref, p_ref, offsets, rows):
    for r in range(SUBLANES):
        acc = jnp.zeros((rows + SUBLANES, x_ref.shape[1]), F32)
        for j, o in enumerate(offsets):
            if o % SUBLANES == r:
                acc = acc + w_ref[j:j + 1, :] * x_ref[pl.ds(o - r, rows + SUBLANES), :]
        p_ref[r] = acc
    out = p_ref[0, 0:rows, :]
    for r in range(1, SUBLANES):
        out = out + p_ref[r, pl.ds(r, rows), :]
    return out


def _mix_fwd(z, scw, ccw, ccb, ccg, ccbb, cos, sin):
    t = z.shape[0]
    tm = min(WIDE_TILE, t)
    nt = t // tm
    h = HALO_FWD
    rc = min(CONV_ROWS, tm)

    def body(z_ref, zp_ref, zn_ref, scw_ref, ccw_ref, ccb_ref, ccg_ref, ccbb_ref, cos_ref, sin_ref,
             ysc_ref, ycc_ref, c_ref, q_ref, k_ref, v_ref, u_s, ch_s, p_s):
        i = pl.program_id(0)
        pz = jnp.where(i == 0, 0.0, zp_ref[...])
        nz = jnp.where(i == nt - 1, 0.0, zn_ref[...])

        def u_of(zz):
            return zz[:, O_CCA:O_CCA + D_CC] * _sigmoid(zz[:, O_CCG:O_CCG + D_CC])

        def ch_of(zz):
            return zz[:, O_SCC:O_SCC + D_SC] * zz[:, O_SCH:O_SCH + D_SC]

        u_s[0:h, :] = u_of(pz)
        u_s[h:h + tm, :] = z_ref[:, O_CCA:O_CCA + D_CC] * _sigmoid(z_ref[:, O_CCG:O_CCG + D_CC])
        u_s[h + tm:2 * h + tm, :] = u_of(nz)
        ch_s[0:h, :] = ch_of(pz)
        ch_s[h:h + tm, :] = z_ref[:, O_SCC:O_SCC + D_SC] * z_ref[:, O_SCH:O_SCH + D_SC]
        ch_s[h + tm:2 * h + tm, :] = ch_of(nz)
        for r0 in range(0, tm, rc):
            c = _taps_aligned(ccw_ref, u_s, p_s, [r0 + h + j - CC_W // 2 for j in range(CC_W)], rc) + ccb_ref[...]
            c_ref[r0:r0 + rc, :] = c
            xhat, _ = _ln_stats(c)
            n = xhat * ccg_ref[...] + ccbb_ref[...]
            ycc_ref[r0:r0 + rc, :] = (n * _sigmoid(n)).astype(BF16)
            acc = jnp.zeros((rc, D_SC), F32)
            for j in range(SC_W):
                acc = acc + scw_ref[j:j + 1, :] * ch_s[pl.ds(r0 + h + j - SC_W // 2, rc), :]
            ysc_ref[r0:r0 + rc, :] = (z_ref[r0:r0 + rc, O_SCB:O_SCB + D_SC] * acc).astype(BF16)
        q = z_ref[:, O_Q:O_Q + D_ATT]
        q_ref[...] = ((q * _wide(cos_ref[...], D_ATT) + _swap_halves(q) * _wide(sin_ref[...], D_ATT)) * (HEAD_DIM ** -0.5)).astype(BF16)
        k = z_ref[:, O_K:O_K + 128]
        k_ref[...] = (k * cos_ref[...] + _swap_halves(k) * sin_ref[...]).astype(BF16)
        v_ref[...] = z_ref[:, O_V:O_V + 128].astype(BF16)

    def full(a):
        return pl.BlockSpec(a.shape, lambda i: (0, 0))

    def rows(w):
        return pl.BlockSpec((tm, w), lambda i: (i, 0))

    return pl.pallas_call(
        body, name="mix_fwd", grid=(nt,),
        in_specs=_halo_specs(t, tm, h, D_IN) + [full(scw), full(ccw), full(ccb), full(ccg), full(ccbb), rows(128), rows(128)],
        out_specs=[rows(D_SC), rows(D_CC), rows(D_CC), rows(D_ATT), rows(128), rows(128)],
        out_shape=[S((t, D_SC), BF16), S((t, D_CC), BF16), S((t, D_CC), F32), S((t, D_ATT), BF16), S((t, 128), BF16),
                   S((t, 128), BF16)],
        scratch_shapes=[pltpu.VMEM((tm + 2 * h, D_CC), F32), pltpu.VMEM((tm + 2 * h, D_SC), F32),
                        pltpu.VMEM((SUBLANES, rc + SUBLANES, D_CC), F32)],
        compiler_params=_params(("parallel",)),
    )(z, z, z, scw, ccw, ccb, ccg, ccbb, cos, sin)


def _band_specs(nb, width):
    return [pl.BlockSpec((BLOCK, width), lambda n: (jnp.maximum(n * ATT_Q_BLOCKS - 1, 0), 0)),
            pl.BlockSpec((ATT_Q_BLOCKS * BLOCK, width), lambda n: (n, 0)),
            pl.BlockSpec((BLOCK, width), lambda n: (jnp.minimum((n + 1) * ATT_Q_BLOCKS, nb - 1), 0))]


def _band_bias(b, nb, bias_s):
    qpos = lax.broadcasted_iota(jnp.int32, (BLOCK, 3 * BLOCK), 0)
    col = lax.broadcasted_iota(jnp.int32, (BLOCK, 3 * BLOCK), 1)
    ok = jnp.abs(qpos - (col - BLOCK)) <= BLOCK
    ok = jnp.logical_and(ok, jnp.logical_or(col >= BLOCK, b > 0))
    ok = jnp.logical_and(ok, jnp.logical_or(col < 2 * BLOCK, b < nb - 1))
    bias_s[...] = jnp.where(ok, 0.0, -1e30)


def _band_cats(before_ref, own_ref, after_ref):
    pieces = [(before_ref, 0)] + [(own_ref, j * BLOCK) for j in range(ATT_Q_BLOCKS)] + [(after_ref, 0)]
    return [[jnp.concatenate([r[r0:r0 + BLOCK, kvh * HEAD_DIM:(kvh + 1) * HEAD_DIM] for r, r0 in pieces[sub:sub + 3]], axis=0)
             for kvh in range(N_KV_HEADS)] for sub in range(ATT_Q_BLOCKS)]


def _head_scores(q_ref, kc, sub, h, bias_s):
    qh = q_ref[sub * BLOCK:(sub + 1) * BLOCK, h * HEAD_DIM:(h + 1) * HEAD_DIM]
    return qh, lax.dot_general(qh, kc, NT, preferred_element_type=F32) + bias_s[sub]


def _softmax_parts(s, sk):
    m = jnp.maximum(jnp.max(s, axis=-1, keepdims=True), sk)
    p = jnp.exp(s - m)
    ps = jnp.exp(sk - m)
    return p, ps, jnp.sum(p, axis=-1, keepdims=True) + ps


def _attn_fwd(qr, kr, vv, sink, after=None):
    t = qr.shape[0]
    nb = t // BLOCK
    units = [(sub, h) for sub in range(ATT_Q_BLOCKS) for h in range(N_Q_HEADS)]

    def body(q_ref, kp_ref, ko_ref, kn_ref, vp_ref, vo_ref, vn_ref, sink_ref, o_ref, bias_s):
        n = pl.program_id(0)
        for sub in range(ATT_Q_BLOCKS):
            _band_bias(n * ATT_Q_BLOCKS + sub, nb, bias_s.at[sub])
        kcs = _band_cats(kp_ref, ko_ref, kn_ref)
        vcs = _band_cats(vp_ref, vo_ref, vn_ref)

        def scores(u):
            sub, h = units[u]
            return _head_scores(q_ref, kcs[sub][h // GROUP], sub, h, bias_s)[1]

        s_next = scores(0)
        for u, (sub, h) in enumerate(units):
            s = s_next
            if u + 1 < len(units):
                s_next = scores(u + 1)
            p, _, denom = _softmax_parts(s, sink_ref[h])
            o = jnp.dot(p.astype(BF16), vcs[sub][h // GROUP], preferred_element_type=F32) * (1.0 / denom)
            o_ref[sub * BLOCK:(sub + 1) * BLOCK, h * HEAD_DIM:(h + 1) * HEAD_DIM] = o.astype(BF16)

    qspec = pl.BlockSpec((ATT_Q_BLOCKS * BLOCK, D_ATT), lambda n: (n, 0))
    body, in_specs, operands = _follow(
        body, 8, [qspec] + _band_specs(nb, 128) + _band_specs(nb, 128) + [pl.BlockSpec(memory_space=pltpu.SMEM)],
        [qr, kr, kr, kr, vv, vv, vv, sink], after)
    return pl.pallas_call(
        body, name="attn_fwd", grid=(nb // ATT_Q_BLOCKS,), in_specs=in_specs,
        out_specs=qspec, out_shape=S((t, D_ATT), BF16),
        scratch_shapes=[pltpu.VMEM((ATT_Q_BLOCKS, BLOCK, 3 * BLOCK), F32)],
        compiler_params=_params(("parallel",)),
    )(*operands)


def _out_ln(ysc, yatt, ycc, wout, x, gam, bet, src_ln):
    t = x.shape[0]
    tm = min(BIG_TILE, t)

    def body(sc_ref, at_ref, cc_ref, w_ref, x_ref, g_ref, b_ref, sg_ref, sb_ref, cat_ref, r_ref, yb_ref):
        cat = jnp.concatenate([sc_ref[...], at_ref[...], cc_ref[...]], axis=1)
        cat_ref[...] = cat
        f = jnp.dot(cat, w_ref[...], preferred_element_type=F32)
        r = ALPHA * _residual(x_ref, (sg_ref, sb_ref)) + f
        r_ref[...] = r
        xhat, _ = _ln_stats(r)
        yb_ref[...] = (xhat * g_ref[...] + b_ref[...]).astype(BF16)

    def rows(w):
        return pl.BlockSpec((tm, w), lambda i: (i, 0))

    vec = pl.BlockSpec((1, D), lambda i: (0, 0))
    return pl.pallas_call(
        body, name="out_ln", grid=(t // tm,),
        in_specs=[rows(D_SC), rows(D_ATT), rows(D_CC), _resident((D, D)), rows(D), vec, vec, vec, vec],
        out_specs=[rows(D), rows(D), rows(D)],
        out_shape=[S((t, D), BF16), S((t, D), F32), S((t, D), BF16)],
        compiler_params=_params(("parallel",)),
    )(ysc, yatt, ycc, wout, x, gam, bet, *src_ln)


def _ln_bwd_block(dy_ref, r_ref, g_ref, dgam_ref, dbet_ref):
    xhat, rstd = _ln_stats(r_ref[...])
    dy = dy_ref[...]

    @pl.when(pl.program_id(0) == 0)
    def _():
        dgam_ref[...] = jnp.zeros_like(dgam_ref)
        dbet_ref[...] = jnp.zeros_like(dbet_ref)

    dgam_ref[...] += jnp.sum(dy * xhat, axis=0, keepdims=True)
    dbet_ref[...] += jnp.sum(dy, axis=0, keepdims=True)
    return _ln_bwd(dy, xhat, rstd, g_ref[...])


def _ffn_bwd(dy, r, gam, wd, gu):
    t = dy.shape[0]
    tm = min(TOKEN_TILE, t)
    chunks = list(zip(FFN_CHUNKS[:-1], FFN_CHUNKS[1:]))

    def body(dy_ref, r_ref, g_ref, w_ref, gu_ref, dr_ref, df_ref, dh_ref, dgam_ref, dbet_ref):
        dr = _ln_bwd_block(dy_ref, r_ref, g_ref, dgam_ref, dbet_ref)
        dr_ref[...] = dr
        dfb = (0.5 * dr).astype(BF16)
        df_ref[...] = dfb
        for lo, hi in chunks:
            da = lax.dot_general(dfb, w_ref[lo:hi, :], NT, preferred_element_type=F32)
            g = gu_ref[:, lo:hi].astype(F32)
            u = gu_ref[:, F + lo:F + hi].astype(F32)
            sg = _sigmoid(g)
            dh_ref[:, lo:hi] = (da * u * (sg * (1.0 + g * (1.0 - sg)))).astype(BF16)
            dh_ref[:, F + lo:F + hi] = (da * (g * sg)).astype(BF16)

    row = pl.BlockSpec((tm, D), lambda i: (i, 0))
    vec = pl.BlockSpec((1, D), lambda i: (0, 0))
    wide = pl.BlockSpec((tm, 2 * F), lambda i: (i, 0))
    return pl.pallas_call(
        body, name="ffn_bwd", grid=(t // tm,),
        in_specs=[row, row, vec, _resident((F, D)), wide],
        out_specs=[row, row, wide, vec, vec],
        out_shape=[S((t, D), F32), S((t, D), BF16), S((t, 2 * F), BF16), S((1, D), F32), S((1, D), F32)],
        compiler_params=_params(("arbitrary",)),
    )(dy, r, gam, wd, gu)


def _ffn_bwd_dx(dy, r, gam, wd, gu, wgut, after=None):
    t = dy.shape[0]
    tm = min(TOKEN_TILE, t)
    n = t // tm
    chunks = list(zip(FFN_CHUNKS[:-1], FFN_CHUNKS[1:]))

    def body(dy_ref, r_ref, g_ref, w_ref, gu_ref, wg_ref, dx_ref, df_ref, dh_ref, dgam_ref, dbet_ref,
             keep_a, keep_b, dr_keep):
        i = pl.program_id(0)

        @pl.when(i == 0)
        def _():
            keep_a[...] = jnp.zeros_like(keep_a)
            keep_b[...] = jnp.zeros_like(keep_b)
            dr_keep[...] = jnp.zeros_like(dr_keep)
            dgam_ref[...] = jnp.zeros_like(dgam_ref)
            dbet_ref[...] = jnp.zeros_like(dbet_ref)

        def step(prev, cur):
            def to_dx(c):
                cols = slice(c * DX_COLS, (c + 1) * DX_COLS)
                dx_ref[:, cols] = ALPHA * dr_keep[:, cols] + jnp.dot(prev[...], wg_ref[:, cols], preferred_element_type=F32)

            to_dx(0)
            xhat, rstd = _ln_stats(r_ref[...])
            dy_t = dy_ref[...]
            live = jnp.where(i < n, 1.0, 0.0)
            dgam_ref[...] += live * jnp.sum(dy_t * xhat, axis=0, keepdims=True)
            dbet_ref[...] += live * jnp.sum(dy_t, axis=0, keepdims=True)
            dr = _ln_bwd(dy_t, xhat, rstd, g_ref[...])
            dfb = (0.5 * dr).astype(BF16)
            df_ref[...] = dfb

            def down(c):
                return lax.dot_general(dfb, w_ref[chunks[c][0]:chunks[c][1], :], NT, preferred_element_type=F32)

            das = {0: down(0), 1: down(1)}
            for c, (lo, hi) in enumerate(chunks):
                if c + 2 < len(chunks):
                    das[c + 2] = down(c + 2)
                if c + 1 < D // DX_COLS:
                    to_dx(c + 1)
                da = das.pop(c)
                g = gu_ref[:, lo:hi].astype(F32)
                u = gu_ref[:, F + lo:F + hi].astype(F32)
                sg = _sigmoid(g)
                dg = (da * u * (sg * (1.0 + g * (1.0 - sg)))).astype(BF16)
                du = (da * (g * sg)).astype(BF16)
                dh_ref[:, lo:hi] = dg
                dh_ref[:, F + lo:F + hi] = du
                cur[:, lo:hi] = dg
                cur[:, F + lo:F + hi] = du
            dr_keep[...] = dr

        @pl.when(i % 2 == 0)
        def _():
            step(keep_b, keep_a)

        @pl.when(i % 2 == 1)
        def _():
            step(keep_a, keep_b)

    cur_row = lambda i: (jnp.minimum(i, n - 1), 0)
    row = pl.BlockSpec((tm, D), cur_row)
    vec = pl.BlockSpec((1, D), lambda i: (0, 0))
    wide = pl.BlockSpec((tm, 2 * F), cur_row)
    body, in_specs, operands = _follow(
        body, 6, [row, row, vec, _resident((F, D)), wide, _resident((2 * F, D))], [dy, r, gam, wd, gu, wgut], after)
    return pl.pallas_call(
        body, name="ffn_bwd_dx", grid=(n + 1,), in_specs=in_specs,
        out_specs=[pl.BlockSpec((tm, D), lambda i: (jnp.maximum(i - 1, 0), 0)), row, wide, vec, vec],
        out_shape=[S((t, D), F32), S((t, D), BF16), S((t, 2 * F), BF16), S((1, D), F32), S((1, D), F32)],
        scratch_shapes=[pltpu.VMEM((tm, 2 * F), BF16), pltpu.VMEM((tm, 2 * F), BF16), pltpu.VMEM((tm, D), F32)],
        compiler_params=_params(("arbitrary",)),
    )(*operands)


def _dx(dr, dh, w, after=None):
    t = dr.shape[0]
    tm = min(BIG_TILE, t)
    kk = dh.shape[1]

    def body(dr_ref, dh_ref, w_ref, o_ref):
        o_ref[...] = ALPHA * dr_ref[...] + jnp.dot(dh_ref[...], w_ref[...], preferred_element_type=F32)

    row = pl.BlockSpec((tm, D), lambda i: (i, 0))
    body, in_specs, operands = _follow(
        body, 3, [row, pl.BlockSpec((tm, kk), lambda i: (i, 0)), _resident((kk, D))], [dr, dh, w], after)
    return pl.pallas_call(
        body, name="dx", grid=(t // tm,), in_specs=in_specs,
        out_specs=row, out_shape=S((t, D), F32), compiler_params=_params(("parallel",)),
    )(*operands)


def _wgrad(a, b, ta, after=None):
    t, ka = a.shape
    tk = min(WGRAD_TOKENS, t)
    nk = t // tk

    def body(a_ref, b_ref, o_ref, acc):
        k = pl.program_id(1)

        @pl.when(k == 0)
        def _():
            acc[...] = jnp.zeros_like(acc)

        acc[...] += lax.dot_general(a_ref[...], b_ref[...], TN, preferred_element_type=F32)

        @pl.when(k == nk - 1)
        def _():
            o_ref[...] = acc[...].astype(BF16)

    body, in_specs, operands = _follow(
        body, 2, [pl.BlockSpec((tk, ta), lambda i, k: (k, i)), pl.BlockSpec((tk, D), lambda i, k: (k, 0))], [a, b], after)
    return pl.pallas_call(
        body, name="wgrad", grid=(ka // ta, nk), in_specs=in_specs,
        out_specs=pl.BlockSpec((ta, D), lambda i, k: (i, 0)), out_shape=S((ka, D), BF16),
        scratch_shapes=[pltpu.VMEM((ta, D), F32)], compiler_params=_params(("parallel", "arbitrary")),
    )(*operands)


def _out_bwd(dy, r, gam, wout, after=None):
    t = dy.shape[0]
    tm = min(BIG_TILE, t)

    def body(dy_ref, r_ref, g_ref, w_ref, dr_ref, dm_ref, dsc_ref, dat_ref, dcc_ref, dgam_ref, dbet_ref):
        dr = _ln_bwd_block(dy_ref, r_ref, g_ref, dgam_ref, dbet_ref)
        dr_ref[...] = dr
        dmb = dr.astype(BF16)
        dm_ref[...] = dmb
        dcat = lax.dot_general(dmb, w_ref[...], NT, preferred_element_type=F32)
        dsc_ref[...] = dcat[:, 0:D_SC]
        dat_ref[...] = dcat[:, D_SC:D_SC + D_ATT]
        dcc_ref[...] = dcat[:, D_SC + D_ATT:D]

    def rows(w):
        return pl.BlockSpec((tm, w), lambda i: (i, 0))

    vec = pl.BlockSpec((1, D), lambda i: (0, 0))
    body, in_specs, operands = _follow(body, 4, [rows(D), rows(D), vec, _resident((D, D))], [dy, r, gam, wout], after)
    return pl.pallas_call(
        body, name="out_bwd", grid=(t // tm,), in_specs=in_specs,
        out_specs=[rows(D), rows(D), rows(D_SC), rows(D_ATT), rows(D_CC), vec, vec],
        out_shape=[S((t, D), F32), S((t, D), BF16), S((t, D_SC), F32), S((t, D_ATT), F32), S((t, D_CC), F32),
                   S((1, D), F32), S((1, D), F32)],
        compiler_params=_params(("arbitrary",)),
    )(*operands)


def _attn_bwd(qr, kr, vv, sink, do, yatt):
    t = qr.shape[0]
    nb = t // BLOCK
    scale = HEAD_DIM ** -0.5
    units = [(sub, h) for sub in range(ATT_Q_BLOCKS) for h in range(N_Q_HEADS)]

    def body(q_ref, kp_ref, ko_ref, kn_ref, vp_ref, vo_ref, vn_ref, sink_ref, do_ref, o_ref,
             dq_ref, dk_ref, dv_ref, dsink_ref, bias_s, ds_s, p_s, q_s, dou_s, sink_s):
        n = pl.program_id(0)
        for sub in range(ATT_Q_BLOCKS):
            _band_bias(n * ATT_Q_BLOCKS + sub, nb, bias_s.at[sub])

        @pl.when(n == 0)
        def _():
            sink_s[...] = jnp.zeros_like(sink_s)

        kcs = _band_cats(kp_ref, ko_ref, kn_ref)
        vcs = _band_cats(vp_ref, vo_ref, vn_ref)

        def scores(u):
            sub, h = units[u]
            return _head_scores(q_ref, kcs[sub][h // GROUP], sub, h, bias_s)

        def probs(u, qh, s):
            sub, h = units[u]
            rows = slice(sub * BLOCK, (sub + 1) * BLOCK)
            cols = slice(h * HEAD_DIM, (h + 1) * HEAD_DIM)
            p, ps, denom = _softmax_parts(s, sink_ref[h])
            doh = do_ref[rows, cols]
            inv = 1.0 / denom
            dd = jnp.sum(doh * o_ref[rows, cols].astype(F32), axis=-1, keepdims=True) * inv
            dou = (doh * inv).astype(BF16)
            dp = lax.dot_general(dou, vcs[sub][h // GROUP], NT, preferred_element_type=F32)
            sink_s[h] -= ps * dd
            return qh, p, dd, dou, dp

        def grads(u, qh, p, dd, dou, dp):
            sub, h = units[u]
            kvh, g = divmod(h, GROUP)
            rows = slice(sub * BLOCK, (sub + 1) * BLOCK)
            cols = slice(h * HEAD_DIM, (h + 1) * HEAD_DIM)
            stack = slice(g * BLOCK, (g + 1) * BLOCK)
            ds = (p * (dp - dd)).astype(BF16)
            dq_ref[rows, cols] = jnp.dot(ds, kcs[sub][kvh], preferred_element_type=F32) * scale
            ds_s[stack, :] = ds
            p_s[stack, :] = p.astype(BF16)
            q_s[stack, :] = qh
            dou_s[stack, :] = dou
            if g == GROUP - 1:
                dk = lax.dot_general(ds_s[...], q_s[...], TN, preferred_element_type=F32)
                dv = lax.dot_general(p_s[...], dou_s[...], TN, preferred_element_type=F32)
                for j in range(3):
                    dk_ref[j, rows, kvh * HEAD_DIM:(kvh + 1) * HEAD_DIM] = dk[j * BLOCK:(j + 1) * BLOCK, :]
                    dv_ref[j, rows, kvh * HEAD_DIM:(kvh + 1) * HEAD_DIM] = dv[j * BLOCK:(j + 1) * BLOCK, :]

        sc = {0: scores(0), 1: scores(1)}
        pr = {0: probs(0, *sc.pop(0))}
        for u in range(len(units)):
            if u + 2 < len(units):
                sc[u + 2] = scores(u + 2)
            if u + 1 < len(units):
                pr[u + 1] = probs(u + 1, *sc.pop(u + 1))
            grads(u, *pr.pop(u))

        @pl.when(n == nb // ATT_Q_BLOCKS - 1)
        def _():
            for h in range(N_Q_HEADS):
                dsink_ref[h:h + 1, :] = jnp.zeros((1, 128), F32) + jnp.sum(sink_s[h])

    qspec = pl.BlockSpec((ATT_Q_BLOCKS * BLOCK, D_ATT), lambda n: (n, 0))
    part = pl.BlockSpec((3, ATT_Q_BLOCKS * BLOCK, 128), lambda n: (0, n, 0))
    stacked = GROUP * BLOCK
    return pl.pallas_call(
        body, name="attn_bwd", grid=(nb // ATT_Q_BLOCKS,),
        in_specs=[qspec] + _band_specs(nb, 128) + _band_specs(nb, 128) + [pl.BlockSpec(memory_space=pltpu.SMEM), qspec, qspec],
        out_specs=[qspec, part, part, pl.BlockSpec((N_Q_HEADS, 128), lambda n: (0, 0))],
        out_shape=[S((t, D_ATT), F32), S((3, t, 128), F32), S((3, t, 128), F32), S((N_Q_HEADS, 128), F32)],
        scratch_shapes=[pltpu.VMEM((ATT_Q_BLOCKS, BLOCK, 3 * BLOCK), F32), pltpu.VMEM((stacked, 3 * BLOCK), BF16),
                        pltpu.VMEM((stacked, 3 * BLOCK), BF16), pltpu.VMEM((stacked, HEAD_DIM), BF16),
                        pltpu.VMEM((stacked, HEAD_DIM), BF16), pltpu.VMEM((N_Q_HEADS, BLOCK, 1), F32)],
        compiler_params=_params(("arbitrary",)),
    )(qr, kr, kr, kr, vv, vv, vv, sink, do, yatt)


def _tap_grads_aligned(d_own, x_ref, d_ref, offsets, out_ref):
    rows = d_own.shape[0]
    padded = jnp.concatenate([d_own, jnp.zeros((SUBLANES, d_own.shape[1]), F32)], axis=0)
    for r in range(SUBLANES):
        d_ref[r] = padded if r == 0 else pltpu.roll(padded, r, 0)
    for j, o in enumerate(offsets):
        r = o % SUBLANES
        out_ref[j:j + 1, :] += jnp.sum(d_ref[r] * x_ref[pl.ds(o - r, rows + SUBLANES), :], axis=0, keepdims=True)


def _mix_bwd(z, c, dysc, dycc, dqr, dkp, dvp, scw, ccw, ccb, ccg, ccbb, cos, sin):
    t = z.shape[0]
    tm = min(MIX_BWD_TILE, t)
    nt = t // tm
    h = HALO_BWD
    half = CC_W // 2
    ext = tm + 2 * h

    def body(z_ref, zp_ref, zn_ref, c_ref, cp_ref, cn_ref, dsc_ref, dscp_ref, dscn_ref, dcc_ref, dccp_ref, dccn_ref,
             dq_ref, dk0_ref, dk1_ref, dk2_ref, dv0_ref, dv1_ref, dv2_ref,
             scw_ref, ccw_ref, ccb_ref, ccg_ref, ccbb_ref, cos_ref, sin_ref,
             dz_ref, dscw_ref, dccw_ref, dvec_ref, u_s, dc_s, ch_s, g_s, p_s, d_s):
        i = pl.program_id(0)
        first, last = i == 0, i == nt - 1

        @pl.when(first)
        def _():
            dscw_ref[...] = jnp.zeros_like(dscw_ref)
            dccw_ref[...] = jnp.zeros_like(dccw_ref)
            dvec_ref[...] = jnp.zeros_like(dvec_ref)

        pz = jnp.where(first, 0.0, zp_ref[...])
        nz = jnp.where(last, 0.0, zn_ref[...])
        zo = z_ref[...]

        def u_of(zz):
            return zz[:, O_CCA:O_CCA + D_CC] * _sigmoid(zz[:, O_CCG:O_CCG + D_CC])

        u_s[0:h, :] = u_of(pz)
        u_s[h:h + tm, :] = u_of(zo)
        u_s[h + tm:ext, :] = u_of(nz)
        c_ext = jnp.concatenate([jnp.where(first, 0.0, cp_ref[...]), c_ref[...], jnp.where(last, 0.0, cn_ref[...])], axis=0)
        xhat, rstd = _ln_stats(c_ext)
        nn = xhat * ccg_ref[...] + ccbb_ref[...]
        sg = _sigmoid(nn)
        dycc_ext = jnp.concatenate([jnp.where(first, 0.0, dccp_ref[...]), dcc_ref[...],
                                    jnp.where(last, 0.0, dccn_ref[...])], axis=0)
        dn = dycc_ext * (sg * (1.0 + nn * (1.0 - sg)))
        dc = _ln_bwd(dn, xhat, rstd, ccg_ref[...])
        dc_s[...] = dc
        dn_own = dn[h:h + tm, :]
        dc_own = dc[h:h + tm, :]
        dvec_ref[0:1, :] += jnp.sum(dc_own, axis=0, keepdims=True)
        dvec_ref[1:2, :] += jnp.sum(dn_own * xhat[h:h + tm, :], axis=0, keepdims=True)
        dvec_ref[2:3, :] += jnp.sum(dn_own, axis=0, keepdims=True)
        du = _taps_aligned(ccw_ref, dc_s, p_s, [h + half - j for j in range(CC_W)], tm)
        _tap_grads_aligned(dc_own, u_s, d_s, [h + j - half for j in range(CC_W)], dccw_ref)
        gate = _sigmoid(zo[:, O_CCG:O_CCG + D_CC])
        a_own = zo[:, O_CCA:O_CCA + D_CC]
        dz_ref[:, O_CCA:O_CCA + D_CC] = (du * gate).astype(BF16)
        dz_ref[:, O_CCG:O_CCG + D_CC] = (du * a_own * gate * (1.0 - gate)).astype(BF16)

        def ch_of(zz):
            return zz[:, O_SCC:O_SCC + D_SC] * zz[:, O_SCH:O_SCH + D_SC]

        ch_s[0:h, :] = ch_of(pz)
        ch_s[h:h + tm, :] = ch_of(zo)
        ch_s[h + tm:ext, :] = ch_of(nz)
        g_s[0:h, :] = jnp.where(first, 0.0, dscp_ref[...]) * pz[:, O_SCB:O_SCB + D_SC]
        g_s[h:h + tm, :] = dsc_ref[...] * zo[:, O_SCB:O_SCB + D_SC]
        g_s[h + tm:ext, :] = jnp.where(last, 0.0, dscn_ref[...]) * nz[:, O_SCB:O_SCB + D_SC]
        conv = jnp.zeros((tm, D_SC), F32)
        dch = jnp.zeros((tm, D_SC), F32)
        g_own = g_s[h:h + tm, :]
        for j in range(SC_W):
            chj = ch_s[pl.ds(h + j - SC_W // 2, tm), :]
            conv = conv + scw_ref[j:j + 1, :] * chj
            dch = dch + scw_ref[j:j + 1, :] * g_s[pl.ds(h + SC_W // 2 - j, tm), :]
            dscw_ref[j:j + 1, :] += jnp.sum(g_own * chj, axis=0, keepdims=True)
        dz_ref[:, O_SCB:O_SCB + D_SC] = (dsc_ref[...] * conv).astype(BF16)
        dz_ref[:, O_SCC:O_SCC + D_SC] = (dch * zo[:, O_SCH:O_SCH + D_SC]).astype(BF16)
        dz_ref[:, O_SCH:O_SCH + D_SC] = (dch * zo[:, O_SCC:O_SCC + D_SC]).astype(BF16)

        dq = dq_ref[...]
        dz_ref[:, O_Q:O_Q + D_ATT] = (dq * _wide(cos_ref[...], D_ATT) + _swap_halves(dq * _wide(sin_ref[...], D_ATT))).astype(BF16)
        dk = dk1_ref[0] + jnp.where(last, 0.0, dk0_ref[0]) + jnp.where(first, 0.0, dk2_ref[0])
        dz_ref[:, O_K:O_K + 128] = (dk * cos_ref[...] + _swap_halves(dk * sin_ref[...])).astype(BF16)
        dv = dv1_ref[0] + jnp.where(last, 0.0, dv0_ref[0]) + jnp.where(first, 0.0, dv2_ref[0])
        dz_ref[:, O_V:O_V + 128] = dv.astype(BF16)

    def full(a):
        return pl.BlockSpec(a.shape, lambda i: (0, 0))

    def rows(w):
        return pl.BlockSpec((tm, w), lambda i: (i, 0))

    parts = [pl.BlockSpec((1, tm, 128), lambda i: (0, jnp.minimum(i + 1, nt - 1), 0)),
             pl.BlockSpec((1, tm, 128), lambda i: (1, i, 0)),
             pl.BlockSpec((1, tm, 128), lambda i: (2, jnp.maximum(i - 1, 0), 0))]
    acc_spec = lambda r: pl.BlockSpec((r, D_CC), lambda i: (0, 0))
    return pl.pallas_call(
        body, name="mix_bwd", grid=(nt,),
        in_specs=(_halo_specs(t, tm, h, D_IN) + _halo_specs(t, tm, h, D_CC) + _halo_specs(t, tm, h, D_SC)
                  + _halo_specs(t, tm, h, D_CC) + [rows(D_ATT)] + parts + parts
                  + [full(scw), full(ccw), full(ccb), full(ccg), full(ccbb), rows(128), rows(128)]),
        out_specs=[rows(D_IN), acc_spec(SC_W), acc_spec(CC_W), acc_spec(3)],
        out_shape=[S((t, D_IN), BF16), S((SC_W, D_SC), F32), S((CC_W, D_CC), F32), S((3, D_CC), F32)],
        scratch_shapes=[pltpu.VMEM((ext, D_CC), F32), pltpu.VMEM((ext, D_CC), F32),
                        pltpu.VMEM((ext, D_SC), F32), pltpu.VMEM((ext, D_SC), F32),
                        pltpu.VMEM((SUBLANES, tm + SUBLANES, D_CC), F32), pltpu.VMEM((SUBLANES, tm + SUBLANES, D_CC), F32)],
        compiler_params=_params(("arbitrary",)),
    )(z, z, z, c, c, c, dysc, dysc, dysc, dycc, dycc, dycc, dqr, dkp, dkp, dkp, dvp, dvp, dvp,
      scw, ccw, ccb, ccg, ccbb, cos, sin)


def _adamw(w, g, m, v):
    m = ADAM_B1 * m + (1.0 - ADAM_B1) * g
    v = ADAM_B2 * v + (1.0 - ADAM_B2) * (g * g)
    m_hat = m / (1.0 - ADAM_B1 ** ADAM_STEP)
    v_hat = v / (1.0 - ADAM_B2 ** ADAM_STEP)
    delta = -ADAM_LR * (m_hat / (jnp.sqrt(v_hat) + ADAM_EPS) + ADAM_WD * w)
    return delta, m, v


def _row_tile(rows):
    for cand in (256, 176, 128):
        if rows % cand == 0:
            return cand
    return rows


def _sum_adam(recv, own, me, w, m, v, transposed):
    nl, rows = len(recv), recv[0].shape[1]
    tile = 256 if transposed else _row_tile(rows)
    nc = (D if transposed else rows) // tile

    def body(me_ref, *refs):
        w_ref, m_ref, v_ref, g_ref, d_ref, mo_ref, vo_ref = refs[2 * nl:]
        for layer in range(nl):
            @pl.when(pl.program_id(0) == layer)
            def _(r_ref=refs[layer], own_ref=refs[nl + layer]):
                mine = own_ref[0].astype(F32)
                g = jnp.where(me_ref[0] == 0, mine, r_ref[0].astype(F32))
                for s in range(1, N_DEV):
                    g = g + jnp.where(me_ref[0] == s, mine, r_ref[s].astype(F32))
                if transposed:
                    g = g.T
                g_ref[0] = g
                d_ref[0], mo_ref[0], vo_ref[0] = _adamw(w_ref[0], g, m_ref[0], v_ref[0])

    def at(layer, l, c):
        return jnp.where(l == layer, c, jnp.where(l < layer, 0, nc - 1))

    def held(layer):
        if transposed:
            return pl.BlockSpec((N_DEV, rows, tile), lambda l, c, me_ref: (0, 0, at(layer, l, c)))
        return pl.BlockSpec((N_DEV, tile, D), lambda l, c, me_ref: (0, at(layer, l, c), 0))

    def mine(layer):
        if transposed:
            return pl.BlockSpec((1, rows, tile), lambda l, c, me_ref: (me_ref[0], 0, at(layer, l, c)))
        return pl.BlockSpec((1, tile, D), lambda l, c, me_ref: (me_ref[0], at(layer, l, c), 0))

    if transposed:
        blk = pl.BlockSpec((1, tile, rows), lambda l, c, me_ref: (l, c, 0))
    else:
        blk = pl.BlockSpec((1, tile, D), lambda l, c, me_ref: (l, c, 0))
    out = S(w.shape, F32)
    return pl.pallas_call(
        body, name="sum_adam_t" if transposed else "sum_adam",
        grid_spec=pltpu.PrefetchScalarGridSpec(
            num_scalar_prefetch=1, grid=(nl, nc),
            in_specs=[held(layer) for layer in range(nl)] + [mine(layer) for layer in range(nl)] + [blk, blk, blk],
            out_specs=[blk] * 4),
        out_shape=[out] * 4, compiler_params=_params(("arbitrary", "arbitrary")),
    )(me, *recv, *own, w, m, v)


def _small_sum(gathered):
    rows = gathered.shape[1]

    def body(g_ref, o_ref):
        acc = g_ref[0]
        for s in range(1, N_DEV):
            acc = acc + g_ref[s]
        o_ref[...] = acc

    return pl.pallas_call(
        body, name="small_sum", in_specs=[pl.BlockSpec(gathered.shape, lambda: (0, 0, 0))],
        out_specs=pl.BlockSpec((rows, 128), lambda: (0, 0)), out_shape=S((rows, 128), F32),
    )(gathered)


def _small_adam(ws, gs, ms, vs):
    n = len(ws)

    def body(*refs):
        for k in range(n):
            delta, m, v = _adamw(refs[k][...], refs[n + k][...], refs[2 * n + k][...], refs[3 * n + k][...])
            refs[4 * n + k][...] = delta
            refs[5 * n + k][...] = m
            refs[6 * n + k][...] = v

    def whole(a):
        return pl.BlockSpec(a.shape, lambda: (0,) * a.ndim)

    return pl.pallas_call(
        body, name="small_adam", in_specs=[whole(a) for a in list(ws) + list(gs) + list(ms) + list(vs)],
        out_specs=[whole(a) for a in ws] * 3, out_shape=[S(a.shape, F32) for a in ws] * 3,
    )(*ws, *gs, *ms, *vs)


def _pack(pieces):
    flat = jnp.concatenate([p.reshape(-1).astype(F32) for p in pieces])
    n = flat.shape[0]
    rows = -(-n // 1024) * 8
    return jnp.pad(flat, (0, rows * 128 - n)).reshape(rows, 128)


def _unpack(packed, shapes):
    flat = packed.reshape(-1)
    out, o = [], 0
    for shp in shapes:
        n = int(np.prod(shp))
        out.append(flat[o:o + n].reshape(shp))
        o += n
    return out


def _rope_tables(t):
    half = HEAD_DIM // 2
    inv_freq = ROPE_THETA ** (-jnp.arange(half, dtype=F32) / half)
    ang = jnp.arange(t).astype(F32)[:, None] * jnp.tile(inv_freq, 128 // half)[None, :]
    sign = jnp.tile(jnp.concatenate([-jnp.ones((half,), F32), jnp.ones((half,), F32)]), 128 // HEAD_DIM)
    return jnp.cos(ang), jnp.sin(ang) * sign[None, :]


BIG = ("ffn1_w_gu", "ffn1_w_down", "w_in", "w_out", "ffn2_w_gu", "ffn2_w_down")
BIG_T = {"ffn1_w_gu": True, "ffn1_w_down": False, "w_in": True, "w_out": False, "ffn2_w_gu": True, "ffn2_w_down": False}
SWAPPED = ("ffn1_w_gu", "ffn2_w_gu")
REPLICATED = ("ln1_g", "ln1_b", "attn_sink", "cc_conv_b", "cc_ln_g", "cc_ln_b", "ln2_g", "ln2_b", "ln3_g", "ln3_b")
CONVS = ("sc_conv_w", "cc_conv_w")
WEIGHTS = ("ffn1_w_gu", "ffn1_w_down", "ln1_g", "ln1_b", "w_in", "sc_conv_w", "attn_sink", "cc_conv_w", "cc_conv_b",
           "cc_ln_g", "cc_ln_b", "w_out", "ln2_g", "ln2_b", "ffn2_w_gu", "ffn2_w_down", "ln3_g", "ln3_b")


def kernel(x, ffn1_w_gu, ffn1_w_down, ln1_g, ln1_b, w_in, sc_conv_w, attn_sink, cc_conv_w, cc_conv_b, cc_ln_g, cc_ln_b, w_out, ln2_g, ln2_b, ffn2_w_gu, ffn2_w_down, ln3_g, ln3_b, loss_target, m_ffn1_w_gu, m_ffn1_w_down, m_ln1_g, m_ln1_b, m_w_in, m_sc_conv_w, m_attn_sink, m_cc_conv_w, m_cc_conv_b, m_cc_ln_g, m_cc_ln_b, m_w_out, m_ln2_g, m_ln2_b, m_ffn2_w_gu, m_ffn2_w_down, m_ln3_g, m_ln3_b, v_ffn1_w_gu, v_ffn1_w_down, v_ln1_g, v_ln1_b, v_w_in, v_sc_conv_w, v_attn_sink, v_cc_conv_w, v_cc_conv_b, v_cc_ln_g, v_cc_ln_b, v_w_out, v_ln2_g, v_ln2_b, v_ffn2_w_gu, v_ffn2_w_down, v_ln3_g, v_ln3_b):
    args = dict(locals())
    w = {n: args[n] for n in WEIGHTS}
    mom = {n: args["m_" + n] for n in WEIGHTS}
    var = {n: args["v_" + n] for n in WEIGHTS}
    x0 = x[0]
    target = loss_target[0]
    t = x0.shape[0]
    idx = 4 * lax.axis_index("x") + 2 * lax.axis_index("y") + lax.axis_index("c")

    blocks = {(n, l): (w[n][l].T if BIG_T[n] else w[n][l]).astype(BF16) for l in range(DEPTH) for n in BIG}
    where = {}

    def start_stage(tag, members, after, extra=()):
        srcs = list(extra) + [blocks[m] for m in members]
        started = _send_start(srcs, [_own_slot(s) for s in srcs], _whole, f"gather_start_{tag}", after)
        for j, m in enumerate(members):
            where[m] = (started, len(extra) + j)
        return started

    def wait_stage(started, k, after, name):
        send, rcv, srcs, lands, _ = started
        return _recv_wait(send, rcv, [k], [srcs[k]], [lands[k]], _whole, after, name)[0]

    def weight(n, l, after):
        g = wait_stage(*where[n, l], after, f"gather_wait_{n}_{l}")
        return g.reshape(N_DEV * g.shape[1], g.shape[2])

    first = start_stage("a", [("ffn1_w_gu", 0)], x0, extra=[_pack([w["sc_conv_w"], w["cc_conv_w"]])])
    res, xb = (x0, None), _cast(x0, first[-1])
    cos, sin = _rope_tables(t)
    conv_all = wait_stage(first, 0, (xb, cos, sin), "gather_wait_convs").reshape(N_DEV, -1)
    n_sc = DEPTH * SC_W * 32
    scw_full = conv_all[:, :n_sc].reshape(N_DEV, DEPTH, SC_W, 32).transpose(1, 2, 0, 3).reshape(DEPTH, SC_W, D_SC)
    ccw_full = conv_all[:, n_sc:n_sc + DEPTH * CC_W * 32].reshape(N_DEV, DEPTH, CC_W, 32).transpose(1, 2, 0, 3).reshape(DEPTH, CC_W, D_CC)

    row = lambda a, l: a[l].reshape(1, -1)

    saved, full = [], {}
    for l in range(DEPTH):
        sv = {"x0b": xb}
        token = None
        full["ffn1_w_gu", l] = weight("ffn1_w_gu", l, (xb, scw_full, ccw_full, *blocks.values()) if l == 0 else xb)
        if l == 0:
            token = start_stage("b", [("ffn1_w_down", 0), ("w_in", 0), ("w_out", 0)], full["ffn1_w_gu", l])[-1]
        gu1, a1 = _ffn_up(xb, full["ffn1_w_gu", l], token)
        full["ffn1_w_down", l] = weight("ffn1_w_down", l, a1)
        if l == 0:
            token = start_stage("c", [("ffn2_w_gu", 0), ("ffn2_w_down", 0)], full["ffn1_w_down", l])[-1]
        r1, x1b = _ffn_down_ln(a1, full["ffn1_w_down", l], res[0], row(ln1_g, l), row(ln1_b, l), res[1], token)
        full["w_in", l] = weight("w_in", l, x1b)
        z = _proj_in(x1b, full["w_in", l])
        ysc, ycc, cpre, qr, kr, vv = _mix_fwd(z, scw_full[l], ccw_full[l], row(cc_conv_b, l), row(cc_ln_g, l), row(cc_ln_b, l), cos, sin)
        if l == 0:
            token = start_stage("d", [("ffn1_w_gu", 1), ("ffn1_w_down", 1)], ysc)[-1]
        yatt = _attn_fwd(qr, kr, vv, attn_sink[l], token)
        full["w_out", l] = weight("w_out", l, yatt)
        ycat, r2, x2b = _out_ln(ysc, yatt, ycc, full["w_out", l], r1, row(ln2_g, l), row(ln2_b, l),
                                (row(ln1_g, l), row(ln1_b, l)))
        full["ffn2_w_gu", l] = weight("ffn2_w_gu", l, x2b)
        if l == 0:
            token = start_stage("e", [("w_in", 1), ("w_out", 1), ("ffn2_w_gu", 1), ("ffn2_w_down", 1)], full["ffn2_w_gu", l])[-1]
        gu2, a2 = _ffn_up(x2b, full["ffn2_w_gu", l], token)
        full["ffn2_w_down", l] = weight("ffn2_w_down", l, a2)
        if l + 1 < DEPTH:
            r3, xb = _ffn_down_ln(a2, full["ffn2_w_down", l], r2, row(ln3_g, l), row(ln3_b, l),
                                  (row(ln2_g, l), row(ln2_b, l)))
            res = (r3, (row(ln3_g, l), row(ln3_b, l)))
        else:
            r3, dy, sq = _ffn_down_ln_loss(a2, full["ffn2_w_down", l], r2, row(ln3_g, l), row(ln3_b, l),
                                           (row(ln2_g, l), row(ln2_b, l)), target)
        sv.update(gu1=gu1, a1=a1, r1=r1, x1b=x1b, z=z, cpre=cpre, qr=qr, kr=kr, vv=vv, yatt=yatt, ycat=ycat, r2=r2, x2b=x2b,
                  gu2=gu2, a2=a2, r3=r3)
        saved.append(sv)

    loss = lax.psum(0.5 * jnp.sum(sq) / D, ("x", "y", "c"))

    sent = []
    small = {n: [None] * DEPTH for n in REPLICATED + CONVS}

    def send_grads(names, l, gs):
        srcs = [g.reshape(N_DEV, g.shape[0] // N_DEV, g.shape[1]) for g in gs]
        lands = [lax.empty(s3.shape, s3.dtype) for s3 in srcs]
        started = _send_start(srcs, lands, _block_of, f"grads_start_{names[0]}_{l}", gs[-1])
        sent.append((names, l, started))
        return started[-1]

    token = None
    for l in reversed(range(DEPTH)):
        sv = saved[l]
        dy, dfb, dh, dg, db = _ffn_bwd_dx(dy, sv["r3"], row(ln3_g, l), full["ffn2_w_down", l], sv["gu2"],
                                          full["ffn2_w_gu", l], token)
        small["ln3_g"][l], small["ln3_b"][l] = dg, db
        token = send_grads(("ffn2_w_down", "ffn2_w_gu"), l,
                           [_wgrad(sv["a2"], dfb, F // 2), _wgrad(dh, sv["x2b"], F // 2)])

        dr, dmb, dysc, dyatt, dycc, dg, db = _out_bwd(dy, sv["r2"], row(ln2_g, l), full["w_out", l], token)
        small["ln2_g"][l], small["ln2_b"][l] = dg, db
        g_out = _wgrad(sv["ycat"], dmb, D)
        dqr, dkp, dvp, dsink = _attn_bwd(sv["qr"], sv["kr"], sv["vv"], attn_sink[l], dyatt, sv["yatt"])
        small["attn_sink"][l] = dsink[:, 0]
        dz, dscw, dccw, dvec = _mix_bwd(sv["z"], sv["cpre"], dysc, dycc, dqr, dkp, dvp, scw_full[l], ccw_full[l],
                                        row(cc_conv_b, l), row(cc_ln_g, l), row(cc_ln_b, l), cos, sin)
        small["sc_conv_w"][l], small["cc_conv_w"][l] = dscw, dccw
        small["cc_conv_b"][l], small["cc_ln_g"][l], small["cc_ln_b"][l] = dvec[0], dvec[1], dvec[2]
        token = send_grads(("w_out", "w_in"), l, [g_out, _wgrad(dz, sv["x1b"], D)])
        dy = _dx(dr, dz, full["w_in", l], token)

        if l > 0:
            dy, dfb, dh, dg, db = _ffn_bwd_dx(dy, sv["r1"], row(ln1_g, l), full["ffn1_w_down", l], sv["gu1"],
                                              full["ffn1_w_gu", l])
            token = send_grads(("ffn1_w_down", "ffn1_w_gu"), l,
                               [_wgrad(sv["a1"], dfb, F // 2), _wgrad(dh, sv["x0b"], F // 2)])
        else:
            dr, dfb, dh, dg, db = _ffn_bwd(dy, sv["r1"], row(ln1_g, l), full["ffn1_w_down", l], sv["gu1"])
            token = send_grads(("ffn1_w_gu",), l, [_wgrad(dh, sv["x0b"], F // 2)])
            token = send_grads(("ffn1_w_down",), l, [_wgrad(sv["a1"], dfb, F // 2, token)])
            dy = _dx(dr, dh, full["ffn1_w_gu", l], token)
        small["ln1_g"][l], small["ln1_b"][l] = dg, db
    grad_x = dy[None]

    small_names = REPLICATED + CONVS
    small_shapes = [(DEPTH,) + tuple(np.shape(small[n][0].reshape(-1))) for n in small_names]
    small_pack = _pack([jnp.stack([small[n][l].reshape(-1) for l in range(DEPTH)]) for n in small_names])
    small_all = _all_gather([small_pack], "gather_small_grads")[0]

    recv = {n: [None] * DEPTH for n in BIG}
    own = {n: [None] * DEPTH for n in BIG}
    me = idx.astype(jnp.int32).reshape(1)
    grads, deltas, new_m, new_v = {}, {}, {}, {}

    def receive(upto, after):
        while len(sent) > upto:
            names, l, (send, rcv, srcs, lands, _) = sent.pop(0)
            got = _recv_wait(send, rcv, list(range(len(names))), srcs, lands, _block_of, after, f"grads_wait_{names[0]}_{l}")
            for n, g, mine in zip(names, got, srcs):
                recv[n][l], own[n][l] = g, mine

    def update(n):
        if n in SWAPPED:
            outs = _sum_adam(recv[n], own[n], me, *[jnp.swapaxes(a, 1, 2) for a in (w[n], mom[n], var[n])], False)
            grads[n], deltas[n], new_m[n], new_v[n] = [jnp.swapaxes(a, 1, 2) for a in outs]
        else:
            grads[n], deltas[n], new_m[n], new_v[n] = _sum_adam(recv[n], own[n], me, w[n], mom[n], var[n], BIG_T[n])

    receive(2, dy)
    for n in ("ffn2_w_down", "ffn2_w_gu", "w_out", "w_in"):
        update(n)
    receive(0, new_v["w_in"])
    update("ffn1_w_gu")
    update("ffn1_w_down")
    small_total = _unpack(_small_sum(small_all), small_shapes)
    for n, g in zip(small_names, small_total):
        if n in CONVS:
            taps = SC_W if n == "sc_conv_w" else CC_W
            g = lax.dynamic_slice_in_dim(g.reshape(DEPTH, taps, D_SC), idx * 32, 32, axis=2)
        grads[n] = g.reshape(w[n].shape)
    outs = _small_adam(*[[src[n] for n in small_names] for src in (w, grads, mom, var)])
    for j, n in enumerate(small_names):
        deltas[n], new_m[n], new_v[n] = outs[j], outs[len(small_names) + j], outs[2 * len(small_names) + j]

    return (loss, grad_x, *[grads[n] for n in WEIGHTS], *[deltas[n] for n in WEIGHTS],
            *[new_m[n] for n in WEIGHTS], *[new_v[n] for n in WEIGHTS])
```
